```python
import math
import jax, jax.numpy as jnp
from jax import lax
import numpy as np


D_MODEL = 1024
BATCH = 8
SEQ = 2048
DEPTH = 1
DEC_BATCH = 32
DEC_SEQ = 4
PAST_LEN = 16384
PAGE_SIZE = 128

A_HEADS = 8
A_HEAD_DIM = 64
A_WIDTH = A_HEADS * A_HEAD_DIM
DILATED = ((128, 1), (512, 4), (2048, 16))
WIN_MAX = max(w for w, _ in DILATED)
QB = 128
POOL_WINDOWS = (2, 4, 8, 16)
POOL_GROUPS = len(POOL_WINDOWS)
B_WIDTH = D_MODEL - A_WIDTH
POOL_GROUP_DIM = B_WIDTH // POOL_GROUPS
POOL_STATE = max(POOL_WINDOWS) - 1
MIX_IN = 3 * A_WIDTH + B_WIDTH
MIX_OUT = A_WIDTH + B_WIDTH
N_MEM = 256
X_HEADS = 4
X_HEAD_DIM = D_MODEL // X_HEADS
X_WIDTH = X_HEADS * X_HEAD_DIM
N_EXPERTS = 32
TOP_K = 4
D_FF = D_MODEL
SWIGLU_LIMIT = 7.0
SWIGLU_ALPHA = 1.702
MOE_BLOCK = 128
N_BUCKETS = 32
REL_MAX_DIST = WIN_MAX
RMS_EPS = 1e-6
NEG = -1e30

kernel_name = 'hybrid_dilated_pool_moe_decode_step'


def _rmsnorm(x, g):
    xf = x.astype(jnp.float32)
    y = xf * lax.rsqrt(jnp.mean(xf * xf, axis=-1, keepdims=True) + RMS_EPS)
    return (y * g.astype(jnp.float32)).astype(x.dtype)


def _t5_bucket(n):
    max_exact = N_BUCKETS // 2
    nf = jnp.maximum(n, 1).astype(jnp.float32)
    large = max_exact + (jnp.log(nf / max_exact) / math.log(REL_MAX_DIST / max_exact)
                         * (N_BUCKETS - max_exact)).astype(jnp.int32)
    return jnp.where(n < max_exact, n, jnp.minimum(large, N_BUCKETS - 1))


def _split_mix_proj(h, w_in):
    B, S, _ = h.shape
    p = h @ w_in
    q = p[..., :A_WIDTH].reshape(B, S, A_HEADS, A_HEAD_DIM) * (A_HEAD_DIM ** -0.5)
    k = p[..., A_WIDTH:2 * A_WIDTH].reshape(B, S, A_HEADS, A_HEAD_DIM)
    v = p[..., 2 * A_WIDTH:3 * A_WIDTH].reshape(B, S, A_HEADS, A_HEAD_DIM)
    u = p[..., 3 * A_WIDTH:]
    return q, k, v, u


def _dilated_branch_prompt(q, k, v, window, dil, rel_bias):
    B, S, H, Dh = q.shape
    steps = window // dil
    L = S // dil
    nb = -(-L // QB)
    Lp = nb * QB
    N = B * dil

    def to_sub(a):
        a = a.reshape(B, L, dil, H, Dh).transpose(0, 2, 1, 3, 4).reshape(N, L, H, Dh)
        return jnp.pad(a, ((0, 0), (0, Lp - L), (0, 0), (0, 0)))

    def band(a):
        a = jnp.pad(a, ((0, 0), (QB, 0), (0, 0), (0, 0))).reshape(N, nb + 1, QB, H, Dh)
        return jnp.concatenate([a[:, :-1], a[:, 1:]], axis=2)

    qb = to_sub(q).reshape(N, nb, QB, H, Dh)
    kb = band(to_sub(k))
    vb = band(to_sub(v))
    qi = jnp.arange(QB)[:, None]
    ki = jnp.arange(2 * QB)[None, :]
    j = qi + QB - ki
    in_band = (j >= 0) & (j <= steps)
    bias = rel_bias[_t5_bucket(jnp.clip(j, 0, steps) * dil)].transpose(2, 0, 1)
    u_k = jnp.arange(nb)[:, None] * QB - QB + ki
    mask = in_band[None] & (u_k >= 0)[:, None, :]
    logits = jnp.einsum('nbqhd,nbkhd->nbhqk', qb, kb, preferred_element_type=jnp.float32)
    logits = logits + bias.astype(jnp.float32)[None, None]
    logits = jnp.where(mask[None, :, None], logits, NEG)
    lse = jax.nn.logsumexp(logits, axis=-1)
    p = jnp.exp(logits - lse[..., None])
    o = jnp.einsum('nbhqk,nbkhd->nbqhd', p.astype(v.dtype), vb)
    o = o.reshape(N, Lp, H, Dh)[:, :L].reshape(B, dil, L, H, Dh).transpose(0, 2, 1, 3, 4).reshape(B, S, H, Dh)
    lse = lse.transpose(0, 1, 3, 2).reshape(N, Lp, H)[:, :L].reshape(B, dil, L, H).transpose(0, 2, 1, 3).reshape(B, S, H)
    return o, lse


def _dilated_branch_sample(q, k_all, v_all, window, dil, rel_bias):
    T = q.shape[1]
    n_hist = k_all.shape[1] - T
    steps = window // dil
    s = jnp.arange(steps + 1)
    idx = (n_hist + jnp.arange(T))[:, None] - s[None, :] * dil
    valid = idx >= 0
    idx_c = jnp.maximum(idx, 0)
    kg = k_all[:, idx_c]
    vg = v_all[:, idx_c]
    bias = rel_bias[_t5_bucket(s * dil)].T.astype(jnp.float32)
    logits = jnp.einsum('bthd,btjhd->bhtj', q, kg, preferred_element_type=jnp.float32) + bias[None, :, None, :]
    logits = jnp.where(valid[None, None], logits, NEG)
    lse = jax.nn.logsumexp(logits, axis=-1)
    p = jnp.exp(logits - lse[..., None])
    o = jnp.einsum('bhtj,btjhd->bthd', p.astype(v_all.dtype), vg)
    return o, lse.transpose(0, 2, 1)


def _merge_branches(branches, dtype):
    outs = jnp.stack([o for o, _ in branches], 0).astype(jnp.float32)
    wts = jax.nn.softmax(jnp.stack([l for _, l in branches], 0), axis=0)
    return jnp.einsum('rbsh,rbshd->bshd', wts, outs).astype(dtype)


def _pool_mix(u_ext, n_prev, w_pool, pool_scale):
    B = u_ext.shape[0]
    S = u_ext.shape[1] - n_prev
    u_new = u_ext[:, n_prev:].astype(jnp.float32)
    cs = jnp.cumsum(jnp.pad(u_ext.astype(jnp.float32), ((0, 0), (1, 0), (0, 0))), axis=1)
    end = n_prev + jnp.arange(S) + 1
    outs = []
    for g, w in enumerate(POOL_WINDOWS):
        sl = slice(g * POOL_GROUP_DIM, (g + 1) * POOL_GROUP_DIM)
        lo = jnp.maximum(end - w, 0)
        cnt = jnp.minimum(end, w).astype(jnp.float32)[:, None]
        mean = (cs[:, end, sl] - cs[:, lo, sl]) / cnt
        outs.append(mean - u_new[..., sl])
    d = jnp.stack(outs, axis=2).astype(u_ext.dtype)
    y = jnp.einsum('bsgc,gce->bsge', d, w_pool) * pool_scale.reshape(POOL_GROUPS, POOL_GROUP_DIM)
    return y.reshape(B, S, B_WIDTH)


def _mix_out(o_a, o_b, w_out):
    B, S = o_b.shape[:2]
    return jnp.concatenate([o_a.reshape(B, S, A_WIDTH), o_b.astype(o_a.dtype)], axis=-1) @ w_out


def _mem_kv(mem, g_mem, w_xk, w_xv):
    B, M, _ = mem.shape
    m = _rmsnorm(mem, g_mem)
    return (m @ w_xk).reshape(B, M, X_HEADS, X_HEAD_DIM), (m @ w_xv).reshape(B, M, X_HEADS, X_HEAD_DIM)


def _cross_attn(h, mk, mv, w_xq, w_xo):
    B, S, _ = h.shape
    q = (h @ w_xq).reshape(B, S, X_HEADS, X_HEAD_DIM) * (X_HEAD_DIM ** -0.5)
    logits = jnp.einsum('bshd,bmhd->bhsm', q, mk, preferred_element_type=jnp.float32)
    p = jax.nn.softmax(logits, axis=-1)
    o = jnp.einsum('bhsm,bmhd->bshd', p.astype(mv.dtype), mv)
    return o.reshape(B, S, X_WIDTH) @ w_xo


def _moe(h, w_router, b_router, w_gate_up, b_gate_up, w_down, b_down):
    B, S, D = h.shape
    T = B * S
    x2 = h.reshape(T, D)
    logits = (x2 @ w_router + b_router).astype(jnp.float32)
    top_v, top_i = lax.top_k(logits, TOP_K)
    gates = jax.nn.softmax(top_v, axis=-1)
    e_flat = top_i.reshape(-1).astype(jnp.int32)
    tok_flat = jnp.broadcast_to(jnp.arange(T, dtype=jnp.int32)[:, None], (T, TOP_K)).reshape(-1)
    order = jnp.argsort(e_flat)
    e_s = e_flat[order]
    tok_s = tok_flat[order]
    g_s = gates.reshape(-1)[order]
    counts = jnp.bincount(e_flat, length=N_EXPERTS)
    padded = (counts + MOE_BLOCK - 1) // MOE_BLOCK * MOE_BLOCK
    ends = jnp.cumsum(padded)
    pstart = ends - padded
    start = jnp.cumsum(counts) - counts
    n_slots = T * TOP_K
    ppos = pstart[e_s] + jnp.arange(n_slots) - start[e_s]
    n_blocks = (n_slots + N_EXPERTS * (MOE_BLOCK - 1) + MOE_BLOCK - 1) // MOE_BLOCK
    slot_tok = jnp.full((n_blocks * MOE_BLOCK,), T, jnp.int32).at[ppos].set(tok_s)
    block_e = jnp.minimum(jnp.searchsorted(ends, jnp.arange(n_blocks) * MOE_BLOCK, side='right'), N_EXPERTS - 1)
    x_pad = jnp.concatenate([x2, jnp.zeros((1, D), x2.dtype)], axis=0)

    def expert_block(args):
        tok_b, e = args
        xb = x_pad[tok_b]
        hu = xb @ w_gate_up[e] + b_gate_up[e]
        g = jnp.minimum(hu[:, :D_FF], SWIGLU_LIMIT)
        u = jnp.clip(hu[:, D_FF:], -SWIGLU_LIMIT, SWIGLU_LIMIT)
        a = g * jax.nn.sigmoid(SWIGLU_ALPHA * g) * (u + 1)
        return a @ w_down[e] + b_down[e]

    out = lax.map(expert_block, (slot_tok.reshape(n_blocks, MOE_BLOCK), block_e))
    y_s = out.reshape(-1, D)[ppos] * g_s[:, None].astype(out.dtype)
    return jax.ops.segment_sum(y_s, tok_s, num_segments=T).reshape(B, S, D)


def setup_inputs(seed: int = 0) -> dict:
    key = jax.random.key(seed)
    ks = iter(jax.random.split(key, 32))

    def nrm(shape, scale=1.0):
        return jax.random.normal(next(ks), shape, jnp.float32) * scale

    def gain(shape):
        return 1.0 + nrm(shape, 0.05)

    lb = min(WIN_MAX, PAST_LEN)
    return {
        'x_prompt': nrm((BATCH, SEQ, D_MODEL)),
        'x_sample': nrm((DEC_BATCH, DEC_SEQ, D_MODEL)),
        'cache_k_win': nrm((DEPTH, DEC_BATCH, lb, A_HEADS, A_HEAD_DIM)),
        'cache_v_win': nrm((DEPTH, DEC_BATCH, lb, A_HEADS, A_HEAD_DIM)),
        'state_pool': nrm((DEPTH, DEC_BATCH, POOL_STATE, B_WIDTH)),
        'cache_mem_k': nrm((DEPTH, DEC_BATCH, N_MEM, X_HEADS, X_HEAD_DIM)),
        'cache_mem_v': nrm((DEPTH, DEC_BATCH, N_MEM, X_HEADS, X_HEAD_DIM)),
        'mem_prompt': nrm((BATCH, N_MEM, D_MODEL)),
        'rel_bias': nrm((N_BUCKETS, A_HEADS), 0.5),
        'g_mix': gain((DEPTH, D_MODEL)),
        'w_in': nrm((DEPTH, D_MODEL, MIX_IN), D_MODEL ** -0.5),
        'w_pool': nrm((DEPTH, POOL_GROUPS, POOL_GROUP_DIM, POOL_GROUP_DIM), POOL_GROUP_DIM ** -0.5),
        'pool_scale': gain((DEPTH, B_WIDTH)),
        'w_out': nrm((DEPTH, MIX_OUT, D_MODEL), MIX_OUT ** -0.5),
        'g_mem': gain((DEPTH, D_MODEL)),
        'g_x': gain((DEPTH, D_MODEL)),
        'w_xq': nrm((DEPTH, D_MODEL, X_WIDTH), D_MODEL ** -0.5),
        'w_xk': nrm((DEPTH, D_MODEL, X_WIDTH), D_MODEL ** -0.5),
        'w_xv': nrm((DEPTH, D_MODEL, X_WIDTH), D_MODEL ** -0.5),
        'w_xo': nrm((DEPTH, X_WIDTH, D_MODEL), X_WIDTH ** -0.5),
        'g_ff': gain((DEPTH, D_MODEL)),
        'w_router': nrm((DEPTH, D_MODEL, N_EXPERTS), D_MODEL ** -0.5),
        'b_router': nrm((DEPTH, N_EXPERTS), 0.01),
        'w_gate_up': nrm((DEPTH, N_EXPERTS, D_MODEL, 2 * D_FF), D_MODEL ** -0.5),
        'b_gate_up': nrm((DEPTH, N_EXPERTS, 2 * D_FF), 0.01),
        'w_down': nrm((DEPTH, N_EXPERTS, D_FF, D_MODEL), D_FF ** -0.5),
        'b_down': nrm((DEPTH, N_EXPERTS, D_MODEL), 0.01),
        'g_final': gain((D_MODEL,)),
    }


def reference(x_prompt, x_sample, cache_k_win, cache_v_win, state_pool, cache_mem_k, cache_mem_v, mem_prompt,
              rel_bias, g_mix, w_in, w_pool, pool_scale, w_out, g_mem, g_x, w_xq, w_xk, w_xv, w_xo, g_ff,
              w_router, b_router, w_gate_up, b_gate_up, w_down, b_down, g_final):
    xp, xs = x_prompt, x_sample
    kw_p, vw_p, pool_p, mk_p, mv_p = [], [], [], [], []
    kw_s, vw_s, pool_s = [], [], []
    for l in range(DEPTH):
        moe_w = (w_router[l], b_router[l], w_gate_up[l], b_gate_up[l], w_down[l], b_down[l])
        q, k, v, u = _split_mix_proj(_rmsnorm(xp, g_mix[l]), w_in[l])
        o_a = _merge_branches([_dilated_branch_prompt(q, k, v, w, d, rel_bias) for (w, d) in DILATED], xp.dtype)
        o_b = _pool_mix(u, 0, w_pool[l], pool_scale[l])
        xp = xp + _mix_out(o_a, o_b, w_out[l])
        mk, mv = _mem_kv(mem_prompt, g_mem[l], w_xk[l], w_xv[l])
        xp = xp + _cross_attn(_rmsnorm(xp, g_x[l]), mk, mv, w_xq[l], w_xo[l])
        xp = xp + _moe(_rmsnorm(xp, g_ff[l]), *moe_w)
        n_keep = min(WIN_MAX, k.shape[1])
        kw_p.append(k[:, k.shape[1] - n_keep:])
        vw_p.append(v[:, v.shape[1] - n_keep:])
        pool_p.append(u[:, u.shape[1] - POOL_STATE:])
        mk_p.append(mk)
        mv_p.append(mv)
        q, k, v, u = _split_mix_proj(_rmsnorm(xs, g_mix[l]), w_in[l])
        k_all = jnp.concatenate([cache_k_win[l].astype(k.dtype), k], axis=1)
        v_all = jnp.concatenate([cache_v_win[l].astype(v.dtype), v], axis=1)
        o_a = _merge_branches([_dilated_branch_sample(q, k_all, v_all, w, d, rel_bias) for (w, d) in DILATED], xs.dtype)
        u_all = jnp.concatenate([state_pool[l].astype(u.dtype), u], axis=1)
        o_b = _pool_mix(u_all, POOL_STATE, w_pool[l], pool_scale[l])
        xs = xs + _mix_out(o_a, o_b, w_out[l])
        xs = xs + _cross_attn(_rmsnorm(xs, g_x[l]), cache_mem_k[l], cache_mem_v[l], w_xq[l], w_xo[l])
        xs = xs + _moe(_rmsnorm(xs, g_ff[l]), *moe_w)
        n_new = k.shape[1]
        kw_s.append(k_all[:, n_new:])
        vw_s.append(v_all[:, n_new:])
        pool_s.append(u_all[:, n_new:])
    y_prompt = _rmsnorm(xp, g_final)
    y_sample = _rmsnorm(xs, g_final)
    k_win_prompt = jnp.stack(kw_p)
    v_win_prompt = jnp.stack(vw_p)
    pool_prompt = jnp.stack(pool_p)
    mem_k_prompt = jnp.stack(mk_p)
    mem_v_prompt = jnp.stack(mv_p)
    k_win_sample = jnp.stack(kw_s)
    v_win_sample = jnp.stack(vw_s)
    pool_sample = jnp.stack(pool_s)
    return (y_prompt, y_sample, k_win_prompt, v_win_prompt, pool_prompt, mem_k_prompt, mem_v_prompt, k_win_sample, v_win_sample, pool_sample)
```

```python
import functools
import math

import numpy as np
import jax
import jax.numpy as jnp
from jax import lax
from jax.experimental import pallas as pl
from jax.experimental.pallas import tpu as pltpu

F32 = jnp.float32
BF16 = jnp.bfloat16

LANES = 128
SUBLANES = 8
VMEM_LIMIT_BYTES = 56 * 1024 * 1024

A_HEADS = 8
A_HEAD_DIM = 64
A_WIDTH = A_HEADS * A_HEAD_DIM
DILATED = ((128, 1), (512, 4), (2048, 16))
QB = 128
POOL_WINDOWS = (2, 4, 8, 16)
POOL_HIST = 16
X_HEADS = 4
N_EXPERTS = 32
TOP_K = 4
SWIGLU_LIMIT = 7.0
SWIGLU_ALPHA = 1.702
N_BUCKETS = 32
RMS_EPS = 1e-6
NEG = -1e30

MOE_BM = 256
COMBINE_TQ = 128


def _cparams(sem):
    return pltpu.CompilerParams(dimension_semantics=sem, vmem_limit_bytes=VMEM_LIMIT_BYTES)


def _rms(x, g):
    return x * lax.rsqrt(jnp.mean(x * x, axis=-1, keepdims=True) + RMS_EPS) * g


def _norm_proj_kernel(x_ref, g_ref, w_ref, *o_refs, widths, scales):
    hb = _rms(x_ref[...], g_ref[...]).astype(BF16)
    off = 0
    for o_ref, width, scale in zip(o_refs, widths, scales):
        p = jnp.dot(hb, w_ref[:, off:off + width], preferred_element_type=F32)
        o_ref[...] = p if scale == 1.0 else p * scale
        off += width


def _norm_proj(x, g, w_bf16, widths, scales, tm):
    t, d = x.shape
    n = w_bf16.shape[1]
    assert sum(widths) == n and t % tm == 0
    return pl.pallas_call(
        functools.partial(_norm_proj_kernel, widths=widths, scales=scales),
        grid=(t // tm,),
        in_specs=[
            pl.BlockSpec((tm, d), lambda i: (i, 0)),
            pl.BlockSpec((1, d), lambda i: (0, 0)),
            pl.BlockSpec((d, n), lambda i: (0, 0)),
        ],
        out_specs=[pl.BlockSpec((tm, wd), lambda i: (i, 0)) for wd in widths],
        out_shape=[jax.ShapeDtypeStruct((t, wd), F32) for wd in widths],
        compiler_params=_cparams(("arbitrary",)),
        name="norm_proj",
    )(x, g.reshape(1, d), w_bf16)


def _t5_bucket_np(n, max_dist):
    max_exact = N_BUCKETS // 2
    nf = np.maximum(n, 1).astype(np.float32)
    large = max_exact + (
        np.log(nf / np.float32(max_exact)) / np.float32(math.log(max_dist / max_exact))
        * np.float32(N_BUCKETS - max_exact)
    ).astype(np.int32)
    return np.where(n < max_exact, n, np.minimum(large, N_BUCKETS - 1))


def _band_bias(rel_bias):
    max_dist = max(w for w, _ in DILATED)
    qi = np.arange(QB)[:, None]
    ki = np.arange(2 * QB)[None, :]
    j = qi + QB - ki
    tabs = []
    for window, dil in DILATED:
        steps = window // dil
        in_band = (j >= 0) & (j <= steps)
        bucket = _t5_bucket_np(np.clip(j, 0, steps) * dil, max_dist)
        b = jnp.transpose(rel_bias[bucket], (2, 0, 1)).astype(F32)
        tabs.append(jnp.where(in_band[None], b, NEG))
    return jnp.stack(tabs)


def _step_bias(rel_bias):
    max_dist = max(w for w, _ in DILATED)
    tabs = []
    for window, dil in DILATED:
        steps = window // dil
        bucket = _t5_bucket_np(np.arange(steps, -1, -1) * dil, max_dist)
        tabs.append(rel_bias[bucket].astype(F32)[:, :, None])
    return jnp.stack(tabs)


def _dil_attn_kernel(q_ref, k_ref, v_ref, bias_ref, o_ref, obr_ref, lbr_ref, *, seq):
    lane = lax.broadcasted_iota(jnp.int32, (QB, LANES), 1)
    head0 = lane < A_HEAD_DIM

    def rows(ref, start, n, dil):
        if dil == 1:
            return ref[pl.ds(start, n), :]
        return ref[pl.ds(start, n, stride=dil), :]

    def block(br, dil, qstart, kstart, nk):
        qs = rows(q_ref, qstart, QB, dil)
        ks = rows(k_ref, kstart, nk, dil).astype(BF16)
        vs = rows(v_ref, kstart, nk, dil).astype(BF16)
        outs, lses = [], []
        for hh in range(2):
            keep = head0 if hh == 0 else jnp.logical_not(head0)
            qm = jnp.where(keep, qs, 0.0).astype(BF16)
            logits = lax.dot_general(qm, ks, (((1,), (1,)), ((), ())), preferred_element_type=F32)
            logits = logits + bias_ref[br, hh, :, 2 * QB - nk:]
            m = jnp.max(logits, axis=-1, keepdims=True)
            p = jnp.exp(logits - m)
            s = jnp.sum(p, axis=-1, keepdims=True)
            o = jnp.dot(p.astype(BF16), vs, preferred_element_type=F32)
            outs.append(o / s)
            lses.append(jnp.broadcast_to(m + jnp.log(s), (QB, LANES)))
        o = jnp.where(head0, outs[0], outs[1])
        lse = jnp.where(head0, lses[0], lses[1])
        if dil == 1:
            obr_ref[br, pl.ds(qstart, QB), :] = o
            lbr_ref[br, pl.ds(qstart, QB), :] = lse
        else:
            obr_ref[br, pl.ds(qstart, QB, stride=dil), :] = o
            lbr_ref[br, pl.ds(qstart, QB, stride=dil), :] = lse

    for br, (window, dil) in enumerate(DILATED):
        assert window // dil == QB
        nblk = seq // (dil * QB)
        for r in range(dil):
            block(br, dil, r, r, QB)
            if nblk > 1:
                def body(i, carry, br=br, dil=dil, r=r):
                    qstart = r + dil * QB * i
                    if dil == 1:
                        qstart = pl.multiple_of(qstart, QB)
                    block(br, dil, qstart, qstart - dil * QB, 2 * QB)
                    return carry
                lax.fori_loop(1, nblk, body, 0)

    l0, l1, l2 = lbr_ref[0], lbr_ref[1], lbr_ref[2]
    m = jnp.maximum(jnp.maximum(l0, l1), l2)
    e0, e1, e2 = jnp.exp(l0 - m), jnp.exp(l1 - m), jnp.exp(l2 - m)
    acc = e0 * obr_ref[0] + e1 * obr_ref[1] + e2 * obr_ref[2]
    o_ref[...] = acc / (e0 + e1 + e2)


def _dil_attn(q, k, v, band_bias, batch, seq):
    t, aw = q.shape
    npair = aw // LANES
    bias = band_bias.reshape(len(DILATED), npair, 2, QB, 2 * QB)
    spec = pl.BlockSpec((seq, LANES), lambda b, hp: (b, hp))
    return pl.pallas_call(
        functools.partial(_dil_attn_kernel, seq=seq),
        grid=(batch, npair),
        in_specs=[
            spec, spec, spec,
            pl.BlockSpec((len(DILATED), None, 2, QB, 2 * QB), lambda b, hp: (0, hp, 0, 0, 0)),
        ],
        out_specs=spec,
        out_shape=jax.ShapeDtypeStruct((t, aw), F32),
        scratch_shapes=[
            pltpu.VMEM((len(DILATED), seq, LANES), F32),
            pltpu.VMEM((len(DILATED), seq, LANES), F32),
        ],
        compiler_params=_cparams(("arbitrary", "arbitrary")),
        name="dil_attn",
    )(q, k, v, bias)


def _dil_attn_dec_kernel(q_ref, kn_ref, vn_ref, k1_ref, v1_ref, k2_ref, v2_ref, k3_ref, v3_ref,
                         rbias_ref, o_ref, *, n_new):
    kn = kn_ref[0]
    vn = vn_ref[0]
    for t in range(n_new):
        q = q_ref[0, t][None]
        outs, lses = [], []
        for br, (window, dil) in enumerate(DILATED):
            if dil == 1:
                kc, vc = k1_ref[0, t:], v1_ref[0, t:]
                bc = rbias_ref[br, 0:QB - t]
                kn_t, vn_t = kn[:t + 1], vn[:t + 1]
                bn = rbias_ref[br, QB - t:QB + 1]
            else:
                kref, vref = (k2_ref, v2_ref) if br == 1 else (k3_ref, v3_ref)
                kc, vc = kref[0, :, t], vref[0, :, t]
                bc = rbias_ref[br, 0:QB]
                kn_t, vn_t = kn[t:t + 1], vn[t:t + 1]
                bn = rbias_ref[br, QB:QB + 1]
            lc = jnp.sum(kc * q, axis=-1, keepdims=True) + bc
            ln = jnp.sum(kn_t * q, axis=-1, keepdims=True) + bn
            m = jnp.maximum(jnp.max(lc, axis=0, keepdims=True), jnp.max(ln, axis=0, keepdims=True))
            pc = jnp.exp(lc - m)
            pn = jnp.exp(ln - m)
            s = jnp.sum(pc, axis=0, keepdims=True) + jnp.sum(pn, axis=0, keepdims=True)
            o = jnp.sum(pc * vc, axis=0, keepdims=True) + jnp.sum(pn * vn_t, axis=0, keepdims=True)
            outs.append(o / s)
            lses.append(m + jnp.log(s))
        m = jnp.maximum(jnp.maximum(lses[0], lses[1]), lses[2])
        es = [jnp.exp(l - m) for l in lses]
        acc = es[0] * outs[0] + es[1] * outs[1] + es[2] * outs[2]
        o_ref[0, t] = (acc / (es[0] + es[1] + es[2]))[0]


def _dil_attn_dec(q, k_new, v_new, cache_k, cache_v, step_bias):
    bd, n_new, h, dh = q.shape
    n_hist = cache_k.shape[1]
    (w1, d1), (w2, d2), (w3, d3) = DILATED
    assert d1 == 1 and n_hist >= w3 and n_new <= d2 and n_hist % d3 == 0 and n_hist % d2 == 0
    assert w1 // d1 == QB and w2 // d2 == QB and w3 // d3 == QB

    def views(c):
        c2 = c.reshape(bd, n_hist // d2, d2, h, dh)
        c3 = c.reshape(bd, n_hist // d3, d3, h, dh)
        return c, c2, c3

    k1, k2, k3 = views(cache_k)
    v1, v2, v3 = views(cache_v)
    new_spec = pl.BlockSpec((1, n_new, h, dh), lambda b: (b, 0, 0, 0))
    s1 = pl.BlockSpec((1, QB, h, dh), lambda b: (b, n_hist // QB - 1, 0, 0))
    s2 = pl.BlockSpec((1, QB, n_new, h, dh), lambda b: (b, n_hist // d2 // QB - 1, 0, 0, 0))
    s3 = pl.BlockSpec((1, QB, n_new, h, dh), lambda b: (b, n_hist // d3 // QB - 1, 0, 0, 0))
    return pl.pallas_call(
        functools.partial(_dil_attn_dec_kernel, n_new=n_new),
        grid=(bd,),
        in_specs=[new_spec, new_spec, new_spec, s1, s1, s2, s2, s3, s3,
                  pl.BlockSpec(step_bias.shape, lambda b: (0, 0, 0, 0))],
        out_specs=new_spec,
        out_shape=jax.ShapeDtypeStruct(q.shape, F32),
        compiler_params=_cparams(("arbitrary",)),
        name="dil_attn_dec",
    )(q, k_new, v_new, k1, v1, k2, v2, k3, v3, step_bias)


def _pool_kernel(hist_ref, cur_ref, w_ref, scale_ref, o_ref, *, pos0_of_tile, tm):
    nb = cur_ref.shape[0]
    gdim = w_ref.shape[1]
    i = pl.program_id(1)
    pos0 = pos0_of_tile(i)
    have_hist = jnp.where(pos0 > 0, 1.0, 0.0).astype(F32)
    t = lax.broadcasted_iota(jnp.int32, (1, tm, 1), 1)
    for g, w in enumerate(POOL_WINDOWS):
        sl = slice(g * gdim, (g + 1) * gdim)
        cur = cur_ref[:, :, sl]
        ext = jnp.concatenate([hist_ref[:, :, sl] * have_hist, cur], axis=1)
        acc, span = ext, 1
        while span < w:
            n = acc.shape[1]
            acc = acc[:, span:n] + acc[:, 0:n - span]
            span *= 2
        wsum = acc[:, POOL_HIST + 1 - w:POOL_HIST + 1 - w + tm]
        cnt = jnp.minimum(pos0 + t + 1, w).astype(F32)
        d = (wsum / cnt - cur).astype(BF16).reshape(nb * tm, gdim)
        y = jnp.dot(d, w_ref[g], preferred_element_type=F32) * scale_ref[:, sl]
        o_ref[:, :, sl] = y.reshape(nb, tm, gdim)


def _pool_prompt(u, w_pool_bf16, pool_scale, batch, seq, tm):
    t, bw = u.shape
    nt = seq // tm
    hb = tm // POOL_HIST
    u3 = u.reshape(1, t, bw)
    out = pl.pallas_call(
        functools.partial(_pool_kernel, pos0_of_tile=lambda i: i * tm, tm=tm),
        grid=(batch, nt),
        in_specs=[
            pl.BlockSpec((1, POOL_HIST, bw), lambda b, i: (0, jnp.maximum((b * nt + i) * hb - 1, 0), 0)),
            pl.BlockSpec((1, tm, bw), lambda b, i: (0, b * nt + i, 0)),
            pl.BlockSpec(w_pool_bf16.shape, lambda b, i: (0, 0, 0)),
            pl.BlockSpec((1, bw), lambda b, i: (0, 0)),
        ],
        out_specs=pl.BlockSpec((1, tm, bw), lambda b, i: (0, b * nt + i, 0)),
        out_shape=jax.ShapeDtypeStruct((1, t, bw), F32),
        compiler_params=_cparams(("arbitrary", "arbitrary")),
        name="pool_prompt",
    )(u3, u3, w_pool_bf16, pool_scale.reshape(1, bw))
    return out.reshape(t, bw)


def _pool_dec(hist, cur, n_prev, w_pool_bf16, pool_scale):
    bd, tm, bw = cur.shape
    return pl.pallas_call(
        functools.partial(_pool_kernel, pos0_of_tile=lambda i: n_prev, tm=tm),
        grid=(1, 1),
        in_specs=[
            pl.BlockSpec(hist.shape, lambda b, i: (0, 0, 0)),
            pl.BlockSpec(cur.shape, lambda b, i: (0, 0, 0)),
            pl.BlockSpec(w_pool_bf16.shape, lambda b, i: (0, 0, 0)),
            pl.BlockSpec((1, bw), lambda b, i: (0, 0)),
        ],
        out_specs=pl.BlockSpec(cur.shape, lambda b, i: (0, 0, 0)),
        out_shape=jax.ShapeDtypeStruct(cur.shape, F32),
        compiler_params=_cparams(("arbitrary", "arbitrary")),
        name="pool_dec",
    )(hist, cur, w_pool_bf16, pool_scale.reshape(1, bw))


def _mix_xattn_kernel(x_ref, oa_ref, ob_ref, mk_ref, mv_ref, woa_ref, wob_ref, gx_ref, wq_ref, wo_ref,
                      gff_ref, wr_ref, br_ref, x2_ref, h_ref, lg_ref, *, rows_per_sub, keys_per_sub):
    tm = x_ref.shape[0]
    nkv = mk_ref.shape[0]
    xh = wq_ref.shape[1] // X_HEADS
    x = x_ref[...]
    x = x + jnp.dot(oa_ref[...].astype(BF16), woa_ref[...], preferred_element_type=F32)
    x = x + jnp.dot(ob_ref[...].astype(BF16), wob_ref[...], preferred_element_type=F32)
    hq = _rms(x, gx_ref[...]).astype(BF16)
    if rows_per_sub is not None:
        rsub = (pl.program_id(1) * tm + lax.broadcasted_iota(jnp.int32, (tm, nkv), 0)) // rows_per_sub
        ksub = lax.broadcasted_iota(jnp.int32, (tm, nkv), 1) // keys_per_sub
        same = rsub == ksub
    heads = []
    for h in range(X_HEADS):
        sl = slice(h * xh, (h + 1) * xh)
        q = jnp.dot(hq, wq_ref[:, sl], preferred_element_type=F32) * (xh ** -0.5)
        kh = mk_ref[:, sl].astype(BF16)
        logits = lax.dot_general(q.astype(BF16), kh, (((1,), (1,)), ((), ())), preferred_element_type=F32)
        if rows_per_sub is not None:
            logits = jnp.where(same, logits, NEG)
        m = jnp.max(logits, axis=-1, keepdims=True)
        p = jnp.exp(logits - m)
        s = jnp.sum(p, axis=-1, keepdims=True)
        o = jnp.dot(p.astype(BF16), mv_ref[:, sl].astype(BF16), preferred_element_type=F32) / s
        heads.append(o.astype(BF16))
    x = x + jnp.dot(jnp.concatenate(heads, axis=-1), wo_ref[...], preferred_element_type=F32)
    x2_ref[...] = x
    hf = _rms(x, gff_ref[...])
    h_ref[...] = hf
    lg_ref[...] = jnp.dot(hf.astype(BF16), wr_ref[...].astype(BF16), preferred_element_type=F32) + br_ref[...]


def _mix_xattn(x, oa, ob, mk, mv, w, groups, rows_per_group, tm, rows_per_sub, keys_per_sub):
    t, d = x.shape
    nt = rows_per_group // tm
    nkv = mk.shape[0] // groups
    ne = w["w_router"].shape[1]
    row = lambda width: pl.BlockSpec((tm, width), lambda g, i: (g * nt + i, 0))
    const = lambda a: pl.BlockSpec(a.shape, lambda g, i: (0,) * a.ndim)
    kv = pl.BlockSpec((nkv, mk.shape[1]), lambda g, i: (g, 0))
    consts = [w["w_out_a"], w["w_out_b"], w["g_x"], w["w_xq"], w["w_xo"], w["g_ff"], w["w_router"], w["b_router"]]
    return pl.pallas_call(
        functools.partial(_mix_xattn_kernel, rows_per_sub=rows_per_sub, keys_per_sub=keys_per_sub),
        grid=(groups, nt),
        in_specs=[row(d), row(oa.shape[1]), row(ob.shape[1]), kv, kv] + [const(a) for a in consts],
        out_specs=[row(d), row(d), row(ne)],
        out_shape=[jax.ShapeDtypeStruct((t, d), F32), jax.ShapeDtypeStruct((t, d), F32),
                   jax.ShapeDtypeStruct((t, ne), F32)],
        compiler_params=_cparams(("arbitrary", "arbitrary")),
        name="mix_xattn",
    )(x, oa, ob, mk, mv, *consts)


def _moe_gather(h_hbm, tok_ref, buf, sem, slot):
    def body(i, carry):
        pltpu.make_async_copy(h_hbm.at[pl.ds(tok_ref[0, 0, i], 1)], buf.at[slot, pl.ds(i, 1)],
                              sem.at[slot]).start()
        return carry
    lax.fori_loop(0, MOE_BM, body, 0)


def _moe_ffn_kernel(be_ref, nu_ref, tok_ref, tokn_ref, h_hbm, wgu_ref, bgu_ref, wd_ref, bd_ref, o_ref,
                    buf, sem, wgu_bf, wd_bf):
    b = pl.program_id(0)
    n_used = nu_ref[0]
    slot = b % 2
    dff = wd_ref.shape[1]

    @pl.when(b == 0)
    def _():
        _moe_gather(h_hbm, tok_ref, buf, sem, 0)

    @pl.when(b + 1 < n_used)
    def _():
        _moe_gather(h_hbm, tokn_ref, buf, sem, 1 - slot)

    new_expert = jnp.logical_or(b == 0, be_ref[b] != be_ref[jnp.maximum(b - 1, 0)])

    @pl.when(jnp.logical_and(b < n_used, new_expert))
    def _():
        wgu_bf[...] = wgu_ref[0].astype(BF16)
        wd_bf[...] = wd_ref[0].astype(BF16)

    @pl.when(b < n_used)
    def _():
        pltpu.make_async_copy(h_hbm.at[pl.ds(0, MOE_BM)], buf.at[slot], sem.at[slot]).wait()
        xb = buf[slot].astype(BF16)
        hu = jnp.dot(xb, wgu_bf[...], preferred_element_type=F32) + bgu_ref[0]
        g = jnp.minimum(hu[:, :dff], SWIGLU_LIMIT)
        u = jnp.clip(hu[:, dff:], -SWIGLU_LIMIT, SWIGLU_LIMIT)
        a = g * jax.nn.sigmoid(SWIGLU_ALPHA * g) * (u + 1.0)
        o_ref[...] = jnp.dot(a.astype(BF16), wd_bf[...], preferred_element_type=F32) + bd_ref[0]

    @pl.when(b >= n_used)
    def _():
        o_ref[...] = jnp.zeros_like(o_ref)


def _moe_ffn(h, slot_tok, block_e, n_used, w_gate_up, b_gate_up, w_down, b_down):
    t, d = h.shape
    ne, _, dff2 = w_gate_up.shape
    dff = dff2 // 2
    nb = block_e.shape[0]
    tok3 = slot_tok.reshape(nb, 1, MOE_BM)
    grid_spec = pltpu.PrefetchScalarGridSpec(
        num_scalar_prefetch=2,
        grid=(nb,),
        in_specs=[
            pl.BlockSpec((1, 1, MOE_BM), lambda b, be, nu: (b, 0, 0), memory_space=pltpu.SMEM),
            pl.BlockSpec((1, 1, MOE_BM), lambda b, be, nu: (jnp.minimum(b + 1, nb - 1), 0, 0),
                         memory_space=pltpu.SMEM),
            pl.BlockSpec(memory_space=pl.ANY),
            pl.BlockSpec((1, d, dff2), lambda b, be, nu: (be[b], 0, 0)),
            pl.BlockSpec((1, 1, dff2), lambda b, be, nu: (be[b], 0, 0)),
            pl.BlockSpec((1, dff, d), lambda b, be, nu: (be[b], 0, 0)),
            pl.BlockSpec((1, 1, d), lambda b, be, nu: (be[b], 0, 0)),
        ],
        out_specs=pl.BlockSpec((MOE_BM, d), lambda b, be, nu: (b, 0)),
        scratch_shapes=[
            pltpu.VMEM((2, MOE_BM, d), F32),
            pltpu.SemaphoreType.DMA((2,)),
            pltpu.VMEM((d, dff2), BF16),
            pltpu.VMEM((dff, d), BF16),
        ],
    )
    return pl.pallas_call(
        _moe_ffn_kernel,
        grid_spec=grid_spec,
        out_shape=jax.ShapeDtypeStruct((nb * MOE_BM, d), F32),
        compiler_params=_cparams(("arbitrary",)),
        name="moe_ffn",
    )(block_e, n_used, tok3, tok3, h, w_gate_up, b_gate_up.reshape(ne, 1, dff2), w_down,
      b_down.reshape(ne, 1, d))


def _combine_gather(y_hbm, pos_ref, buf, sem, slot):
    def body(i, carry):
        pltpu.make_async_copy(y_hbm.at[pl.ds(pos_ref[0, 0, i], 1)], buf.at[slot, pl.ds(i, 1)],
                              sem.at[slot]).start()
        return carry
    lax.fori_loop(0, TOP_K * COMBINE_TQ, body, 0)


def _combine_kernel(pos_ref, posn_ref, y_hbm, x_ref, gate_ref, g_ref, o_ref, buf, sem, *, n_tiles):
    i = pl.program_id(0)
    slot = i % 2

    @pl.when(i == 0)
    def _():
        _combine_gather(y_hbm, pos_ref, buf, sem, 0)

    if n_tiles > 1:
        @pl.when(i + 1 < n_tiles)
        def _():
            _combine_gather(y_hbm, posn_ref, buf, sem, 1 - slot)

    pltpu.make_async_copy(y_hbm.at[pl.ds(0, TOP_K * COMBINE_TQ)], buf.at[slot], sem.at[slot]).wait()
    x = x_ref[...]
    gates = gate_ref[...]
    for k in range(TOP_K):
        x = x + gates[:, k:k + 1] * buf[slot, k * COMBINE_TQ:(k + 1) * COMBINE_TQ, :]
    o_ref[...] = _rms(x, g_ref[...])


def _combine(y_sorted, x2, pos, gates, g_final):
    t, d = x2.shape
    tq = COMBINE_TQ
    nt = t // tq
    pos3 = pos.reshape(nt, tq, TOP_K).transpose(0, 2, 1).reshape(nt, 1, TOP_K * tq)
    return pl.pallas_call(
        functools.partial(_combine_kernel, n_tiles=nt),
        grid=(nt,),
        in_specs=[
            pl.BlockSpec((1, 1, TOP_K * tq), lambda i: (i, 0, 0), memory_space=pltpu.SMEM),
            pl.BlockSpec((1, 1, TOP_K * tq), lambda i: (jnp.minimum(i + 1, nt - 1), 0, 0),
                         memory_space=pltpu.SMEM),
            pl.BlockSpec(memory_space=pl.ANY),
            pl.BlockSpec((tq, d), lambda i: (i, 0)),
            pl.BlockSpec((tq, TOP_K), lambda i: (i, 0)),
            pl.BlockSpec((1, d), lambda i: (0, 0)),
        ],
        out_specs=pl.BlockSpec((tq, d), lambda i: (i, 0)),
        out_shape=jax.ShapeDtypeStruct((t, d), F32),
        scratch_shapes=[pltpu.VMEM((2, TOP_K * tq, d), F32), pltpu.SemaphoreType.DMA((2,))],
        compiler_params=_cparams(("arbitrary",)),
        name="combine",
    )(pos3, pos3, y_sorted, x2, gates, g_final.reshape(1, d))


def _route(logits):
    t = logits.shape[0]
    top_v, top_i = lax.top_k(logits, TOP_K)
    gates = jax.nn.softmax(top_v, axis=-1)
    onehot = (top_i[:, :, None] == jnp.arange(N_EXPERTS, dtype=top_i.dtype)[None, None, :])
    per_tok = jnp.sum(onehot, axis=1).astype(jnp.int32)
    before = jnp.cumsum(per_tok, axis=0) - per_tok
    counts = jnp.sum(per_tok, axis=0)
    nblk = (counts + MOE_BM - 1) // MOE_BM
    blk_end = jnp.cumsum(nblk)
    pstart = (blk_end - nblk) * MOE_BM
    rank = jnp.take_along_axis(before, top_i, axis=1)
    pos = (pstart[top_i] + rank).astype(jnp.int32)
    nb = (t * TOP_K + N_EXPERTS * (MOE_BM - 1) + MOE_BM - 1) // MOE_BM
    n_used = blk_end[-1].astype(jnp.int32)
    tok = jnp.broadcast_to(jnp.arange(t, dtype=jnp.int32)[:, None], (t, TOP_K))
    slot_tok = jnp.zeros((nb * MOE_BM,), jnp.int32).at[pos.reshape(-1)].set(tok.reshape(-1))
    blk = jnp.minimum(jnp.arange(nb, dtype=jnp.int32), n_used - 1)
    block_e = jnp.searchsorted(blk_end, blk, side="right").astype(jnp.int32)
    return gates, pos, slot_tok, block_e, n_used.reshape(1)


def kernel(x_prompt, x_sample, cache_k_win, cache_v_win, state_pool, cache_mem_k, cache_mem_v, mem_prompt,
           rel_bias, g_mix, w_in, w_pool, pool_scale, w_out, g_mem, g_x, w_xq, w_xk, w_xv, w_xo, g_ff,
           w_router, b_router, w_gate_up, b_gate_up, w_down, b_down, g_final):
    depth = g_mix.shape[0]
    assert depth == 1
    l = 0
    batch, seq, d = x_prompt.shape
    bd, n_new, _ = x_sample.shape
    n_mem = mem_prompt.shape[1]
    bw = d - A_WIDTH
    n_hist = cache_k_win.shape[2]
    n_pool = state_pool.shape[2]
    tp, ts = batch * seq, bd * n_new

    w_in_b = w_in[l].astype(BF16)
    w_pool_b = w_pool[l].astype(BF16)
    w_xkv_b = jnp.concatenate([w_xk[l], w_xv[l]], axis=1).astype(BF16)
    wts = {
        "w_out_a": w_out[l, :A_WIDTH].astype(BF16), "w_out_b": w_out[l, A_WIDTH:].astype(BF16),
        "g_x": g_x[l].reshape(1, d), "w_xq": w_xq[l].astype(BF16), "w_xo": w_xo[l].astype(BF16),
        "g_ff": g_ff[l].reshape(1, d), "w_router": w_router[l], "b_router": b_router[l].reshape(1, N_EXPERTS),
    }
    band_bias = _band_bias(rel_bias)
    step_bias = _step_bias(rel_bias)
    widths = (A_WIDTH, A_WIDTH, A_WIDTH, bw)
    scales = (A_HEAD_DIM ** -0.5, 1.0, 1.0, 1.0)

    xp = x_prompt.reshape(tp, d)
    q, k, v, u = _norm_proj(xp, g_mix[l], w_in_b, widths, scales, tm=512)
    o_a = _dil_attn(q, k, v, band_bias, batch, seq)
    o_b = _pool_prompt(u, w_pool_b, pool_scale[l], batch, seq, tm=512)
    mk, mv = _norm_proj(mem_prompt.reshape(batch * n_mem, d), g_mem[l], w_xkv_b, (d, d), (1.0, 1.0), tm=512)
    x2p, hp, lgp = _mix_xattn(xp, o_a, o_b, mk, mv, wts, groups=batch, rows_per_group=seq, tm=512,
                              rows_per_sub=None, keys_per_sub=None)

    xs = x_sample.reshape(ts, d)
    qs, ks, vs, us = _norm_proj(xs, g_mix[l], w_in_b, widths, scales, tm=ts)
    r4 = lambda a: a.reshape(bd, n_new, A_HEADS, A_HEAD_DIM)
    ck, cv = cache_k_win[l], cache_v_win[l]
    o_as = _dil_attn_dec(r4(qs), r4(ks), r4(vs), ck, cv, step_bias).reshape(ts, A_WIDTH)
    us3 = us.reshape(bd, n_new, bw)
    hist = jnp.concatenate([jnp.zeros((bd, POOL_HIST - n_pool, bw), F32), state_pool[l]], axis=1)
    cur = jnp.concatenate([us3, jnp.zeros((bd, SUBLANES - n_new, bw), F32)], axis=1)
    o_bs = _pool_dec(hist, cur, n_pool, w_pool_b, pool_scale[l])[:, :n_new].reshape(ts, bw)
    sub = 8
    x2s, hs, lgs = _mix_xattn(xs, o_as, o_bs, cache_mem_k[l].reshape(bd * n_mem, d),
                              cache_mem_v[l].reshape(bd * n_mem, d), wts, groups=bd // sub,
                              rows_per_group=sub * n_new, tm=sub * n_new, rows_per_sub=n_new, keys_per_sub=n_mem)

    h_all = jnp.concatenate([hp, hs], axis=0)
    gates, pos, slot_tok, block_e, n_used = _route(jnp.concatenate([lgp, lgs], axis=0))
    y_sorted = _moe_ffn(h_all, slot_tok, block_e, n_used, w_gate_up[l], b_gate_up[l], w_down[l], b_down[l])
    y_prompt = _combine(y_sorted, x2p, pos[:tp], gates[:tp], g_final).reshape(batch, seq, d)
    y_sample = _combine(y_sorted, x2s, pos[tp:], gates[tp:], g_final).reshape(bd, n_new, d)

    a5 = lambda a, b_: a.reshape(1, b_, -1, A_HEADS, A_HEAD_DIM)
    k_win_prompt, v_win_prompt = a5(k, batch), a5(v, batch)
    pool_prompt = u.reshape(batch, seq, bw)[:, seq - n_pool:][None]
    mem_k_prompt = mk.reshape(1, batch, n_mem, X_HEADS, d // X_HEADS)
    mem_v_prompt = mv.reshape(1, batch, n_mem, X_HEADS, d // X_HEADS)
    k_win_sample = jnp.concatenate([ck[:, n_new:], r4(ks)], axis=1)[None]
    v_win_sample = jnp.concatenate([cv[:, n_new:], r4(vs)], axis=1)[None]
    pool_sample = jnp.concatenate([state_pool[l][:, n_new:], us3], axis=1)[None]
    return (y_prompt, y_sample, k_win_prompt, v_win_prompt, pool_prompt, mem_k_prompt, mem_v_prompt,
            k_win_sample, v_win_sample, pool_sample)
```

```python
import functools
import math

import numpy as np
import jax
import jax.numpy as jnp
from jax import lax
from jax.experimental import pallas as pl
from jax.experimental.pallas import tpu as pltpu

F32 = jnp.float32
BF16 = jnp.bfloat16

LANES = 128
SUBLANES = 8
VMEM_LIMIT_BYTES = 56 * 1024 * 1024

A_HEADS = 8
A_HEAD_DIM = 64
A_WIDTH = A_HEADS * A_HEAD_DIM
DILATED = ((128, 1), (512, 4), (2048, 16))
QB = 128
POOL_WINDOWS = (2, 4, 8, 16)
POOL_HIST = 16
X_HEADS = 4
N_EXPERTS = 32
TOP_K = 4
SWIGLU_LIMIT = 7.0
SWIGLU_ALPHA = 1.702
N_BUCKETS = 32
RMS_EPS = 1e-6
NEG = -1e30

MOE_BM = 256
COMBINE_TQ = 128


def _cparams(sem):
    return pltpu.CompilerParams(dimension_semantics=sem, vmem_limit_bytes=VMEM_LIMIT_BYTES)


def _rms(x, g):
    return x * lax.rsqrt(jnp.mean(x * x, axis=-1, keepdims=True) + RMS_EPS) * g


def _norm_proj_kernel(x_ref, g_ref, w_ref, *o_refs, widths, scales):
    hb = _rms(x_ref[...], g_ref[...]).astype(BF16)
    off = 0
    for o_ref, width, scale in zip(o_refs, widths, scales):
        p = jnp.dot(hb, w_ref[:, off:off + width], preferred_element_type=F32)
        o_ref[...] = p if scale == 1.0 else p * scale
        off += width


def _norm_proj(x, g, w_bf16, widths, scales, tm):
    t, d = x.shape
    n = w_bf16.shape[1]
    assert sum(widths) == n and t % tm == 0
    return pl.pallas_call(
        functools.partial(_norm_proj_kernel, widths=widths, scales=scales),
        grid=(t // tm,),
        in_specs=[
            pl.BlockSpec((tm, d), lambda i: (i, 0)),
            pl.BlockSpec((1, d), lambda i: (0, 0)),
            pl.BlockSpec((d, n), lambda i: (0, 0)),
        ],
        out_specs=[pl.BlockSpec((tm, wd), lambda i: (i, 0)) for wd in widths],
        out_shape=[jax.ShapeDtypeStruct((t, wd), F32) for wd in widths],
        compiler_params=_cparams(("arbitrary",)),
        name="norm_proj",
    )(x, g.reshape(1, d), w_bf16)


def _t5_bucket_np(n, max_dist):
    max_exact = N_BUCKETS // 2
    nf = np.maximum(n, 1).astype(np.float32)
    large = max_exact + (
        np.log(nf / np.float32(max_exact)) / np.float32(math.log(max_dist / max_exact))
        * np.float32(N_BUCKETS - max_exact)
    ).astype(np.int32)
    return np.where(n < max_exact, n, np.minimum(large, N_BUCKETS - 1))


def _band_bias(rel_bias):
    max_dist = max(w for w, _ in DILATED)
    qi = np.arange(QB)[:, None]
    ki = np.arange(2 * QB)[None, :]
    j = qi + QB - ki
    tabs = []
    for window, dil in DILATED:
        steps = window // dil
        in_band = (j >= 0) & (j <= steps)
        bucket = _t5_bucket_np(np.clip(j, 0, steps) * dil, max_dist)
        onehot = (bucket[..., None] == np.arange(N_BUCKETS)).astype(np.float32)
        b = jnp.einsum("qkb,bh->hqk", onehot, rel_bias.astype(F32), precision=lax.Precision.HIGHEST)
        tabs.append(jnp.where(in_band[None], b, NEG))
    return jnp.stack(tabs)


def _step_bias(rel_bias):
    max_dist = max(w for w, _ in DILATED)
    tabs = []
    for window, dil in DILATED:
        steps = window // dil
        bucket = _t5_bucket_np(np.arange(steps, -1, -1) * dil, max_dist)
        tabs.append(rel_bias[bucket].astype(F32)[:, :, None])
    return jnp.stack(tabs)


def _dil_attn_kernel(q_ref, k_ref, v_ref, bias_ref, o_ref, obr_ref, lbr_ref, *, seq):
    lane = lax.broadcasted_iota(jnp.int32, (QB, LANES), 1)
    head0 = lane < A_HEAD_DIM

    def rows(ref, start, n, dil):
        if dil == 1:
            return ref[pl.ds(start, n), :]
        return ref[pl.ds(start, n, stride=dil), :]

    def block(br, dil, qstart, kstart, nk):
        qs = rows(q_ref, qstart, QB, dil)
        ks = rows(k_ref, kstart, nk, dil).astype(BF16)
        vs = rows(v_ref, kstart, nk, dil).astype(BF16)
        outs, lses = [], []
        for hh in range(2):
            keep = head0 if hh == 0 else jnp.logical_not(head0)
            qm = jnp.where(keep, qs, 0.0).astype(BF16)
            logits = lax.dot_general(qm, ks, (((1,), (1,)), ((), ())), preferred_element_type=F32)
            logits = logits + bias_ref[br, hh, :, 2 * QB - nk:]
            m = jnp.max(logits, axis=-1, keepdims=True)
            p = jnp.exp(logits - m)
            s = jnp.sum(p, axis=-1, keepdims=True)
            o = jnp.dot(p.astype(BF16), vs, preferred_element_type=F32)
            outs.append(o / s)
            lses.append(jnp.broadcast_to(m + jnp.log(s), (QB, LANES)))
        o = jnp.where(head0, outs[0], outs[1])
        lse = jnp.where(head0, lses[0], lses[1])
        if dil == 1:
            obr_ref[br, pl.ds(qstart, QB), :] = o
            lbr_ref[br, pl.ds(qstart, QB), :] = lse
        else:
            obr_ref[br, pl.ds(qstart, QB, stride=dil), :] = o
            lbr_ref[br, pl.ds(qstart, QB, stride=dil), :] = lse

    for br, (window, dil) in enumerate(DILATED):
        assert window // dil == QB
        nblk = seq // (dil * QB)
        for r in range(dil):
            block(br, dil, r, r, QB)
            if nblk > 1:
                def body(i, carry, br=br, dil=dil, r=r):
                    qstart = r + dil * QB * i
                    if dil == 1:
                        qstart = pl.multiple_of(qstart, QB)
                    block(br, dil, qstart, qstart - dil * QB, 2 * QB)
                    return carry
                lax.fori_loop(1, nblk, body, 0)

    l0, l1, l2 = lbr_ref[0], lbr_ref[1], lbr_ref[2]
    m = jnp.maximum(jnp.maximum(l0, l1), l2)
    e0, e1, e2 = jnp.exp(l0 - m), jnp.exp(l1 - m), jnp.exp(l2 - m)
    acc = e0 * obr_ref[0] + e1 * obr_ref[1] + e2 * obr_ref[2]
    o_ref[...] = acc / (e0 + e1 + e2)


def _dil_attn(q, k, v, band_bias, batch, seq):
    t, aw = q.shape
    npair = aw // LANES
    bias = band_bias.reshape(len(DILATED), npair, 2, QB, 2 * QB)
    spec = pl.BlockSpec((seq, LANES), lambda b, hp: (b, hp))
    return pl.pallas_call(
        functools.partial(_dil_attn_kernel, seq=seq),
        grid=(batch, npair),
        in_specs=[
            spec, spec, spec,
            pl.BlockSpec((len(DILATED), None, 2, QB, 2 * QB), lambda b, hp: (0, hp, 0, 0, 0)),
        ],
        out_specs=spec,
        out_shape=jax.ShapeDtypeStruct((t, aw), F32),
        scratch_shapes=[
            pltpu.VMEM((len(DILATED), seq, LANES), F32),
            pltpu.VMEM((len(DILATED), seq, LANES), F32),
        ],
        compiler_params=_cparams(("arbitrary", "arbitrary")),
        name="dil_attn",
    )(q, k, v, bias)


def _dil_attn_dec_kernel(q_ref, kn_ref, vn_ref, k1_ref, v1_ref, k2_ref, v2_ref, k3_ref, v3_ref,
                         rbias_ref, o_ref, *, n_new):
    kn = kn_ref[0]
    vn = vn_ref[0]
    for t in range(n_new):
        q = q_ref[0, t][None]
        outs, lses = [], []
        for br, (window, dil) in enumerate(DILATED):
            if dil == 1:
                kc, vc = k1_ref[0, t:], v1_ref[0, t:]
                bc = rbias_ref[br, 0:QB - t]
                kn_t, vn_t = kn[:t + 1], vn[:t + 1]
                bn = rbias_ref[br, QB - t:QB + 1]
            else:
                kref, vref = (k2_ref, v2_ref) if br == 1 else (k3_ref, v3_ref)
                kc, vc = kref[0, :, t], vref[0, :, t]
                bc = rbias_ref[br, 0:QB]
                kn_t, vn_t = kn[t:t + 1], vn[t:t + 1]
                bn = rbias_ref[br, QB:QB + 1]
            lc = jnp.sum(kc * q, axis=-1, keepdims=True) + bc
            ln = jnp.sum(kn_t * q, axis=-1, keepdims=True) + bn
            m = jnp.maximum(jnp.max(lc, axis=0, keepdims=True), jnp.max(ln, axis=0, keepdims=True))
            pc = jnp.exp(lc - m)
            pn = jnp.exp(ln - m)
            s = jnp.sum(pc, axis=0, keepdims=True) + jnp.sum(pn, axis=0, keepdims=True)
            o = jnp.sum(pc * vc, axis=0, keepdims=True) + jnp.sum(pn * vn_t, axis=0, keepdims=True)
            outs.append(o / s)
            lses.append(m + jnp.log(s))
        m = jnp.maximum(jnp.maximum(lses[0], lses[1]), lses[2])
        es = [jnp.exp(l - m) for l in lses]
        acc = es[0] * outs[0] + es[1] * outs[1] + es[2] * outs[2]
        o_ref[0, t] = (acc / (es[0] + es[1] + es[2]))[0]


def _dil_attn_dec(q, k_new, v_new, cache_k, cache_v, step_bias):
    bd, n_new, h, dh = q.shape
    n_hist = cache_k.shape[1]
    (w1, d1), (w2, d2), (w3, d3) = DILATED
    assert d1 == 1 and n_hist >= w3 and n_new <= d2 and n_hist % d3 == 0 and n_hist % d2 == 0
    assert w1 // d1 == QB and w2 // d2 == QB and w3 // d3 == QB

    def views(c):
        c2 = c.reshape(bd, n_hist // d2, d2, h, dh)
        c3 = c.reshape(bd, n_hist // d3, d3, h, dh)
        return c, c2, c3

    k1, k2, k3 = views(cache_k)
    v1, v2, v3 = views(cache_v)
    new_spec = pl.BlockSpec((1, n_new, h, dh), lambda b: (b, 0, 0, 0))
    s1 = pl.BlockSpec((1, QB, h, dh), lambda b: (b, n_hist // QB - 1, 0, 0))
    s2 = pl.BlockSpec((1, QB, n_new, h, dh), lambda b: (b, n_hist // d2 // QB - 1, 0, 0, 0))
    s3 = pl.BlockSpec((1, QB, n_new, h, dh), lambda b: (b, n_hist // d3 // QB - 1, 0, 0, 0))
    return pl.pallas_call(
        functools.partial(_dil_attn_dec_kernel, n_new=n_new),
        grid=(bd,),
        in_specs=[new_spec, new_spec, new_spec, s1, s1, s2, s2, s3, s3,
                  pl.BlockSpec(step_bias.shape, lambda b: (0, 0, 0, 0))],
        out_specs=new_spec,
        out_shape=jax.ShapeDtypeStruct(q.shape, F32),
        compiler_params=_cparams(("arbitrary",)),
        name="dil_attn_dec",
    )(q, k_new, v_new, k1, v1, k2, v2, k3, v3, step_bias)


def _pool_kernel(hist_ref, cur_ref, w_ref, scale_ref, o_ref, *, pos0_of_tile, tm):
    nb = cur_ref.shape[0]
    gdim = w_ref.shape[1]
    i = pl.program_id(1)
    pos0 = pos0_of_tile(i)
    have_hist = jnp.where(pos0 > 0, 1.0, 0.0).astype(F32)
    t = lax.broadcasted_iota(jnp.int32, (1, tm, 1), 1)
    for g, w in enumerate(POOL_WINDOWS):
        sl = slice(g * gdim, (g + 1) * gdim)
        cur = cur_ref[:, :, sl]
        ext = jnp.concatenate([hist_ref[:, :, sl] * have_hist, cur], axis=1)
        acc, span = ext, 1
        while span < w:
            n = acc.shape[1]
            acc = acc[:, span:n] + acc[:, 0:n - span]
            span *= 2
        wsum = acc[:, POOL_HIST + 1 - w:POOL_HIST + 1 - w + tm]
        cnt = jnp.minimum(pos0 + t + 1, w).astype(F32)
        d = (wsum / cnt - cur).astype(BF16).reshape(nb * tm, gdim)
        y = jnp.dot(d, w_ref[g], preferred_element_type=F32) * scale_ref[:, sl]
        o_ref[:, :, sl] = y.reshape(nb, tm, gdim)


def _pool_prompt(u, w_pool_bf16, pool_scale, batch, seq, tm):
    t, bw = u.shape
    nt = seq // tm
    hb = tm // POOL_HIST
    u3 = u.reshape(1, t, bw)
    out = pl.pallas_call(
        functools.partial(_pool_kernel, pos0_of_tile=lambda i: i * tm, tm=tm),
        grid=(batch, nt),
        in_specs=[
            pl.BlockSpec((1, POOL_HIST, bw), lambda b, i: (0, jnp.maximum((b * nt + i) * hb - 1, 0), 0)),
            pl.BlockSpec((1, tm, bw), lambda b, i: (0, b * nt + i, 0)),
            pl.BlockSpec(w_pool_bf16.shape, lambda b, i: (0, 0, 0)),
            pl.BlockSpec((1, bw), lambda b, i: (0, 0)),
        ],
        out_specs=pl.BlockSpec((1, tm, bw), lambda b, i: (0, b * nt + i, 0)),
        out_shape=jax.ShapeDtypeStruct((1, t, bw), F32),
        compiler_params=_cparams(("arbitrary", "arbitrary")),
        name="pool_prompt",
    )(u3, u3, w_pool_bf16, pool_scale.reshape(1, bw))
    return out.reshape(t, bw)


def _pool_dec(hist, cur, n_prev, w_pool_bf16, pool_scale):
    bd, tm, bw = cur.shape
    return pl.pallas_call(
        functools.partial(_pool_kernel, pos0_of_tile=lambda i: n_prev, tm=tm),
        grid=(1, 1),
        in_specs=[
            pl.BlockSpec(hist.shape, lambda b, i: (0, 0, 0)),
            pl.BlockSpec(cur.shape, lambda b, i: (0, 0, 0)),
            pl.BlockSpec(w_pool_bf16.shape, lambda b, i: (0, 0, 0)),
            pl.BlockSpec((1, bw), lambda b, i: (0, 0)),
        ],
        out_specs=pl.BlockSpec(cur.shape, lambda b, i: (0, 0, 0)),
        out_shape=jax.ShapeDtypeStruct(cur.shape, F32),
        compiler_params=_cparams(("arbitrary", "arbitrary")),
        name="pool_dec",
    )(hist, cur, w_pool_bf16, pool_scale.reshape(1, bw))


def _mix_xattn_kernel(x_ref, oa_ref, ob_ref, mk_ref, mv_ref, woa_ref, wob_ref, gx_ref, wq_ref, wo_ref,
                      gff_ref, wr_ref, br_ref, x2_ref, h_ref, lg_ref, *, rows_per_sub, keys_per_sub):
    tm = x_ref.shape[0]
    nkv = mk_ref.shape[0]
    xh = wq_ref.shape[1] // X_HEADS
    x = x_ref[...]
    x = x + jnp.dot(oa_ref[...].astype(BF16), woa_ref[...], preferred_element_type=F32)
    x = x + jnp.dot(ob_ref[...].astype(BF16), wob_ref[...], preferred_element_type=F32)
    hq = _rms(x, gx_ref[...]).astype(BF16)
    if rows_per_sub is not None:
        rsub = (pl.program_id(1) * tm + lax.broadcasted_iota(jnp.int32, (tm, nkv), 0)) // rows_per_sub
        ksub = lax.broadcasted_iota(jnp.int32, (tm, nkv), 1) // keys_per_sub
        same = rsub == ksub
    heads = []
    for h in range(X_HEADS):
        sl = slice(h * xh, (h + 1) * xh)
        q = jnp.dot(hq, wq_ref[:, sl], preferred_element_type=F32) * (xh ** -0.5)
        kh = mk_ref[:, sl].astype(BF16)
        logits = lax.dot_general(q.astype(BF16), kh, (((1,), (1,)), ((), ())), preferred_element_type=F32)
        if rows_per_sub is not None:
            logits = jnp.where(same, logits, NEG)
        m = jnp.max(logits, axis=-1, keepdims=True)
        p = jnp.exp(logits - m)
        s = jnp.sum(p, axis=-1, keepdims=True)
        o = jnp.dot(p.astype(BF16), mv_ref[:, sl].astype(BF16), preferred_element_type=F32) / s
        heads.append(o.astype(BF16))
    x = x + jnp.dot(jnp.concatenate(heads, axis=-1), wo_ref[...], preferred_element_type=F32)
    x2_ref[...] = x
    hf = _rms(x, gff_ref[...])
    h_ref[...] = hf
    lg_ref[...] = jnp.dot(hf.astype(BF16), wr_ref[...].astype(BF16), preferred_element_type=F32) + br_ref[...]


def _mix_xattn(x, oa, ob, mk, mv, w, groups, rows_per_group, tm, rows_per_sub, keys_per_sub):
    t, d = x.shape
    nt = rows_per_group // tm
    nkv = mk.shape[0] // groups
    ne = w["w_router"].shape[1]
    row = lambda width: pl.BlockSpec((tm, width), lambda g, i: (g * nt + i, 0))
    const = lambda a: pl.BlockSpec(a.shape, lambda g, i: (0,) * a.ndim)
    kv = pl.BlockSpec((nkv, mk.shape[1]), lambda g, i: (g, 0))
    consts = [w["w_out_a"], w["w_out_b"], w["g_x"], w["w_xq"], w["w_xo"], w["g_ff"], w["w_router"], w["b_router"]]
    return pl.pallas_call(
        functools.partial(_mix_xattn_kernel, rows_per_sub=rows_per_sub, keys_per_sub=keys_per_sub),
        grid=(groups, nt),
        in_specs=[row(d), row(oa.shape[1]), row(ob.shape[1]), kv, kv] + [const(a) for a in consts],
        out_specs=[row(d), row(d), row(ne)],
        out_shape=[jax.ShapeDtypeStruct((t, d), F32), jax.ShapeDtypeStruct((t, d), F32),
                   jax.ShapeDtypeStruct((t, ne), F32)],
        compiler_params=_cparams(("arbitrary", "arbitrary")),
        name="mix_xattn",
    )(x, oa, ob, mk, mv, *consts)


def _moe_gather(h_hbm, tok_ref, buf, sem, slot):
    def body(i, carry):
        pltpu.make_async_copy(h_hbm.at[pl.ds(tok_ref[0, 0, i], 1)], buf.at[slot, pl.ds(i, 1)],
                              sem.at[slot]).start()
        return carry
    lax.fori_loop(0, MOE_BM, body, 0)


def _moe_ffn_kernel(be_ref, nu_ref, tok_ref, tokn_ref, h_hbm, wgu_ref, bgu_ref, wd_ref, bd_ref, o_ref,
                    buf, sem, wgu_bf, wd_bf):
    b = pl.program_id(0)
    n_used = nu_ref[0]
    slot = b % 2
    dff = wd_ref.shape[1]

    @pl.when(b == 0)
    def _():
        _moe_gather(h_hbm, tok_ref, buf, sem, 0)

    @pl.when(b + 1 < n_used)
    def _():
        _moe_gather(h_hbm, tokn_ref, buf, sem, 1 - slot)

    new_expert = jnp.logical_or(b == 0, be_ref[b] != be_ref[jnp.maximum(b - 1, 0)])

    @pl.when(jnp.logical_and(b < n_used, new_expert))
    def _():
        wgu_bf[...] = wgu_ref[0].astype(BF16)
        wd_bf[...] = wd_ref[0].astype(BF16)

    @pl.when(b < n_used)
    def _():
        pltpu.make_async_copy(h_hbm.at[pl.ds(0, MOE_BM)], buf.at[slot], sem.at[slot]).wait()
        xb = buf[slot].astype(BF16)
        hu = jnp.dot(xb, wgu_bf[...], preferred_element_type=F32) + bgu_ref[0]
        g = jnp.minimum(hu[:, :dff], SWIGLU_LIMIT)
        u = jnp.clip(hu[:, dff:], -SWIGLU_LIMIT, SWIGLU_LIMIT)
        a = g * jax.nn.sigmoid(SWIGLU_ALPHA * g) * (u + 1.0)
        o_ref[...] = jnp.dot(a.astype(BF16), wd_bf[...], preferred_element_type=F32) + bd_ref[0]

    @pl.when(b >= n_used)
    def _():
        o_ref[...] = jnp.zeros_like(o_ref)


def _moe_ffn(h, slot_tok, block_e, n_used, w_gate_up, b_gate_up, w_down, b_down):
    t, d = h.shape
    ne, _, dff2 = w_gate_up.shape
    dff = dff2 // 2
    nb = block_e.shape[0]
    tok3 = slot_tok.reshape(nb, 1, MOE_BM)
    grid_spec = pltpu.PrefetchScalarGridSpec(
        num_scalar_prefetch=2,
        grid=(nb,),
        in_specs=[
            pl.BlockSpec((1, 1, MOE_BM), lambda b, be, nu: (b, 0, 0), memory_space=pltpu.SMEM),
            pl.BlockSpec((1, 1, MOE_BM), lambda b, be, nu: (jnp.minimum(b + 1, nb - 1), 0, 0),
                         memory_space=pltpu.SMEM),
            pl.BlockSpec(memory_space=pl.ANY),
            pl.BlockSpec((1, d, dff2), lambda b, be, nu: (be[b], 0, 0)),
            pl.BlockSpec((1, 1, dff2), lambda b, be, nu: (be[b], 0, 0)),
            pl.BlockSpec((1, dff, d), lambda b, be, nu: (be[b], 0, 0)),
            pl.BlockSpec((1, 1, d), lambda b, be, nu: (be[b], 0, 0)),
        ],
        out_specs=pl.BlockSpec((MOE_BM, d), lambda b, be, nu: (b, 0)),
        scratch_shapes=[
            pltpu.VMEM((2, MOE_BM, d), F32),
            pltpu.SemaphoreType.DMA((2,)),
            pltpu.VMEM((d, dff2), BF16),
            pltpu.VMEM((dff, d), BF16),
        ],
    )
    return pl.pallas_call(
        _moe_ffn_kernel,
        grid_spec=grid_spec,
        out_shape=jax.ShapeDtypeStruct((nb * MOE_BM, d), F32),
        compiler_params=_cparams(("arbitrary",)),
        name="moe_ffn",
    )(block_e, n_used, tok3, tok3, h, w_gate_up, b_gate_up.reshape(ne, 1, dff2), w_down,
      b_down.reshape(ne, 1, d))


def _combine_gather(y_hbm, pos_ref, buf, sem, slot):
    def body(i, carry):
        pltpu.make_async_copy(y_hbm.at[pl.ds(pos_ref[0, 0, i], 1)], buf.at[slot, pl.ds(i, 1)],
                              sem.at[slot]).start()
        return carry
    lax.fori_loop(0, TOP_K * COMBINE_TQ, body, 0)


def _combine_kernel(pos_ref, posn_ref, y_hbm, x_ref, gate_ref, g_ref, o_ref, buf, sem, *, n_tiles):
    i = pl.program_id(0)
    slot = i % 2

    @pl.when(i == 0)
    def _():
        _combine_gather(y_hbm, pos_ref, buf, sem, 0)

    if n_tiles > 1:
        @pl.when(i + 1 < n_tiles)
        def _():
            _combine_gather(y_hbm, posn_ref, buf, sem, 1 - slot)

    pltpu.make_async_copy(y_hbm.at[pl.ds(0, TOP_K * COMBINE_TQ)], buf.at[slot], sem.at[slot]).wait()
    x = x_ref[...]
    gates = gate_ref[...]
    for k in range(TOP_K):
        x = x + gates[:, k:k + 1] * buf[slot, k * COMBINE_TQ:(k + 1) * COMBINE_TQ, :]
    o_ref[...] = _rms(x, g_ref[...])


def _combine(y_sorted, x2, pos, gates, g_final):
    t, d = x2.shape
    tq = COMBINE_TQ
    nt = t // tq
    pos3 = pos.reshape(nt, tq, TOP_K).transpose(0, 2, 1).reshape(nt, 1, TOP_K * tq)
    return pl.pallas_call(
        functools.partial(_combine_kernel, n_tiles=nt),
        grid=(nt,),
        in_specs=[
            pl.BlockSpec((1, 1, TOP_K * tq), lambda i: (i, 0, 0), memory_space=pltpu.SMEM),
            pl.BlockSpec((1, 1, TOP_K * tq), lambda i: (jnp.minimum(i + 1, nt - 1), 0, 0),
                         memory_space=pltpu.SMEM),
            pl.BlockSpec(memory_space=pl.ANY),
            pl.BlockSpec((tq, d), lambda i: (i, 0)),
            pl.BlockSpec((tq, TOP_K), lambda i: (i, 0)),
            pl.BlockSpec((1, d), lambda i: (0, 0)),
        ],
        out_specs=pl.BlockSpec((tq, d), lambda i: (i, 0)),
        out_shape=jax.ShapeDtypeStruct((t, d), F32),
        scratch_shapes=[pltpu.VMEM((2, TOP_K * tq, d), F32), pltpu.SemaphoreType.DMA((2,))],
        compiler_params=_cparams(("arbitrary",)),
        name="combine",
    )(pos3, pos3, y_sorted, x2, gates, g_final.reshape(1, d))


def _cache_shift_kernel(ck_ref, cv_ref, kn_ref, vn_ref, ok_ref, ov_ref, sem, *, n_new):
    bd, n_hist = ck_ref.shape[0], ck_ref.shape[1]
    keep = n_hist - n_new
    copies = []
    for src, new, dst in ((ck_ref, kn_ref, ok_ref), (cv_ref, vn_ref, ov_ref)):
        for b in range(bd):
            copies.append(pltpu.make_async_copy(src.at[b, pl.ds(n_new, keep)], dst.at[b, pl.ds(0, keep)], sem.at[0]))
        copies.append(pltpu.make_async_copy(new, dst.at[:, pl.ds(keep, n_new)], sem.at[1]))
    for c in copies:
        c.start()
    for c in copies:
        c.wait()


def _cache_shift(cache_k, cache_v, k_new, v_new):
    n_new = k_new.shape[1]
    any_spec = pl.BlockSpec(memory_space=pl.ANY)
    return pl.pallas_call(
        functools.partial(_cache_shift_kernel, n_new=n_new),
        in_specs=[any_spec] * 4,
        out_specs=[any_spec] * 2,
        out_shape=[jax.ShapeDtypeStruct(cache_k.shape, cache_k.dtype)] * 2,
        scratch_shapes=[pltpu.SemaphoreType.DMA((2,))],
        name="cache_shift",
    )(cache_k, cache_v, k_new, v_new)


def _route(logits):
    t = logits.shape[0]
    top_v, top_i = lax.top_k(logits, TOP_K)
    gates = jax.nn.softmax(top_v, axis=-1)
    onehot = (top_i[:, :, None] == jnp.arange(N_EXPERTS, dtype=top_i.dtype)[None, None, :])
    per_tok = jnp.sum(onehot, axis=1).astype(jnp.int32)
    before = jnp.cumsum(per_tok, axis=0) - per_tok
    counts = jnp.sum(per_tok, axis=0)
    nblk = (counts + MOE_BM - 1) // MOE_BM
    blk_end = jnp.cumsum(nblk)
    pstart = (blk_end - nblk) * MOE_BM
    row_of = (pstart[None, :] + before)[:, None, :]
    pos = jnp.sum(jnp.where(onehot, row_of, 0), axis=-1).astype(jnp.int32)
    nb = (t * TOP_K + N_EXPERTS * (MOE_BM - 1) + MOE_BM - 1) // MOE_BM
    n_used = blk_end[-1].astype(jnp.int32)
    tok = jnp.broadcast_to(jnp.arange(t, dtype=jnp.int32)[:, None], (t, TOP_K))
    slot_tok = jnp.zeros((nb * MOE_BM,), jnp.int32).at[pos.reshape(-1)].set(tok.reshape(-1))
    blk = jnp.minimum(jnp.arange(nb, dtype=jnp.int32), n_used - 1)
    block_e = jnp.sum(blk_end[None, :] <= blk[:, None], axis=1).astype(jnp.int32)
    return gates, pos, slot_tok, block_e, n_used.reshape(1)


def kernel(x_prompt, x_sample, cache_k_win, cache_v_win, state_pool, cache_mem_k, cache_mem_v, mem_prompt,
           rel_bias, g_mix, w_in, w_pool, pool_scale, w_out, g_mem, g_x, w_xq, w_xk, w_xv, w_xo, g_ff,
           w_router, b_router, w_gate_up, b_gate_up, w_down, b_down, g_final):
    depth = g_mix.shape[0]
    assert depth == 1
    l = 0
    batch, seq, d = x_prompt.shape
    bd, n_new, _ = x_sample.shape
    n_mem = mem_prompt.shape[1]
    bw = d - A_WIDTH
    n_hist = cache_k_win.shape[2]
    n_pool = state_pool.shape[2]
    tp, ts = batch * seq, bd * n_new

    w_in_b = w_in[l].astype(BF16)
    w_pool_b = w_pool[l].astype(BF16)
    w_xkv_b = jnp.concatenate([w_xk[l], w_xv[l]], axis=1).astype(BF16)
    wts = {
        "w_out_a": w_out[l, :A_WIDTH].astype(BF16), "w_out_b": w_out[l, A_WIDTH:].astype(BF16),
        "g_x": g_x[l].reshape(1, d), "w_xq": w_xq[l].astype(BF16), "w_xo": w_xo[l].astype(BF16),
        "g_ff": g_ff[l].reshape(1, d), "w_router": w_router[l], "b_router": b_router[l].reshape(1, N_EXPERTS),
    }
    band_bias = _band_bias(rel_bias)
    step_bias = _step_bias(rel_bias)
    widths = (A_WIDTH, A_WIDTH, A_WIDTH, bw)
    scales = (A_HEAD_DIM ** -0.5, 1.0, 1.0, 1.0)

    xp = x_prompt.reshape(tp, d)
    q, k, v, u = _norm_proj(xp, g_mix[l], w_in_b, widths, scales, tm=512)
    o_a = _dil_attn(q, k, v, band_bias, batch, seq)
    o_b = _pool_prompt(u, w_pool_b, pool_scale[l], batch, seq, tm=512)
    mk, mv = _norm_proj(mem_prompt.reshape(batch * n_mem, d), g_mem[l], w_xkv_b, (d, d), (1.0, 1.0), tm=512)
    x2p, hp, lgp = _mix_xattn(xp, o_a, o_b, mk, mv, wts, groups=batch, rows_per_group=seq, tm=512,
                              rows_per_sub=None, keys_per_sub=None)

    xs = x_sample.reshape(ts, d)
    qs, ks, vs, us = _norm_proj(xs, g_mix[l], w_in_b, widths, scales, tm=ts)
    r4 = lambda a: a.reshape(bd, n_new, A_HEADS, A_HEAD_DIM)
    ck, cv = cache_k_win[l], cache_v_win[l]
    o_as = _dil_attn_dec(r4(qs), r4(ks), r4(vs), ck, cv, step_bias).reshape(ts, A_WIDTH)
    us3 = us.reshape(bd, n_new, bw)
    hist = jnp.concatenate([jnp.zeros((bd, POOL_HIST - n_pool, bw), F32), state_pool[l]], axis=1)
    cur = jnp.concatenate([us3, jnp.zeros((bd, SUBLANES - n_new, bw), F32)], axis=1)
    o_bs = _pool_dec(hist, cur, n_pool, w_pool_b, pool_scale[l])[:, :n_new].reshape(ts, bw)
    sub = 8
    x2s, hs, lgs = _mix_xattn(xs, o_as, o_bs, cache_mem_k[l].reshape(bd * n_mem, d),
                              cache_mem_v[l].reshape(bd * n_mem, d), wts, groups=bd // sub,
                              rows_per_group=sub * n_new, tm=sub * n_new, rows_per_sub=n_new, keys_per_sub=n_mem)

    h_all = jnp.concatenate([hp, hs], axis=0)
    gates, pos, slot_tok, block_e, n_used = _route(jnp.concatenate([lgp, lgs], axis=0))
    y_sorted = _moe_ffn(h_all, slot_tok, block_e, n_used, w_gate_up[l], b_gate_up[l], w_down[l], b_down[l])
    y_prompt = _combine(y_sorted, x2p, pos[:tp], gates[:tp], g_final).reshape(batch, seq, d)
    y_sample = _combine(y_sorted, x2s, pos[tp:], gates[tp:], g_final).reshape(bd, n_new, d)

    a5 = lambda a, b_: a.reshape(1, b_, -1, A_HEADS, A_HEAD_DIM)
    k_win_prompt, v_win_prompt = a5(k, batch), a5(v, batch)
    pool_prompt = u.reshape(batch, seq, bw)[:, seq - n_pool:][None]
    mem_k_prompt = mk.reshape(1, batch, n_mem, X_HEADS, d // X_HEADS)
    mem_v_prompt = mv.reshape(1, batch, n_mem, X_HEADS, d // X_HEADS)
    k_win_sample, v_win_sample = _cache_shift(ck, cv, r4(ks), r4(vs))
    k_win_sample, v_win_sample = k_win_sample[None], v_win_sample[None]
    pool_sample = jnp.concatenate([state_pool[l][:, n_new:], us3], axis=1)[None]
    return (y_prompt, y_sample, k_win_prompt, v_win_prompt, pool_prompt, mem_k_prompt, mem_v_prompt,
            k_win_sample, v_win_sample, pool_sample)
```

```python
import functools
import math

import numpy as np
import jax
import jax.numpy as jnp
from jax import lax
from jax.experimental import pallas as pl
from jax.experimental.pallas import tpu as pltpu

F32 = jnp.float32
BF16 = jnp.bfloat16

LANES = 128
SUBLANES = 8
VMEM_LIMIT_BYTES = 56 * 1024 * 1024

A_HEADS = 8
A_HEAD_DIM = 64
A_WIDTH = A_HEADS * A_HEAD_DIM
DILATED = ((128, 1), (512, 4), (2048, 16))
QB = 128
POOL_WINDOWS = (2, 4, 8, 16)
POOL_HIST = 16
X_HEADS = 4
N_EXPERTS = 32
TOP_K = 4
SWIGLU_LIMIT = 7.0
SWIGLU_ALPHA = 1.702
N_BUCKETS = 32
RMS_EPS = 1e-6
NEG = -1e30

MOE_BM = 256
COMBINE_TQ = 128


def _cparams(sem):
    return pltpu.CompilerParams(dimension_semantics=sem, vmem_limit_bytes=VMEM_LIMIT_BYTES)


def _rms(x, g):
    return x * lax.rsqrt(jnp.mean(x * x, axis=-1, keepdims=True) + RMS_EPS) * g


def _norm_proj_kernel(x_ref, g_ref, w_ref, *o_refs, widths, scales, heads):
    hb = _rms(x_ref[...], g_ref[...]).astype(BF16)
    off = 0
    for o_ref, width, scale, nh in zip(o_refs, widths, scales, heads):
        p = jnp.dot(hb, w_ref[:, off:off + width], preferred_element_type=F32)
        p = p if scale == 1.0 else p * scale
        if nh is None:
            o_ref[...] = p
        else:
            hd = width // nh
            for h in range(nh):
                o_ref[:, h, :] = p[:, h * hd:(h + 1) * hd]
        off += width


def _norm_proj(x, g, w_bf16, widths, scales, tm, heads=None):
    t, d = x.shape
    n = w_bf16.shape[1]
    heads = heads or (None,) * len(widths)
    assert sum(widths) == n and t % tm == 0
    shapes = [(wd,) if nh is None else (nh, wd // nh) for wd, nh in zip(widths, heads)]
    return pl.pallas_call(
        functools.partial(_norm_proj_kernel, widths=widths, scales=scales, heads=heads),
        grid=(t // tm,),
        in_specs=[
            pl.BlockSpec((tm, d), lambda i: (i, 0)),
            pl.BlockSpec((1, d), lambda i: (0, 0)),
            pl.BlockSpec((d, n), lambda i: (0, 0)),
        ],
        out_specs=[pl.BlockSpec((tm,) + sh, lambda i, nd=len(sh): (i,) + (0,) * nd) for sh in shapes],
        out_shape=[jax.ShapeDtypeStruct((t,) + sh, F32) for sh in shapes],
        compiler_params=_cparams(("arbitrary",)),
        name="norm_proj",
    )(x, g.reshape(1, d), w_bf16)


def _t5_bucket_np(n, max_dist):
    max_exact = N_BUCKETS // 2
    nf = np.maximum(n, 1).astype(np.float32)
    large = max_exact + (
        np.log(nf / np.float32(max_exact)) / np.float32(math.log(max_dist / max_exact))
        * np.float32(N_BUCKETS - max_exact)
    ).astype(np.int32)
    return np.where(n < max_exact, n, np.minimum(large, N_BUCKETS - 1))


def _band_bias(rel_bias):
    max_dist = max(w for w, _ in DILATED)
    qi = np.arange(QB)[:, None]
    ki = np.arange(2 * QB)[None, :]
    j = qi + QB - ki
    tabs = []
    for window, dil in DILATED:
        steps = window // dil
        in_band = (j >= 0) & (j <= steps)
        bucket = _t5_bucket_np(np.clip(j, 0, steps) * dil, max_dist)
        onehot = (bucket[..., None] == np.arange(N_BUCKETS)).astype(np.float32)
        b = jnp.einsum("qkb,bh->hqk", onehot, rel_bias.astype(F32), precision=lax.Precision.HIGHEST)
        tabs.append(jnp.where(in_band[None], b, NEG))
    return jnp.stack(tabs)


def _step_bias(rel_bias):
    max_dist = max(w for w, _ in DILATED)
    tabs = []
    for window, dil in DILATED:
        steps = window // dil
        bucket = _t5_bucket_np(np.arange(steps, -1, -1) * dil, max_dist)
        tabs.append(rel_bias[bucket].astype(F32)[:, :, None])
    return jnp.stack(tabs)


def _dil_attn_kernel(q_ref, k_ref, v_ref, bias_ref, o_ref, obr_ref, lbr_ref, *, seq):
    lane = lax.broadcasted_iota(jnp.int32, (QB, LANES), 1)
    head0 = lane < A_HEAD_DIM

    def rows(ref, start, n, dil):
        if dil == 1:
            return ref[pl.ds(start, n), :]
        return ref[pl.ds(start, n, stride=dil), :]

    def block(br, dil, qstart, kstart, nk):
        qs = rows(q_ref, qstart, QB, dil)
        ks = rows(k_ref, kstart, nk, dil).astype(BF16)
        vs = rows(v_ref, kstart, nk, dil).astype(BF16)
        outs, lses = [], []
        for hh in range(2):
            keep = head0 if hh == 0 else jnp.logical_not(head0)
            qm = jnp.where(keep, qs, 0.0).astype(BF16)
            logits = lax.dot_general(qm, ks, (((1,), (1,)), ((), ())), preferred_element_type=F32)
            logits = logits + bias_ref[br, hh, :, 2 * QB - nk:]
            m = jnp.max(logits, axis=-1, keepdims=True)
            p = jnp.exp(logits - m)
            s = jnp.sum(p, axis=-1, keepdims=True)
            o = jnp.dot(p.astype(BF16), vs, preferred_element_type=F32)
            outs.append(o / s)
            lses.append(jnp.broadcast_to(m + jnp.log(s), (QB, LANES)))
        o = jnp.where(head0, outs[0], outs[1])
        lse = jnp.where(head0, lses[0], lses[1])
        if dil == 1:
            obr_ref[br, pl.ds(qstart, QB), :] = o
            lbr_ref[br, pl.ds(qstart, QB), :] = lse
        else:
            obr_ref[br, pl.ds(qstart, QB, stride=dil), :] = o
            lbr_ref[br, pl.ds(qstart, QB, stride=dil), :] = lse

    for br, (window, dil) in enumerate(DILATED):
        assert window // dil == QB
        nblk = seq // (dil * QB)
        for r in range(dil):
            block(br, dil, r, r, QB)
            if nblk > 1:
                def body(i, carry, br=br, dil=dil, r=r):
                    qstart = r + dil * QB * i
                    if dil == 1:
                        qstart = pl.multiple_of(qstart, QB)
                    block(br, dil, qstart, qstart - dil * QB, 2 * QB)
                    return carry
                lax.fori_loop(1, nblk, body, 0)

    l0, l1, l2 = lbr_ref[0], lbr_ref[1], lbr_ref[2]
    m = jnp.maximum(jnp.maximum(l0, l1), l2)
    e0, e1, e2 = jnp.exp(l0 - m), jnp.exp(l1 - m), jnp.exp(l2 - m)
    acc = e0 * obr_ref[0] + e1 * obr_ref[1] + e2 * obr_ref[2]
    o_ref[...] = acc / (e0 + e1 + e2)


def _dil_attn(q, k, v, band_bias, batch, seq):
    t, aw = q.shape
    npair = aw // LANES
    bias = band_bias.reshape(len(DILATED), npair, 2, QB, 2 * QB)
    spec = pl.BlockSpec((seq, LANES), lambda b, hp: (b, hp))
    return pl.pallas_call(
        functools.partial(_dil_attn_kernel, seq=seq),
        grid=(batch, npair),
        in_specs=[
            spec, spec, spec,
            pl.BlockSpec((len(DILATED), None, 2, QB, 2 * QB), lambda b, hp: (0, hp, 0, 0, 0)),
        ],
        out_specs=spec,
        out_shape=jax.ShapeDtypeStruct((t, aw), F32),
        scratch_shapes=[
            pltpu.VMEM((len(DILATED), seq, LANES), F32),
            pltpu.VMEM((len(DILATED), seq, LANES), F32),
        ],
        compiler_params=_cparams(("arbitrary", "arbitrary")),
        name="dil_attn",
    )(q, k, v, bias)


def _dil_attn_dec_kernel(q_ref, kn_ref, vn_ref, k1_ref, v1_ref, k2_ref, v2_ref, k3_ref, v3_ref,
                         rbias_ref, o_ref, *, n_new):
    kn = kn_ref[0]
    vn = vn_ref[0]
    for t in range(n_new):
        q = q_ref[0, t][None]
        outs, lses = [], []
        for br, (window, dil) in enumerate(DILATED):
            if dil == 1:
                kc, vc = k1_ref[0, t:], v1_ref[0, t:]
                bc = rbias_ref[br, 0:QB - t]
                kn_t, vn_t = kn[:t + 1], vn[:t + 1]
                bn = rbias_ref[br, QB - t:QB + 1]
            else:
                kref, vref = (k2_ref, v2_ref) if br == 1 else (k3_ref, v3_ref)
                kc, vc = kref[0, :, t], vref[0, :, t]
                bc = rbias_ref[br, 0:QB]
                kn_t, vn_t = kn[t:t + 1], vn[t:t + 1]
                bn = rbias_ref[br, QB:QB + 1]
            lc = jnp.sum(kc * q, axis=-1, keepdims=True) + bc
            ln = jnp.sum(kn_t * q, axis=-1, keepdims=True) + bn
            m = jnp.maximum(jnp.max(lc, axis=0, keepdims=True), jnp.max(ln, axis=0, keepdims=True))
            pc = jnp.exp(lc - m)
            pn = jnp.exp(ln - m)
            s = jnp.sum(pc, axis=0, keepdims=True) + jnp.sum(pn, axis=0, keepdims=True)
            o = jnp.sum(pc * vc, axis=0, keepdims=True) + jnp.sum(pn * vn_t, axis=0, keepdims=True)
            outs.append(o / s)
            lses.append(m + jnp.log(s))
        m = jnp.maximum(jnp.maximum(lses[0], lses[1]), lses[2])
        es = [jnp.exp(l - m) for l in lses]
        acc = es[0] * outs[0] + es[1] * outs[1] + es[2] * outs[2]
        o_ref[0, t] = (acc / (es[0] + es[1] + es[2]))[0]


def _dil_attn_dec(q, k_new, v_new, cache_k, cache_v, step_bias):
    bd, n_new, h, dh = q.shape
    n_hist = cache_k.shape[1]
    (w1, d1), (w2, d2), (w3, d3) = DILATED
    assert d1 == 1 and n_hist >= w3 and n_new <= d2 and n_hist % d3 == 0 and n_hist % d2 == 0
    assert w1 // d1 == QB and w2 // d2 == QB and w3 // d3 == QB

    def views(c):
        c2 = c.reshape(bd, n_hist // d2, d2, h, dh)
        c3 = c.reshape(bd, n_hist // d3, d3, h, dh)
        return c, c2, c3

    k1, k2, k3 = views(cache_k)
    v1, v2, v3 = views(cache_v)
    new_spec = pl.BlockSpec((1, n_new, h, dh), lambda b: (b, 0, 0, 0))
    s1 = pl.BlockSpec((1, QB, h, dh), lambda b: (b, n_hist // QB - 1, 0, 0))
    s2 = pl.BlockSpec((1, QB, n_new, h, dh), lambda b: (b, n_hist // d2 // QB - 1, 0, 0, 0))
    s3 = pl.BlockSpec((1, QB, n_new, h, dh), lambda b: (b, n_hist // d3 // QB - 1, 0, 0, 0))
    return pl.pallas_call(
        functools.partial(_dil_attn_dec_kernel, n_new=n_new),
        grid=(bd,),
        in_specs=[new_spec, new_spec, new_spec, s1, s1, s2, s2, s3, s3,
                  pl.BlockSpec(step_bias.shape, lambda b: (0, 0, 0, 0))],
        out_specs=new_spec,
        out_shape=jax.ShapeDtypeStruct(q.shape, F32),
        compiler_params=_cparams(("arbitrary",)),
        name="dil_attn_dec",
    )(q, k_new, v_new, k1, v1, k2, v2, k3, v3, step_bias)


def _pool_kernel(hist_ref, cur_ref, w_ref, scale_ref, o_ref, *, pos0_of_tile, tm):
    nb = cur_ref.shape[0]
    gdim = w_ref.shape[1]
    i = pl.program_id(1)
    pos0 = pos0_of_tile(i)
    have_hist = jnp.where(pos0 > 0, 1.0, 0.0).astype(F32)
    t = lax.broadcasted_iota(jnp.int32, (1, tm, 1), 1)
    for g, w in enumerate(POOL_WINDOWS):
        sl = slice(g * gdim, (g + 1) * gdim)
        cur = cur_ref[:, :, sl]
        ext = jnp.concatenate([hist_ref[:, :, sl] * have_hist, cur], axis=1)
        acc, span = ext, 1
        while span < w:
            n = acc.shape[1]
            acc = acc[:, span:n] + acc[:, 0:n - span]
            span *= 2
        wsum = acc[:, POOL_HIST + 1 - w:POOL_HIST + 1 - w + tm]
        cnt = jnp.minimum(pos0 + t + 1, w).astype(F32)
        d = (wsum / cnt - cur).astype(BF16).reshape(nb * tm, gdim)
        y = jnp.dot(d, w_ref[g], preferred_element_type=F32) * scale_ref[:, sl]
        o_ref[:, :, sl] = y.reshape(nb, tm, gdim)


def _pool_prompt(u, w_pool_bf16, pool_scale, batch, seq, tm):
    t, bw = u.shape
    nt = seq // tm
    hb = tm // POOL_HIST
    u3 = u.reshape(1, t, bw)
    out = pl.pallas_call(
        functools.partial(_pool_kernel, pos0_of_tile=lambda i: i * tm, tm=tm),
        grid=(batch, nt),
        in_specs=[
            pl.BlockSpec((1, POOL_HIST, bw), lambda b, i: (0, jnp.maximum((b * nt + i) * hb - 1, 0), 0)),
            pl.BlockSpec((1, tm, bw), lambda b, i: (0, b * nt + i, 0)),
            pl.BlockSpec(w_pool_bf16.shape, lambda b, i: (0, 0, 0)),
            pl.BlockSpec((1, bw), lambda b, i: (0, 0)),
        ],
        out_specs=pl.BlockSpec((1, tm, bw), lambda b, i: (0, b * nt + i, 0)),
        out_shape=jax.ShapeDtypeStruct((1, t, bw), F32),
        compiler_params=_cparams(("arbitrary", "arbitrary")),
        name="pool_prompt",
    )(u3, u3, w_pool_bf16, pool_scale.reshape(1, bw))
    return out.reshape(t, bw)


def _pool_dec(hist, cur, n_prev, w_pool_bf16, pool_scale):
    bd, tm, bw = cur.shape
    return pl.pallas_call(
        functools.partial(_pool_kernel, pos0_of_tile=lambda i: n_prev, tm=tm),
        grid=(1, 1),
        in_specs=[
            pl.BlockSpec(hist.shape, lambda b, i: (0, 0, 0)),
            pl.BlockSpec(cur.shape, lambda b, i: (0, 0, 0)),
            pl.BlockSpec(w_pool_bf16.shape, lambda b, i: (0, 0, 0)),
            pl.BlockSpec((1, bw), lambda b, i: (0, 0)),
        ],
        out_specs=pl.BlockSpec(cur.shape, lambda b, i: (0, 0, 0)),
        out_shape=jax.ShapeDtypeStruct(cur.shape, F32),
        compiler_params=_cparams(("arbitrary", "arbitrary")),
        name="pool_dec",
    )(hist, cur, w_pool_bf16, pool_scale.reshape(1, bw))


def _mix_xattn_kernel(x_ref, oa_ref, ob_ref, mk_ref, mv_ref, woa_ref, wob_ref, gx_ref, wq_ref, wo_ref,
                      gff_ref, wr_ref, br_ref, x2_ref, h_ref, lg_ref, *, rows_per_sub, keys_per_sub):
    tm = x_ref.shape[0]
    nkv = mk_ref.shape[0]
    xh = wq_ref.shape[1] // X_HEADS
    x = x_ref[...]
    x = x + jnp.dot(oa_ref[...].astype(BF16), woa_ref[...], preferred_element_type=F32)
    x = x + jnp.dot(ob_ref[...].astype(BF16), wob_ref[...], preferred_element_type=F32)
    hq = _rms(x, gx_ref[...]).astype(BF16)
    if rows_per_sub is not None:
        rsub = (pl.program_id(1) * tm + lax.broadcasted_iota(jnp.int32, (tm, nkv), 0)) // rows_per_sub
        ksub = lax.broadcasted_iota(jnp.int32, (tm, nkv), 1) // keys_per_sub
        same = rsub == ksub
    heads = []
    for h in range(X_HEADS):
        sl = slice(h * xh, (h + 1) * xh)
        q = jnp.dot(hq, wq_ref[:, sl], preferred_element_type=F32) * (xh ** -0.5)
        kh = mk_ref[:, h, :].astype(BF16)
        logits = lax.dot_general(q.astype(BF16), kh, (((1,), (1,)), ((), ())), preferred_element_type=F32)
        if rows_per_sub is not None:
            logits = jnp.where(same, logits, NEG)
        m = jnp.max(logits, axis=-1, keepdims=True)
        p = jnp.exp(logits - m)
        s = jnp.sum(p, axis=-1, keepdims=True)
        o = jnp.dot(p.astype(BF16), mv_ref[:, h, :].astype(BF16), preferred_element_type=F32) / s
        heads.append(o.astype(BF16))
    x = x + jnp.dot(jnp.concatenate(heads, axis=-1), wo_ref[...], preferred_element_type=F32)
    x2_ref[...] = x
    hf = _rms(x, gff_ref[...])
    h_ref[...] = hf
    lg_ref[...] = jnp.dot(hf.astype(BF16), wr_ref[...].astype(BF16), preferred_element_type=F32) + br_ref[...]


def _mix_xattn(x, oa, ob, mk, mv, w, groups, rows_per_group, tm, rows_per_sub, keys_per_sub):
    t, d = x.shape
    nt = rows_per_group // tm
    nkv = mk.shape[0] // groups
    ne = w["w_router"].shape[1]
    row = lambda width: pl.BlockSpec((tm, width), lambda g, i: (g * nt + i, 0))
    const = lambda a: pl.BlockSpec(a.shape, lambda g, i: (0,) * a.ndim)
    kv = pl.BlockSpec((nkv,) + mk.shape[1:], lambda g, i: (g, 0, 0))
    consts = [w["w_out_a"], w["w_out_b"], w["g_x"], w["w_xq"], w["w_xo"], w["g_ff"], w["w_router"], w["b_router"]]
    return pl.pallas_call(
        functools.partial(_mix_xattn_kernel, rows_per_sub=rows_per_sub, keys_per_sub=keys_per_sub),
        grid=(groups, nt),
        in_specs=[row(d), row(oa.shape[1]), row(ob.shape[1]), kv, kv] + [const(a) for a in consts],
        out_specs=[row(d), row(d), row(ne)],
        out_shape=[jax.ShapeDtypeStruct((t, d), F32), jax.ShapeDtypeStruct((t, d), F32),
                   jax.ShapeDtypeStruct((t, ne), F32)],
        compiler_params=_cparams(("arbitrary", "arbitrary")),
        name="mix_xattn",
    )(x, oa, ob, mk, mv, *consts)


def _moe_gather(h_hbm, tok_ref, buf, sem, slot):
    def body(i, carry):
        pltpu.make_async_copy(h_hbm.at[pl.ds(tok_ref[0, 0, i], 1)], buf.at[slot, pl.ds(i, 1)],
                              sem.at[slot]).start()
        return carry
    lax.fori_loop(0, MOE_BM, body, 0)


def _moe_ffn_kernel(be_ref, nu_ref, tok_ref, tokn_ref, h_hbm, wgu_ref, bgu_ref, wd_ref, bd_ref, o_ref,
                    buf, sem, wgu_bf, wd_bf):
    b = pl.program_id(0)
    n_used = nu_ref[0]
    slot = b % 2
    dff = wd_ref.shape[1]

    @pl.when(b == 0)
    def _():
        _moe_gather(h_hbm, tok_ref, buf, sem, 0)

    @pl.when(b + 1 < n_used)
    def _():
        _moe_gather(h_hbm, tokn_ref, buf, sem, 1 - slot)

    new_expert = jnp.logical_or(b == 0, be_ref[b] != be_ref[jnp.maximum(b - 1, 0)])

    @pl.when(jnp.logical_and(b < n_used, new_expert))
    def _():
        wgu_bf[...] = wgu_ref[0].astype(BF16)
        wd_bf[...] = wd_ref[0].astype(BF16)

    @pl.when(b < n_used)
    def _():
        pltpu.make_async_copy(h_hbm.at[pl.ds(0, MOE_BM)], buf.at[slot], sem.at[slot]).wait()
        xb = buf[slot].astype(BF16)
        hu = jnp.dot(xb, wgu_bf[...], preferred_element_type=F32) + bgu_ref[0]
        g = jnp.minimum(hu[:, :dff], SWIGLU_LIMIT)
        u = jnp.clip(hu[:, dff:], -SWIGLU_LIMIT, SWIGLU_LIMIT)
        a = g * jax.nn.sigmoid(SWIGLU_ALPHA * g) * (u + 1.0)
        o_ref[...] = jnp.dot(a.astype(BF16), wd_bf[...], preferred_element_type=F32) + bd_ref[0]

    @pl.when(b >= n_used)
    def _():
        o_ref[...] = jnp.zeros_like(o_ref)


def _moe_ffn(h, slot_tok, block_e, n_used, w_gate_up, b_gate_up, w_down, b_down):
    t, d = h.shape
    ne, _, dff2 = w_gate_up.shape
    dff = dff2 // 2
    nb = block_e.shape[0]
    tok3 = slot_tok.reshape(nb, 1, MOE_BM)
    grid_spec = pltpu.PrefetchScalarGridSpec(
        num_scalar_prefetch=2,
        grid=(nb,),
        in_specs=[
            pl.BlockSpec((1, 1, MOE_BM), lambda b, be, nu: (b, 0, 0), memory_space=pltpu.SMEM),
            pl.BlockSpec((1, 1, MOE_BM), lambda b, be, nu: (jnp.minimum(b + 1, nb - 1), 0, 0),
                         memory_space=pltpu.SMEM),
            pl.BlockSpec(memory_space=pl.ANY),
            pl.BlockSpec((1, d, dff2), lambda b, be, nu: (be[b], 0, 0)),
            pl.BlockSpec((1, 1, dff2), lambda b, be, nu: (be[b], 0, 0)),
            pl.BlockSpec((1, dff, d), lambda b, be, nu: (be[b], 0, 0)),
            pl.BlockSpec((1, 1, d), lambda b, be, nu: (be[b], 0, 0)),
        ],
        out_specs=pl.BlockSpec((MOE_BM, d), lambda b, be, nu: (b, 0)),
        scratch_shapes=[
            pltpu.VMEM((2, MOE_BM, d), F32),
            pltpu.SemaphoreType.DMA((2,)),
            pltpu.VMEM((d, dff2), BF16),
            pltpu.VMEM((dff, d), BF16),
        ],
    )
    return pl.pallas_call(
        _moe_ffn_kernel,
        grid_spec=grid_spec,
        out_shape=jax.ShapeDtypeStruct((nb * MOE_BM, d), F32),
        compiler_params=_cparams(("arbitrary",)),
        name="moe_ffn",
    )(block_e, n_used, tok3, tok3, h, w_gate_up, b_gate_up.reshape(ne, 1, dff2), w_down,
      b_down.reshape(ne, 1, d))


def _combine_gather(y_hbm, pos_ref, buf, sem, slot):
    def body(i, carry):
        pltpu.make_async_copy(y_hbm.at[pl.ds(pos_ref[0, 0, i], 1)], buf.at[slot, pl.ds(i, 1)],
                              sem.at[slot]).start()
        return carry
    lax.fori_loop(0, TOP_K * COMBINE_TQ, body, 0)


def _combine_kernel(pos_ref, posn_ref, y_hbm, x_ref, gate_ref, g_ref, o_ref, buf, sem, *, n_tiles):
    i = pl.program_id(0)
    slot = i % 2

    @pl.when(i == 0)
    def _():
        _combine_gather(y_hbm, pos_ref, buf, sem, 0)

    if n_tiles > 1:
        @pl.when(i + 1 < n_tiles)
        def _():
            _combine_gather(y_hbm, posn_ref, buf, sem, 1 - slot)

    pltpu.make_async_copy(y_hbm.at[pl.ds(0, TOP_K * COMBINE_TQ)], buf.at[slot], sem.at[slot]).wait()
    x = x_ref[...]
    gates = gate_ref[...]
    for k in range(TOP_K):
        x = x + gates[:, k:k + 1] * buf[slot, k * COMBINE_TQ:(k + 1) * COMBINE_TQ, :]
    o_ref[...] = _rms(x, g_ref[...])


def _combine(y_sorted, x2, pos, gates, g_final):
    t, d = x2.shape
    tq = COMBINE_TQ
    nt = t // tq
    pos3 = pos.reshape(nt, tq, TOP_K).transpose(0, 2, 1).reshape(nt, 1, TOP_K * tq)
    return pl.pallas_call(
        functools.partial(_combine_kernel, n_tiles=nt),
        grid=(nt,),
        in_specs=[
            pl.BlockSpec((1, 1, TOP_K * tq), lambda i: (i, 0, 0), memory_space=pltpu.SMEM),
            pl.BlockSpec((1, 1, TOP_K * tq), lambda i: (jnp.minimum(i + 1, nt - 1), 0, 0),
                         memory_space=pltpu.SMEM),
            pl.BlockSpec(memory_space=pl.ANY),
            pl.BlockSpec((tq, d), lambda i: (i, 0)),
            pl.BlockSpec((tq, TOP_K), lambda i: (i, 0)),
            pl.BlockSpec((1, d), lambda i: (0, 0)),
        ],
        out_specs=pl.BlockSpec((tq, d), lambda i: (i, 0)),
        out_shape=jax.ShapeDtypeStruct((t, d), F32),
        scratch_shapes=[pltpu.VMEM((2, TOP_K * tq, d), F32), pltpu.SemaphoreType.DMA((2,))],
        compiler_params=_cparams(("arbitrary",)),
        name="combine",
    )(pos3, pos3, y_sorted, x2, gates, g_final.reshape(1, d))


def _cache_shift_kernel(ck_ref, cv_ref, kn_ref, vn_ref, ok_ref, ov_ref, sem, *, n_new):
    bd, n_hist = ck_ref.shape[0], ck_ref.shape[1]
    keep = n_hist - n_new
    copies = []
    for src, new, dst in ((ck_ref, kn_ref, ok_ref), (cv_ref, vn_ref, ov_ref)):
        for b in range(bd):
            copies.append(pltpu.make_async_copy(src.at[b, pl.ds(n_new, keep)], dst.at[b, pl.ds(0, keep)], sem.at[0]))
        copies.append(pltpu.make_async_copy(new, dst.at[:, pl.ds(keep, n_new)], sem.at[1]))
    for c in copies:
        c.start()
    for c in copies:
        c.wait()


def _cache_shift(cache_k, cache_v, k_new, v_new):
    n_new = k_new.shape[1]
    any_spec = pl.BlockSpec(memory_space=pl.ANY)
    return pl.pallas_call(
        functools.partial(_cache_shift_kernel, n_new=n_new),
        in_specs=[any_spec] * 4,
        out_specs=[any_spec] * 2,
        out_shape=[jax.ShapeDtypeStruct(cache_k.shape, cache_k.dtype)] * 2,
        scratch_shapes=[pltpu.SemaphoreType.DMA((2,))],
        name="cache_shift",
    )(cache_k, cache_v, k_new, v_new)


def _route(logits):
    t = logits.shape[0]
    top_v, top_i = lax.top_k(logits, TOP_K)
    gates = jax.nn.softmax(top_v, axis=-1)
    onehot = (top_i[:, :, None] == jnp.arange(N_EXPERTS, dtype=top_i.dtype)[None, None, :])
    per_tok = jnp.sum(onehot, axis=1).astype(jnp.int32)
    before = jnp.cumsum(per_tok, axis=0) - per_tok
    counts = jnp.sum(per_tok, axis=0)
    nblk = (counts + MOE_BM - 1) // MOE_BM
    blk_end = jnp.cumsum(nblk)
    pstart = (blk_end - nblk) * MOE_BM
    row_of = (pstart[None, :] + before)[:, None, :]
    pos = jnp.sum(jnp.where(onehot, row_of, 0), axis=-1).astype(jnp.int32)
    nb = (t * TOP_K + N_EXPERTS * (MOE_BM - 1) + MOE_BM - 1) // MOE_BM
    n_used = blk_end[-1].astype(jnp.int32)
    tok = jnp.broadcast_to(jnp.arange(t, dtype=jnp.int32)[:, None], (t, TOP_K))
    slot_tok = jnp.zeros((nb * MOE_BM,), jnp.int32).at[pos.reshape(-1)].set(tok.reshape(-1))
    blk = jnp.minimum(jnp.arange(nb, dtype=jnp.int32), n_used - 1)
    block_e = jnp.sum(blk_end[None, :] <= blk[:, None], axis=1).astype(jnp.int32)
    return gates, pos, slot_tok, block_e, n_used.reshape(1)


ROUTE_TT = 128
STRIP_BITS = tuple(1 << j for j in range(ROUTE_TT.bit_length()))
PAD_BITS = tuple(1 << j for j in range((MOE_BM - 1).bit_length()))


def _strip_dmas(n, bits, make_copy, wait=False):
    for bit in bits:
        @pl.when((n & bit) != 0)
        def _(bit=bit):
            c = make_copy(n & (bit - 1), bit)
            if wait:
                c.wait()
            else:
                c.start()


def _dispatch_kernel(d_ref, n_ref, o_ref, ps_ref, pn_ref, lst_ref, hp_ref, hs_ref, xs_hbm, buf, zbuf, sem, psem,
                     *, n_tiles, n_prompt_tiles):
    i = pl.program_id(0)
    slot = i % 2
    rows = TOP_K * ROUTE_TT

    def wait_tile(s):
        pltpu.make_async_copy(xs_hbm.at[pl.ds(0, rows)], buf.at[s], sem.at[s]).wait()

    def put_rows(s, val):
        for c in range(val.shape[1] // LANES):
            buf[s, :, c, :] = val[:, c * LANES:(c + 1) * LANES]

    def pad_strips(wait):
        for e in range(N_EXPERTS):
            start = ps_ref[e]
            _strip_dmas(pn_ref[e], PAD_BITS, lambda off, size: pltpu.make_async_copy(
                zbuf.at[pl.ds(0, size)], xs_hbm.at[pl.ds(start + off, size)], psem.at[0]), wait)
        chunk = zbuf.shape[0]
        used_rows = ps_ref[N_EXPERTS - 1] + pn_ref[N_EXPERTS - 1]

        def body(c, carry):
            cp = pltpu.make_async_copy(zbuf, xs_hbm.at[pl.ds(c * chunk, chunk)], psem.at[0])
            if wait:
                cp.wait()
            else:
                cp.start()
            return carry
        lax.fori_loop(used_rows // chunk, xs_hbm.shape[0] // chunk, body, 0)

    @pl.when(i >= 2)
    def _():
        wait_tile(slot)

    @pl.when(i == 0)
    def _():
        zbuf[...] = jnp.zeros_like(zbuf)
        pad_strips(False)

    srow = lax.broadcasted_iota(jnp.int32, (rows, ROUTE_TT), 0)
    place = srow == lst_ref[0:1, :]
    for k in range(1, TOP_K):
        place = jnp.logical_or(place, srow == lst_ref[k:k + 1, :])
    x = jnp.where(i < n_prompt_tiles, hp_ref[...], hs_ref[...]).astype(BF16)
    put_rows(slot, jnp.dot(jnp.where(place, 1.0, 0.0).astype(BF16), x, preferred_element_type=F32))
    for e in range(N_EXPERTS):
        src0 = o_ref[i * N_EXPERTS + e]
        dst0 = d_ref[i * N_EXPERTS + e]
        _strip_dmas(n_ref[i * N_EXPERTS + e], STRIP_BITS, lambda off, size: pltpu.make_async_copy(
            buf.at[slot, pl.ds(src0 + off, size)], xs_hbm.at[pl.ds(dst0 + off, size)], sem.at[slot]))

    @pl.when(i == n_tiles - 1)
    def _():
        wait_tile(slot)
        if n_tiles > 1:
            wait_tile(1 - slot)
        pad_strips(True)


def _dispatch(hp, hs, ls_t, d_tile, n_tile, o_tile, pad_start, pad_len, n_rows):
    tp, d = hp.shape
    ts = hs.shape[0]
    npt, nst = tp // ROUTE_TT, ts // ROUTE_TT
    assert nst == 1 and tp % ROUTE_TT == 0 and ts % ROUTE_TT == 0
    nt = npt + nst
    grid_spec = pltpu.PrefetchScalarGridSpec(
        num_scalar_prefetch=5,
        grid=(nt,),
        in_specs=[
            pl.BlockSpec((TOP_K, ROUTE_TT), lambda i, *_: (0, i)),
            pl.BlockSpec((ROUTE_TT, d), lambda i, *_: (jnp.minimum(i, npt - 1), 0)),
            pl.BlockSpec((ROUTE_TT, d), lambda i, *_: (0, 0)),
        ],
        out_specs=pl.BlockSpec(memory_space=pl.ANY),
        scratch_shapes=[
            pltpu.VMEM((2, TOP_K * ROUTE_TT, d // LANES, LANES), F32),
            pltpu.VMEM((PAD_BITS[-1], d // LANES, LANES), F32),
            pltpu.SemaphoreType.DMA((2,)),
            pltpu.SemaphoreType.DMA((1,)),
        ],
    )
    return pl.pallas_call(
        functools.partial(_dispatch_kernel, n_tiles=nt, n_prompt_tiles=npt),
        grid_spec=grid_spec,
        out_shape=jax.ShapeDtypeStruct((n_rows, d // LANES, LANES), F32),
        compiler_params=_cparams(("arbitrary",)),
        name="moe_dispatch",
    )(d_tile.reshape(-1), n_tile.reshape(-1), o_tile.reshape(-1), pad_start, pad_len, ls_t, hp, hs)


def _expert_ffn_kernel(be_ref, nu_ref, x_ref, wgu_ref, bgu_ref, wd_ref, bd_ref, o_ref, wgu_bf, wd_bf):
    b = pl.program_id(0)
    dff = wd_ref.shape[1]
    new_expert = jnp.logical_or(b == 0, be_ref[b] != be_ref[jnp.maximum(b - 1, 0)])

    @pl.when(jnp.logical_and(b < nu_ref[0], new_expert))
    def _():
        wgu_bf[...] = wgu_ref[0].astype(BF16)
        wd_bf[...] = wd_ref[0].astype(BF16)

    @pl.when(b < nu_ref[0])
    def _():
        x = jnp.concatenate([x_ref[:, c, :] for c in range(x_ref.shape[1])], axis=-1).astype(BF16)
        hu = jnp.dot(x, wgu_bf[...], preferred_element_type=F32) + bgu_ref[0]
        g = jnp.minimum(hu[:, :dff], SWIGLU_LIMIT)
        u = jnp.clip(hu[:, dff:], -SWIGLU_LIMIT, SWIGLU_LIMIT)
        a = g * jax.nn.sigmoid(SWIGLU_ALPHA * g) * (u + 1.0)
        out = jnp.dot(a.astype(BF16), wd_bf[...], preferred_element_type=F32) + bd_ref[0]
        for c in range(o_ref.shape[1]):
            o_ref[:, c, :] = out[:, c * LANES:(c + 1) * LANES]

    @pl.when(b >= nu_ref[0])
    def _():
        o_ref[...] = jnp.zeros_like(o_ref)


def _expert_ffn(x_sorted, block_e, n_used, w_gate_up, b_gate_up, w_down, b_down):
    n_rows = x_sorted.shape[0]
    ne, d, dff2 = w_gate_up.shape
    dff = dff2 // 2
    nb = n_rows // MOE_BM
    used = lambda b, nu: jnp.minimum(b, nu[0] - 1)
    grid_spec = pltpu.PrefetchScalarGridSpec(
        num_scalar_prefetch=2,
        grid=(nb,),
        in_specs=[
            pl.BlockSpec((MOE_BM, d // LANES, LANES), lambda b, be, nu: (used(b, nu), 0, 0)),
            pl.BlockSpec((1, d, dff2), lambda b, be, nu: (be[b], 0, 0)),
            pl.BlockSpec((1, 1, dff2), lambda b, be, nu: (be[b], 0, 0)),
            pl.BlockSpec((1, dff, d), lambda b, be, nu: (be[b], 0, 0)),
            pl.BlockSpec((1, 1, d), lambda b, be, nu: (be[b], 0, 0)),
        ],
        out_specs=pl.BlockSpec((MOE_BM, d // LANES, LANES), lambda b, be, nu: (b, 0, 0)),
        scratch_shapes=[pltpu.VMEM((d, dff2), BF16), pltpu.VMEM((dff, d), BF16)],
    )
    return pl.pallas_call(
        _expert_ffn_kernel,
        grid_spec=grid_spec,
        out_shape=jax.ShapeDtypeStruct(x_sorted.shape, F32),
        compiler_params=_cparams(("arbitrary",)),
        name="expert_ffn",
    )(block_e, n_used, x_sorted, w_gate_up, b_gate_up.reshape(ne, 1, dff2), w_down, b_down.reshape(ne, 1, d))


def _moe_combine_kernel(d_ref, n_ref, o_ref, ls_ref, gate_ref, xp_ref, xs_ref, g_ref, y_hbm, yp_ref, ys_ref,
                        buf, sem, *, n_tiles, n_prompt_tiles):
    j = pl.program_id(0)
    rows = TOP_K * ROUTE_TT

    @pl.when(j < n_tiles)
    def _():
        slot = j % 2
        for e in range(N_EXPERTS):
            src0 = d_ref[j * N_EXPERTS + e]
            dst0 = o_ref[j * N_EXPERTS + e]
            _strip_dmas(n_ref[j * N_EXPERTS + e], STRIP_BITS, lambda off, size: pltpu.make_async_copy(
                y_hbm.at[pl.ds(src0 + off, size)], buf.at[slot, pl.ds(dst0 + off, size)], sem.at[slot]))

    @pl.when(j >= 1)
    def _():
        slot = (j - 1) % 2
        pltpu.make_async_copy(y_hbm.at[pl.ds(0, rows)], buf.at[slot], sem.at[slot]).wait()
        r = jnp.concatenate([buf[slot, :, c, :] for c in range(buf.shape[2])], axis=-1)
        r_hi = r.astype(BF16)
        r_lo = (r - r_hi.astype(F32)).astype(BF16)
        lane = lax.broadcasted_iota(jnp.int32, (ROUTE_TT, rows), 1)
        pick = jnp.zeros((ROUTE_TT, rows), F32)
        for k in range(TOP_K):
            pick = pick + jnp.where(lane == ls_ref[:, k:k + 1], gate_ref[:, k:k + 1], 0.0)
        p_hi = pick.astype(BF16)
        p_lo = (pick - p_hi.astype(F32)).astype(BF16)
        moe = (jnp.dot(p_hi, r_hi, preferred_element_type=F32) + jnp.dot(p_lo, r_hi, preferred_element_type=F32)
               + jnp.dot(p_hi, r_lo, preferred_element_type=F32))
        is_prompt = j - 1 < n_prompt_tiles
        y = _rms(jnp.where(is_prompt, xp_ref[...], xs_ref[...]) + moe, g_ref[...])

        @pl.when(is_prompt)
        def _():
            yp_ref[...] = y

        @pl.when(jnp.logical_not(is_prompt))
        def _():
            ys_ref[...] = y


def _moe_combine(y_sorted, x2p, x2s, ls, gates, d_tile, n_tile, o_tile, g_final):
    tp, d = x2p.shape
    ts = x2s.shape[0]
    npt, nst = tp // ROUTE_TT, ts // ROUTE_TT
    assert nst == 1
    nt = npt + nst
    tile = lambda j: jnp.clip(j - 1, 0, nt - 1)
    ptile = lambda j: jnp.clip(j - 1, 0, npt - 1)
    grid_spec = pltpu.PrefetchScalarGridSpec(
        num_scalar_prefetch=3,
        grid=(nt + 1,),
        in_specs=[
            pl.BlockSpec((ROUTE_TT, TOP_K), lambda j, *_: (tile(j), 0)),
            pl.BlockSpec((ROUTE_TT, TOP_K), lambda j, *_: (tile(j), 0)),
            pl.BlockSpec((ROUTE_TT, d), lambda j, *_: (ptile(j), 0)),
            pl.BlockSpec((ROUTE_TT, d), lambda j, *_: (0, 0)),
            pl.BlockSpec((1, d), lambda j, *_: (0, 0)),
            pl.BlockSpec(memory_space=pl.ANY),
        ],
        out_specs=[
            pl.BlockSpec((ROUTE_TT, d), lambda j, *_: (ptile(j), 0)),
            pl.BlockSpec((ROUTE_TT, d), lambda j, *_: (0, 0)),
        ],
        scratch_shapes=[pltpu.VMEM((2, TOP_K * ROUTE_TT, d // LANES, LANES), F32), pltpu.SemaphoreType.DMA((2,))],
    )
    return pl.pallas_call(
        functools.partial(_moe_combine_kernel, n_tiles=nt, n_prompt_tiles=npt),
        grid_spec=grid_spec,
        out_shape=[jax.ShapeDtypeStruct((tp, d), F32), jax.ShapeDtypeStruct((ts, d), F32)],
        compiler_params=_cparams(("arbitrary",)),
        name="moe_combine",
    )(d_tile.reshape(-1), n_tile.reshape(-1), o_tile.reshape(-1), ls, gates, x2p, x2s, g_final.reshape(1, d),
      y_sorted)


def _route_tiles(logits):
    t = logits.shape[0]
    nt = t // ROUTE_TT
    top_v, top_i = lax.top_k(logits, TOP_K)
    gates = jax.nn.softmax(top_v, axis=-1)
    onehot = (top_i[:, :, None] == jnp.arange(N_EXPERTS, dtype=top_i.dtype)[None, None, :])
    per_tok = jnp.sum(onehot, axis=1).astype(jnp.int32)
    before = jnp.cumsum(per_tok, axis=0) - per_tok
    counts = jnp.sum(per_tok, axis=0)
    nblk = (counts + MOE_BM - 1) // MOE_BM
    blk_end = jnp.cumsum(nblk)
    pstart = (blk_end - nblk) * MOE_BM
    n_used = blk_end[-1].astype(jnp.int32)
    nb = (t * TOP_K + N_EXPERTS * (MOE_BM - 1) + MOE_BM - 1) // MOE_BM
    blk = jnp.minimum(jnp.arange(nb, dtype=jnp.int32), n_used - 1)
    block_e = jnp.sum(blk_end[None, :] <= blk[:, None], axis=1).astype(jnp.int32)
    before_tile = before[::ROUTE_TT]
    n_tile = jnp.sum(per_tok.reshape(nt, ROUTE_TT, N_EXPERTS), axis=1)
    o_tile = jnp.cumsum(n_tile, axis=1) - n_tile
    d_tile = pstart[None, :] + before_tile
    local = (o_tile - before_tile)[:, None, :] + before.reshape(nt, ROUTE_TT, N_EXPERTS)
    ls = jnp.sum(jnp.where(onehot, local.reshape(t, 1, N_EXPERTS), 0), axis=-1).astype(jnp.int32)
    i32 = lambda a: a.astype(jnp.int32)
    return (gates, ls, i32(d_tile), i32(n_tile), i32(o_tile), i32(pstart + counts), i32(nblk * MOE_BM - counts),
            block_e, n_used.reshape(1), nb * MOE_BM)


def kernel(x_prompt, x_sample, cache_k_win, cache_v_win, state_pool, cache_mem_k, cache_mem_v, mem_prompt,
           rel_bias, g_mix, w_in, w_pool, pool_scale, w_out, g_mem, g_x, w_xq, w_xk, w_xv, w_xo, g_ff,
           w_router, b_router, w_gate_up, b_gate_up, w_down, b_down, g_final):
    depth = g_mix.shape[0]
    assert depth == 1
    l = 0
    batch, seq, d = x_prompt.shape
    bd, n_new, _ = x_sample.shape
    n_mem = mem_prompt.shape[1]
    bw = d - A_WIDTH
    n_hist = cache_k_win.shape[2]
    n_pool = state_pool.shape[2]
    tp, ts = batch * seq, bd * n_new

    w_in_b = w_in[l].astype(BF16)
    w_pool_b = w_pool[l].astype(BF16)
    w_xkv_b = jnp.concatenate([w_xk[l], w_xv[l]], axis=1).astype(BF16)
    wts = {
        "w_out_a": w_out[l, :A_WIDTH].astype(BF16), "w_out_b": w_out[l, A_WIDTH:].astype(BF16),
        "g_x": g_x[l].reshape(1, d), "w_xq": w_xq[l].astype(BF16), "w_xo": w_xo[l].astype(BF16),
        "g_ff": g_ff[l].reshape(1, d), "w_router": w_router[l], "b_router": b_router[l].reshape(1, N_EXPERTS),
    }
    band_bias = _band_bias(rel_bias)
    step_bias = _step_bias(rel_bias)
    widths = (A_WIDTH, A_WIDTH, A_WIDTH, bw)
    scales = (A_HEAD_DIM ** -0.5, 1.0, 1.0, 1.0)

    xp = x_prompt.reshape(tp, d)
    q, k, v, u = _norm_proj(xp, g_mix[l], w_in_b, widths, scales, tm=512)
    o_a = _dil_attn(q, k, v, band_bias, batch, seq)
    o_b = _pool_prompt(u, w_pool_b, pool_scale[l], batch, seq, tm=512)
    mk, mv = _norm_proj(mem_prompt.reshape(batch * n_mem, d), g_mem[l], w_xkv_b, (d, d), (1.0, 1.0), tm=512,
                        heads=(X_HEADS, X_HEADS))
    x2p, hp, lgp = _mix_xattn(xp, o_a, o_b, mk, mv, wts, groups=batch, rows_per_group=seq, tm=512,
                              rows_per_sub=None, keys_per_sub=None)

    xs = x_sample.reshape(ts, d)
    qs, ks, vs, us = _norm_proj(xs, g_mix[l], w_in_b, widths, scales, tm=ts)
    r4 = lambda a: a.reshape(bd, n_new, A_HEADS, A_HEAD_DIM)
    ck, cv = cache_k_win[l], cache_v_win[l]
    o_as = _dil_attn_dec(r4(qs), r4(ks), r4(vs), ck, cv, step_bias).reshape(ts, A_WIDTH)
    us3 = us.reshape(bd, n_new, bw)
    hist = jnp.concatenate([jnp.zeros((bd, POOL_HIST - n_pool, bw), F32), state_pool[l]], axis=1)
    cur = jnp.concatenate([us3, jnp.zeros((bd, SUBLANES - n_new, bw), F32)], axis=1)
    o_bs = _pool_dec(hist, cur, n_pool, w_pool_b, pool_scale[l])[:, :n_new].reshape(ts, bw)
    sub = 8
    xhd = d // X_HEADS
    x2s, hs, lgs = _mix_xattn(xs, o_as, o_bs, cache_mem_k[l].reshape(bd * n_mem, X_HEADS, xhd),
                              cache_mem_v[l].reshape(bd * n_mem, X_HEADS, xhd), wts, groups=bd // sub,
                              rows_per_group=sub * n_new, tm=sub * n_new, rows_per_sub=n_new, keys_per_sub=n_mem)

    (gates, ls, d_tile, n_tile, o_tile, pad_start, pad_len, block_e, n_used,
     n_rows) = _route_tiles(jnp.concatenate([lgp, lgs], axis=0))
    x_sorted = _dispatch(hp, hs, ls.T, d_tile, n_tile, o_tile, pad_start, pad_len, n_rows)
    y_sorted = _expert_ffn(x_sorted, block_e, n_used, w_gate_up[l], b_gate_up[l], w_down[l], b_down[l])
    y_prompt, y_sample = _moe_combine(y_sorted, x2p, x2s, ls, gates, d_tile, n_tile, o_tile, g_final)
    y_prompt = y_prompt.reshape(batch, seq, d)
    y_sample = y_sample.reshape(bd, n_new, d)

    a5 = lambda a, b_: a.reshape(1, b_, -1, A_HEADS, A_HEAD_DIM)
    k_win_prompt, v_win_prompt = a5(k, batch), a5(v, batch)
    pool_prompt = u.reshape(batch, seq, bw)[:, seq - n_pool:][None]
    mem_k_prompt = mk.reshape(1, batch, n_mem, X_HEADS, xhd)
    mem_v_prompt = mv.reshape(1, batch, n_mem, X_HEADS, xhd)
    k_win_sample = jnp.concatenate([ck[:, n_new:], r4(ks)], axis=1)[None]
    v_win_sample = jnp.concatenate([cv[:, n_new:], r4(vs)], axis=1)[None]
    pool_sample = jnp.concatenate([state_pool[l][:, n_new:], us3], axis=1)[None]
    return (y_prompt, y_sample, k_win_prompt, v_win_prompt, pool_prompt, mem_k_prompt, mem_v_prompt,
            k_win_sample, v_win_sample, pool_sample)
```

```python
import functools
import math

import numpy as np
import jax
import jax.numpy as jnp
from jax import lax
from jax.experimental import pallas as pl
from jax.experimental.pallas import tpu as pltpu

F32 = jnp.float32
BF16 = jnp.bfloat16

LANES = 128
SUBLANES = 8
VMEM_LIMIT_BYTES = 56 * 1024 * 1024

A_HEADS = 8
A_HEAD_DIM = 64
A_WIDTH = A_HEADS * A_HEAD_DIM
DILATED = ((128, 1), (512, 4), (2048, 16))
QB = 128
POOL_WINDOWS = (2, 4, 8, 16)
POOL_HIST = 16
X_HEADS = 4
N_EXPERTS = 32
TOP_K = 4
SWIGLU_LIMIT = 7.0
SWIGLU_ALPHA = 1.702
N_BUCKETS = 32
RMS_EPS = 1e-6
NEG = -1e30

MOE_BM = 256
COMBINE_TQ = 128


def _cparams(sem):
    return pltpu.CompilerParams(dimension_semantics=sem, vmem_limit_bytes=VMEM_LIMIT_BYTES)


def _rms(x, g):
    return x * lax.rsqrt(jnp.mean(x * x, axis=-1, keepdims=True) + RMS_EPS) * g


def _norm_proj_kernel(x_ref, g_ref, w_ref, *o_refs, widths, scales, heads):
    hb = _rms(x_ref[...], g_ref[...]).astype(BF16)
    off = 0
    for o_ref, width, scale, nh in zip(o_refs, widths, scales, heads):
        p = jnp.dot(hb, w_ref[:, off:off + width], preferred_element_type=F32)
        p = p if scale == 1.0 else p * scale
        if nh is None:
            o_ref[...] = p
        else:
            hd = width // nh
            for h in range(nh):
                o_ref[:, h, :] = p[:, h * hd:(h + 1) * hd]
        off += width


def _norm_proj(x, g, w_bf16, widths, scales, tm, heads=None):
    t, d = x.shape
    n = w_bf16.shape[1]
    heads = heads or (None,) * len(widths)
    assert sum(widths) == n and t % tm == 0
    shapes = [(wd,) if nh is None else (nh, wd // nh) for wd, nh in zip(widths, heads)]
    return pl.pallas_call(
        functools.partial(_norm_proj_kernel, widths=widths, scales=scales, heads=heads),
        grid=(t // tm,),
        in_specs=[
            pl.BlockSpec((tm, d), lambda i: (i, 0)),
            pl.BlockSpec((1, d), lambda i: (0, 0)),
            pl.BlockSpec((d, n), lambda i: (0, 0)),
        ],
        out_specs=[pl.BlockSpec((tm,) + sh, lambda i, nd=len(sh): (i,) + (0,) * nd) for sh in shapes],
        out_shape=[jax.ShapeDtypeStruct((t,) + sh, F32) for sh in shapes],
        compiler_params=_cparams(("arbitrary",)),
        name="norm_proj",
    )(x, g.reshape(1, d), w_bf16)


def _in_proj_kernel(x_ref, g_ref, w_ref, wt_ref, *o_refs, widths, scales, n_t):
    hb = _rms(x_ref[...], g_ref[...]).astype(BF16)
    off = 0
    for o_ref, width, scale in zip(o_refs, widths, scales):
        p = jnp.dot(hb, w_ref[:, off:off + width], preferred_element_type=F32)
        o_ref[...] = p if scale == 1.0 else p * scale
        off += width
    for j in range(n_t):
        o_refs[len(widths) + j][0] = lax.dot_general(wt_ref[j], hb, (((1,), (1,)), ((), ())),
                                                    preferred_element_type=F32)


def _in_proj(x, g, w_bf16, wt_bf16, widths, scales, batch, seq, tm):
    t, d = x.shape
    n = w_bf16.shape[1]
    n_t, wt_width, _ = wt_bf16.shape
    nt = seq // tm
    assert sum(widths) == n and seq % tm == 0 and t == batch * seq
    return pl.pallas_call(
        functools.partial(_in_proj_kernel, widths=widths, scales=scales, n_t=n_t),
        grid=(t // tm,),
        in_specs=[
            pl.BlockSpec((tm, d), lambda i: (i, 0)),
            pl.BlockSpec((1, d), lambda i: (0, 0)),
            pl.BlockSpec((d, n), lambda i: (0, 0)),
            pl.BlockSpec(wt_bf16.shape, lambda i: (0, 0, 0)),
        ],
        out_specs=[pl.BlockSpec((tm, wd), lambda i: (i, 0)) for wd in widths]
        + [pl.BlockSpec((1, wt_width, tm), lambda i: (i // nt, 0, i % nt))] * n_t,
        out_shape=[jax.ShapeDtypeStruct((t, wd), F32) for wd in widths]
        + [jax.ShapeDtypeStruct((batch, wt_width, seq), F32)] * n_t,
        compiler_params=_cparams(("arbitrary",)),
        name="in_proj",
    )(x, g.reshape(1, d), w_bf16, wt_bf16)


def _t5_bucket_np(n, max_dist):
    max_exact = N_BUCKETS // 2
    nf = np.maximum(n, 1).astype(np.float32)
    large = max_exact + (
        np.log(nf / np.float32(max_exact)) / np.float32(math.log(max_dist / max_exact))
        * np.float32(N_BUCKETS - max_exact)
    ).astype(np.int32)
    return np.where(n < max_exact, n, np.minimum(large, N_BUCKETS - 1))


def _band_bias(rel_bias):
    max_dist = max(w for w, _ in DILATED)
    qi = np.arange(QB)[:, None]
    ki = np.arange(2 * QB)[None, :]
    j = qi + QB - ki
    tabs = []
    for window, dil in DILATED:
        steps = window // dil
        in_band = (j >= 0) & (j <= steps)
        bucket = _t5_bucket_np(np.clip(j, 0, steps) * dil, max_dist)
        onehot = (bucket[..., None] == np.arange(N_BUCKETS)).astype(np.float32)
        b = jnp.einsum("qkb,bh->hqk", onehot, rel_bias.astype(F32), precision=lax.Precision.HIGHEST)
        tabs.append(jnp.where(in_band[None], b, NEG))
    return jnp.stack(tabs)


def _step_bias(rel_bias):
    max_dist = max(w for w, _ in DILATED)
    tabs = []
    for window, dil in DILATED:
        steps = window // dil
        bucket = _t5_bucket_np(np.arange(steps, -1, -1) * dil, max_dist)
        tabs.append(rel_bias[bucket].astype(F32)[:, :, None])
    return jnp.stack(tabs)


def _decode_bias(rel_bias, n_hist, n_new):
    max_dist = max(w for w, _ in DILATED)
    t = np.arange(n_new)[:, None]
    tabs_c, tabs_n = [], []
    for window, dil in DILATED:
        out = []
        for dist in (n_hist + t - np.arange(n_hist)[None, :], t - np.arange(n_new)[None, :]):
            ok = (dist >= 0) & (dist % dil == 0) & (dist <= window)
            onehot = (_t5_bucket_np(np.clip(dist, 0, window), max_dist)[..., None] == np.arange(N_BUCKETS))
            b = jnp.einsum("tpb,bh->htp", onehot.astype(np.float32), rel_bias.astype(F32),
                           precision=lax.Precision.HIGHEST)
            out.append(jnp.where(ok[None], b, NEG))
        tabs_c.append(out[0])
        tabs_n.append(out[1])
    return jnp.stack(tabs_c), jnp.stack(tabs_n)


def _decode_attn_kernel(q_ref, kn_ref, vn_ref, knt_ref, vnt_ref, ck_ref, cv_ref, bc_ref, bn_ref,
                        o_ref, ok_ref, ov_ref, *, n_new, heads):
    n_hist = ck_ref.shape[2]
    hd = A_HEAD_DIM
    outs = []
    for h in range(heads):
        rows = slice(h * hd, (h + 1) * hd)
        q = q_ref[0, :, rows].astype(BF16)
        kt = ck_ref[0, rows, :].astype(BF16)
        vt = cv_ref[0, rows, :].astype(BF16)
        lc = jnp.dot(q, kt, preferred_element_type=F32)
        ln = lax.dot_general(q, kn_ref[0, :, rows].astype(BF16), (((1,), (1,)), ((), ())),
                             preferred_element_type=F32)
        vn = vn_ref[0, :, rows].astype(BF16)
        o_br, lse_br = [], []
        for br in range(len(DILATED)):
            bl = lc + bc_ref[br, h]
            bln = ln + bn_ref[br, h]
            m = jnp.maximum(jnp.max(bl, axis=-1, keepdims=True), jnp.max(bln, axis=-1, keepdims=True))
            p = jnp.exp(bl - m)
            pn = jnp.exp(bln - m)
            s = jnp.sum(p, axis=-1, keepdims=True) + jnp.sum(pn, axis=-1, keepdims=True)
            o = lax.dot_general(p.astype(BF16), vt, (((1,), (1,)), ((), ())), preferred_element_type=F32)
            o = o + jnp.dot(pn.astype(BF16), vn, preferred_element_type=F32)
            o_br.append(o / s)
            lse_br.append(m + jnp.log(s))
        m = jnp.maximum(jnp.maximum(lse_br[0], lse_br[1]), lse_br[2])
        es = [jnp.exp(l - m) for l in lse_br]
        outs.append((es[0] * o_br[0] + es[1] * o_br[1] + es[2] * o_br[2]) / (es[0] + es[1] + es[2]))
    o_ref[0] = jnp.concatenate(outs, axis=-1)

    lane = lax.broadcasted_iota(jnp.int32, (ck_ref.shape[1], LANES), 1)
    for c_ref, nt_ref, dst in ((ck_ref, knt_ref, ok_ref), (cv_ref, vnt_ref, ov_ref)):
        shifted = pltpu.roll(c_ref[0], n_hist - n_new, axis=1)
        tail = shifted[:, n_hist - LANES:]
        for j in range(n_new):
            tail = jnp.where(lane == LANES - n_new + j, nt_ref[0, :, j:j + 1], tail)
        dst[0, :, :n_hist - LANES] = shifted[:, :n_hist - LANES]
        dst[0, :, n_hist - LANES:] = tail


def _decode_attn(q, k_new, v_new, ck_t, cv_t, bias_c, bias_n, heads_per_step):
    bd, n_new, aw = q.shape
    n_hist = ck_t.shape[2]
    gw = heads_per_step * A_HEAD_DIM
    ng = aw // gw
    row = pl.BlockSpec((1, n_new, gw), lambda b, g: (b, 0, g))
    col = pl.BlockSpec((1, gw, n_new), lambda b, g: (b, g, 0))
    cache = pl.BlockSpec((1, gw, n_hist), lambda b, g: (b, g, 0))
    nbr = len(DILATED)
    return pl.pallas_call(
        functools.partial(_decode_attn_kernel, n_new=n_new, heads=heads_per_step),
        grid=(bd, ng),
        in_specs=[row, row, row, col, col, cache, cache,
                  pl.BlockSpec((nbr, heads_per_step, n_new, n_hist), lambda b, g: (0, g, 0, 0)),
                  pl.BlockSpec((nbr, heads_per_step, n_new, n_new), lambda b, g: (0, g, 0, 0))],
        out_specs=[row, cache, cache],
        out_shape=[jax.ShapeDtypeStruct(q.shape, F32), jax.ShapeDtypeStruct(ck_t.shape, F32),
                   jax.ShapeDtypeStruct(cv_t.shape, F32)],
        compiler_params=_cparams(("arbitrary", "arbitrary")),
        name="decode_attn",
    )(q, k_new, v_new, jnp.swapaxes(k_new, 1, 2), jnp.swapaxes(v_new, 1, 2), ck_t, cv_t, bias_c, bias_n)


def _dil_attn_kernel(q_ref, k_ref, v_ref, bias_ref, o_ref, obr_ref, lbr_ref, *, seq):
    lane = lax.broadcasted_iota(jnp.int32, (QB, LANES), 1)
    head0 = lane < A_HEAD_DIM

    def rows(ref, start, n, dil):
        if dil == 1:
            return ref[pl.ds(start, n), :]
        return ref[pl.ds(start, n, stride=dil), :]

    def block(br, dil, qstart, kstart, nk):
        qs = rows(q_ref, qstart, QB, dil)
        ks = rows(k_ref, kstart, nk, dil).astype(BF16)
        vs = rows(v_ref, kstart, nk, dil).astype(BF16)
        outs, lses = [], []
        for hh in range(2):
            keep = head0 if hh == 0 else jnp.logical_not(head0)
            qm = jnp.where(keep, qs, 0.0).astype(BF16)
            logits = lax.dot_general(qm, ks, (((1,), (1,)), ((), ())), preferred_element_type=F32)
            logits = logits + bias_ref[br, hh, :, 2 * QB - nk:]
            m = jnp.max(logits, axis=-1, keepdims=True)
            p = jnp.exp(logits - m)
            s = jnp.sum(p, axis=-1, keepdims=True)
            o = jnp.dot(p.astype(BF16), vs, preferred_element_type=F32)
            outs.append(o / s)
            lses.append(jnp.broadcast_to(m + jnp.log(s), (QB, LANES)))
        o = jnp.where(head0, outs[0], outs[1])
        lse = jnp.where(head0, lses[0], lses[1])
        if dil == 1:
            obr_ref[br, pl.ds(qstart, QB), :] = o
            lbr_ref[br, pl.ds(qstart, QB), :] = lse
        else:
            obr_ref[br, pl.ds(qstart, QB, stride=dil), :] = o
            lbr_ref[br, pl.ds(qstart, QB, stride=dil), :] = lse

    for br, (window, dil) in enumerate(DILATED):
        assert window // dil == QB
        nblk = seq // (dil * QB)
        for r in range(dil):
            block(br, dil, r, r, QB)
            if nblk > 1:
                def body(i, carry, br=br, dil=dil, r=r):
                    qstart = r + dil * QB * i
                    if dil == 1:
                        qstart = pl.multiple_of(qstart, QB)
                    block(br, dil, qstart, qstart - dil * QB, 2 * QB)
                    return carry
                lax.fori_loop(1, nblk, body, 0)

    l0, l1, l2 = lbr_ref[0], lbr_ref[1], lbr_ref[2]
    m = jnp.maximum(jnp.maximum(l0, l1), l2)
    e0, e1, e2 = jnp.exp(l0 - m), jnp.exp(l1 - m), jnp.exp(l2 - m)
    acc = e0 * obr_ref[0] + e1 * obr_ref[1] + e2 * obr_ref[2]
    o_ref[...] = acc / (e0 + e1 + e2)


def _dil_attn(q, k, v, band_bias, batch, seq):
    t, aw = q.shape
    npair = aw // LANES
    bias = band_bias.reshape(len(DILATED), npair, 2, QB, 2 * QB)
    spec = pl.BlockSpec((seq, LANES), lambda b, hp: (b, hp))
    return pl.pallas_call(
        functools.partial(_dil_attn_kernel, seq=seq),
        grid=(batch, npair),
        in_specs=[
            spec, spec, spec,
            pl.BlockSpec((len(DILATED), None, 2, QB, 2 * QB), lambda b, hp: (0, hp, 0, 0, 0)),
        ],
        out_specs=spec,
        out_shape=jax.ShapeDtypeStruct((t, aw), F32),
        scratch_shapes=[
            pltpu.VMEM((len(DILATED), seq, LANES), F32),
            pltpu.VMEM((len(DILATED), seq, LANES), F32),
        ],
        compiler_params=_cparams(("arbitrary", "arbitrary")),
        name="dil_attn",
    )(q, k, v, bias)


def _dil_attn_dec_kernel(q_ref, kn_ref, vn_ref, k1_ref, v1_ref, k2_ref, v2_ref, k3_ref, v3_ref,
                         rbias_ref, o_ref, *, n_new):
    kn = kn_ref[0]
    vn = vn_ref[0]
    for t in range(n_new):
        q = q_ref[0, t][None]
        outs, lses = [], []
        for br, (window, dil) in enumerate(DILATED):
            if dil == 1:
                kc, vc = k1_ref[0, t:], v1_ref[0, t:]
                bc = rbias_ref[br, 0:QB - t]
                kn_t, vn_t = kn[:t + 1], vn[:t + 1]
                bn = rbias_ref[br, QB - t:QB + 1]
            else:
                kref, vref = (k2_ref, v2_ref) if br == 1 else (k3_ref, v3_ref)
                kc, vc = kref[0, :, t], vref[0, :, t]
                bc = rbias_ref[br, 0:QB]
                kn_t, vn_t = kn[t:t + 1], vn[t:t + 1]
                bn = rbias_ref[br, QB:QB + 1]
            lc = jnp.sum(kc * q, axis=-1, keepdims=True) + bc
            ln = jnp.sum(kn_t * q, axis=-1, keepdims=True) + bn
            m = jnp.maximum(jnp.max(lc, axis=0, keepdims=True), jnp.max(ln, axis=0, keepdims=True))
            pc = jnp.exp(lc - m)
            pn = jnp.exp(ln - m)
            s = jnp.sum(pc, axis=0, keepdims=True) + jnp.sum(pn, axis=0, keepdims=True)
            o = jnp.sum(pc * vc, axis=0, keepdims=True) + jnp.sum(pn * vn_t, axis=0, keepdims=True)
            outs.append(o / s)
            lses.append(m + jnp.log(s))
        m = jnp.maximum(jnp.maximum(lses[0], lses[1]), lses[2])
        es = [jnp.exp(l - m) for l in lses]
        acc = es[0] * outs[0] + es[1] * outs[1] + es[2] * outs[2]
        o_ref[0, t] = (acc / (es[0] + es[1] + es[2]))[0]


def _dil_attn_dec(q, k_new, v_new, cache_k, cache_v, step_bias):
    bd, n_new, h, dh = q.shape
    n_hist = cache_k.shape[1]
    (w1, d1), (w2, d2), (w3, d3) = DILATED
    assert d1 == 1 and n_hist >= w3 and n_new <= d2 and n_hist % d3 == 0 and n_hist % d2 == 0
    assert w1 // d1 == QB and w2 // d2 == QB and w3 // d3 == QB

    def views(c):
        c2 = c.reshape(bd, n_hist // d2, d2, h, dh)
        c3 = c.reshape(bd, n_hist // d3, d3, h, dh)
        return c, c2, c3

    k1, k2, k3 = views(cache_k)
    v1, v2, v3 = views(cache_v)
    new_spec = pl.BlockSpec((1, n_new, h, dh), lambda b: (b, 0, 0, 0))
    s1 = pl.BlockSpec((1, QB, h, dh), lambda b: (b, n_hist // QB - 1, 0, 0))
    s2 = pl.BlockSpec((1, QB, n_new, h, dh), lambda b: (b, n_hist // d2 // QB - 1, 0, 0, 0))
    s3 = pl.BlockSpec((1, QB, n_new, h, dh), lambda b: (b, n_hist // d3 // QB - 1, 0, 0, 0))
    return pl.pallas_call(
        functools.partial(_dil_attn_dec_kernel, n_new=n_new),
        grid=(bd,),
        in_specs=[new_spec, new_spec, new_spec, s1, s1, s2, s2, s3, s3,
                  pl.BlockSpec(step_bias.shape, lambda b: (0, 0, 0, 0))],
        out_specs=new_spec,
        out_shape=jax.ShapeDtypeStruct(q.shape, F32),
        compiler_params=_cparams(("arbitrary",)),
        name="dil_attn_dec",
    )(q, k_new, v_new, k1, v1, k2, v2, k3, v3, step_bias)


def _pool_kernel(hist_ref, cur_ref, w_ref, scale_ref, o_ref, *, pos0_of_tile, tm):
    nb = cur_ref.shape[0]
    gdim = w_ref.shape[1]
    i = pl.program_id(1)
    pos0 = pos0_of_tile(i)
    have_hist = jnp.where(pos0 > 0, 1.0, 0.0).astype(F32)
    t = lax.broadcasted_iota(jnp.int32, (1, tm, 1), 1)
    for g, w in enumerate(POOL_WINDOWS):
        sl = slice(g * gdim, (g + 1) * gdim)
        cur = cur_ref[:, :, sl]
        ext = jnp.concatenate([hist_ref[:, :, sl] * have_hist, cur], axis=1)
        acc, span = ext, 1
        while span < w:
            n = acc.shape[1]
            acc = acc[:, span:n] + acc[:, 0:n - span]
            span *= 2
        wsum = acc[:, POOL_HIST + 1 - w:POOL_HIST + 1 - w + tm]
        cnt = jnp.minimum(pos0 + t + 1, w).astype(F32)
        d = (wsum / cnt - cur).astype(BF16).reshape(nb * tm, gdim)
        y = jnp.dot(d, w_ref[g], preferred_element_type=F32) * scale_ref[:, sl]
        o_ref[:, :, sl] = y.reshape(nb, tm, gdim)


def _pool_prompt(u, w_pool_bf16, pool_scale, batch, seq, tm):
    t, bw = u.shape
    nt = seq // tm
    hb = tm // POOL_HIST
    u3 = u.reshape(1, t, bw)
    out = pl.pallas_call(
        functools.partial(_pool_kernel, pos0_of_tile=lambda i: i * tm, tm=tm),
        grid=(batch, nt),
        in_specs=[
            pl.BlockSpec((1, POOL_HIST, bw), lambda b, i: (0, jnp.maximum((b * nt + i) * hb - 1, 0), 0)),
            pl.BlockSpec((1, tm, bw), lambda b, i: (0, b * nt + i, 0)),
            pl.BlockSpec(w_pool_bf16.shape, lambda b, i: (0, 0, 0)),
            pl.BlockSpec((1, bw), lambda b, i: (0, 0)),
        ],
        out_specs=pl.BlockSpec((1, tm, bw), lambda b, i: (0, b * nt + i, 0)),
        out_shape=jax.ShapeDtypeStruct((1, t, bw), F32),
        compiler_params=_cparams(("arbitrary", "arbitrary")),
        name="pool_prompt",
    )(u3, u3, w_pool_bf16, pool_scale.reshape(1, bw))
    return out.reshape(t, bw)


def _pool_dec(hist, cur, n_prev, w_pool_bf16, pool_scale):
    bd, tm, bw = cur.shape
    return pl.pallas_call(
        functools.partial(_pool_kernel, pos0_of_tile=lambda i: n_prev, tm=tm),
        grid=(1, 1),
        in_specs=[
            pl.BlockSpec(hist.shape, lambda b, i: (0, 0, 0)),
            pl.BlockSpec(cur.shape, lambda b, i: (0, 0, 0)),
            pl.BlockSpec(w_pool_bf16.shape, lambda b, i: (0, 0, 0)),
            pl.BlockSpec((1, bw), lambda b, i: (0, 0)),
        ],
        out_specs=pl.BlockSpec(cur.shape, lambda b, i: (0, 0, 0)),
        out_shape=jax.ShapeDtypeStruct(cur.shape, F32),
        compiler_params=_cparams(("arbitrary", "arbitrary")),
        name="pool_dec",
    )(hist, cur, w_pool_bf16, pool_scale.reshape(1, bw))


def _mix_xattn_kernel(x_ref, oa_ref, ob_ref, mk_ref, mv_ref, woa_ref, wob_ref, gx_ref, wq_ref, wo_ref,
                      gff_ref, wr_ref, br_ref, x2_ref, h_ref, lg_ref, *, rows_per_sub, keys_per_sub):
    tm = x_ref.shape[0]
    nkv = mk_ref.shape[0]
    xh = wq_ref.shape[1] // X_HEADS
    x = x_ref[...]
    x = x + jnp.dot(oa_ref[...].astype(BF16), woa_ref[...], preferred_element_type=F32)
    x = x + jnp.dot(ob_ref[...].astype(BF16), wob_ref[...], preferred_element_type=F32)
    hq = _rms(x, gx_ref[...]).astype(BF16)
    if rows_per_sub is not None:
        rsub = (pl.program_id(1) * tm + lax.broadcasted_iota(jnp.int32, (tm, nkv), 0)) // rows_per_sub
        ksub = lax.broadcasted_iota(jnp.int32, (tm, nkv), 1) // keys_per_sub
        same = rsub == ksub
    heads = []
    for h in range(X_HEADS):
        sl = slice(h * xh, (h + 1) * xh)
        q = jnp.dot(hq, wq_ref[:, sl], preferred_element_type=F32) * (xh ** -0.5)
        kh = mk_ref[:, h, :].astype(BF16)
        logits = lax.dot_general(q.astype(BF16), kh, (((1,), (1,)), ((), ())), preferred_element_type=F32)
        if rows_per_sub is not None:
            logits = jnp.where(same, logits, NEG)
        m = jnp.max(logits, axis=-1, keepdims=True)
        p = jnp.exp(logits - m)
        s = jnp.sum(p, axis=-1, keepdims=True)
        o = jnp.dot(p.astype(BF16), mv_ref[:, h, :].astype(BF16), preferred_element_type=F32) / s
        heads.append(o.astype(BF16))
    x = x + jnp.dot(jnp.concatenate(heads, axis=-1), wo_ref[...], preferred_element_type=F32)
    x2_ref[...] = x
    hf = _rms(x, gff_ref[...])
    h_ref[...] = hf
    lg_ref[...] = jnp.dot(hf.astype(BF16), wr_ref[...].astype(BF16), preferred_element_type=F32) + br_ref[...]


def _mix_xattn(x, oa, ob, mk, mv, w, groups, rows_per_group, tm, rows_per_sub, keys_per_sub):
    t, d = x.shape
    nt = rows_per_group // tm
    nkv = mk.shape[0] // groups
    ne = w["w_router"].shape[1]
    row = lambda width: pl.BlockSpec((tm, width), lambda g, i: (g * nt + i, 0))
    const = lambda a: pl.BlockSpec(a.shape, lambda g, i: (0,) * a.ndim)
    kv = pl.BlockSpec((nkv,) + mk.shape[1:], lambda g, i: (g, 0, 0))
    consts = [w["w_out_a"], w["w_out_b"], w["g_x"], w["w_xq"], w["w_xo"], w["g_ff"], w["w_router"], w["b_router"]]
    return pl.pallas_call(
        functools.partial(_mix_xattn_kernel, rows_per_sub=rows_per_sub, keys_per_sub=keys_per_sub),
        grid=(groups, nt),
        in_specs=[row(d), row(oa.shape[1]), row(ob.shape[1]), kv, kv] + [const(a) for a in consts],
        out_specs=[row(d), row(d), row(ne)],
        out_shape=[jax.ShapeDtypeStruct((t, d), F32), jax.ShapeDtypeStruct((t, d), F32),
                   jax.ShapeDtypeStruct((t, ne), F32)],
        compiler_params=_cparams(("arbitrary", "arbitrary")),
        name="mix_xattn",
    )(x, oa, ob, mk, mv, *consts)


def _moe_gather(h_hbm, tok_ref, buf, sem, slot):
    def body(i, carry):
        pltpu.make_async_copy(h_hbm.at[pl.ds(tok_ref[0, 0, i], 1)], buf.at[slot, pl.ds(i, 1)],
                              sem.at[slot]).start()
        return carry
    lax.fori_loop(0, MOE_BM, body, 0)


def _moe_ffn_kernel(be_ref, nu_ref, tok_ref, tokn_ref, h_hbm, wgu_ref, bgu_ref, wd_ref, bd_ref, o_ref,
                    buf, sem, wgu_bf, wd_bf):
    b = pl.program_id(0)
    n_used = nu_ref[0]
    slot = b % 2
    dff = wd_ref.shape[1]

    @pl.when(b == 0)
    def _():
        _moe_gather(h_hbm, tok_ref, buf, sem, 0)

    @pl.when(b + 1 < n_used)
    def _():
        _moe_gather(h_hbm, tokn_ref, buf, sem, 1 - slot)

    new_expert = jnp.logical_or(b == 0, be_ref[b] != be_ref[jnp.maximum(b - 1, 0)])

    @pl.when(jnp.logical_and(b < n_used, new_expert))
    def _():
        wgu_bf[...] = wgu_ref[0].astype(BF16)
        wd_bf[...] = wd_ref[0].astype(BF16)

    @pl.when(b < n_used)
    def _():
        pltpu.make_async_copy(h_hbm.at[pl.ds(0, MOE_BM)], buf.at[slot], sem.at[slot]).wait()
        xb = buf[slot].astype(BF16)
        hu = jnp.dot(xb, wgu_bf[...], preferred_element_type=F32) + bgu_ref[0]
        g = jnp.minimum(hu[:, :dff], SWIGLU_LIMIT)
        u = jnp.clip(hu[:, dff:], -SWIGLU_LIMIT, SWIGLU_LIMIT)
        a = g * jax.nn.sigmoid(SWIGLU_ALPHA * g) * (u + 1.0)
        o_ref[...] = jnp.dot(a.astype(BF16), wd_bf[...], preferred_element_type=F32) + bd_ref[0]

    @pl.when(b >= n_used)
    def _():
        o_ref[...] = jnp.zeros_like(o_ref)


def _moe_ffn(h, slot_tok, block_e, n_used, w_gate_up, b_gate_up, w_down, b_down):
    t, d = h.shape
    ne, _, dff2 = w_gate_up.shape
    dff = dff2 // 2
    nb = block_e.shape[0]
    tok3 = slot_tok.reshape(nb, 1, MOE_BM)
    grid_spec = pltpu.PrefetchScalarGridSpec(
        num_scalar_prefetch=2,
        grid=(nb,),
        in_specs=[
            pl.BlockSpec((1, 1, MOE_BM), lambda b, be, nu: (b, 0, 0), memory_space=pltpu.SMEM),
            pl.BlockSpec((1, 1, MOE_BM), lambda b, be, nu: (jnp.minimum(b + 1, nb - 1), 0, 0),
                         memory_space=pltpu.SMEM),
            pl.BlockSpec(memory_space=pl.ANY),
            pl.BlockSpec((1, d, dff2), lambda b, be, nu: (be[b], 0, 0)),
            pl.BlockSpec((1, 1, dff2), lambda b, be, nu: (be[b], 0, 0)),
            pl.BlockSpec((1, dff, d), lambda b, be, nu: (be[b], 0, 0)),
            pl.BlockSpec((1, 1, d), lambda b, be, nu: (be[b], 0, 0)),
        ],
        out_specs=pl.BlockSpec((MOE_BM, d), lambda b, be, nu: (b, 0)),
        scratch_shapes=[
            pltpu.VMEM((2, MOE_BM, d), F32),
            pltpu.SemaphoreType.DMA((2,)),
            pltpu.VMEM((d, dff2), BF16),
            pltpu.VMEM((dff, d), BF16),
        ],
    )
    return pl.pallas_call(
        _moe_ffn_kernel,
        grid_spec=grid_spec,
        out_shape=jax.ShapeDtypeStruct((nb * MOE_BM, d), F32),
        compiler_params=_cparams(("arbitrary",)),
        name="moe_ffn",
    )(block_e, n_used, tok3, tok3, h, w_gate_up, b_gate_up.reshape(ne, 1, dff2), w_down,
      b_down.reshape(ne, 1, d))


def _combine_gather(y_hbm, pos_ref, buf, sem, slot):
    def body(i, carry):
        pltpu.make_async_copy(y_hbm.at[pl.ds(pos_ref[0, 0, i], 1)], buf.at[slot, pl.ds(i, 1)],
                              sem.at[slot]).start()
        return carry
    lax.fori_loop(0, TOP_K * COMBINE_TQ, body, 0)


def _combine_kernel(pos_ref, posn_ref, y_hbm, x_ref, gate_ref, g_ref, o_ref, buf, sem, *, n_tiles):
    i = pl.program_id(0)
    slot = i % 2

    @pl.when(i == 0)
    def _():
        _combine_gather(y_hbm, pos_ref, buf, sem, 0)

    if n_tiles > 1:
        @pl.when(i + 1 < n_tiles)
        def _():
            _combine_gather(y_hbm, posn_ref, buf, sem, 1 - slot)

    pltpu.make_async_copy(y_hbm.at[pl.ds(0, TOP_K * COMBINE_TQ)], buf.at[slot], sem.at[slot]).wait()
    x = x_ref[...]
    gates = gate_ref[...]
    for k in range(TOP_K):
        x = x + gates[:, k:k + 1] * buf[slot, k * COMBINE_TQ:(k + 1) * COMBINE_TQ, :]
    o_ref[...] = _rms(x, g_ref[...])


def _combine(y_sorted, x2, pos, gates, g_final):
    t, d = x2.shape
    tq = COMBINE_TQ
    nt = t // tq
    pos3 = pos.reshape(nt, tq, TOP_K).transpose(0, 2, 1).reshape(nt, 1, TOP_K * tq)
    return pl.pallas_call(
        functools.partial(_combine_kernel, n_tiles=nt),
        grid=(nt,),
        in_specs=[
            pl.BlockSpec((1, 1, TOP_K * tq), lambda i: (i, 0, 0), memory_space=pltpu.SMEM),
            pl.BlockSpec((1, 1, TOP_K * tq), lambda i: (jnp.minimum(i + 1, nt - 1), 0, 0),
                         memory_space=pltpu.SMEM),
            pl.BlockSpec(memory_space=pl.ANY),
            pl.BlockSpec((tq, d), lambda i: (i, 0)),
            pl.BlockSpec((tq, TOP_K), lambda i: (i, 0)),
            pl.BlockSpec((1, d), lambda i: (0, 0)),
        ],
        out_specs=pl.BlockSpec((tq, d), lambda i: (i, 0)),
        out_shape=jax.ShapeDtypeStruct((t, d), F32),
        scratch_shapes=[pltpu.VMEM((2, TOP_K * tq, d), F32), pltpu.SemaphoreType.DMA((2,))],
        compiler_params=_cparams(("arbitrary",)),
        name="combine",
    )(pos3, pos3, y_sorted, x2, gates, g_final.reshape(1, d))


def _cache_shift_kernel(ck_ref, cv_ref, kn_ref, vn_ref, ok_ref, ov_ref, sem, *, n_new):
    bd, n_hist = ck_ref.shape[0], ck_ref.shape[1]
    keep = n_hist - n_new
    copies = []
    for src, new, dst in ((ck_ref, kn_ref, ok_ref), (cv_ref, vn_ref, ov_ref)):
        for b in range(bd):
            copies.append(pltpu.make_async_copy(src.at[b, pl.ds(n_new, keep)], dst.at[b, pl.ds(0, keep)], sem.at[0]))
        copies.append(pltpu.make_async_copy(new, dst.at[:, pl.ds(keep, n_new)], sem.at[1]))
    for c in copies:
        c.start()
    for c in copies:
        c.wait()


def _cache_shift(cache_k, cache_v, k_new, v_new):
    n_new = k_new.shape[1]
    any_spec = pl.BlockSpec(memory_space=pl.ANY)
    return pl.pallas_call(
        functools.partial(_cache_shift_kernel, n_new=n_new),
        in_specs=[any_spec] * 4,
        out_specs=[any_spec] * 2,
        out_shape=[jax.ShapeDtypeStruct(cache_k.shape, cache_k.dtype)] * 2,
        scratch_shapes=[pltpu.SemaphoreType.DMA((2,))],
        name="cache_shift",
    )(cache_k, cache_v, k_new, v_new)


def _route(logits):
    t = logits.shape[0]
    top_v, top_i = lax.top_k(logits, TOP_K)
    gates = jax.nn.softmax(top_v, axis=-1)
    onehot = (top_i[:, :, None] == jnp.arange(N_EXPERTS, dtype=top_i.dtype)[None, None, :])
    per_tok = jnp.sum(onehot, axis=1).astype(jnp.int32)
    before = jnp.cumsum(per_tok, axis=0) - per_tok
    counts = jnp.sum(per_tok, axis=0)
    nblk = (counts + MOE_BM - 1) // MOE_BM
    blk_end = jnp.cumsum(nblk)
    pstart = (blk_end - nblk) * MOE_BM
    row_of = (pstart[None, :] + before)[:, None, :]
    pos = jnp.sum(jnp.where(onehot, row_of, 0), axis=-1).astype(jnp.int32)
    nb = (t * TOP_K + N_EXPERTS * (MOE_BM - 1) + MOE_BM - 1) // MOE_BM
    n_used = blk_end[-1].astype(jnp.int32)
    tok = jnp.broadcast_to(jnp.arange(t, dtype=jnp.int32)[:, None], (t, TOP_K))
    slot_tok = jnp.zeros((nb * MOE_BM,), jnp.int32).at[pos.reshape(-1)].set(tok.reshape(-1))
    blk = jnp.minimum(jnp.arange(nb, dtype=jnp.int32), n_used - 1)
    block_e = jnp.sum(blk_end[None, :] <= blk[:, None], axis=1).astype(jnp.int32)
    return gates, pos, slot_tok, block_e, n_used.reshape(1)


ROUTE_TT = 128
STRIP_BITS = tuple(1 << j for j in range(ROUTE_TT.bit_length()))
PAD_BITS = tuple(1 << j for j in range((MOE_BM - 1).bit_length()))


def _strip_dmas(n, bits, make_copy, wait=False):
    for bit in bits:
        @pl.when((n & bit) != 0)
        def _(bit=bit):
            c = make_copy(n & (bit - 1), bit)
            if wait:
                c.wait()
            else:
                c.start()


def _dispatch_kernel(d_ref, n_ref, o_ref, ps_ref, pn_ref, lst_ref, hp_ref, hs_ref, xs_hbm, buf, zbuf, sem, psem,
                     *, n_tiles, n_prompt_tiles):
    i = pl.program_id(0)
    slot = i % 2
    rows = TOP_K * ROUTE_TT

    def wait_tile(s):
        pltpu.make_async_copy(xs_hbm.at[pl.ds(0, rows)], buf.at[s], sem.at[s]).wait()

    def put_rows(s, val):
        for c in range(val.shape[1] // LANES):
            buf[s, :, c, :] = val[:, c * LANES:(c + 1) * LANES]

    def pad_strips(wait):
        for e in range(N_EXPERTS):
            start = ps_ref[e]
            _strip_dmas(pn_ref[e], PAD_BITS, lambda off, size: pltpu.make_async_copy(
                zbuf.at[pl.ds(0, size)], xs_hbm.at[pl.ds(start + off, size)], psem.at[0]), wait)
        chunk = zbuf.shape[0]
        used_rows = ps_ref[N_EXPERTS - 1] + pn_ref[N_EXPERTS - 1]

        def body(c, carry):
            cp = pltpu.make_async_copy(zbuf, xs_hbm.at[pl.ds(c * chunk, chunk)], psem.at[0])
            if wait:
                cp.wait()
            else:
                cp.start()
            return carry
        lax.fori_loop(used_rows // chunk, xs_hbm.shape[0] // chunk, body, 0)

    @pl.when(i >= 2)
    def _():
        wait_tile(slot)

    @pl.when(i == 0)
    def _():
        zbuf[...] = jnp.zeros_like(zbuf)
        pad_strips(False)

    srow = lax.broadcasted_iota(jnp.int32, (rows, ROUTE_TT), 0)
    place = srow == lst_ref[0:1, :]
    for k in range(1, TOP_K):
        place = jnp.logical_or(place, srow == lst_ref[k:k + 1, :])
    x = jnp.where(i < n_prompt_tiles, hp_ref[...], hs_ref[...]).astype(BF16)
    put_rows(slot, jnp.dot(jnp.where(place, 1.0, 0.0).astype(BF16), x, preferred_element_type=F32))
    for e in range(N_EXPERTS):
        src0 = o_ref[i * N_EXPERTS + e]
        dst0 = d_ref[i * N_EXPERTS + e]
        _strip_dmas(n_ref[i * N_EXPERTS + e], STRIP_BITS, lambda off, size: pltpu.make_async_copy(
            buf.at[slot, pl.ds(src0 + off, size)], xs_hbm.at[pl.ds(dst0 + off, size)], sem.at[slot]))

    @pl.when(i == n_tiles - 1)
    def _():
        wait_tile(slot)
        if n_tiles > 1:
            wait_tile(1 - slot)
        pad_strips(True)


def _dispatch(hp, hs, ls_t, d_tile, n_tile, o_tile, pad_start, pad_len, n_rows):
    tp, d = hp.shape
    ts = hs.shape[0]
    npt, nst = tp // ROUTE_TT, ts // ROUTE_TT
    assert nst == 1 and tp % ROUTE_TT == 0 and ts % ROUTE_TT == 0
    nt = npt + nst
    grid_spec = pltpu.PrefetchScalarGridSpec(
        num_scalar_prefetch=5,
        grid=(nt,),
        in_specs=[
            pl.BlockSpec((TOP_K, ROUTE_TT), lambda i, *_: (0, i)),
            pl.BlockSpec((ROUTE_TT, d), lambda i, *_: (jnp.minimum(i, npt - 1), 0)),
            pl.BlockSpec((ROUTE_TT, d), lambda i, *_: (0, 0)),
        ],
        out_specs=pl.BlockSpec(memory_space=pl.ANY),
        scratch_shapes=[
            pltpu.VMEM((2, TOP_K * ROUTE_TT, d // LANES, LANES), F32),
            pltpu.VMEM((PAD_BITS[-1], d // LANES, LANES), F32),
            pltpu.SemaphoreType.DMA((2,)),
            pltpu.SemaphoreType.DMA((1,)),
        ],
    )
    return pl.pallas_call(
        functools.partial(_dispatch_kernel, n_tiles=nt, n_prompt_tiles=npt),
        grid_spec=grid_spec,
        out_shape=jax.ShapeDtypeStruct((n_rows, d // LANES, LANES), F32),
        compiler_params=_cparams(("arbitrary",)),
        name="moe_dispatch",
    )(d_tile.reshape(-1), n_tile.reshape(-1), o_tile.reshape(-1), pad_start, pad_len, ls_t, hp, hs)


def _expert_ffn_kernel(be_ref, nu_ref, x_ref, wgu_ref, bgu_ref, wd_ref, bd_ref, o_ref, wgu_bf, wd_bf):
    b = pl.program_id(0)
    dff = wd_ref.shape[1]
    new_expert = jnp.logical_or(b == 0, be_ref[b] != be_ref[jnp.maximum(b - 1, 0)])

    @pl.when(jnp.logical_and(b < nu_ref[0], new_expert))
    def _():
        wgu_bf[...] = wgu_ref[0].astype(BF16)
        wd_bf[...] = wd_ref[0].astype(BF16)

    @pl.when(b < nu_ref[0])
    def _():
        x = jnp.concatenate([x_ref[:, c, :] for c in range(x_ref.shape[1])], axis=-1).astype(BF16)
        hu = jnp.dot(x, wgu_bf[...], preferred_element_type=F32) + bgu_ref[0]
        g = jnp.minimum(hu[:, :dff], SWIGLU_LIMIT)
        u = jnp.clip(hu[:, dff:], -SWIGLU_LIMIT, SWIGLU_LIMIT)
        a = g * jax.nn.sigmoid(SWIGLU_ALPHA * g) * (u + 1.0)
        out = jnp.dot(a.astype(BF16), wd_bf[...], preferred_element_type=F32) + bd_ref[0]
        for c in range(o_ref.shape[1]):
            o_ref[:, c, :] = out[:, c * LANES:(c + 1) * LANES]

    @pl.when(b >= nu_ref[0])
    def _():
        o_ref[...] = jnp.zeros_like(o_ref)


def _expert_ffn(x_sorted, block_e, n_used, w_gate_up, b_gate_up, w_down, b_down):
    n_rows = x_sorted.shape[0]
    ne, d, dff2 = w_gate_up.shape
    dff = dff2 // 2
    nb = n_rows // MOE_BM
    used = lambda b, nu: jnp.minimum(b, nu[0] - 1)
    grid_spec = pltpu.PrefetchScalarGridSpec(
        num_scalar_prefetch=2,
        grid=(nb,),
        in_specs=[
            pl.BlockSpec((MOE_BM, d // LANES, LANES), lambda b, be, nu: (used(b, nu), 0, 0)),
            pl.BlockSpec((1, d, dff2), lambda b, be, nu: (be[b], 0, 0)),
            pl.BlockSpec((1, 1, dff2), lambda b, be, nu: (be[b], 0, 0)),
            pl.BlockSpec((1, dff, d), lambda b, be, nu: (be[b], 0, 0)),
            pl.BlockSpec((1, 1, d), lambda b, be, nu: (be[b], 0, 0)),
        ],
        out_specs=pl.BlockSpec((MOE_BM, d // LANES, LANES), lambda b, be, nu: (b, 0, 0)),
        scratch_shapes=[pltpu.VMEM((d, dff2), BF16), pltpu.VMEM((dff, d), BF16)],
    )
    return pl.pallas_call(
        _expert_ffn_kernel,
        grid_spec=grid_spec,
        out_shape=jax.ShapeDtypeStruct(x_sorted.shape, F32),
        compiler_params=_cparams(("arbitrary",)),
        name="expert_ffn",
    )(block_e, n_used, x_sorted, w_gate_up, b_gate_up.reshape(ne, 1, dff2), w_down, b_down.reshape(ne, 1, d))


def _moe_combine_kernel(d_ref, n_ref, o_ref, ls_ref, gate_ref, xp_ref, xs_ref, g_ref, y_hbm, yp_ref, ys_ref,
                        buf, sem, *, n_tiles, n_prompt_tiles):
    j = pl.program_id(0)
    rows = TOP_K * ROUTE_TT

    @pl.when(j < n_tiles)
    def _():
        slot = j % 2
        for e in range(N_EXPERTS):
            src0 = d_ref[j * N_EXPERTS + e]
            dst0 = o_ref[j * N_EXPERTS + e]
            _strip_dmas(n_ref[j * N_EXPERTS + e], STRIP_BITS, lambda off, size: pltpu.make_async_copy(
                y_hbm.at[pl.ds(src0 + off, size)], buf.at[slot, pl.ds(dst0 + off, size)], sem.at[slot]))

    @pl.when(j >= 1)
    def _():
        slot = (j - 1) % 2
        pltpu.make_async_copy(y_hbm.at[pl.ds(0, rows)], buf.at[slot], sem.at[slot]).wait()
        r = jnp.concatenate([buf[slot, :, c, :] for c in range(buf.shape[2])], axis=-1)
        r_hi = r.astype(BF16)
        r_lo = (r - r_hi.astype(F32)).astype(BF16)
        lane = lax.broadcasted_iota(jnp.int32, (ROUTE_TT, rows), 1)
        pick = jnp.zeros((ROUTE_TT, rows), F32)
        for k in range(TOP_K):
            pick = pick + jnp.where(lane == ls_ref[:, k:k + 1], gate_ref[:, k:k + 1], 0.0)
        p_hi = pick.astype(BF16)
        p_lo = (pick - p_hi.astype(F32)).astype(BF16)
        moe = (jnp.dot(p_hi, r_hi, preferred_element_type=F32) + jnp.dot(p_lo, r_hi, preferred_element_type=F32)
               + jnp.dot(p_hi, r_lo, preferred_element_type=F32))
        is_prompt = j - 1 < n_prompt_tiles
        y = _rms(jnp.where(is_prompt, xp_ref[...], xs_ref[...]) + moe, g_ref[...])

        @pl.when(is_prompt)
        def _():
            yp_ref[...] = y

        @pl.when(jnp.logical_not(is_prompt))
        def _():
            ys_ref[...] = y


def _moe_combine(y_sorted, x2p, x2s, ls, gates, d_tile, n_tile, o_tile, g_final):
    tp, d = x2p.shape
    ts = x2s.shape[0]
    npt, nst = tp // ROUTE_TT, ts // ROUTE_TT
    assert nst == 1
    nt = npt + nst
    tile = lambda j: jnp.clip(j - 1, 0, nt - 1)
    ptile = lambda j: jnp.clip(j - 1, 0, npt - 1)
    grid_spec = pltpu.PrefetchScalarGridSpec(
        num_scalar_prefetch=3,
        grid=(nt + 1,),
        in_specs=[
            pl.BlockSpec((ROUTE_TT, TOP_K), lambda j, *_: (tile(j), 0)),
            pl.BlockSpec((ROUTE_TT, TOP_K), lambda j, *_: (tile(j), 0)),
            pl.BlockSpec((ROUTE_TT, d), lambda j, *_: (ptile(j), 0)),
            pl.BlockSpec((ROUTE_TT, d), lambda j, *_: (0, 0)),
            pl.BlockSpec((1, d), lambda j, *_: (0, 0)),
            pl.BlockSpec(memory_space=pl.ANY),
        ],
        out_specs=[
            pl.BlockSpec((ROUTE_TT, d), lambda j, *_: (ptile(j), 0)),
            pl.BlockSpec((ROUTE_TT, d), lambda j, *_: (0, 0)),
        ],
        scratch_shapes=[pltpu.VMEM((2, TOP_K * ROUTE_TT, d // LANES, LANES), F32), pltpu.SemaphoreType.DMA((2,))],
    )
    return pl.pallas_call(
        functools.partial(_moe_combine_kernel, n_tiles=nt, n_prompt_tiles=npt),
        grid_spec=grid_spec,
        out_shape=[jax.ShapeDtypeStruct((tp, d), F32), jax.ShapeDtypeStruct((ts, d), F32)],
        compiler_params=_cparams(("arbitrary",)),
        name="moe_combine",
    )(d_tile.reshape(-1), n_tile.reshape(-1), o_tile.reshape(-1), ls, gates, x2p, x2s, g_final.reshape(1, d),
      y_sorted)


def _route_tiles(logits):
    t = logits.shape[0]
    nt = t // ROUTE_TT
    top_v, top_i = lax.top_k(logits, TOP_K)
    gates = jax.nn.softmax(top_v, axis=-1)
    onehot = (top_i[:, :, None] == jnp.arange(N_EXPERTS, dtype=top_i.dtype)[None, None, :])
    per_tok = jnp.sum(onehot, axis=1).astype(jnp.int32)
    before = jnp.cumsum(per_tok, axis=0) - per_tok
    counts = jnp.sum(per_tok, axis=0)
    nblk = (counts + MOE_BM - 1) // MOE_BM
    blk_end = jnp.cumsum(nblk)
    pstart = (blk_end - nblk) * MOE_BM
    n_used = blk_end[-1].astype(jnp.int32)
    nb = (t * TOP_K + N_EXPERTS * (MOE_BM - 1) + MOE_BM - 1) // MOE_BM
    blk = jnp.minimum(jnp.arange(nb, dtype=jnp.int32), n_used - 1)
    block_e = jnp.sum(blk_end[None, :] <= blk[:, None], axis=1).astype(jnp.int32)
    before_tile = before[::ROUTE_TT]
    n_tile = jnp.sum(per_tok.reshape(nt, ROUTE_TT, N_EXPERTS), axis=1)
    o_tile = jnp.cumsum(n_tile, axis=1) - n_tile
    d_tile = pstart[None, :] + before_tile
    local = (o_tile - before_tile)[:, None, :] + before.reshape(nt, ROUTE_TT, N_EXPERTS)
    ls = jnp.sum(jnp.where(onehot, local.reshape(t, 1, N_EXPERTS), 0), axis=-1).astype(jnp.int32)
    i32 = lambda a: a.astype(jnp.int32)
    return (gates, ls, i32(d_tile), i32(n_tile), i32(o_tile), i32(pstart + counts), i32(nblk * MOE_BM - counts),
            block_e, n_used.reshape(1), nb * MOE_BM)


def kernel(x_prompt, x_sample, cache_k_win, cache_v_win, state_pool, cache_mem_k, cache_mem_v, mem_prompt,
           rel_bias, g_mix, w_in, w_pool, pool_scale, w_out, g_mem, g_x, w_xq, w_xk, w_xv, w_xo, g_ff,
           w_router, b_router, w_gate_up, b_gate_up, w_down, b_down, g_final):
    depth = g_mix.shape[0]
    assert depth == 1
    l = 0
    batch, seq, d = x_prompt.shape
    bd, n_new, _ = x_sample.shape
    n_mem = mem_prompt.shape[1]
    bw = d - A_WIDTH
    n_hist = cache_k_win.shape[2]
    n_pool = state_pool.shape[2]
    tp, ts = batch * seq, bd * n_new

    w_in_b = w_in[l].astype(BF16)
    w_pool_b = w_pool[l].astype(BF16)
    w_xkv_b = jnp.concatenate([w_xk[l], w_xv[l]], axis=1).astype(BF16)
    wts = {
        "w_out_a": w_out[l, :A_WIDTH].astype(BF16), "w_out_b": w_out[l, A_WIDTH:].astype(BF16),
        "g_x": g_x[l].reshape(1, d), "w_xq": w_xq[l].astype(BF16), "w_xo": w_xo[l].astype(BF16),
        "g_ff": g_ff[l].reshape(1, d), "w_router": w_router[l], "b_router": b_router[l].reshape(1, N_EXPERTS),
    }
    band_bias = _band_bias(rel_bias)
    step_bias = _step_bias(rel_bias)
    widths = (A_WIDTH, A_WIDTH, A_WIDTH, bw)
    scales = (A_HEAD_DIM ** -0.5, 1.0, 1.0, 1.0)

    xp = x_prompt.reshape(tp, d)
    w_kv_t = jnp.transpose(w_in[l][:, A_WIDTH:3 * A_WIDTH].reshape(d, 2, A_WIDTH), (1, 2, 0)).astype(BF16)
    q, k, v, u, k_t, v_t = _in_proj(xp, g_mix[l], w_in_b, w_kv_t, widths, scales, batch, seq, tm=512)
    o_a = _dil_attn(q, k, v, band_bias, batch, seq)
    o_b = _pool_prompt(u, w_pool_b, pool_scale[l], batch, seq, tm=512)
    mk, mv = _norm_proj(mem_prompt.reshape(batch * n_mem, d), g_mem[l], w_xkv_b, (d, d), (1.0, 1.0), tm=512,
                        heads=(X_HEADS, X_HEADS))
    x2p, hp, lgp = _mix_xattn(xp, o_a, o_b, mk, mv, wts, groups=batch, rows_per_group=seq, tm=512,
                              rows_per_sub=None, keys_per_sub=None)

    xs = x_sample.reshape(ts, d)
    qs, ks, vs, us = _norm_proj(xs, g_mix[l], w_in_b, widths, scales, tm=ts)
    r4 = lambda a: a.reshape(bd, n_new, A_HEADS, A_HEAD_DIM)
    ck, cv = cache_k_win[l], cache_v_win[l]
    new3 = lambda a: a.reshape(bd, n_new, A_WIDTH)
    to_t = lambda c: jnp.transpose(c, (0, 2, 3, 1)).reshape(bd, A_WIDTH, n_hist)
    from_t = lambda c: jnp.transpose(c.reshape(bd, A_HEADS, A_HEAD_DIM, n_hist), (0, 3, 1, 2))[None]
    bias_c, bias_n = _decode_bias(rel_bias, n_hist, n_new)
    o_as, ck_t, cv_t = _decode_attn(new3(qs), new3(ks), new3(vs), to_t(ck), to_t(cv), bias_c, bias_n,
                                    heads_per_step=4)
    o_as = o_as.reshape(ts, A_WIDTH)
    us3 = us.reshape(bd, n_new, bw)
    hist = jnp.concatenate([jnp.zeros((bd, POOL_HIST - n_pool, bw), F32), state_pool[l]], axis=1)
    cur = jnp.concatenate([us3, jnp.zeros((bd, SUBLANES - n_new, bw), F32)], axis=1)
    o_bs = _pool_dec(hist, cur, n_pool, w_pool_b, pool_scale[l])[:, :n_new].reshape(ts, bw)
    sub = 8
    xhd = d // X_HEADS
    x2s, hs, lgs = _mix_xattn(xs, o_as, o_bs, cache_mem_k[l].reshape(bd * n_mem, X_HEADS, xhd),
                              cache_mem_v[l].reshape(bd * n_mem, X_HEADS, xhd), wts, groups=bd // sub,
                              rows_per_group=sub * n_new, tm=sub * n_new, rows_per_sub=n_new, keys_per_sub=n_mem)

    (gates, ls, d_tile, n_tile, o_tile, pad_start, pad_len, block_e, n_used,
     n_rows) = _route_tiles(jnp.concatenate([lgp, lgs], axis=0))
    x_sorted = _dispatch(hp, hs, ls.T, d_tile, n_tile, o_tile, pad_start, pad_len, n_rows)
    y_sorted = _expert_ffn(x_sorted, block_e, n_used, w_gate_up[l], b_gate_up[l], w_down[l], b_down[l])
    y_prompt, y_sample = _moe_combine(y_sorted, x2p, x2s, ls, gates, d_tile, n_tile, o_tile, g_final)
    y_prompt = y_prompt.reshape(batch, seq, d)
    y_sample = y_sample.reshape(bd, n_new, d)

    a5 = lambda a, b_: a.reshape(1, b_, -1, A_HEADS, A_HEAD_DIM)
    from_tp = lambda c: jnp.transpose(c.reshape(batch, A_HEADS, A_HEAD_DIM, seq), (0, 3, 1, 2))[None]
    k_win_prompt, v_win_prompt = from_tp(k_t), from_tp(v_t)
    pool_prompt = u.reshape(batch, seq, bw)[:, seq - n_pool:][None]
    mem_k_prompt = mk.reshape(1, batch, n_mem, X_HEADS, xhd)
    mem_v_prompt = mv.reshape(1, batch, n_mem, X_HEADS, xhd)
    k_win_sample, v_win_sample = from_t(ck_t), from_t(cv_t)
    pool_sample = jnp.concatenate([state_pool[l][:, n_new:], us3], axis=1)[None]
    return (y_prompt, y_sample, k_win_prompt, v_win_prompt, pool_prompt, mem_k_prompt, mem_v_prompt,
            k_win_sample, v_win_sample, pool_sample)
```

```python
import functools
import math

import numpy as np
import jax
import jax.numpy as jnp
from jax import lax
from jax.experimental import pallas as pl
from jax.experimental.pallas import tpu as pltpu

F32 = jnp.float32
BF16 = jnp.bfloat16

LANES = 128
SUBLANES = 8
VMEM_LIMIT_BYTES = 56 * 1024 * 1024

A_HEADS = 8
A_HEAD_DIM = 64
A_WIDTH = A_HEADS * A_HEAD_DIM
DILATED = ((128, 1), (512, 4), (2048, 16))
QB = 128
POOL_WINDOWS = (2, 4, 8, 16)
POOL_HIST = 16
X_HEADS = 4
N_EXPERTS = 32
TOP_K = 4
SWIGLU_LIMIT = 7.0
SWIGLU_ALPHA = 1.702
N_BUCKETS = 32
RMS_EPS = 1e-6
NEG = -1e30

MOE_BM = 256
COMBINE_TQ = 128


def _cparams(sem):
    return pltpu.CompilerParams(dimension_semantics=sem, vmem_limit_bytes=VMEM_LIMIT_BYTES)


def _rms(x, g):
    return x * lax.rsqrt(jnp.mean(x * x, axis=-1, keepdims=True) + RMS_EPS) * g


def _norm_proj_kernel(x_ref, g_ref, w_ref, *o_refs, widths, scales, heads):
    hb = _rms(x_ref[...], g_ref[...]).astype(BF16)
    off = 0
    for o_ref, width, scale, nh in zip(o_refs, widths, scales, heads):
        p = jnp.dot(hb, w_ref[:, off:off + width], preferred_element_type=F32)
        p = p if scale == 1.0 else p * scale
        if nh is None:
            o_ref[...] = p
        else:
            hd = width // nh
            for h in range(nh):
                o_ref[:, h, :] = p[:, h * hd:(h + 1) * hd]
        off += width


def _norm_proj(x, g, w_bf16, widths, scales, tm, heads=None):
    t, d = x.shape
    n = w_bf16.shape[1]
    heads = heads or (None,) * len(widths)
    assert sum(widths) == n and t % tm == 0
    shapes = [(wd,) if nh is None else (nh, wd // nh) for wd, nh in zip(widths, heads)]
    return pl.pallas_call(
        functools.partial(_norm_proj_kernel, widths=widths, scales=scales, heads=heads),
        grid=(t // tm,),
        in_specs=[
            pl.BlockSpec((tm, d), lambda i: (i, 0)),
            pl.BlockSpec((1, d), lambda i: (0, 0)),
            pl.BlockSpec((d, n), lambda i: (0, 0)),
        ],
        out_specs=[pl.BlockSpec((tm,) + sh, lambda i, nd=len(sh): (i,) + (0,) * nd) for sh in shapes],
        out_shape=[jax.ShapeDtypeStruct((t,) + sh, F32) for sh in shapes],
        compiler_params=_cparams(("arbitrary",)),
        name="norm_proj",
    )(x, g.reshape(1, d), w_bf16)


def _in_proj_kernel(x_ref, g_ref, w_ref, wt_ref, *o_refs, widths, scales, n_t):
    hb = _rms(x_ref[...], g_ref[...]).astype(BF16)
    off = 0
    for o_ref, width, scale in zip(o_refs, widths, scales):
        p = jnp.dot(hb, w_ref[:, off:off + width], preferred_element_type=F32)
        o_ref[...] = p if scale == 1.0 else p * scale
        off += width
    for j in range(n_t):
        o_refs[len(widths) + j][0] = lax.dot_general(wt_ref[j], hb, (((1,), (1,)), ((), ())),
                                                    preferred_element_type=F32)


def _in_proj(x, g, w_bf16, wt_bf16, widths, scales, batch, seq, tm):
    t, d = x.shape
    n = w_bf16.shape[1]
    n_t, wt_width, _ = wt_bf16.shape
    nt = seq // tm
    assert sum(widths) == n and seq % tm == 0 and t == batch * seq
    return pl.pallas_call(
        functools.partial(_in_proj_kernel, widths=widths, scales=scales, n_t=n_t),
        grid=(t // tm,),
        in_specs=[
            pl.BlockSpec((tm, d), lambda i: (i, 0)),
            pl.BlockSpec((1, d), lambda i: (0, 0)),
            pl.BlockSpec((d, n), lambda i: (0, 0)),
            pl.BlockSpec(wt_bf16.shape, lambda i: (0, 0, 0)),
        ],
        out_specs=[pl.BlockSpec((tm, wd), lambda i: (i, 0)) for wd in widths]
        + [pl.BlockSpec((1, wt_width, tm), lambda i: (i // nt, 0, i % nt))] * n_t,
        out_shape=[jax.ShapeDtypeStruct((t, wd), F32) for wd in widths]
        + [jax.ShapeDtypeStruct((batch, wt_width, seq), F32)] * n_t,
        compiler_params=_cparams(("arbitrary",)),
        name="in_proj",
    )(x, g.reshape(1, d), w_bf16, wt_bf16)


def _t5_bucket_np(n, max_dist):
    max_exact = N_BUCKETS // 2
    nf = np.maximum(n, 1).astype(np.float32)
    large = max_exact + (
        np.log(nf / np.float32(max_exact)) / np.float32(math.log(max_dist / max_exact))
        * np.float32(N_BUCKETS - max_exact)
    ).astype(np.int32)
    return np.where(n < max_exact, n, np.minimum(large, N_BUCKETS - 1))


def _band_bias(rel_bias):
    max_dist = max(w for w, _ in DILATED)
    qi = np.arange(QB)[:, None]
    ki = np.arange(2 * QB)[None, :]
    j = qi + QB - ki
    tabs = []
    for window, dil in DILATED:
        steps = window // dil
        in_band = (j >= 0) & (j <= steps)
        bucket = _t5_bucket_np(np.clip(j, 0, steps) * dil, max_dist)
        onehot = (bucket[..., None] == np.arange(N_BUCKETS)).astype(np.float32)
        b = jnp.einsum("qkb,bh->hqk", onehot, rel_bias.astype(F32), precision=lax.Precision.HIGHEST)
        tabs.append(jnp.where(in_band[None], b, NEG))
    return jnp.stack(tabs)


def _step_bias(rel_bias):
    max_dist = max(w for w, _ in DILATED)
    tabs = []
    for window, dil in DILATED:
        steps = window // dil
        bucket = _t5_bucket_np(np.arange(steps, -1, -1) * dil, max_dist)
        tabs.append(rel_bias[bucket].astype(F32)[:, :, None])
    return jnp.stack(tabs)


def _decode_bias(rel_bias, n_hist, n_new):
    max_dist = max(w for w, _ in DILATED)
    t = np.arange(n_new)[:, None]
    tabs_c, tabs_n = [], []
    for window, dil in DILATED:
        out = []
        for dist in (n_hist + t - np.arange(n_hist)[None, :], t - np.arange(n_new)[None, :]):
            ok = (dist >= 0) & (dist % dil == 0) & (dist <= window)
            onehot = (_t5_bucket_np(np.clip(dist, 0, window), max_dist)[..., None] == np.arange(N_BUCKETS))
            b = jnp.einsum("tpb,bh->htp", onehot.astype(np.float32), rel_bias.astype(F32),
                           precision=lax.Precision.HIGHEST)
            out.append(jnp.where(ok[None], b, NEG))
        tabs_c.append(out[0])
        tabs_n.append(out[1])
    return jnp.stack(tabs_c), jnp.stack(tabs_n)


def _decode_attn_kernel(q_ref, kn_ref, vn_ref, knt_ref, vnt_ref, ck_ref, cv_ref, bc_ref, bn_ref,
                        o_ref, ok_ref, ov_ref, *, n_new, heads):
    n_hist = ck_ref.shape[2]
    hd = A_HEAD_DIM
    outs = []
    for h in range(heads):
        rows = slice(h * hd, (h + 1) * hd)
        q = q_ref[0, :, rows].astype(BF16)
        kt = ck_ref[0, rows, :].astype(BF16)
        vt = cv_ref[0, rows, :].astype(BF16)
        lc = jnp.dot(q, kt, preferred_element_type=F32)
        ln = lax.dot_general(q, kn_ref[0, :, rows].astype(BF16), (((1,), (1,)), ((), ())),
                             preferred_element_type=F32)
        vn = vn_ref[0, :, rows].astype(BF16)
        o_br, lse_br = [], []
        for br in range(len(DILATED)):
            bl = lc + bc_ref[br, h]
            bln = ln + bn_ref[br, h]
            m = jnp.maximum(jnp.max(bl, axis=-1, keepdims=True), jnp.max(bln, axis=-1, keepdims=True))
            p = jnp.exp(bl - m)
            pn = jnp.exp(bln - m)
            s = jnp.sum(p, axis=-1, keepdims=True) + jnp.sum(pn, axis=-1, keepdims=True)
            o = lax.dot_general(p.astype(BF16), vt, (((1,), (1,)), ((), ())), preferred_element_type=F32)
            o = o + jnp.dot(pn.astype(BF16), vn, preferred_element_type=F32)
            o_br.append(o / s)
            lse_br.append(m + jnp.log(s))
        m = jnp.maximum(jnp.maximum(lse_br[0], lse_br[1]), lse_br[2])
        es = [jnp.exp(l - m) for l in lse_br]
        outs.append((es[0] * o_br[0] + es[1] * o_br[1] + es[2] * o_br[2]) / (es[0] + es[1] + es[2]))
    o_ref[0] = jnp.concatenate(outs, axis=-1)

    lane = lax.broadcasted_iota(jnp.int32, (ck_ref.shape[1], LANES), 1)
    for c_ref, nt_ref, dst in ((ck_ref, knt_ref, ok_ref), (cv_ref, vnt_ref, ov_ref)):
        shifted = pltpu.roll(c_ref[0], n_hist - n_new, axis=1)
        tail = shifted[:, n_hist - LANES:]
        for j in range(n_new):
            tail = jnp.where(lane == LANES - n_new + j, nt_ref[0, :, j:j + 1], tail)
        dst[0, :, :n_hist - LANES] = shifted[:, :n_hist - LANES]
        dst[0, :, n_hist - LANES:] = tail


def _decode_attn(q, k_new, v_new, ck_t, cv_t, bias_c, bias_n, heads_per_step):
    bd, n_new, aw = q.shape
    n_hist = ck_t.shape[2]
    gw = heads_per_step * A_HEAD_DIM
    ng = aw // gw
    row = pl.BlockSpec((1, n_new, gw), lambda b, g: (b, 0, g))
    col = pl.BlockSpec((1, gw, n_new), lambda b, g: (b, g, 0))
    cache = pl.BlockSpec((1, gw, n_hist), lambda b, g: (b, g, 0))
    nbr = len(DILATED)
    return pl.pallas_call(
        functools.partial(_decode_attn_kernel, n_new=n_new, heads=heads_per_step),
        grid=(bd, ng),
        in_specs=[row, row, row, col, col, cache, cache,
                  pl.BlockSpec((nbr, heads_per_step, n_new, n_hist), lambda b, g: (0, g, 0, 0)),
                  pl.BlockSpec((nbr, heads_per_step, n_new, n_new), lambda b, g: (0, g, 0, 0))],
        out_specs=[row, cache, cache],
        out_shape=[jax.ShapeDtypeStruct(q.shape, F32), jax.ShapeDtypeStruct(ck_t.shape, F32),
                   jax.ShapeDtypeStruct(cv_t.shape, F32)],
        compiler_params=_cparams(("arbitrary", "arbitrary")),
        name="decode_attn",
    )(q, k_new, v_new, jnp.swapaxes(k_new, 1, 2), jnp.swapaxes(v_new, 1, 2), ck_t, cv_t, bias_c, bias_n)


def _dil_attn_kernel(q_ref, k_ref, v_ref, bias_ref, o_ref, obr_ref, lbr_ref, *, seq):
    lane = lax.broadcasted_iota(jnp.int32, (QB, LANES), 1)
    head0 = lane < A_HEAD_DIM

    def rows(ref, start, n, dil):
        if dil == 1:
            return ref[pl.ds(start, n), :]
        return ref[pl.ds(start, n, stride=dil), :]

    def block(br, dil, qstart, kstart, nk):
        qs = rows(q_ref, qstart, QB, dil)
        ks = rows(k_ref, kstart, nk, dil).astype(BF16)
        vs = rows(v_ref, kstart, nk, dil).astype(BF16)
        outs, lses = [], []
        for hh in range(2):
            keep = head0 if hh == 0 else jnp.logical_not(head0)
            qm = jnp.where(keep, qs, 0.0).astype(BF16)
            logits = lax.dot_general(qm, ks, (((1,), (1,)), ((), ())), preferred_element_type=F32)
            logits = logits + bias_ref[br, hh, :, 2 * QB - nk:]
            m = jnp.max(logits, axis=-1, keepdims=True)
            p = jnp.exp(logits - m)
            s = jnp.sum(p, axis=-1, keepdims=True)
            o = jnp.dot(p.astype(BF16), vs, preferred_element_type=F32)
            outs.append(o / s)
            lses.append(jnp.broadcast_to(m + jnp.log(s), (QB, LANES)))
        o = jnp.where(head0, outs[0], outs[1])
        lse = jnp.where(head0, lses[0], lses[1])
        if dil == 1:
            obr_ref[br, pl.ds(qstart, QB), :] = o
            lbr_ref[br, pl.ds(qstart, QB), :] = lse
        else:
            obr_ref[br, pl.ds(qstart, QB, stride=dil), :] = o
            lbr_ref[br, pl.ds(qstart, QB, stride=dil), :] = lse

    for br, (window, dil) in enumerate(DILATED):
        assert window // dil == QB
        nblk = seq // (dil * QB)
        for r in range(dil):
            block(br, dil, r, r, QB)
            if nblk > 1:
                def body(i, carry, br=br, dil=dil, r=r):
                    qstart = r + dil * QB * i
                    if dil == 1:
                        qstart = pl.multiple_of(qstart, QB)
                    block(br, dil, qstart, qstart - dil * QB, 2 * QB)
                    return carry
                lax.fori_loop(1, nblk, body, 0)

    l0, l1, l2 = lbr_ref[0], lbr_ref[1], lbr_ref[2]
    m = jnp.maximum(jnp.maximum(l0, l1), l2)
    e0, e1, e2 = jnp.exp(l0 - m), jnp.exp(l1 - m), jnp.exp(l2 - m)
    acc = e0 * obr_ref[0] + e1 * obr_ref[1] + e2 * obr_ref[2]
    o_ref[...] = acc / (e0 + e1 + e2)


def _dil_attn(q, k, v, band_bias, batch, seq):
    t, aw = q.shape
    npair = aw // LANES
    bias = band_bias.reshape(len(DILATED), npair, 2, QB, 2 * QB)
    spec = pl.BlockSpec((seq, LANES), lambda b, hp: (b, hp))
    return pl.pallas_call(
        functools.partial(_dil_attn_kernel, seq=seq),
        grid=(batch, npair),
        in_specs=[
            spec, spec, spec,
            pl.BlockSpec((len(DILATED), None, 2, QB, 2 * QB), lambda b, hp: (0, hp, 0, 0, 0)),
        ],
        out_specs=spec,
        out_shape=jax.ShapeDtypeStruct((t, aw), F32),
        scratch_shapes=[
            pltpu.VMEM((len(DILATED), seq, LANES), F32),
            pltpu.VMEM((len(DILATED), seq, LANES), F32),
        ],
        compiler_params=_cparams(("arbitrary", "arbitrary")),
        name="dil_attn",
    )(q, k, v, bias)


def _dil_attn_dec_kernel(q_ref, kn_ref, vn_ref, k1_ref, v1_ref, k2_ref, v2_ref, k3_ref, v3_ref,
                         rbias_ref, o_ref, *, n_new):
    kn = kn_ref[0]
    vn = vn_ref[0]
    for t in range(n_new):
        q = q_ref[0, t][None]
        outs, lses = [], []
        for br, (window, dil) in enumerate(DILATED):
            if dil == 1:
                kc, vc = k1_ref[0, t:], v1_ref[0, t:]
                bc = rbias_ref[br, 0:QB - t]
                kn_t, vn_t = kn[:t + 1], vn[:t + 1]
                bn = rbias_ref[br, QB - t:QB + 1]
            else:
                kref, vref = (k2_ref, v2_ref) if br == 1 else (k3_ref, v3_ref)
                kc, vc = kref[0, :, t], vref[0, :, t]
                bc = rbias_ref[br, 0:QB]
                kn_t, vn_t = kn[t:t + 1], vn[t:t + 1]
                bn = rbias_ref[br, QB:QB + 1]
            lc = jnp.sum(kc * q, axis=-1, keepdims=True) + bc
            ln = jnp.sum(kn_t * q, axis=-1, keepdims=True) + bn
            m = jnp.maximum(jnp.max(lc, axis=0, keepdims=True), jnp.max(ln, axis=0, keepdims=True))
            pc = jnp.exp(lc - m)
            pn = jnp.exp(ln - m)
            s = jnp.sum(pc, axis=0, keepdims=True) + jnp.sum(pn, axis=0, keepdims=True)
            o = jnp.sum(pc * vc, axis=0, keepdims=True) + jnp.sum(pn * vn_t, axis=0, keepdims=True)
            outs.append(o / s)
            lses.append(m + jnp.log(s))
        m = jnp.maximum(jnp.maximum(lses[0], lses[1]), lses[2])
        es = [jnp.exp(l - m) for l in lses]
        acc = es[0] * outs[0] + es[1] * outs[1] + es[2] * outs[2]
        o_ref[0, t] = (acc / (es[0] + es[1] + es[2]))[0]


def _dil_attn_dec(q, k_new, v_new, cache_k, cache_v, step_bias):
    bd, n_new, h, dh = q.shape
    n_hist = cache_k.shape[1]
    (w1, d1), (w2, d2), (w3, d3) = DILATED
    assert d1 == 1 and n_hist >= w3 and n_new <= d2 and n_hist % d3 == 0 and n_hist % d2 == 0
    assert w1 // d1 == QB and w2 // d2 == QB and w3 // d3 == QB

    def views(c):
        c2 = c.reshape(bd, n_hist // d2, d2, h, dh)
        c3 = c.reshape(bd, n_hist // d3, d3, h, dh)
        return c, c2, c3

    k1, k2, k3 = views(cache_k)
    v1, v2, v3 = views(cache_v)
    new_spec = pl.BlockSpec((1, n_new, h, dh), lambda b: (b, 0, 0, 0))
    s1 = pl.BlockSpec((1, QB, h, dh), lambda b: (b, n_hist // QB - 1, 0, 0))
    s2 = pl.BlockSpec((1, QB, n_new, h, dh), lambda b: (b, n_hist // d2 // QB - 1, 0, 0, 0))
    s3 = pl.BlockSpec((1, QB, n_new, h, dh), lambda b: (b, n_hist // d3 // QB - 1, 0, 0, 0))
    return pl.pallas_call(
        functools.partial(_dil_attn_dec_kernel, n_new=n_new),
        grid=(bd,),
        in_specs=[new_spec, new_spec, new_spec, s1, s1, s2, s2, s3, s3,
                  pl.BlockSpec(step_bias.shape, lambda b: (0, 0, 0, 0))],
        out_specs=new_spec,
        out_shape=jax.ShapeDtypeStruct(q.shape, F32),
        compiler_params=_cparams(("arbitrary",)),
        name="dil_attn_dec",
    )(q, k_new, v_new, k1, v1, k2, v2, k3, v3, step_bias)


def _pool_kernel(hist_ref, cur_ref, w_ref, scale_ref, o_ref, *, pos0_of_tile, tm):
    nb = cur_ref.shape[0]
    gdim = w_ref.shape[1]
    i = pl.program_id(1)
    pos0 = pos0_of_tile(i)
    have_hist = jnp.where(pos0 > 0, 1.0, 0.0).astype(F32)
    t = lax.broadcasted_iota(jnp.int32, (1, tm, 1), 1)
    for g, w in enumerate(POOL_WINDOWS):
        sl = slice(g * gdim, (g + 1) * gdim)
        cur = cur_ref[:, :, sl]
        ext = jnp.concatenate([hist_ref[:, :, sl] * have_hist, cur], axis=1)
        acc, span = ext, 1
        while span < w:
            n = acc.shape[1]
            acc = acc[:, span:n] + acc[:, 0:n - span]
            span *= 2
        wsum = acc[:, POOL_HIST + 1 - w:POOL_HIST + 1 - w + tm]
        cnt = jnp.minimum(pos0 + t + 1, w).astype(F32)
        d = (wsum / cnt - cur).astype(BF16).reshape(nb * tm, gdim)
        y = jnp.dot(d, w_ref[g], preferred_element_type=F32) * scale_ref[:, sl]
        o_ref[:, :, sl] = y.reshape(nb, tm, gdim)


def _pool_prompt(u, w_pool_bf16, pool_scale, batch, seq, tm):
    t, bw = u.shape
    nt = seq // tm
    hb = tm // POOL_HIST
    u3 = u.reshape(1, t, bw)
    out = pl.pallas_call(
        functools.partial(_pool_kernel, pos0_of_tile=lambda i: i * tm, tm=tm),
        grid=(batch, nt),
        in_specs=[
            pl.BlockSpec((1, POOL_HIST, bw), lambda b, i: (0, jnp.maximum((b * nt + i) * hb - 1, 0), 0)),
            pl.BlockSpec((1, tm, bw), lambda b, i: (0, b * nt + i, 0)),
            pl.BlockSpec(w_pool_bf16.shape, lambda b, i: (0, 0, 0)),
            pl.BlockSpec((1, bw), lambda b, i: (0, 0)),
        ],
        out_specs=pl.BlockSpec((1, tm, bw), lambda b, i: (0, b * nt + i, 0)),
        out_shape=jax.ShapeDtypeStruct((1, t, bw), F32),
        compiler_params=_cparams(("arbitrary", "arbitrary")),
        name="pool_prompt",
    )(u3, u3, w_pool_bf16, pool_scale.reshape(1, bw))
    return out.reshape(t, bw)


def _pool_dec(hist, cur, n_prev, w_pool_bf16, pool_scale):
    bd, tm, bw = cur.shape
    return pl.pallas_call(
        functools.partial(_pool_kernel, pos0_of_tile=lambda i: n_prev, tm=tm),
        grid=(1, 1),
        in_specs=[
            pl.BlockSpec(hist.shape, lambda b, i: (0, 0, 0)),
            pl.BlockSpec(cur.shape, lambda b, i: (0, 0, 0)),
            pl.BlockSpec(w_pool_bf16.shape, lambda b, i: (0, 0, 0)),
            pl.BlockSpec((1, bw), lambda b, i: (0, 0)),
        ],
        out_specs=pl.BlockSpec(cur.shape, lambda b, i: (0, 0, 0)),
        out_shape=jax.ShapeDtypeStruct(cur.shape, F32),
        compiler_params=_cparams(("arbitrary", "arbitrary")),
        name="pool_dec",
    )(hist, cur, w_pool_bf16, pool_scale.reshape(1, bw))


def _mix_xattn_kernel(x_ref, oa_ref, ob_ref, mk_ref, mv_ref, woa_ref, wob_ref, gx_ref, wq_ref, wo_ref,
                      gff_ref, wr_ref, br_ref, x2_ref, h_ref, lg_ref, *, rows_per_sub, keys_per_sub):
    tm = x_ref.shape[0]
    nkv = mk_ref.shape[0]
    xh = wq_ref.shape[1] // X_HEADS
    x = x_ref[...]
    x = x + jnp.dot(oa_ref[...].astype(BF16), woa_ref[...], preferred_element_type=F32)
    x = x + jnp.dot(ob_ref[...].astype(BF16), wob_ref[...], preferred_element_type=F32)
    hq = _rms(x, gx_ref[...]).astype(BF16)
    if rows_per_sub is not None:
        rsub = (pl.program_id(1) * tm + lax.broadcasted_iota(jnp.int32, (tm, nkv), 0)) // rows_per_sub
        ksub = lax.broadcasted_iota(jnp.int32, (tm, nkv), 1) // keys_per_sub
        same = rsub == ksub
    heads = []
    for h in range(X_HEADS):
        sl = slice(h * xh, (h + 1) * xh)
        q = jnp.dot(hq, wq_ref[:, sl], preferred_element_type=F32) * (xh ** -0.5)
        kh = mk_ref[:, h, :].astype(BF16)
        logits = lax.dot_general(q.astype(BF16), kh, (((1,), (1,)), ((), ())), preferred_element_type=F32)
        if rows_per_sub is not None:
            logits = jnp.where(same, logits, NEG)
        m = jnp.max(logits, axis=-1, keepdims=True)
        p = jnp.exp(logits - m)
        s = jnp.sum(p, axis=-1, keepdims=True)
        o = jnp.dot(p.astype(BF16), mv_ref[:, h, :].astype(BF16), preferred_element_type=F32) / s
        heads.append(o.astype(BF16))
    x = x + jnp.dot(jnp.concatenate(heads, axis=-1), wo_ref[...], preferred_element_type=F32)
    x2_ref[...] = x
    hf = _rms(x, gff_ref[...])
    h_ref[...] = hf
    lg_ref[...] = jnp.dot(hf.astype(BF16), wr_ref[...].astype(BF16), preferred_element_type=F32) + br_ref[...]


def _mix_xattn(x, oa, ob, mk, mv, w, groups, rows_per_group, tm, rows_per_sub, keys_per_sub):
    t, d = x.shape
    nt = rows_per_group // tm
    nkv = mk.shape[0] // groups
    ne = w["w_router"].shape[1]
    row = lambda width: pl.BlockSpec((tm, width), lambda g, i: (g * nt + i, 0))
    const = lambda a: pl.BlockSpec(a.shape, lambda g, i: (0,) * a.ndim)
    kv = pl.BlockSpec((nkv,) + mk.shape[1:], lambda g, i: (g, 0, 0))
    consts = [w["w_out_a"], w["w_out_b"], w["g_x"], w["w_xq"], w["w_xo"], w["g_ff"], w["w_router"], w["b_router"]]
    return pl.pallas_call(
        functools.partial(_mix_xattn_kernel, rows_per_sub=rows_per_sub, keys_per_sub=keys_per_sub),
        grid=(groups, nt),
        in_specs=[row(d), row(oa.shape[1]), row(ob.shape[1]), kv, kv] + [const(a) for a in consts],
        out_specs=[row(d), row(d), row(ne)],
        out_shape=[jax.ShapeDtypeStruct((t, d), F32), jax.ShapeDtypeStruct((t, d), F32),
                   jax.ShapeDtypeStruct((t, ne), F32)],
        compiler_params=_cparams(("arbitrary", "arbitrary")),
        name="mix_xattn",
    )(x, oa, ob, mk, mv, *consts)


def _moe_gather(h_hbm, tok_ref, buf, sem, slot):
    def body(i, carry):
        pltpu.make_async_copy(h_hbm.at[pl.ds(tok_ref[0, 0, i], 1)], buf.at[slot, pl.ds(i, 1)],
                              sem.at[slot]).start()
        return carry
    lax.fori_loop(0, MOE_BM, body, 0)


def _moe_ffn_kernel(be_ref, nu_ref, tok_ref, tokn_ref, h_hbm, wgu_ref, bgu_ref, wd_ref, bd_ref, o_ref,
                    buf, sem, wgu_bf, wd_bf):
    b = pl.program_id(0)
    n_used = nu_ref[0]
    slot = b % 2
    dff = wd_ref.shape[1]

    @pl.when(b == 0)
    def _():
        _moe_gather(h_hbm, tok_ref, buf, sem, 0)

    @pl.when(b + 1 < n_used)
    def _():
        _moe_gather(h_hbm, tokn_ref, buf, sem, 1 - slot)

    new_expert = jnp.logical_or(b == 0, be_ref[b] != be_ref[jnp.maximum(b - 1, 0)])

    @pl.when(jnp.logical_and(b < n_used, new_expert))
    def _():
        wgu_bf[...] = wgu_ref[0].astype(BF16)
        wd_bf[...] = wd_ref[0].astype(BF16)

    @pl.when(b < n_used)
    def _():
        pltpu.make_async_copy(h_hbm.at[pl.ds(0, MOE_BM)], buf.at[slot], sem.at[slot]).wait()
        xb = buf[slot].astype(BF16)
        hu = jnp.dot(xb, wgu_bf[...], preferred_element_type=F32) + bgu_ref[0]
        g = jnp.minimum(hu[:, :dff], SWIGLU_LIMIT)
        u = jnp.clip(hu[:, dff:], -SWIGLU_LIMIT, SWIGLU_LIMIT)
        a = g * jax.nn.sigmoid(SWIGLU_ALPHA * g) * (u + 1.0)
        o_ref[...] = jnp.dot(a.astype(BF16), wd_bf[...], preferred_element_type=F32) + bd_ref[0]

    @pl.when(b >= n_used)
    def _():
        o_ref[...] = jnp.zeros_like(o_ref)


def _moe_ffn(h, slot_tok, block_e, n_used, w_gate_up, b_gate_up, w_down, b_down):
    t, d = h.shape
    ne, _, dff2 = w_gate_up.shape
    dff = dff2 // 2
    nb = block_e.shape[0]
    tok3 = slot_tok.reshape(nb, 1, MOE_BM)
    grid_spec = pltpu.PrefetchScalarGridSpec(
        num_scalar_prefetch=2,
        grid=(nb,),
        in_specs=[
            pl.BlockSpec((1, 1, MOE_BM), lambda b, be, nu: (b, 0, 0), memory_space=pltpu.SMEM),
            pl.BlockSpec((1, 1, MOE_BM), lambda b, be, nu: (jnp.minimum(b + 1, nb - 1), 0, 0),
                         memory_space=pltpu.SMEM),
            pl.BlockSpec(memory_space=pl.ANY),
            pl.BlockSpec((1, d, dff2), lambda b, be, nu: (be[b], 0, 0)),
            pl.BlockSpec((1, 1, dff2), lambda b, be, nu: (be[b], 0, 0)),
            pl.BlockSpec((1, dff, d), lambda b, be, nu: (be[b], 0, 0)),
            pl.BlockSpec((1, 1, d), lambda b, be, nu: (be[b], 0, 0)),
        ],
        out_specs=pl.BlockSpec((MOE_BM, d), lambda b, be, nu: (b, 0)),
        scratch_shapes=[
            pltpu.VMEM((2, MOE_BM, d), F32),
            pltpu.SemaphoreType.DMA((2,)),
            pltpu.VMEM((d, dff2), BF16),
            pltpu.VMEM((dff, d), BF16),
        ],
    )
    return pl.pallas_call(
        _moe_ffn_kernel,
        grid_spec=grid_spec,
        out_shape=jax.ShapeDtypeStruct((nb * MOE_BM, d), F32),
        compiler_params=_cparams(("arbitrary",)),
        name="moe_ffn",
    )(block_e, n_used, tok3, tok3, h, w_gate_up, b_gate_up.reshape(ne, 1, dff2), w_down,
      b_down.reshape(ne, 1, d))


def _combine_gather(y_hbm, pos_ref, buf, sem, slot):
    def body(i, carry):
        pltpu.make_async_copy(y_hbm.at[pl.ds(pos_ref[0, 0, i], 1)], buf.at[slot, pl.ds(i, 1)],
                              sem.at[slot]).start()
        return carry
    lax.fori_loop(0, TOP_K * COMBINE_TQ, body, 0)


def _combine_kernel(pos_ref, posn_ref, y_hbm, x_ref, gate_ref, g_ref, o_ref, buf, sem, *, n_tiles):
    i = pl.program_id(0)
    slot = i % 2

    @pl.when(i == 0)
    def _():
        _combine_gather(y_hbm, pos_ref, buf, sem, 0)

    if n_tiles > 1:
        @pl.when(i + 1 < n_tiles)
        def _():
            _combine_gather(y_hbm, posn_ref, buf, sem, 1 - slot)

    pltpu.make_async_copy(y_hbm.at[pl.ds(0, TOP_K * COMBINE_TQ)], buf.at[slot], sem.at[slot]).wait()
    x = x_ref[...]
    gates = gate_ref[...]
    for k in range(TOP_K):
        x = x + gates[:, k:k + 1] * buf[slot, k * COMBINE_TQ:(k + 1) * COMBINE_TQ, :]
    o_ref[...] = _rms(x, g_ref[...])


def _combine(y_sorted, x2, pos, gates, g_final):
    t, d = x2.shape
    tq = COMBINE_TQ
    nt = t // tq
    pos3 = pos.reshape(nt, tq, TOP_K).transpose(0, 2, 1).reshape(nt, 1, TOP_K * tq)
    return pl.pallas_call(
        functools.partial(_combine_kernel, n_tiles=nt),
        grid=(nt,),
        in_specs=[
            pl.BlockSpec((1, 1, TOP_K * tq), lambda i: (i, 0, 0), memory_space=pltpu.SMEM),
            pl.BlockSpec((1, 1, TOP_K * tq), lambda i: (jnp.minimum(i + 1, nt - 1), 0, 0),
                         memory_space=pltpu.SMEM),
            pl.BlockSpec(memory_space=pl.ANY),
            pl.BlockSpec((tq, d), lambda i: (i, 0)),
            pl.BlockSpec((tq, TOP_K), lambda i: (i, 0)),
            pl.BlockSpec((1, d), lambda i: (0, 0)),
        ],
        out_specs=pl.BlockSpec((tq, d), lambda i: (i, 0)),
        out_shape=jax.ShapeDtypeStruct((t, d), F32),
        scratch_shapes=[pltpu.VMEM((2, TOP_K * tq, d), F32), pltpu.SemaphoreType.DMA((2,))],
        compiler_params=_cparams(("arbitrary",)),
        name="combine",
    )(pos3, pos3, y_sorted, x2, gates, g_final.reshape(1, d))


def _cache_shift_kernel(ck_ref, cv_ref, kn_ref, vn_ref, ok_ref, ov_ref, sem, *, n_new):
    bd, n_hist = ck_ref.shape[0], ck_ref.shape[1]
    keep = n_hist - n_new
    copies = []
    for src, new, dst in ((ck_ref, kn_ref, ok_ref), (cv_ref, vn_ref, ov_ref)):
        for b in range(bd):
            copies.append(pltpu.make_async_copy(src.at[b, pl.ds(n_new, keep)], dst.at[b, pl.ds(0, keep)], sem.at[0]))
        copies.append(pltpu.make_async_copy(new, dst.at[:, pl.ds(keep, n_new)], sem.at[1]))
    for c in copies:
        c.start()
    for c in copies:
        c.wait()


def _cache_shift(cache_k, cache_v, k_new, v_new):
    n_new = k_new.shape[1]
    any_spec = pl.BlockSpec(memory_space=pl.ANY)
    return pl.pallas_call(
        functools.partial(_cache_shift_kernel, n_new=n_new),
        in_specs=[any_spec] * 4,
        out_specs=[any_spec] * 2,
        out_shape=[jax.ShapeDtypeStruct(cache_k.shape, cache_k.dtype)] * 2,
        scratch_shapes=[pltpu.SemaphoreType.DMA((2,))],
        name="cache_shift",
    )(cache_k, cache_v, k_new, v_new)


def _route(logits):
    t = logits.shape[0]
    top_v, top_i = lax.top_k(logits, TOP_K)
    gates = jax.nn.softmax(top_v, axis=-1)
    onehot = (top_i[:, :, None] == jnp.arange(N_EXPERTS, dtype=top_i.dtype)[None, None, :])
    per_tok = jnp.sum(onehot, axis=1).astype(jnp.int32)
    before = jnp.cumsum(per_tok, axis=0) - per_tok
    counts = jnp.sum(per_tok, axis=0)
    nblk = (counts + MOE_BM - 1) // MOE_BM
    blk_end = jnp.cumsum(nblk)
    pstart = (blk_end - nblk) * MOE_BM
    row_of = (pstart[None, :] + before)[:, None, :]
    pos = jnp.sum(jnp.where(onehot, row_of, 0), axis=-1).astype(jnp.int32)
    nb = (t * TOP_K + N_EXPERTS * (MOE_BM - 1) + MOE_BM - 1) // MOE_BM
    n_used = blk_end[-1].astype(jnp.int32)
    tok = jnp.broadcast_to(jnp.arange(t, dtype=jnp.int32)[:, None], (t, TOP_K))
    slot_tok = jnp.zeros((nb * MOE_BM,), jnp.int32).at[pos.reshape(-1)].set(tok.reshape(-1))
    blk = jnp.minimum(jnp.arange(nb, dtype=jnp.int32), n_used - 1)
    block_e = jnp.sum(blk_end[None, :] <= blk[:, None], axis=1).astype(jnp.int32)
    return gates, pos, slot_tok, block_e, n_used.reshape(1)


ROUTE_TT = 128
STRIP_BITS = tuple(1 << j for j in range(ROUTE_TT.bit_length()))
PAD_BITS = tuple(1 << j for j in range((MOE_BM - 1).bit_length()))


def _strip_dmas(n, bits, make_copy, wait=False):
    for bit in bits:
        @pl.when((n & bit) != 0)
        def _(bit=bit):
            c = make_copy(n & (bit - 1), bit)
            if wait:
                c.wait()
            else:
                c.start()


def _dispatch_kernel(d_ref, n_ref, o_ref, ps_ref, pn_ref, lst_ref, hp_ref, hs_ref, xs_hbm, buf, zbuf, sem, psem,
                     *, n_tiles, n_prompt_tiles):
    i = pl.program_id(0)
    slot = i % 2
    rows = TOP_K * ROUTE_TT

    def wait_tile(s):
        pltpu.make_async_copy(xs_hbm.at[pl.ds(0, rows)], buf.at[s], sem.at[s]).wait()

    def put_rows(s, val):
        for c in range(val.shape[1] // LANES):
            buf[s, :, c, :] = val[:, c * LANES:(c + 1) * LANES]

    def pad_strips(wait):
        for e in range(N_EXPERTS):
            start = ps_ref[e]
            _strip_dmas(pn_ref[e], PAD_BITS, lambda off, size: pltpu.make_async_copy(
                zbuf.at[pl.ds(0, size)], xs_hbm.at[pl.ds(start + off, size)], psem.at[0]), wait)
        chunk = zbuf.shape[0]
        used_rows = ps_ref[N_EXPERTS - 1] + pn_ref[N_EXPERTS - 1]

        def body(c, carry):
            cp = pltpu.make_async_copy(zbuf, xs_hbm.at[pl.ds(c * chunk, chunk)], psem.at[0])
            if wait:
                cp.wait()
            else:
                cp.start()
            return carry
        lax.fori_loop(used_rows // chunk, xs_hbm.shape[0] // chunk, body, 0)

    @pl.when(i >= 2)
    def _():
        wait_tile(slot)

    @pl.when(i == 0)
    def _():
        zbuf[...] = jnp.zeros_like(zbuf)
        pad_strips(False)

    srow = lax.broadcasted_iota(jnp.int32, (rows, ROUTE_TT), 0)
    place = srow == lst_ref[0:1, :]
    for k in range(1, TOP_K):
        place = jnp.logical_or(place, srow == lst_ref[k:k + 1, :])
    x = jnp.where(i < n_prompt_tiles, hp_ref[...], hs_ref[...]).astype(BF16)
    put_rows(slot, jnp.dot(jnp.where(place, 1.0, 0.0).astype(BF16), x, preferred_element_type=F32))
    for e in range(N_EXPERTS):
        src0 = o_ref[i * N_EXPERTS + e]
        dst0 = d_ref[i * N_EXPERTS + e]
        _strip_dmas(n_ref[i * N_EXPERTS + e], STRIP_BITS, lambda off, size: pltpu.make_async_copy(
            buf.at[slot, pl.ds(src0 + off, size)], xs_hbm.at[pl.ds(dst0 + off, size)], sem.at[slot]))

    @pl.when(i == n_tiles - 1)
    def _():
        wait_tile(slot)
        if n_tiles > 1:
            wait_tile(1 - slot)
        pad_strips(True)


def _dispatch(hp, hs, ls_t, d_tile, n_tile, o_tile, pad_start, pad_len, n_rows):
    tp, d = hp.shape
    ts = hs.shape[0]
    npt, nst = tp // ROUTE_TT, ts // ROUTE_TT
    assert nst == 1 and tp % ROUTE_TT == 0 and ts % ROUTE_TT == 0
    nt = npt + nst
    grid_spec = pltpu.PrefetchScalarGridSpec(
        num_scalar_prefetch=5,
        grid=(nt,),
        in_specs=[
            pl.BlockSpec((TOP_K, ROUTE_TT), lambda i, *_: (0, i)),
            pl.BlockSpec((ROUTE_TT, d), lambda i, *_: (jnp.minimum(i, npt - 1), 0)),
            pl.BlockSpec((ROUTE_TT, d), lambda i, *_: (0, 0)),
        ],
        out_specs=pl.BlockSpec(memory_space=pl.ANY),
        scratch_shapes=[
            pltpu.VMEM((2, TOP_K * ROUTE_TT, d // LANES, LANES), F32),
            pltpu.VMEM((PAD_BITS[-1], d // LANES, LANES), F32),
            pltpu.SemaphoreType.DMA((2,)),
            pltpu.SemaphoreType.DMA((1,)),
        ],
    )
    return pl.pallas_call(
        functools.partial(_dispatch_kernel, n_tiles=nt, n_prompt_tiles=npt),
        grid_spec=grid_spec,
        out_shape=jax.ShapeDtypeStruct((n_rows, d // LANES, LANES), F32),
        compiler_params=_cparams(("arbitrary",)),
        name="moe_dispatch",
    )(d_tile.reshape(-1), n_tile.reshape(-1), o_tile.reshape(-1), pad_start, pad_len, ls_t, hp, hs)


def _expert_ffn_kernel(be_ref, nu_ref, x_ref, wgu_ref, bgu_ref, wd_ref, bd_ref, o_ref, wgu_bf, wd_bf):
    b = pl.program_id(0)
    dff = wd_ref.shape[1]
    new_expert = jnp.logical_or(b == 0, be_ref[b] != be_ref[jnp.maximum(b - 1, 0)])

    @pl.when(jnp.logical_and(b < nu_ref[0], new_expert))
    def _():
        wgu_bf[...] = wgu_ref[0].astype(BF16)
        wd_bf[...] = wd_ref[0].astype(BF16)

    @pl.when(b < nu_ref[0])
    def _():
        x = jnp.concatenate([x_ref[:, c, :] for c in range(x_ref.shape[1])], axis=-1).astype(BF16)
        hu = jnp.dot(x, wgu_bf[...], preferred_element_type=F32) + bgu_ref[0]
        g = jnp.minimum(hu[:, :dff], SWIGLU_LIMIT)
        u = jnp.clip(hu[:, dff:], -SWIGLU_LIMIT, SWIGLU_LIMIT)
        a = g * jax.nn.sigmoid(SWIGLU_ALPHA * g) * (u + 1.0)
        out = jnp.dot(a.astype(BF16), wd_bf[...], preferred_element_type=F32) + bd_ref[0]
        for c in range(o_ref.shape[1]):
            o_ref[:, c, :] = out[:, c * LANES:(c + 1) * LANES]

    @pl.when(b >= nu_ref[0])
    def _():
        o_ref[...] = jnp.zeros_like(o_ref)


def _expert_ffn(x_sorted, block_e, n_used, w_gate_up, b_gate_up, w_down, b_down):
    n_rows = x_sorted.shape[0]
    ne, d, dff2 = w_gate_up.shape
    dff = dff2 // 2
    nb = n_rows // MOE_BM
    used = lambda b, nu: jnp.minimum(b, nu[0] - 1)
    grid_spec = pltpu.PrefetchScalarGridSpec(
        num_scalar_prefetch=2,
        grid=(nb,),
        in_specs=[
            pl.BlockSpec((MOE_BM, d // LANES, LANES), lambda b, be, nu: (used(b, nu), 0, 0)),
            pl.BlockSpec((1, d, dff2), lambda b, be, nu: (be[b], 0, 0)),
            pl.BlockSpec((1, 1, dff2), lambda b, be, nu: (be[b], 0, 0)),
            pl.BlockSpec((1, dff, d), lambda b, be, nu: (be[b], 0, 0)),
            pl.BlockSpec((1, 1, d), lambda b, be, nu: (be[b], 0, 0)),
        ],
        out_specs=pl.BlockSpec((MOE_BM, d // LANES, LANES), lambda b, be, nu: (b, 0, 0)),
        scratch_shapes=[pltpu.VMEM((d, dff2), BF16), pltpu.VMEM((dff, d), BF16)],
    )
    return pl.pallas_call(
        _expert_ffn_kernel,
        grid_spec=grid_spec,
        out_shape=jax.ShapeDtypeStruct(x_sorted.shape, F32),
        compiler_params=_cparams(("arbitrary",)),
        name="expert_ffn",
    )(block_e, n_used, x_sorted, w_gate_up, b_gate_up.reshape(ne, 1, dff2), w_down, b_down.reshape(ne, 1, d))


def _moe_combine_kernel(d_ref, n_ref, o_ref, ls_ref, gate_ref, xp_ref, xs_ref, g_ref, y_hbm, yp_ref, ys_ref,
                        buf, sem, *, n_tiles, n_prompt_tiles):
    j = pl.program_id(0)
    rows = TOP_K * ROUTE_TT

    @pl.when(j < n_tiles)
    def _():
        slot = j % 2
        for e in range(N_EXPERTS):
            src0 = d_ref[j * N_EXPERTS + e]
            dst0 = o_ref[j * N_EXPERTS + e]
            _strip_dmas(n_ref[j * N_EXPERTS + e], STRIP_BITS, lambda off, size: pltpu.make_async_copy(
                y_hbm.at[pl.ds(src0 + off, size)], buf.at[slot, pl.ds(dst0 + off, size)], sem.at[slot]))

    @pl.when(j >= 1)
    def _():
        slot = (j - 1) % 2
        pltpu.make_async_copy(y_hbm.at[pl.ds(0, rows)], buf.at[slot], sem.at[slot]).wait()
        r = jnp.concatenate([buf[slot, :, c, :] for c in range(buf.shape[2])], axis=-1)
        r_hi = r.astype(BF16)
        r_lo = (r - r_hi.astype(F32)).astype(BF16)
        lane = lax.broadcasted_iota(jnp.int32, (ROUTE_TT, rows), 1)
        pick = jnp.zeros((ROUTE_TT, rows), F32)
        for k in range(TOP_K):
            pick = pick + jnp.where(lane == ls_ref[:, k:k + 1], gate_ref[:, k:k + 1], 0.0)
        p_hi = pick.astype(BF16)
        p_lo = (pick - p_hi.astype(F32)).astype(BF16)
        moe = (jnp.dot(p_hi, r_hi, preferred_element_type=F32) + jnp.dot(p_lo, r_hi, preferred_element_type=F32)
               + jnp.dot(p_hi, r_lo, preferred_element_type=F32))
        is_prompt = j - 1 < n_prompt_tiles
        y = _rms(jnp.where(is_prompt, xp_ref[...], xs_ref[...]) + moe, g_ref[...])

        @pl.when(is_prompt)
        def _():
            yp_ref[...] = y

        @pl.when(jnp.logical_not(is_prompt))
        def _():
            ys_ref[...] = y


def _moe_combine(y_sorted, x2p, x2s, ls, gates, d_tile, n_tile, o_tile, g_final):
    tp, d = x2p.shape
    ts = x2s.shape[0]
    npt, nst = tp // ROUTE_TT, ts // ROUTE_TT
    assert nst == 1
    nt = npt + nst
    tile = lambda j: jnp.clip(j - 1, 0, nt - 1)
    ptile = lambda j: jnp.clip(j - 1, 0, npt - 1)
    grid_spec = pltpu.PrefetchScalarGridSpec(
        num_scalar_prefetch=3,
        grid=(nt + 1,),
        in_specs=[
            pl.BlockSpec((ROUTE_TT, TOP_K), lambda j, *_: (tile(j), 0)),
            pl.BlockSpec((ROUTE_TT, TOP_K), lambda j, *_: (tile(j), 0)),
            pl.BlockSpec((ROUTE_TT, d), lambda j, *_: (ptile(j), 0)),
            pl.BlockSpec((ROUTE_TT, d), lambda j, *_: (0, 0)),
            pl.BlockSpec((1, d), lambda j, *_: (0, 0)),
            pl.BlockSpec(memory_space=pl.ANY),
        ],
        out_specs=[
            pl.BlockSpec((ROUTE_TT, d), lambda j, *_: (ptile(j), 0)),
            pl.BlockSpec((ROUTE_TT, d), lambda j, *_: (0, 0)),
        ],
        scratch_shapes=[pltpu.VMEM((2, TOP_K * ROUTE_TT, d // LANES, LANES), F32), pltpu.SemaphoreType.DMA((2,))],
    )
    return pl.pallas_call(
        functools.partial(_moe_combine_kernel, n_tiles=nt, n_prompt_tiles=npt),
        grid_spec=grid_spec,
        out_shape=[jax.ShapeDtypeStruct((tp, d), F32), jax.ShapeDtypeStruct((ts, d), F32)],
        compiler_params=_cparams(("arbitrary",)),
        name="moe_combine",
    )(d_tile.reshape(-1), n_tile.reshape(-1), o_tile.reshape(-1), ls, gates, x2p, x2s, g_final.reshape(1, d),
      y_sorted)


def _route_tiles(logits):
    t = logits.shape[0]
    nt = t // ROUTE_TT
    top_v, top_i = lax.top_k(logits, TOP_K)
    gates = jax.nn.softmax(top_v, axis=-1)
    onehot = (top_i[:, :, None] == jnp.arange(N_EXPERTS, dtype=top_i.dtype)[None, None, :])
    per_tok = jnp.sum(onehot, axis=1).astype(jnp.int32)
    before = jnp.cumsum(per_tok, axis=0) - per_tok
    counts = jnp.sum(per_tok, axis=0)
    nblk = (counts + MOE_BM - 1) // MOE_BM
    blk_end = jnp.cumsum(nblk)
    pstart = (blk_end - nblk) * MOE_BM
    n_used = blk_end[-1].astype(jnp.int32)
    nb = (t * TOP_K + N_EXPERTS * (MOE_BM - 1) + MOE_BM - 1) // MOE_BM
    blk = jnp.minimum(jnp.arange(nb, dtype=jnp.int32), n_used - 1)
    block_e = jnp.sum(blk_end[None, :] <= blk[:, None], axis=1).astype(jnp.int32)
    before_tile = before[::ROUTE_TT]
    n_tile = jnp.sum(per_tok.reshape(nt, ROUTE_TT, N_EXPERTS), axis=1)
    o_tile = jnp.cumsum(n_tile, axis=1) - n_tile
    d_tile = pstart[None, :] + before_tile
    local = (o_tile - before_tile)[:, None, :] + before.reshape(nt, ROUTE_TT, N_EXPERTS)
    ls = jnp.sum(jnp.where(onehot, local.reshape(t, 1, N_EXPERTS), 0), axis=-1).astype(jnp.int32)
    i32 = lambda a: a.astype(jnp.int32)
    return (gates, ls, i32(d_tile), i32(n_tile), i32(o_tile), i32(pstart + counts), i32(nblk * MOE_BM - counts),
            block_e, n_used.reshape(1), nb * MOE_BM)


TILE_TOKENS = 256
TILE_ROWS = TOP_K * TILE_TOKENS + N_EXPERTS * (SUBLANES - 1)
assert TILE_ROWS % SUBLANES == 0


def _bits(lo, hi):
    return tuple(lo << j for j in range((hi // lo).bit_length()))


STRIP_LEN_BITS = _bits(SUBLANES, TILE_TOKENS)
PAD_LEN_BITS = _bits(SUBLANES, MOE_BM - SUBLANES)
TILE_LEN_BITS = _bits(SUBLANES, TILE_ROWS)


def _tile_strips(i, slot, d_ref, n_ref, o_ref, local, hbm, sem, to_hbm):
    for e in range(N_EXPERTS):
        l0 = pl.multiple_of(o_ref[i * N_EXPERTS + e], SUBLANES)
        g0 = pl.multiple_of(d_ref[i * N_EXPERTS + e], SUBLANES)

        def make_copy(off, size, l0=l0, g0=g0):
            lo = local.at[slot, pl.ds(pl.multiple_of(l0 + off, SUBLANES), size)]
            gl = hbm.at[pl.ds(pl.multiple_of(g0 + off, SUBLANES), size)]
            return pltpu.make_async_copy(lo, gl, sem.at[slot]) if to_hbm else pltpu.make_async_copy(gl, lo, sem.at[slot])
        _strip_dmas(n_ref[i * N_EXPERTS + e], STRIP_LEN_BITS, make_copy)


def _tile_wait(rows, slot, local, hbm, sem):
    _strip_dmas(rows, TILE_LEN_BITS, lambda off, size: pltpu.make_async_copy(
        hbm.at[pl.ds(0, size)], local.at[slot, pl.ds(0, size)], sem.at[slot]), wait=True)


def _dispatch2_kernel(d_ref, n_ref, o_ref, r_ref, ps_ref, pn_ref, lst_ref, hp_ref, hs_ref, xs_hbm, buf, zbuf, sem,
                      psem, *, n_tiles, n_prompt_tiles):
    i = pl.program_id(0)
    slot = i % 2

    def pad_strips(wait):
        for e in range(N_EXPERTS):
            start = ps_ref[e]
            _strip_dmas(pn_ref[e], PAD_LEN_BITS, lambda off, size: pltpu.make_async_copy(
                zbuf.at[pl.ds(0, size)], xs_hbm.at[pl.ds(pl.multiple_of(start + off, SUBLANES), size)],
                psem.at[0]), wait)
        chunk = zbuf.shape[0]
        used_rows = ps_ref[N_EXPERTS - 1] + pn_ref[N_EXPERTS - 1]

        def body(c, carry):
            cp = pltpu.make_async_copy(zbuf, xs_hbm.at[pl.ds(pl.multiple_of(c * chunk, chunk), chunk)], psem.at[0])
            if wait:
                cp.wait()
            else:
                cp.start()
            return carry
        lax.fori_loop(used_rows // chunk, xs_hbm.shape[0] // chunk, body, 0)

    @pl.when(i >= 2)
    def _():
        _tile_wait(r_ref[i - 2], slot, buf, xs_hbm, sem)

    @pl.when(i == 0)
    def _():
        zbuf[...] = jnp.zeros_like(zbuf)
        pad_strips(False)

    srow = lax.broadcasted_iota(jnp.int32, (TILE_ROWS, TILE_TOKENS), 0)
    place = srow == lst_ref[0:1, :]
    for k in range(1, TOP_K):
        place = jnp.logical_or(place, srow == lst_ref[k:k + 1, :])
    x = jnp.where(i < n_prompt_tiles, hp_ref[...], hs_ref[...]).astype(BF16)
    buf[slot] = jnp.dot(jnp.where(place, 1.0, 0.0).astype(BF16), x, preferred_element_type=F32)
    _tile_strips(i, slot, d_ref, n_ref, o_ref, buf, xs_hbm, sem, to_hbm=True)

    @pl.when(i == n_tiles - 1)
    def _():
        _tile_wait(r_ref[i], slot, buf, xs_hbm, sem)
        if n_tiles > 1:
            _tile_wait(r_ref[jnp.maximum(i - 1, 0)], 1 - slot, buf, xs_hbm, sem)
        pad_strips(True)


def _dispatch2(hp, hs, ls_t, tabs, n_rows):
    tp, d = hp.shape
    npt = tp // TILE_TOKENS
    assert tp % TILE_TOKENS == 0 and hs.shape[0] == TILE_TOKENS
    nt = npt + 1
    grid_spec = pltpu.PrefetchScalarGridSpec(
        num_scalar_prefetch=6,
        grid=(nt,),
        in_specs=[
            pl.BlockSpec((TOP_K, TILE_TOKENS), lambda i, *_: (0, i)),
            pl.BlockSpec((TILE_TOKENS, d), lambda i, *_: (jnp.minimum(i, npt - 1), 0)),
            pl.BlockSpec((TILE_TOKENS, d), lambda i, *_: (0, 0)),
        ],
        out_specs=pl.BlockSpec(memory_space=pl.ANY),
        scratch_shapes=[
            pltpu.VMEM((2, TILE_ROWS, d), F32),
            pltpu.VMEM((PAD_LEN_BITS[-1], d), F32),
            pltpu.SemaphoreType.DMA((2,)),
            pltpu.SemaphoreType.DMA((1,)),
        ],
    )
    return pl.pallas_call(
        functools.partial(_dispatch2_kernel, n_tiles=nt, n_prompt_tiles=npt),
        grid_spec=grid_spec,
        out_shape=jax.ShapeDtypeStruct((n_rows, d), F32),
        compiler_params=_cparams(("arbitrary",)),
        name="moe_dispatch",
    )(tabs["d"], tabs["n"], tabs["o"], tabs["rows"], tabs["pad_start"], tabs["pad_len"], ls_t, hp, hs)


def _expert_ffn2_kernel(be_ref, nu_ref, x_ref, wgu_ref, bgu_ref, wd_ref, bd_ref, o_ref, wgu_bf, wd_bf):
    b = pl.program_id(0)
    dff = wd_ref.shape[1]
    new_expert = jnp.logical_or(b == 0, be_ref[b] != be_ref[jnp.maximum(b - 1, 0)])

    @pl.when(jnp.logical_and(b < nu_ref[0], new_expert))
    def _():
        wgu_bf[...] = wgu_ref[0].astype(BF16)
        wd_bf[...] = wd_ref[0].astype(BF16)

    @pl.when(b < nu_ref[0])
    def _():
        hu = jnp.dot(x_ref[...].astype(BF16), wgu_bf[...], preferred_element_type=F32) + bgu_ref[0]
        g = jnp.minimum(hu[:, :dff], SWIGLU_LIMIT)
        u = jnp.clip(hu[:, dff:], -SWIGLU_LIMIT, SWIGLU_LIMIT)
        a = g * jax.nn.sigmoid(SWIGLU_ALPHA * g) * (u + 1.0)
        o_ref[...] = jnp.dot(a.astype(BF16), wd_bf[...], preferred_element_type=F32) + bd_ref[0]

    @pl.when(b >= nu_ref[0])
    def _():
        o_ref[...] = jnp.zeros_like(o_ref)


def _expert_ffn2(x_sorted, block_e, n_used, w_gate_up, b_gate_up, w_down, b_down):
    n_rows, d = x_sorted.shape
    ne, _, dff2 = w_gate_up.shape
    dff = dff2 // 2
    nb = n_rows // MOE_BM
    used = lambda b, nu: jnp.minimum(b, nu[0] - 1)
    grid_spec = pltpu.PrefetchScalarGridSpec(
        num_scalar_prefetch=2,
        grid=(nb,),
        in_specs=[
            pl.BlockSpec((MOE_BM, d), lambda b, be, nu: (used(b, nu), 0)),
            pl.BlockSpec((1, d, dff2), lambda b, be, nu: (be[b], 0, 0)),
            pl.BlockSpec((1, 1, dff2), lambda b, be, nu: (be[b], 0, 0)),
            pl.BlockSpec((1, dff, d), lambda b, be, nu: (be[b], 0, 0)),
            pl.BlockSpec((1, 1, d), lambda b, be, nu: (be[b], 0, 0)),
        ],
        out_specs=pl.BlockSpec((MOE_BM, d), lambda b, be, nu: (b, 0)),
        scratch_shapes=[pltpu.VMEM((d, dff2), BF16), pltpu.VMEM((dff, d), BF16)],
    )
    return pl.pallas_call(
        _expert_ffn2_kernel,
        grid_spec=grid_spec,
        out_shape=jax.ShapeDtypeStruct((n_rows, d), F32),
        compiler_params=_cparams(("arbitrary",)),
        name="expert_ffn",
    )(block_e, n_used, x_sorted, w_gate_up, b_gate_up.reshape(ne, 1, dff2), w_down, b_down.reshape(ne, 1, d))


def _combine2_kernel(d_ref, n_ref, o_ref, r_ref, ls_ref, gate_ref, xp_ref, xs_ref, g_ref, y_hbm, yp_ref, ys_ref,
                     buf, sem, *, n_tiles, n_prompt_tiles):
    j = pl.program_id(0)

    @pl.when(j == 0)
    def _():
        buf[...] = jnp.zeros_like(buf)

    @pl.when(j < n_tiles)
    def _():
        _tile_strips(j, j % 2, d_ref, n_ref, o_ref, buf, y_hbm, sem, to_hbm=False)

    @pl.when(j >= 1)
    def _():
        slot = (j - 1) % 2
        _tile_wait(r_ref[j - 1], slot, buf, y_hbm, sem)
        r_hi = buf[slot].astype(BF16)
        lane = lax.broadcasted_iota(jnp.int32, (TILE_TOKENS, TILE_ROWS), 1)
        pick = jnp.zeros((TILE_TOKENS, TILE_ROWS), F32)
        for k in range(TOP_K):
            pick = pick + jnp.where(lane == ls_ref[:, k:k + 1], gate_ref[:, k:k + 1], 0.0)
        p_hi = pick.astype(BF16)
        p_lo = (pick - p_hi.astype(F32)).astype(BF16)
        moe = jnp.dot(p_hi, r_hi, preferred_element_type=F32) + jnp.dot(p_lo, r_hi, preferred_element_type=F32)
        is_prompt = j - 1 < n_prompt_tiles
        y = _rms(jnp.where(is_prompt, xp_ref[...], xs_ref[...]) + moe, g_ref[...])

        @pl.when(is_prompt)
        def _():
            yp_ref[...] = y

        @pl.when(jnp.logical_not(is_prompt))
        def _():
            ys_ref[...] = y


def _combine2(y_sorted, x2p, x2s, ls, gates, tabs, g_final):
    tp, d = x2p.shape
    npt = tp // TILE_TOKENS
    assert x2s.shape[0] == TILE_TOKENS
    nt = npt + 1
    tile = lambda j: jnp.clip(j - 1, 0, nt - 1)
    ptile = lambda j: jnp.clip(j - 1, 0, npt - 1)
    grid_spec = pltpu.PrefetchScalarGridSpec(
        num_scalar_prefetch=4,
        grid=(nt + 1,),
        in_specs=[
            pl.BlockSpec((TILE_TOKENS, TOP_K), lambda j, *_: (tile(j), 0)),
            pl.BlockSpec((TILE_TOKENS, TOP_K), lambda j, *_: (tile(j), 0)),
            pl.BlockSpec((TILE_TOKENS, d), lambda j, *_: (ptile(j), 0)),
            pl.BlockSpec((TILE_TOKENS, d), lambda j, *_: (0, 0)),
            pl.BlockSpec((1, d), lambda j, *_: (0, 0)),
            pl.BlockSpec(memory_space=pl.ANY),
        ],
        out_specs=[
            pl.BlockSpec((TILE_TOKENS, d), lambda j, *_: (ptile(j), 0)),
            pl.BlockSpec((TILE_TOKENS, d), lambda j, *_: (0, 0)),
        ],
        scratch_shapes=[pltpu.VMEM((2, TILE_ROWS, d), F32), pltpu.SemaphoreType.DMA((2,))],
    )
    return pl.pallas_call(
        functools.partial(_combine2_kernel, n_tiles=nt, n_prompt_tiles=npt),
        grid_spec=grid_spec,
        out_shape=[jax.ShapeDtypeStruct((tp, d), F32), jax.ShapeDtypeStruct((TILE_TOKENS, d), F32)],
        compiler_params=_cparams(("arbitrary",)),
        name="moe_combine",
    )(tabs["d"], tabs["n"], tabs["o"], tabs["rows"], ls, gates, x2p, x2s, g_final.reshape(1, d), y_sorted)


def _route_tiles2(logits):
    t = logits.shape[0]
    nt = -(-t // TILE_TOKENS)
    tpad = nt * TILE_TOKENS
    up = lambda a, m: (a + m - 1) // m * m
    top_v, top_i = lax.top_k(logits, TOP_K)
    gates = jax.nn.softmax(top_v, axis=-1)
    onehot = (top_i[:, :, None] == jnp.arange(N_EXPERTS, dtype=top_i.dtype)[None, None, :])
    per_tok = jnp.pad(jnp.sum(onehot, axis=1).astype(jnp.int32), ((0, tpad - t), (0, 0)))
    before = jnp.cumsum(per_tok, axis=0) - per_tok
    n_tile = up(jnp.sum(per_tok.reshape(nt, TILE_TOKENS, N_EXPERTS), axis=1), SUBLANES)
    o_tile = jnp.cumsum(n_tile, axis=1) - n_tile
    rows_e = jnp.sum(n_tile, axis=0)
    nblk = up(rows_e, MOE_BM) // MOE_BM
    blk_end = jnp.cumsum(nblk)
    pstart = (blk_end - nblk) * MOE_BM
    d_tile = pstart[None, :] + jnp.cumsum(n_tile, axis=0) - n_tile
    local = (o_tile - before[::TILE_TOKENS])[:, None, :] + before.reshape(nt, TILE_TOKENS, N_EXPERTS)
    ls = jnp.sum(jnp.where(onehot, local.reshape(tpad, 1, N_EXPERTS)[:t], 0), axis=-1)
    ls = jnp.pad(ls.astype(jnp.int32), ((0, tpad - t), (0, 0)), constant_values=-1)
    gates = jnp.pad(gates, ((0, tpad - t), (0, 0)))
    n_used = blk_end[-1].astype(jnp.int32)
    nb = (t * TOP_K + N_EXPERTS * nt * (SUBLANES - 1) + N_EXPERTS * (MOE_BM - 1) + MOE_BM - 1) // MOE_BM
    blk = jnp.minimum(jnp.arange(nb, dtype=jnp.int32), n_used - 1)
    block_e = jnp.sum(blk_end[None, :] <= blk[:, None], axis=1).astype(jnp.int32)
    i32 = lambda a: a.astype(jnp.int32).reshape(-1)
    tabs = {"d": i32(d_tile), "n": i32(n_tile), "o": i32(o_tile), "rows": i32(jnp.sum(n_tile, axis=1)),
            "pad_start": i32(pstart + rows_e), "pad_len": i32(nblk * MOE_BM - rows_e)}
    return gates, ls, tabs, block_e, n_used.reshape(1), nb * MOE_BM


def kernel(x_prompt, x_sample, cache_k_win, cache_v_win, state_pool, cache_mem_k, cache_mem_v, mem_prompt,
           rel_bias, g_mix, w_in, w_pool, pool_scale, w_out, g_mem, g_x, w_xq, w_xk, w_xv, w_xo, g_ff,
           w_router, b_router, w_gate_up, b_gate_up, w_down, b_down, g_final):
    depth = g_mix.shape[0]
    assert depth == 1
    l = 0
    batch, seq, d = x_prompt.shape
    bd, n_new, _ = x_sample.shape
    n_mem = mem_prompt.shape[1]
    bw = d - A_WIDTH
    n_hist = cache_k_win.shape[2]
    n_pool = state_pool.shape[2]
    tp, ts = batch * seq, bd * n_new

    w_in_b = w_in[l].astype(BF16)
    w_pool_b = w_pool[l].astype(BF16)
    w_xkv_b = jnp.concatenate([w_xk[l], w_xv[l]], axis=1).astype(BF16)
    wts = {
        "w_out_a": w_out[l, :A_WIDTH].astype(BF16), "w_out_b": w_out[l, A_WIDTH:].astype(BF16),
        "g_x": g_x[l].reshape(1, d), "w_xq": w_xq[l].astype(BF16), "w_xo": w_xo[l].astype(BF16),
        "g_ff": g_ff[l].reshape(1, d), "w_router": w_router[l], "b_router": b_router[l].reshape(1, N_EXPERTS),
    }
    band_bias = _band_bias(rel_bias)
    step_bias = _step_bias(rel_bias)
    widths = (A_WIDTH, A_WIDTH, A_WIDTH, bw)
    scales = (A_HEAD_DIM ** -0.5, 1.0, 1.0, 1.0)

    xp = x_prompt.reshape(tp, d)
    w_kv_t = jnp.transpose(w_in[l][:, A_WIDTH:3 * A_WIDTH].reshape(d, 2, A_WIDTH), (1, 2, 0)).astype(BF16)
    q, k, v, u, k_t, v_t = _in_proj(xp, g_mix[l], w_in_b, w_kv_t, widths, scales, batch, seq, tm=512)
    o_a = _dil_attn(q, k, v, band_bias, batch, seq)
    o_b = _pool_prompt(u, w_pool_b, pool_scale[l], batch, seq, tm=512)
    mk, mv = _norm_proj(mem_prompt.reshape(batch * n_mem, d), g_mem[l], w_xkv_b, (d, d), (1.0, 1.0), tm=512,
                        heads=(X_HEADS, X_HEADS))
    x2p, hp, lgp = _mix_xattn(xp, o_a, o_b, mk, mv, wts, groups=batch, rows_per_group=seq, tm=512,
                              rows_per_sub=None, keys_per_sub=None)

    xs = x_sample.reshape(ts, d)
    qs, ks, vs, us = _norm_proj(xs, g_mix[l], w_in_b, widths, scales, tm=ts)
    r4 = lambda a: a.reshape(bd, n_new, A_HEADS, A_HEAD_DIM)
    ck, cv = cache_k_win[l], cache_v_win[l]
    new3 = lambda a: a.reshape(bd, n_new, A_WIDTH)
    to_t = lambda c: jnp.transpose(c, (0, 2, 3, 1)).reshape(bd, A_WIDTH, n_hist)
    from_t = lambda c: jnp.transpose(c.reshape(bd, A_HEADS, A_HEAD_DIM, n_hist), (0, 3, 1, 2))[None]
    bias_c, bias_n = _decode_bias(rel_bias, n_hist, n_new)
    o_as, ck_t, cv_t = _decode_attn(new3(qs), new3(ks), new3(vs), to_t(ck), to_t(cv), bias_c, bias_n,
                                    heads_per_step=4)
    o_as = o_as.reshape(ts, A_WIDTH)
    us3 = us.reshape(bd, n_new, bw)
    hist = jnp.concatenate([jnp.zeros((bd, POOL_HIST - n_pool, bw), F32), state_pool[l]], axis=1)
    cur = jnp.concatenate([us3, jnp.zeros((bd, SUBLANES - n_new, bw), F32)], axis=1)
    o_bs = _pool_dec(hist, cur, n_pool, w_pool_b, pool_scale[l])[:, :n_new].reshape(ts, bw)
    sub = 8
    xhd = d // X_HEADS
    x2s, hs, lgs = _mix_xattn(xs, o_as, o_bs, cache_mem_k[l].reshape(bd * n_mem, X_HEADS, xhd),
                              cache_mem_v[l].reshape(bd * n_mem, X_HEADS, xhd), wts, groups=bd // sub,
                              rows_per_group=sub * n_new, tm=sub * n_new, rows_per_sub=n_new, keys_per_sub=n_mem)

    assert ts <= TILE_TOKENS
    pad_s = lambda a: jnp.pad(a, ((0, TILE_TOKENS - ts), (0, 0)))
    gates, ls, tabs, block_e, n_used, n_rows = _route_tiles2(jnp.concatenate([lgp, lgs], axis=0))
    x_sorted = _dispatch2(hp, pad_s(hs), ls.T, tabs, n_rows)
    y_sorted = _expert_ffn2(x_sorted, block_e, n_used, w_gate_up[l], b_gate_up[l], w_down[l], b_down[l])
    y_prompt, y_sample = _combine2(y_sorted, x2p, pad_s(x2s), ls, gates, tabs, g_final)
    y_prompt = y_prompt.reshape(batch, seq, d)
    y_sample = y_sample[:ts].reshape(bd, n_new, d)

    a5 = lambda a, b_: a.reshape(1, b_, -1, A_HEADS, A_HEAD_DIM)
    from_tp = lambda c: jnp.transpose(c.reshape(batch, A_HEADS, A_HEAD_DIM, seq), (0, 3, 1, 2))[None]
    k_win_prompt, v_win_prompt = from_tp(k_t), from_tp(v_t)
    pool_prompt = u.reshape(batch, seq, bw)[:, seq - n_pool:][None]
    mem_k_prompt = mk.reshape(1, batch, n_mem, X_HEADS, xhd)
    mem_v_prompt = mv.reshape(1, batch, n_mem, X_HEADS, xhd)
    k_win_sample, v_win_sample = from_t(ck_t), from_t(cv_t)
    pool_sample = jnp.concatenate([state_pool[l][:, n_new:], us3], axis=1)[None]
    return (y_prompt, y_sample, k_win_prompt, v_win_prompt, pool_prompt, mem_k_prompt, mem_v_prompt,
            k_win_sample, v_win_sample, pool_sample)
```

```python
import functools
import math

import numpy as np
import jax
import jax.numpy as jnp
from jax import lax
from jax.experimental import pallas as pl
from jax.experimental.pallas import tpu as pltpu

F32 = jnp.float32
BF16 = jnp.bfloat16

LANES = 128
SUBLANES = 8
VMEM_LIMIT_BYTES = 56 * 1024 * 1024

A_HEADS = 8
A_HEAD_DIM = 64
A_WIDTH = A_HEADS * A_HEAD_DIM
DILATED = ((128, 1), (512, 4), (2048, 16))
QB = 128
POOL_WINDOWS = (2, 4, 8, 16)
POOL_HIST = 16
X_HEADS = 4
N_EXPERTS = 32
TOP_K = 4
SWIGLU_LIMIT = 7.0
SWIGLU_ALPHA = 1.702
N_BUCKETS = 32
RMS_EPS = 1e-6
NEG = -1e30

MOE_BM = 256
COMBINE_TQ = 128


def _cparams(sem):
    return pltpu.CompilerParams(dimension_semantics=sem, vmem_limit_bytes=VMEM_LIMIT_BYTES)


def _rms(x, g):
    return x * lax.rsqrt(jnp.mean(x * x, axis=-1, keepdims=True) + RMS_EPS) * g


def _norm_proj_kernel(x_ref, g_ref, w_ref, *o_refs, widths, scales, heads):
    hb = _rms(x_ref[...], g_ref[...]).astype(BF16)
    off = 0
    for o_ref, width, scale, nh in zip(o_refs, widths, scales, heads):
        p = jnp.dot(hb, w_ref[:, off:off + width], preferred_element_type=F32)
        p = p if scale == 1.0 else p * scale
        if nh is None:
            o_ref[...] = p
        else:
            hd = width // nh
            for h in range(nh):
                o_ref[:, h, :] = p[:, h * hd:(h + 1) * hd]
        off += width


def _norm_proj(x, g, w_bf16, widths, scales, tm, heads=None):
    t, d = x.shape
    n = w_bf16.shape[1]
    heads = heads or (None,) * len(widths)
    assert sum(widths) == n and t % tm == 0
    shapes = [(wd,) if nh is None else (nh, wd // nh) for wd, nh in zip(widths, heads)]
    return pl.pallas_call(
        functools.partial(_norm_proj_kernel, widths=widths, scales=scales, heads=heads),
        grid=(t // tm,),
        in_specs=[
            pl.BlockSpec((tm, d), lambda i: (i, 0)),
            pl.BlockSpec((1, d), lambda i: (0, 0)),
            pl.BlockSpec((d, n), lambda i: (0, 0)),
        ],
        out_specs=[pl.BlockSpec((tm,) + sh, lambda i, nd=len(sh): (i,) + (0,) * nd) for sh in shapes],
        out_shape=[jax.ShapeDtypeStruct((t,) + sh, F32) for sh in shapes],
        compiler_params=_cparams(("arbitrary",)),
        name="norm_proj",
    )(x, g.reshape(1, d), w_bf16)


def _in_proj_kernel(x_ref, g_ref, w_ref, wt_ref, *o_refs, widths, scales, n_t):
    hb = _rms(x_ref[...], g_ref[...]).astype(BF16)
    off = 0
    for o_ref, width, scale in zip(o_refs, widths, scales):
        p = jnp.dot(hb, w_ref[:, off:off + width], preferred_element_type=F32)
        o_ref[...] = p if scale == 1.0 else p * scale
        off += width
    for j in range(n_t):
        o_refs[len(widths) + j][0] = lax.dot_general(wt_ref[j], hb, (((1,), (1,)), ((), ())),
                                                    preferred_element_type=F32)


def _in_proj(x, g, w_bf16, wt_bf16, widths, scales, batch, seq, tm):
    t, d = x.shape
    n = w_bf16.shape[1]
    n_t, wt_width, _ = wt_bf16.shape
    nt = seq // tm
    assert sum(widths) == n and seq % tm == 0 and t == batch * seq
    return pl.pallas_call(
        functools.partial(_in_proj_kernel, widths=widths, scales=scales, n_t=n_t),
        grid=(t // tm,),
        in_specs=[
            pl.BlockSpec((tm, d), lambda i: (i, 0)),
            pl.BlockSpec((1, d), lambda i: (0, 0)),
            pl.BlockSpec((d, n), lambda i: (0, 0)),
            pl.BlockSpec(wt_bf16.shape, lambda i: (0, 0, 0)),
        ],
        out_specs=[pl.BlockSpec((tm, wd), lambda i: (i, 0)) for wd in widths]
        + [pl.BlockSpec((1, wt_width, tm), lambda i: (i // nt, 0, i % nt))] * n_t,
        out_shape=[jax.ShapeDtypeStruct((t, wd), F32) for wd in widths]
        + [jax.ShapeDtypeStruct((batch, wt_width, seq), F32)] * n_t,
        compiler_params=_cparams(("arbitrary",)),
        name="in_proj",
    )(x, g.reshape(1, d), w_bf16, wt_bf16)


def _t5_bucket_np(n, max_dist):
    max_exact = N_BUCKETS // 2
    nf = np.maximum(n, 1).astype(np.float32)
    large = max_exact + (
        np.log(nf / np.float32(max_exact)) / np.float32(math.log(max_dist / max_exact))
        * np.float32(N_BUCKETS - max_exact)
    ).astype(np.int32)
    return np.where(n < max_exact, n, np.minimum(large, N_BUCKETS - 1))


def _band_bias(rel_bias):
    max_dist = max(w for w, _ in DILATED)
    qi = np.arange(QB)[:, None]
    ki = np.arange(2 * QB)[None, :]
    j = qi + QB - ki
    tabs = []
    for window, dil in DILATED:
        steps = window // dil
        in_band = (j >= 0) & (j <= steps)
        bucket = _t5_bucket_np(np.clip(j, 0, steps) * dil, max_dist)
        onehot = (bucket[..., None] == np.arange(N_BUCKETS)).astype(np.float32)
        b = jnp.einsum("qkb,bh->hqk", onehot, rel_bias.astype(F32), precision=lax.Precision.HIGHEST)
        tabs.append(jnp.where(in_band[None], b, NEG))
    return jnp.stack(tabs)


def _step_bias(rel_bias):
    max_dist = max(w for w, _ in DILATED)
    tabs = []
    for window, dil in DILATED:
        steps = window // dil
        bucket = _t5_bucket_np(np.arange(steps, -1, -1) * dil, max_dist)
        tabs.append(rel_bias[bucket].astype(F32)[:, :, None])
    return jnp.stack(tabs)


def _decode_bias(rel_bias, n_hist, n_new):
    max_dist = max(w for w, _ in DILATED)
    t = np.arange(n_new)[:, None]
    tabs_c, tabs_n = [], []
    for window, dil in DILATED:
        out = []
        for dist in (n_hist + t - np.arange(n_hist)[None, :], t - np.arange(n_new)[None, :]):
            ok = (dist >= 0) & (dist % dil == 0) & (dist <= window)
            onehot = (_t5_bucket_np(np.clip(dist, 0, window), max_dist)[..., None] == np.arange(N_BUCKETS))
            b = jnp.einsum("tpb,bh->htp", onehot.astype(np.float32), rel_bias.astype(F32),
                           precision=lax.Precision.HIGHEST)
            out.append(jnp.where(ok[None], b, NEG))
        tabs_c.append(out[0])
        tabs_n.append(out[1])
    return jnp.stack(tabs_c), jnp.stack(tabs_n)


def _decode_attn_kernel(q_ref, kn_ref, vn_ref, knt_ref, vnt_ref, ck_ref, cv_ref, bc_ref, bn_ref,
                        o_ref, ok_ref, ov_ref, *, n_new, heads):
    n_hist = ck_ref.shape[2]
    hd = A_HEAD_DIM
    outs = []
    for h in range(heads):
        rows = slice(h * hd, (h + 1) * hd)
        q = q_ref[0, :, rows].astype(BF16)
        kt = ck_ref[0, rows, :].astype(BF16)
        vt = cv_ref[0, rows, :].astype(BF16)
        lc = jnp.dot(q, kt, preferred_element_type=F32)
        ln = lax.dot_general(q, kn_ref[0, :, rows].astype(BF16), (((1,), (1,)), ((), ())),
                             preferred_element_type=F32)
        vn = vn_ref[0, :, rows].astype(BF16)
        o_br, lse_br = [], []
        for br in range(len(DILATED)):
            bl = lc + bc_ref[br, h]
            bln = ln + bn_ref[br, h]
            m = jnp.maximum(jnp.max(bl, axis=-1, keepdims=True), jnp.max(bln, axis=-1, keepdims=True))
            p = jnp.exp(bl - m)
            pn = jnp.exp(bln - m)
            s = jnp.sum(p, axis=-1, keepdims=True) + jnp.sum(pn, axis=-1, keepdims=True)
            o = lax.dot_general(p.astype(BF16), vt, (((1,), (1,)), ((), ())), preferred_element_type=F32)
            o = o + jnp.dot(pn.astype(BF16), vn, preferred_element_type=F32)
            o_br.append(o / s)
            lse_br.append(m + jnp.log(s))
        m = jnp.maximum(jnp.maximum(lse_br[0], lse_br[1]), lse_br[2])
        es = [jnp.exp(l - m) for l in lse_br]
        outs.append((es[0] * o_br[0] + es[1] * o_br[1] + es[2] * o_br[2]) / (es[0] + es[1] + es[2]))
    o_ref[0] = jnp.concatenate(outs, axis=-1)

    lane = lax.broadcasted_iota(jnp.int32, (ck_ref.shape[1], LANES), 1)
    for c_ref, nt_ref, dst in ((ck_ref, knt_ref, ok_ref), (cv_ref, vnt_ref, ov_ref)):
        shifted = pltpu.roll(c_ref[0], n_hist - n_new, axis=1)
        tail = shifted[:, n_hist - LANES:]
        for j in range(n_new):
            tail = jnp.where(lane == LANES - n_new + j, nt_ref[0, :, j:j + 1], tail)
        dst[0, :, :n_hist - LANES] = shifted[:, :n_hist - LANES]
        dst[0, :, n_hist - LANES:] = tail


def _decode_attn(q, k_new, v_new, ck_t, cv_t, bias_c, bias_n, heads_per_step):
    bd, n_new, aw = q.shape
    n_hist = ck_t.shape[2]
    gw = heads_per_step * A_HEAD_DIM
    ng = aw // gw
    row = pl.BlockSpec((1, n_new, gw), lambda b, g: (b, 0, g))
    col = pl.BlockSpec((1, gw, n_new), lambda b, g: (b, g, 0))
    cache = pl.BlockSpec((1, gw, n_hist), lambda b, g: (b, g, 0))
    nbr = len(DILATED)
    return pl.pallas_call(
        functools.partial(_decode_attn_kernel, n_new=n_new, heads=heads_per_step),
        grid=(bd, ng),
        in_specs=[row, row, row, col, col, cache, cache,
                  pl.BlockSpec((nbr, heads_per_step, n_new, n_hist), lambda b, g: (0, g, 0, 0)),
                  pl.BlockSpec((nbr, heads_per_step, n_new, n_new), lambda b, g: (0, g, 0, 0))],
        out_specs=[row, cache, cache],
        out_shape=[jax.ShapeDtypeStruct(q.shape, F32), jax.ShapeDtypeStruct(ck_t.shape, F32),
                   jax.ShapeDtypeStruct(cv_t.shape, F32)],
        compiler_params=_cparams(("arbitrary", "arbitrary")),
        name="decode_attn",
    )(q, k_new, v_new, jnp.swapaxes(k_new, 1, 2), jnp.swapaxes(v_new, 1, 2), ck_t, cv_t, bias_c, bias_n)


def _dil_attn_kernel(q_ref, k_ref, v_ref, bias_ref, o_ref, obr_ref, lbr_ref, *, seq):
    lane = lax.broadcasted_iota(jnp.int32, (QB, LANES), 1)
    head0 = lane < A_HEAD_DIM

    def rows(ref, start, n, dil):
        if dil == 1:
            return ref[pl.ds(start, n), :]
        return ref[pl.ds(start, n, stride=dil), :]

    def block(br, dil, qstart, kstart, nk):
        qs = rows(q_ref, qstart, QB, dil)
        ks = rows(k_ref, kstart, nk, dil).astype(BF16)
        vs = rows(v_ref, kstart, nk, dil).astype(BF16)
        outs, lses = [], []
        for hh in range(2):
            keep = head0 if hh == 0 else jnp.logical_not(head0)
            qm = jnp.where(keep, qs, 0.0).astype(BF16)
            logits = lax.dot_general(qm, ks, (((1,), (1,)), ((), ())), preferred_element_type=F32)
            logits = logits + bias_ref[br, hh, :, 2 * QB - nk:]
            m = jnp.max(logits, axis=-1, keepdims=True)
            p = jnp.exp(logits - m)
            s = jnp.sum(p, axis=-1, keepdims=True)
            o = jnp.dot(p.astype(BF16), vs, preferred_element_type=F32)
            outs.append(o / s)
            lses.append(jnp.broadcast_to(m + jnp.log(s), (QB, LANES)))
        o = jnp.where(head0, outs[0], outs[1])
        lse = jnp.where(head0, lses[0], lses[1])
        if dil == 1:
            obr_ref[br, pl.ds(qstart, QB), :] = o
            lbr_ref[br, pl.ds(qstart, QB), :] = lse
        else:
            obr_ref[br, pl.ds(qstart, QB, stride=dil), :] = o
            lbr_ref[br, pl.ds(qstart, QB, stride=dil), :] = lse

    for br, (window, dil) in enumerate(DILATED):
        assert window // dil == QB
        nblk = seq // (dil * QB)
        for r in range(dil):
            block(br, dil, r, r, QB)
            for i in range(1, nblk):
                qstart = r + dil * QB * i
                block(br, dil, qstart, qstart - dil * QB, 2 * QB)

    l0, l1, l2 = lbr_ref[0], lbr_ref[1], lbr_ref[2]
    m = jnp.maximum(jnp.maximum(l0, l1), l2)
    e0, e1, e2 = jnp.exp(l0 - m), jnp.exp(l1 - m), jnp.exp(l2 - m)
    acc = e0 * obr_ref[0] + e1 * obr_ref[1] + e2 * obr_ref[2]
    o_ref[...] = acc / (e0 + e1 + e2)


def _dil_attn(q, k, v, band_bias, batch, seq):
    t, aw = q.shape
    npair = aw // LANES
    bias = band_bias.reshape(len(DILATED), npair, 2, QB, 2 * QB)
    spec = pl.BlockSpec((seq, LANES), lambda b, hp: (b, hp))
    return pl.pallas_call(
        functools.partial(_dil_attn_kernel, seq=seq),
        grid=(batch, npair),
        in_specs=[
            spec, spec, spec,
            pl.BlockSpec((len(DILATED), None, 2, QB, 2 * QB), lambda b, hp: (0, hp, 0, 0, 0)),
        ],
        out_specs=spec,
        out_shape=jax.ShapeDtypeStruct((t, aw), F32),
        scratch_shapes=[
            pltpu.VMEM((len(DILATED), seq, LANES), F32),
            pltpu.VMEM((len(DILATED), seq, LANES), F32),
        ],
        compiler_params=_cparams(("arbitrary", "arbitrary")),
        name="dil_attn",
    )(q, k, v, bias)


def _dil_attn_dec_kernel(q_ref, kn_ref, vn_ref, k1_ref, v1_ref, k2_ref, v2_ref, k3_ref, v3_ref,
                         rbias_ref, o_ref, *, n_new):
    kn = kn_ref[0]
    vn = vn_ref[0]
    for t in range(n_new):
        q = q_ref[0, t][None]
        outs, lses = [], []
        for br, (window, dil) in enumerate(DILATED):
            if dil == 1:
                kc, vc = k1_ref[0, t:], v1_ref[0, t:]
                bc = rbias_ref[br, 0:QB - t]
                kn_t, vn_t = kn[:t + 1], vn[:t + 1]
                bn = rbias_ref[br, QB - t:QB + 1]
            else:
                kref, vref = (k2_ref, v2_ref) if br == 1 else (k3_ref, v3_ref)
                kc, vc = kref[0, :, t], vref[0, :, t]
                bc = rbias_ref[br, 0:QB]
                kn_t, vn_t = kn[t:t + 1], vn[t:t + 1]
                bn = rbias_ref[br, QB:QB + 1]
            lc = jnp.sum(kc * q, axis=-1, keepdims=True) + bc
            ln = jnp.sum(kn_t * q, axis=-1, keepdims=True) + bn
            m = jnp.maximum(jnp.max(lc, axis=0, keepdims=True), jnp.max(ln, axis=0, keepdims=True))
            pc = jnp.exp(lc - m)
            pn = jnp.exp(ln - m)
            s = jnp.sum(pc, axis=0, keepdims=True) + jnp.sum(pn, axis=0, keepdims=True)
            o = jnp.sum(pc * vc, axis=0, keepdims=True) + jnp.sum(pn * vn_t, axis=0, keepdims=True)
            outs.append(o / s)
            lses.append(m + jnp.log(s))
        m = jnp.maximum(jnp.maximum(lses[0], lses[1]), lses[2])
        es = [jnp.exp(l - m) for l in lses]
        acc = es[0] * outs[0] + es[1] * outs[1] + es[2] * outs[2]
        o_ref[0, t] = (acc / (es[0] + es[1] + es[2]))[0]


def _dil_attn_dec(q, k_new, v_new, cache_k, cache_v, step_bias):
    bd, n_new, h, dh = q.shape
    n_hist = cache_k.shape[1]
    (w1, d1), (w2, d2), (w3, d3) = DILATED
    assert d1 == 1 and n_hist >= w3 and n_new <= d2 and n_hist % d3 == 0 and n_hist % d2 == 0
    assert w1 // d1 == QB and w2 // d2 == QB and w3 // d3 == QB

    def views(c):
        c2 = c.reshape(bd, n_hist // d2, d2, h, dh)
        c3 = c.reshape(bd, n_hist // d3, d3, h, dh)
        return c, c2, c3

    k1, k2, k3 = views(cache_k)
    v1, v2, v3 = views(cache_v)
    new_spec = pl.BlockSpec((1, n_new, h, dh), lambda b: (b, 0, 0, 0))
    s1 = pl.BlockSpec((1, QB, h, dh), lambda b: (b, n_hist // QB - 1, 0, 0))
    s2 = pl.BlockSpec((1, QB, n_new, h, dh), lambda b: (b, n_hist // d2 // QB - 1, 0, 0, 0))
    s3 = pl.BlockSpec((1, QB, n_new, h, dh), lambda b: (b, n_hist // d3 // QB - 1, 0, 0, 0))
    return pl.pallas_call(
        functools.partial(_dil_attn_dec_kernel, n_new=n_new),
        grid=(bd,),
        in_specs=[new_spec, new_spec, new_spec, s1, s1, s2, s2, s3, s3,
                  pl.BlockSpec(step_bias.shape, lambda b: (0, 0, 0, 0))],
        out_specs=new_spec,
        out_shape=jax.ShapeDtypeStruct(q.shape, F32),
        compiler_params=_cparams(("arbitrary",)),
        name="dil_attn_dec",
    )(q, k_new, v_new, k1, v1, k2, v2, k3, v3, step_bias)


def _pool_kernel(hist_ref, cur_ref, w_ref, scale_ref, o_ref, *, pos0_of_tile, tm):
    nb = cur_ref.shape[0]
    gdim = w_ref.shape[1]
    i = pl.program_id(1)
    pos0 = pos0_of_tile(i)
    have_hist = jnp.where(pos0 > 0, 1.0, 0.0).astype(F32)
    t = lax.broadcasted_iota(jnp.int32, (1, tm, 1), 1)
    for g, w in enumerate(POOL_WINDOWS):
        sl = slice(g * gdim, (g + 1) * gdim)
        cur = cur_ref[:, :, sl]
        ext = jnp.concatenate([hist_ref[:, :, sl] * have_hist, cur], axis=1)
        acc, span = ext, 1
        while span < w:
            n = acc.shape[1]
            acc = acc[:, span:n] + acc[:, 0:n - span]
            span *= 2
        wsum = acc[:, POOL_HIST + 1 - w:POOL_HIST + 1 - w + tm]
        cnt = jnp.minimum(pos0 + t + 1, w).astype(F32)
        d = (wsum / cnt - cur).astype(BF16).reshape(nb * tm, gdim)
        y = jnp.dot(d, w_ref[g], preferred_element_type=F32) * scale_ref[:, sl]
        o_ref[:, :, sl] = y.reshape(nb, tm, gdim)


def _pool_prompt(u, w_pool_bf16, pool_scale, batch, seq, tm):
    t, bw = u.shape
    nt = seq // tm
    hb = tm // POOL_HIST
    u3 = u.reshape(1, t, bw)
    out = pl.pallas_call(
        functools.partial(_pool_kernel, pos0_of_tile=lambda i: i * tm, tm=tm),
        grid=(batch, nt),
        in_specs=[
            pl.BlockSpec((1, POOL_HIST, bw), lambda b, i: (0, jnp.maximum((b * nt + i) * hb - 1, 0), 0)),
            pl.BlockSpec((1, tm, bw), lambda b, i: (0, b * nt + i, 0)),
            pl.BlockSpec(w_pool_bf16.shape, lambda b, i: (0, 0, 0)),
            pl.BlockSpec((1, bw), lambda b, i: (0, 0)),
        ],
        out_specs=pl.BlockSpec((1, tm, bw), lambda b, i: (0, b * nt + i, 0)),
        out_shape=jax.ShapeDtypeStruct((1, t, bw), F32),
        compiler_params=_cparams(("arbitrary", "arbitrary")),
        name="pool_prompt",
    )(u3, u3, w_pool_bf16, pool_scale.reshape(1, bw))
    return out.reshape(t, bw)


def _pool_dec(hist, cur, n_prev, w_pool_bf16, pool_scale):
    bd, tm, bw = cur.shape
    return pl.pallas_call(
        functools.partial(_pool_kernel, pos0_of_tile=lambda i: n_prev, tm=tm),
        grid=(1, 1),
        in_specs=[
            pl.BlockSpec(hist.shape, lambda b, i: (0, 0, 0)),
            pl.BlockSpec(cur.shape, lambda b, i: (0, 0, 0)),
            pl.BlockSpec(w_pool_bf16.shape, lambda b, i: (0, 0, 0)),
            pl.BlockSpec((1, bw), lambda b, i: (0, 0)),
        ],
        out_specs=pl.BlockSpec(cur.shape, lambda b, i: (0, 0, 0)),
        out_shape=jax.ShapeDtypeStruct(cur.shape, F32),
        compiler_params=_cparams(("arbitrary", "arbitrary")),
        name="pool_dec",
    )(hist, cur, w_pool_bf16, pool_scale.reshape(1, bw))


def _mix_xattn_kernel(x_ref, oa_ref, ob_ref, mk_ref, mv_ref, woa_ref, wob_ref, gx_ref, wq_ref, wo_ref,
                      gff_ref, wr_ref, br_ref, x2_ref, h_ref, lg_ref, *, rows_per_sub, keys_per_sub):
    tm = x_ref.shape[0]
    nkv = mk_ref.shape[0]
    xh = wq_ref.shape[1] // X_HEADS
    x = x_ref[...]
    x = x + jnp.dot(oa_ref[...].astype(BF16), woa_ref[...], preferred_element_type=F32)
    x = x + jnp.dot(ob_ref[...].astype(BF16), wob_ref[...], preferred_element_type=F32)
    hq = _rms(x, gx_ref[...]).astype(BF16)
    if rows_per_sub is not None:
        rsub = (pl.program_id(1) * tm + lax.broadcasted_iota(jnp.int32, (tm, nkv), 0)) // rows_per_sub
        ksub = lax.broadcasted_iota(jnp.int32, (tm, nkv), 1) // keys_per_sub
        same = rsub == ksub
    heads = []
    for h in range(X_HEADS):
        sl = slice(h * xh, (h + 1) * xh)
        q = jnp.dot(hq, wq_ref[:, sl], preferred_element_type=F32) * (xh ** -0.5)
        kh = mk_ref[:, h, :].astype(BF16)
        logits = lax.dot_general(q.astype(BF16), kh, (((1,), (1,)), ((), ())), preferred_element_type=F32)
        if rows_per_sub is not None:
            logits = jnp.where(same, logits, NEG)
        m = jnp.max(logits, axis=-1, keepdims=True)
        p = jnp.exp(logits - m)
        s = jnp.sum(p, axis=-1, keepdims=True)
        o = jnp.dot(p.astype(BF16), mv_ref[:, h, :].astype(BF16), preferred_element_type=F32) / s
        heads.append(o.astype(BF16))
    x = x + jnp.dot(jnp.concatenate(heads, axis=-1), wo_ref[...], preferred_element_type=F32)
    x2_ref[...] = x
    hf = _rms(x, gff_ref[...])
    h_ref[...] = hf
    lg_ref[...] = jnp.dot(hf.astype(BF16), wr_ref[...].astype(BF16), preferred_element_type=F32) + br_ref[...]


def _mix_xattn(x, oa, ob, mk, mv, w, groups, rows_per_group, tm, rows_per_sub, keys_per_sub):
    t, d = x.shape
    nt = rows_per_group // tm
    nkv = mk.shape[0] // groups
    ne = w["w_router"].shape[1]
    row = lambda width: pl.BlockSpec((tm, width), lambda g, i: (g * nt + i, 0))
    const = lambda a: pl.BlockSpec(a.shape, lambda g, i: (0,) * a.ndim)
    kv = pl.BlockSpec((nkv,) + mk.shape[1:], lambda g, i: (g, 0, 0))
    consts = [w["w_out_a"], w["w_out_b"], w["g_x"], w["w_xq"], w["w_xo"], w["g_ff"], w["w_router"], w["b_router"]]
    return pl.pallas_call(
        functools.partial(_mix_xattn_kernel, rows_per_sub=rows_per_sub, keys_per_sub=keys_per_sub),
        grid=(groups, nt),
        in_specs=[row(d), row(oa.shape[1]), row(ob.shape[1]), kv, kv] + [const(a) for a in consts],
        out_specs=[row(d), row(d), row(ne)],
        out_shape=[jax.ShapeDtypeStruct((t, d), F32), jax.ShapeDtypeStruct((t, d), F32),
                   jax.ShapeDtypeStruct((t, ne), F32)],
        compiler_params=_cparams(("arbitrary", "arbitrary")),
        name="mix_xattn",
    )(x, oa, ob, mk, mv, *consts)


def _moe_gather(h_hbm, tok_ref, buf, sem, slot):
    def body(i, carry):
        pltpu.make_async_copy(h_hbm.at[pl.ds(tok_ref[0, 0, i], 1)], buf.at[slot, pl.ds(i, 1)],
                              sem.at[slot]).start()
        return carry
    lax.fori_loop(0, MOE_BM, body, 0)


def _moe_ffn_kernel(be_ref, nu_ref, tok_ref, tokn_ref, h_hbm, wgu_ref, bgu_ref, wd_ref, bd_ref, o_ref,
                    buf, sem, wgu_bf, wd_bf):
    b = pl.program_id(0)
    n_used = nu_ref[0]
    slot = b % 2
    dff = wd_ref.shape[1]

    @pl.when(b == 0)
    def _():
        _moe_gather(h_hbm, tok_ref, buf, sem, 0)

    @pl.when(b + 1 < n_used)
    def _():
        _moe_gather(h_hbm, tokn_ref, buf, sem, 1 - slot)

    new_expert = jnp.logical_or(b == 0, be_ref[b] != be_ref[jnp.maximum(b - 1, 0)])

    @pl.when(jnp.logical_and(b < n_used, new_expert))
    def _():
        wgu_bf[...] = wgu_ref[0].astype(BF16)
        wd_bf[...] = wd_ref[0].astype(BF16)

    @pl.when(b < n_used)
    def _():
        pltpu.make_async_copy(h_hbm.at[pl.ds(0, MOE_BM)], buf.at[slot], sem.at[slot]).wait()
        xb = buf[slot].astype(BF16)
        hu = jnp.dot(xb, wgu_bf[...], preferred_element_type=F32) + bgu_ref[0]
        g = jnp.minimum(hu[:, :dff], SWIGLU_LIMIT)
        u = jnp.clip(hu[:, dff:], -SWIGLU_LIMIT, SWIGLU_LIMIT)
        a = g * jax.nn.sigmoid(SWIGLU_ALPHA * g) * (u + 1.0)
        o_ref[...] = jnp.dot(a.astype(BF16), wd_bf[...], preferred_element_type=F32) + bd_ref[0]

    @pl.when(b >= n_used)
    def _():
        o_ref[...] = jnp.zeros_like(o_ref)


def _moe_ffn(h, slot_tok, block_e, n_used, w_gate_up, b_gate_up, w_down, b_down):
    t, d = h.shape
    ne, _, dff2 = w_gate_up.shape
    dff = dff2 // 2
    nb = block_e.shape[0]
    tok3 = slot_tok.reshape(nb, 1, MOE_BM)
    grid_spec = pltpu.PrefetchScalarGridSpec(
        num_scalar_prefetch=2,
        grid=(nb,),
        in_specs=[
            pl.BlockSpec((1, 1, MOE_BM), lambda b, be, nu: (b, 0, 0), memory_space=pltpu.SMEM),
            pl.BlockSpec((1, 1, MOE_BM), lambda b, be, nu: (jnp.minimum(b + 1, nb - 1), 0, 0),
                         memory_space=pltpu.SMEM),
            pl.BlockSpec(memory_space=pl.ANY),
            pl.BlockSpec((1, d, dff2), lambda b, be, nu: (be[b], 0, 0)),
            pl.BlockSpec((1, 1, dff2), lambda b, be, nu: (be[b], 0, 0)),
            pl.BlockSpec((1, dff, d), lambda b, be, nu: (be[b], 0, 0)),
            pl.BlockSpec((1, 1, d), lambda b, be, nu: (be[b], 0, 0)),
        ],
        out_specs=pl.BlockSpec((MOE_BM, d), lambda b, be, nu: (b, 0)),
        scratch_shapes=[
            pltpu.VMEM((2, MOE_BM, d), F32),
            pltpu.SemaphoreType.DMA((2,)),
            pltpu.VMEM((d, dff2), BF16),
            pltpu.VMEM((dff, d), BF16),
        ],
    )
    return pl.pallas_call(
        _moe_ffn_kernel,
        grid_spec=grid_spec,
        out_shape=jax.ShapeDtypeStruct((nb * MOE_BM, d), F32),
        compiler_params=_cparams(("arbitrary",)),
        name="moe_ffn",
    )(block_e, n_used, tok3, tok3, h, w_gate_up, b_gate_up.reshape(ne, 1, dff2), w_down,
      b_down.reshape(ne, 1, d))


def _combine_gather(y_hbm, pos_ref, buf, sem, slot):
    def body(i, carry):
        pltpu.make_async_copy(y_hbm.at[pl.ds(pos_ref[0, 0, i], 1)], buf.at[slot, pl.ds(i, 1)],
                              sem.at[slot]).start()
        return carry
    lax.fori_loop(0, TOP_K * COMBINE_TQ, body, 0)


def _combine_kernel(pos_ref, posn_ref, y_hbm, x_ref, gate_ref, g_ref, o_ref, buf, sem, *, n_tiles):
    i = pl.program_id(0)
    slot = i % 2

    @pl.when(i == 0)
    def _():
        _combine_gather(y_hbm, pos_ref, buf, sem, 0)

    if n_tiles > 1:
        @pl.when(i + 1 < n_tiles)
        def _():
            _combine_gather(y_hbm, posn_ref, buf, sem, 1 - slot)

    pltpu.make_async_copy(y_hbm.at[pl.ds(0, TOP_K * COMBINE_TQ)], buf.at[slot], sem.at[slot]).wait()
    x = x_ref[...]
    gates = gate_ref[...]
    for k in range(TOP_K):
        x = x + gates[:, k:k + 1] * buf[slot, k * COMBINE_TQ:(k + 1) * COMBINE_TQ, :]
    o_ref[...] = _rms(x, g_ref[...])


def _combine(y_sorted, x2, pos, gates, g_final):
    t, d = x2.shape
    tq = COMBINE_TQ
    nt = t // tq
    pos3 = pos.reshape(nt, tq, TOP_K).transpose(0, 2, 1).reshape(nt, 1, TOP_K * tq)
    return pl.pallas_call(
        functools.partial(_combine_kernel, n_tiles=nt),
        grid=(nt,),
        in_specs=[
            pl.BlockSpec((1, 1, TOP_K * tq), lambda i: (i, 0, 0), memory_space=pltpu.SMEM),
            pl.BlockSpec((1, 1, TOP_K * tq), lambda i: (jnp.minimum(i + 1, nt - 1), 0, 0),
                         memory_space=pltpu.SMEM),
            pl.BlockSpec(memory_space=pl.ANY),
            pl.BlockSpec((tq, d), lambda i: (i, 0)),
            pl.BlockSpec((tq, TOP_K), lambda i: (i, 0)),
            pl.BlockSpec((1, d), lambda i: (0, 0)),
        ],
        out_specs=pl.BlockSpec((tq, d), lambda i: (i, 0)),
        out_shape=jax.ShapeDtypeStruct((t, d), F32),
        scratch_shapes=[pltpu.VMEM((2, TOP_K * tq, d), F32), pltpu.SemaphoreType.DMA((2,))],
        compiler_params=_cparams(("arbitrary",)),
        name="combine",
    )(pos3, pos3, y_sorted, x2, gates, g_final.reshape(1, d))


def _cache_shift_kernel(ck_ref, cv_ref, kn_ref, vn_ref, ok_ref, ov_ref, sem, *, n_new):
    bd, n_hist = ck_ref.shape[0], ck_ref.shape[1]
    keep = n_hist - n_new
    copies = []
    for src, new, dst in ((ck_ref, kn_ref, ok_ref), (cv_ref, vn_ref, ov_ref)):
        for b in range(bd):
            copies.append(pltpu.make_async_copy(src.at[b, pl.ds(n_new, keep)], dst.at[b, pl.ds(0, keep)], sem.at[0]))
        copies.append(pltpu.make_async_copy(new, dst.at[:, pl.ds(keep, n_new)], sem.at[1]))
    for c in copies:
        c.start()
    for c in copies:
        c.wait()


def _cache_shift(cache_k, cache_v, k_new, v_new):
    n_new = k_new.shape[1]
    any_spec = pl.BlockSpec(memory_space=pl.ANY)
    return pl.pallas_call(
        functools.partial(_cache_shift_kernel, n_new=n_new),
        in_specs=[any_spec] * 4,
        out_specs=[any_spec] * 2,
        out_shape=[jax.ShapeDtypeStruct(cache_k.shape, cache_k.dtype)] * 2,
        scratch_shapes=[pltpu.SemaphoreType.DMA((2,))],
        name="cache_shift",
    )(cache_k, cache_v, k_new, v_new)


def _route(logits):
    t = logits.shape[0]
    top_v, top_i = lax.top_k(logits, TOP_K)
    gates = jax.nn.softmax(top_v, axis=-1)
    onehot = (top_i[:, :, None] == jnp.arange(N_EXPERTS, dtype=top_i.dtype)[None, None, :])
    per_tok = jnp.sum(onehot, axis=1).astype(jnp.int32)
    before = jnp.cumsum(per_tok, axis=0) - per_tok
    counts = jnp.sum(per_tok, axis=0)
    nblk = (counts + MOE_BM - 1) // MOE_BM
    blk_end = jnp.cumsum(nblk)
    pstart = (blk_end - nblk) * MOE_BM
    row_of = (pstart[None, :] + before)[:, None, :]
    pos = jnp.sum(jnp.where(onehot, row_of, 0), axis=-1).astype(jnp.int32)
    nb = (t * TOP_K + N_EXPERTS * (MOE_BM - 1) + MOE_BM - 1) // MOE_BM
    n_used = blk_end[-1].astype(jnp.int32)
    tok = jnp.broadcast_to(jnp.arange(t, dtype=jnp.int32)[:, None], (t, TOP_K))
    slot_tok = jnp.zeros((nb * MOE_BM,), jnp.int32).at[pos.reshape(-1)].set(tok.reshape(-1))
    blk = jnp.minimum(jnp.arange(nb, dtype=jnp.int32), n_used - 1)
    block_e = jnp.sum(blk_end[None, :] <= blk[:, None], axis=1).astype(jnp.int32)
    return gates, pos, slot_tok, block_e, n_used.reshape(1)


ROUTE_TT = 128
STRIP_BITS = tuple(1 << j for j in range(ROUTE_TT.bit_length()))
PAD_BITS = tuple(1 << j for j in range((MOE_BM - 1).bit_length()))


def _strip_dmas(n, bits, make_copy, wait=False):
    for bit in bits:
        @pl.when((n & bit) != 0)
        def _(bit=bit):
            c = make_copy(n & (bit - 1), bit)
            if wait:
                c.wait()
            else:
                c.start()


def _dispatch_kernel(d_ref, n_ref, o_ref, ps_ref, pn_ref, lst_ref, hp_ref, hs_ref, xs_hbm, buf, zbuf, sem, psem,
                     *, n_tiles, n_prompt_tiles):
    i = pl.program_id(0)
    slot = i % 2
    rows = TOP_K * ROUTE_TT

    def wait_tile(s):
        pltpu.make_async_copy(xs_hbm.at[pl.ds(0, rows)], buf.at[s], sem.at[s]).wait()

    def put_rows(s, val):
        for c in range(val.shape[1] // LANES):
            buf[s, :, c, :] = val[:, c * LANES:(c + 1) * LANES]

    def pad_strips(wait):
        for e in range(N_EXPERTS):
            start = ps_ref[e]
            _strip_dmas(pn_ref[e], PAD_BITS, lambda off, size: pltpu.make_async_copy(
                zbuf.at[pl.ds(0, size)], xs_hbm.at[pl.ds(start + off, size)], psem.at[0]), wait)
        chunk = zbuf.shape[0]
        used_rows = ps_ref[N_EXPERTS - 1] + pn_ref[N_EXPERTS - 1]

        def body(c, carry):
            cp = pltpu.make_async_copy(zbuf, xs_hbm.at[pl.ds(c * chunk, chunk)], psem.at[0])
            if wait:
                cp.wait()
            else:
                cp.start()
            return carry
        lax.fori_loop(used_rows // chunk, xs_hbm.shape[0] // chunk, body, 0)

    @pl.when(i >= 2)
    def _():
        wait_tile(slot)

    @pl.when(i == 0)
    def _():
        zbuf[...] = jnp.zeros_like(zbuf)
        pad_strips(False)

    srow = lax.broadcasted_iota(jnp.int32, (rows, ROUTE_TT), 0)
    place = srow == lst_ref[0:1, :]
    for k in range(1, TOP_K):
        place = jnp.logical_or(place, srow == lst_ref[k:k + 1, :])
    x = jnp.where(i < n_prompt_tiles, hp_ref[...], hs_ref[...]).astype(BF16)
    put_rows(slot, jnp.dot(jnp.where(place, 1.0, 0.0).astype(BF16), x, preferred_element_type=F32))
    for e in range(N_EXPERTS):
        src0 = o_ref[i * N_EXPERTS + e]
        dst0 = d_ref[i * N_EXPERTS + e]
        _strip_dmas(n_ref[i * N_EXPERTS + e], STRIP_BITS, lambda off, size: pltpu.make_async_copy(
            buf.at[slot, pl.ds(src0 + off, size)], xs_hbm.at[pl.ds(dst0 + off, size)], sem.at[slot]))

    @pl.when(i == n_tiles - 1)
    def _():
        wait_tile(slot)
        if n_tiles > 1:
            wait_tile(1 - slot)
        pad_strips(True)


def _dispatch(hp, hs, ls_t, d_tile, n_tile, o_tile, pad_start, pad_len, n_rows):
    tp, d = hp.shape
    ts = hs.shape[0]
    npt, nst = tp // ROUTE_TT, ts // ROUTE_TT
    assert nst == 1 and tp % ROUTE_TT == 0 and ts % ROUTE_TT == 0
    nt = npt + nst
    grid_spec = pltpu.PrefetchScalarGridSpec(
        num_scalar_prefetch=5,
        grid=(nt,),
        in_specs=[
            pl.BlockSpec((TOP_K, ROUTE_TT), lambda i, *_: (0, i)),
            pl.BlockSpec((ROUTE_TT, d), lambda i, *_: (jnp.minimum(i, npt - 1), 0)),
            pl.BlockSpec((ROUTE_TT, d), lambda i, *_: (0, 0)),
        ],
        out_specs=pl.BlockSpec(memory_space=pl.ANY),
        scratch_shapes=[
            pltpu.VMEM((2, TOP_K * ROUTE_TT, d // LANES, LANES), F32),
            pltpu.VMEM((PAD_BITS[-1], d // LANES, LANES), F32),
            pltpu.SemaphoreType.DMA((2,)),
            pltpu.SemaphoreType.DMA((1,)),
        ],
    )
    return pl.pallas_call(
        functools.partial(_dispatch_kernel, n_tiles=nt, n_prompt_tiles=npt),
        grid_spec=grid_spec,
        out_shape=jax.ShapeDtypeStruct((n_rows, d // LANES, LANES), F32),
        compiler_params=_cparams(("arbitrary",)),
        name="moe_dispatch",
    )(d_tile.reshape(-1), n_tile.reshape(-1), o_tile.reshape(-1), pad_start, pad_len, ls_t, hp, hs)


def _expert_ffn_kernel(be_ref, nu_ref, x_ref, wgu_ref, bgu_ref, wd_ref, bd_ref, o_ref, wgu_bf, wd_bf):
    b = pl.program_id(0)
    dff = wd_ref.shape[1]
    new_expert = jnp.logical_or(b == 0, be_ref[b] != be_ref[jnp.maximum(b - 1, 0)])

    @pl.when(jnp.logical_and(b < nu_ref[0], new_expert))
    def _():
        wgu_bf[...] = wgu_ref[0].astype(BF16)
        wd_bf[...] = wd_ref[0].astype(BF16)

    @pl.when(b < nu_ref[0])
    def _():
        x = jnp.concatenate([x_ref[:, c, :] for c in range(x_ref.shape[1])], axis=-1).astype(BF16)
        hu = jnp.dot(x, wgu_bf[...], preferred_element_type=F32) + bgu_ref[0]
        g = jnp.minimum(hu[:, :dff], SWIGLU_LIMIT)
        u = jnp.clip(hu[:, dff:], -SWIGLU_LIMIT, SWIGLU_LIMIT)
        a = g * jax.nn.sigmoid(SWIGLU_ALPHA * g) * (u + 1.0)
        out = jnp.dot(a.astype(BF16), wd_bf[...], preferred_element_type=F32) + bd_ref[0]
        for c in range(o_ref.shape[1]):
            o_ref[:, c, :] = out[:, c * LANES:(c + 1) * LANES]

    @pl.when(b >= nu_ref[0])
    def _():
        o_ref[...] = jnp.zeros_like(o_ref)


def _expert_ffn(x_sorted, block_e, n_used, w_gate_up, b_gate_up, w_down, b_down):
    n_rows = x_sorted.shape[0]
    ne, d, dff2 = w_gate_up.shape
    dff = dff2 // 2
    nb = n_rows // MOE_BM
    used = lambda b, nu: jnp.minimum(b, nu[0] - 1)
    grid_spec = pltpu.PrefetchScalarGridSpec(
        num_scalar_prefetch=2,
        grid=(nb,),
        in_specs=[
            pl.BlockSpec((MOE_BM, d // LANES, LANES), lambda b, be, nu: (used(b, nu), 0, 0)),
            pl.BlockSpec((1, d, dff2), lambda b, be, nu: (be[b], 0, 0)),
            pl.BlockSpec((1, 1, dff2), lambda b, be, nu: (be[b], 0, 0)),
            pl.BlockSpec((1, dff, d), lambda b, be, nu: (be[b], 0, 0)),
            pl.BlockSpec((1, 1, d), lambda b, be, nu: (be[b], 0, 0)),
        ],
        out_specs=pl.BlockSpec((MOE_BM, d // LANES, LANES), lambda b, be, nu: (b, 0, 0)),
        scratch_shapes=[pltpu.VMEM((d, dff2), BF16), pltpu.VMEM((dff, d), BF16)],
    )
    return pl.pallas_call(
        _expert_ffn_kernel,
        grid_spec=grid_spec,
        out_shape=jax.ShapeDtypeStruct(x_sorted.shape, F32),
        compiler_params=_cparams(("arbitrary",)),
        name="expert_ffn",
    )(block_e, n_used, x_sorted, w_gate_up, b_gate_up.reshape(ne, 1, dff2), w_down, b_down.reshape(ne, 1, d))


def _moe_combine_kernel(d_ref, n_ref, o_ref, ls_ref, gate_ref, xp_ref, xs_ref, g_ref, y_hbm, yp_ref, ys_ref,
                        buf, sem, *, n_tiles, n_prompt_tiles):
    j = pl.program_id(0)
    rows = TOP_K * ROUTE_TT

    @pl.when(j < n_tiles)
    def _():
        slot = j % 2
        for e in range(N_EXPERTS):
            src0 = d_ref[j * N_EXPERTS + e]
            dst0 = o_ref[j * N_EXPERTS + e]
            _strip_dmas(n_ref[j * N_EXPERTS + e], STRIP_BITS, lambda off, size: pltpu.make_async_copy(
                y_hbm.at[pl.ds(src0 + off, size)], buf.at[slot, pl.ds(dst0 + off, size)], sem.at[slot]))

    @pl.when(j >= 1)
    def _():
        slot = (j - 1) % 2
        pltpu.make_async_copy(y_hbm.at[pl.ds(0, rows)], buf.at[slot], sem.at[slot]).wait()
        r = jnp.concatenate([buf[slot, :, c, :] for c in range(buf.shape[2])], axis=-1)
        r_hi = r.astype(BF16)
        r_lo = (r - r_hi.astype(F32)).astype(BF16)
        lane = lax.broadcasted_iota(jnp.int32, (ROUTE_TT, rows), 1)
        pick = jnp.zeros((ROUTE_TT, rows), F32)
        for k in range(TOP_K):
            pick = pick + jnp.where(lane == ls_ref[:, k:k + 1], gate_ref[:, k:k + 1], 0.0)
        p_hi = pick.astype(BF16)
        p_lo = (pick - p_hi.astype(F32)).astype(BF16)
        moe = (jnp.dot(p_hi, r_hi, preferred_element_type=F32) + jnp.dot(p_lo, r_hi, preferred_element_type=F32)
               + jnp.dot(p_hi, r_lo, preferred_element_type=F32))
        is_prompt = j - 1 < n_prompt_tiles
        y = _rms(jnp.where(is_prompt, xp_ref[...], xs_ref[...]) + moe, g_ref[...])

        @pl.when(is_prompt)
        def _():
            yp_ref[...] = y

        @pl.when(jnp.logical_not(is_prompt))
        def _():
            ys_ref[...] = y


def _moe_combine(y_sorted, x2p, x2s, ls, gates, d_tile, n_tile, o_tile, g_final):
    tp, d = x2p.shape
    ts = x2s.shape[0]
    npt, nst = tp // ROUTE_TT, ts // ROUTE_TT
    assert nst == 1
    nt = npt + nst
    tile = lambda j: jnp.clip(j - 1, 0, nt - 1)
    ptile = lambda j: jnp.clip(j - 1, 0, npt - 1)
    grid_spec = pltpu.PrefetchScalarGridSpec(
        num_scalar_prefetch=3,
        grid=(nt + 1,),
        in_specs=[
            pl.BlockSpec((ROUTE_TT, TOP_K), lambda j, *_: (tile(j), 0)),
            pl.BlockSpec((ROUTE_TT, TOP_K), lambda j, *_: (tile(j), 0)),
            pl.BlockSpec((ROUTE_TT, d), lambda j, *_: (ptile(j), 0)),
            pl.BlockSpec((ROUTE_TT, d), lambda j, *_: (0, 0)),
            pl.BlockSpec((1, d), lambda j, *_: (0, 0)),
            pl.BlockSpec(memory_space=pl.ANY),
        ],
        out_specs=[
            pl.BlockSpec((ROUTE_TT, d), lambda j, *_: (ptile(j), 0)),
            pl.BlockSpec((ROUTE_TT, d), lambda j, *_: (0, 0)),
        ],
        scratch_shapes=[pltpu.VMEM((2, TOP_K * ROUTE_TT, d // LANES, LANES), F32), pltpu.SemaphoreType.DMA((2,))],
    )
    return pl.pallas_call(
        functools.partial(_moe_combine_kernel, n_tiles=nt, n_prompt_tiles=npt),
        grid_spec=grid_spec,
        out_shape=[jax.ShapeDtypeStruct((tp, d), F32), jax.ShapeDtypeStruct((ts, d), F32)],
        compiler_params=_cparams(("arbitrary",)),
        name="moe_combine",
    )(d_tile.reshape(-1), n_tile.reshape(-1), o_tile.reshape(-1), ls, gates, x2p, x2s, g_final.reshape(1, d),
      y_sorted)


def _route_tiles(logits):
    t = logits.shape[0]
    nt = t // ROUTE_TT
    top_v, top_i = lax.top_k(logits, TOP_K)
    gates = jax.nn.softmax(top_v, axis=-1)
    onehot = (top_i[:, :, None] == jnp.arange(N_EXPERTS, dtype=top_i.dtype)[None, None, :])
    per_tok = jnp.sum(onehot, axis=1).astype(jnp.int32)
    before = jnp.cumsum(per_tok, axis=0) - per_tok
    counts = jnp.sum(per_tok, axis=0)
    nblk = (counts + MOE_BM - 1) // MOE_BM
    blk_end = jnp.cumsum(nblk)
    pstart = (blk_end - nblk) * MOE_BM
    n_used = blk_end[-1].astype(jnp.int32)
    nb = (t * TOP_K + N_EXPERTS * (MOE_BM - 1) + MOE_BM - 1) // MOE_BM
    blk = jnp.minimum(jnp.arange(nb, dtype=jnp.int32), n_used - 1)
    block_e = jnp.sum(blk_end[None, :] <= blk[:, None], axis=1).astype(jnp.int32)
    before_tile = before[::ROUTE_TT]
    n_tile = jnp.sum(per_tok.reshape(nt, ROUTE_TT, N_EXPERTS), axis=1)
    o_tile = jnp.cumsum(n_tile, axis=1) - n_tile
    d_tile = pstart[None, :] + before_tile
    local = (o_tile - before_tile)[:, None, :] + before.reshape(nt, ROUTE_TT, N_EXPERTS)
    ls = jnp.sum(jnp.where(onehot, local.reshape(t, 1, N_EXPERTS), 0), axis=-1).astype(jnp.int32)
    i32 = lambda a: a.astype(jnp.int32)
    return (gates, ls, i32(d_tile), i32(n_tile), i32(o_tile), i32(pstart + counts), i32(nblk * MOE_BM - counts),
            block_e, n_used.reshape(1), nb * MOE_BM)


TILE_TOKENS = 256
TILE_ROWS = TOP_K * TILE_TOKENS + N_EXPERTS * (SUBLANES - 1)
assert TILE_ROWS % SUBLANES == 0


def _bits(lo, hi):
    return tuple(lo << j for j in range((hi // lo).bit_length()))


STRIP_LEN_BITS = _bits(SUBLANES, TILE_TOKENS)
PAD_LEN_BITS = _bits(SUBLANES, MOE_BM - SUBLANES)
TILE_LEN_BITS = _bits(SUBLANES, TILE_ROWS)


def _tile_strips(i, slot, d_ref, n_ref, o_ref, local, hbm, sem, to_hbm):
    for e in range(N_EXPERTS):
        l0 = pl.multiple_of(o_ref[i * N_EXPERTS + e], SUBLANES)
        g0 = pl.multiple_of(d_ref[i * N_EXPERTS + e], SUBLANES)

        def make_copy(off, size, l0=l0, g0=g0):
            lo = local.at[slot, pl.ds(pl.multiple_of(l0 + off, SUBLANES), size)]
            gl = hbm.at[pl.ds(pl.multiple_of(g0 + off, SUBLANES), size)]
            return pltpu.make_async_copy(lo, gl, sem.at[slot]) if to_hbm else pltpu.make_async_copy(gl, lo, sem.at[slot])
        _strip_dmas(n_ref[i * N_EXPERTS + e], STRIP_LEN_BITS, make_copy)


def _tile_wait(rows, slot, local, hbm, sem):
    _strip_dmas(rows, TILE_LEN_BITS, lambda off, size: pltpu.make_async_copy(
        hbm.at[pl.ds(0, size)], local.at[slot, pl.ds(0, size)], sem.at[slot]), wait=True)


def _dispatch2_kernel(d_ref, n_ref, o_ref, r_ref, ps_ref, pn_ref, lst_ref, hp_ref, hs_ref, xs_hbm, buf, zbuf, sem,
                      psem, *, n_tiles, n_prompt_tiles):
    i = pl.program_id(0)
    slot = i % 2

    def pad_strips(wait):
        for e in range(N_EXPERTS):
            start = ps_ref[e]
            _strip_dmas(pn_ref[e], PAD_LEN_BITS, lambda off, size: pltpu.make_async_copy(
                zbuf.at[pl.ds(0, size)], xs_hbm.at[pl.ds(pl.multiple_of(start + off, SUBLANES), size)],
                psem.at[0]), wait)
        chunk = zbuf.shape[0]
        used_rows = ps_ref[N_EXPERTS - 1] + pn_ref[N_EXPERTS - 1]

        def body(c, carry):
            cp = pltpu.make_async_copy(zbuf, xs_hbm.at[pl.ds(pl.multiple_of(c * chunk, chunk), chunk)], psem.at[0])
            if wait:
                cp.wait()
            else:
                cp.start()
            return carry
        lax.fori_loop(used_rows // chunk, xs_hbm.shape[0] // chunk, body, 0)

    @pl.when(i >= 2)
    def _():
        _tile_wait(r_ref[i - 2], slot, buf, xs_hbm, sem)

    @pl.when(i == 0)
    def _():
        zbuf[...] = jnp.zeros_like(zbuf)
        pad_strips(False)

    srow = lax.broadcasted_iota(jnp.int32, (TILE_ROWS, TILE_TOKENS), 0)
    place = srow == lst_ref[0:1, :]
    for k in range(1, TOP_K):
        place = jnp.logical_or(place, srow == lst_ref[k:k + 1, :])
    x = jnp.where(i < n_prompt_tiles, hp_ref[...], hs_ref[...]).astype(BF16)
    buf[slot] = jnp.dot(jnp.where(place, 1.0, 0.0).astype(BF16), x, preferred_element_type=F32)
    _tile_strips(i, slot, d_ref, n_ref, o_ref, buf, xs_hbm, sem, to_hbm=True)

    @pl.when(i == n_tiles - 1)
    def _():
        _tile_wait(r_ref[i], slot, buf, xs_hbm, sem)
        if n_tiles > 1:
            _tile_wait(r_ref[jnp.maximum(i - 1, 0)], 1 - slot, buf, xs_hbm, sem)
        pad_strips(True)


def _dispatch2(hp, hs, ls_t, tabs, n_rows):
    tp, d = hp.shape
    npt = tp // TILE_TOKENS
    assert tp % TILE_TOKENS == 0 and hs.shape[0] == TILE_TOKENS
    nt = npt + 1
    grid_spec = pltpu.PrefetchScalarGridSpec(
        num_scalar_prefetch=6,
        grid=(nt,),
        in_specs=[
            pl.BlockSpec((TOP_K, TILE_TOKENS), lambda i, *_: (0, i)),
            pl.BlockSpec((TILE_TOKENS, d), lambda i, *_: (jnp.minimum(i, npt - 1), 0)),
            pl.BlockSpec((TILE_TOKENS, d), lambda i, *_: (0, 0)),
        ],
        out_specs=pl.BlockSpec(memory_space=pl.ANY),
        scratch_shapes=[
            pltpu.VMEM((2, TILE_ROWS, d), F32),
            pltpu.VMEM((PAD_LEN_BITS[-1], d), F32),
            pltpu.SemaphoreType.DMA((2,)),
            pltpu.SemaphoreType.DMA((1,)),
        ],
    )
    return pl.pallas_call(
        functools.partial(_dispatch2_kernel, n_tiles=nt, n_prompt_tiles=npt),
        grid_spec=grid_spec,
        out_shape=jax.ShapeDtypeStruct((n_rows, d), F32),
        compiler_params=_cparams(("arbitrary",)),
        name="moe_dispatch",
    )(tabs["d"], tabs["n"], tabs["o"], tabs["rows"], tabs["pad_start"], tabs["pad_len"], ls_t, hp, hs)


def _expert_ffn2_kernel(be_ref, nu_ref, x_ref, wgu_ref, bgu_ref, wd_ref, bd_ref, o_ref, wgu_bf, wd_bf):
    b = pl.program_id(0)
    dff = wd_ref.shape[1]
    new_expert = jnp.logical_or(b == 0, be_ref[b] != be_ref[jnp.maximum(b - 1, 0)])

    @pl.when(jnp.logical_and(b < nu_ref[0], new_expert))
    def _():
        wgu_bf[...] = wgu_ref[0].astype(BF16)
        wd_bf[...] = wd_ref[0].astype(BF16)

    @pl.when(b < nu_ref[0])
    def _():
        hu = jnp.dot(x_ref[...].astype(BF16), wgu_bf[...], preferred_element_type=F32) + bgu_ref[0]
        g = jnp.minimum(hu[:, :dff], SWIGLU_LIMIT)
        u = jnp.clip(hu[:, dff:], -SWIGLU_LIMIT, SWIGLU_LIMIT)
        a = g * jax.nn.sigmoid(SWIGLU_ALPHA * g) * (u + 1.0)
        o_ref[...] = jnp.dot(a.astype(BF16), wd_bf[...], preferred_element_type=F32) + bd_ref[0]

    @pl.when(b >= nu_ref[0])
    def _():
        o_ref[...] = jnp.zeros_like(o_ref)


def _expert_ffn2(x_sorted, block_e, n_used, w_gate_up, b_gate_up, w_down, b_down):
    n_rows, d = x_sorted.shape
    ne, _, dff2 = w_gate_up.shape
    dff = dff2 // 2
    nb = n_rows // MOE_BM
    used = lambda b, nu: jnp.minimum(b, nu[0] - 1)
    grid_spec = pltpu.PrefetchScalarGridSpec(
        num_scalar_prefetch=2,
        grid=(nb,),
        in_specs=[
            pl.BlockSpec((MOE_BM, d), lambda b, be, nu: (used(b, nu), 0)),
            pl.BlockSpec((1, d, dff2), lambda b, be, nu: (be[b], 0, 0)),
            pl.BlockSpec((1, 1, dff2), lambda b, be, nu: (be[b], 0, 0)),
            pl.BlockSpec((1, dff, d), lambda b, be, nu: (be[b], 0, 0)),
            pl.BlockSpec((1, 1, d), lambda b, be, nu: (be[b], 0, 0)),
        ],
        out_specs=pl.BlockSpec((MOE_BM, d), lambda b, be, nu: (b, 0)),
        scratch_shapes=[pltpu.VMEM((d, dff2), BF16), pltpu.VMEM((dff, d), BF16)],
    )
    return pl.pallas_call(
        _expert_ffn2_kernel,
        grid_spec=grid_spec,
        out_shape=jax.ShapeDtypeStruct((n_rows, d), F32),
        compiler_params=_cparams(("arbitrary",)),
        name="expert_ffn",
    )(block_e, n_used, x_sorted, w_gate_up, b_gate_up.reshape(ne, 1, dff2), w_down, b_down.reshape(ne, 1, d))


def _expert_ffn3_kernel(b0_ref, nb_ref, x_hbm, wgu_ref, bgu_ref, wd_ref, bd_ref, y_hbm, xbuf, ybuf, xsem, ysem,
                        wgu_bf, wd_bf, *, n_blocks):
    e = pl.program_id(0)
    b0 = b0_ref[e]
    n = nb_ref[e]
    dff = wd_ref.shape[1]

    def rows(k):
        return pl.ds(pl.multiple_of((b0 + k) * MOE_BM, MOE_BM), MOE_BM)

    def x_copy(k, slot):
        return pltpu.make_async_copy(x_hbm.at[rows(k)], xbuf.at[slot], xsem.at[slot])

    def y_copy(k, slot):
        return pltpu.make_async_copy(ybuf.at[slot], y_hbm.at[rows(k)], ysem.at[slot])

    @pl.when(n > 0)
    def _():
        x_copy(0, 0).start()
        wgu_bf[...] = wgu_ref[0].astype(BF16)
        wd_bf[...] = wd_ref[0].astype(BF16)

        def body(k, carry):
            slot = k % 2

            @pl.when(k + 1 < n)
            def _():
                x_copy(k + 1, 1 - slot).start()

            x_copy(k, slot).wait()

            @pl.when(k >= 2)
            def _():
                y_copy(k - 2, slot).wait()

            hu = jnp.dot(xbuf[slot].astype(BF16), wgu_bf[...], preferred_element_type=F32) + bgu_ref[0]
            g = jnp.minimum(hu[:, :dff], SWIGLU_LIMIT)
            u = jnp.clip(hu[:, dff:], -SWIGLU_LIMIT, SWIGLU_LIMIT)
            a = g * jax.nn.sigmoid(SWIGLU_ALPHA * g) * (u + 1.0)
            ybuf[slot] = jnp.dot(a.astype(BF16), wd_bf[...], preferred_element_type=F32) + bd_ref[0]
            y_copy(k, slot).start()
            return carry
        lax.fori_loop(0, n, body, 0)

        @pl.when(n >= 2)
        def _():
            y_copy(n - 2, n % 2).wait()
        y_copy(n - 1, (n - 1) % 2).wait()

    @pl.when(e == N_EXPERTS - 1)
    def _():
        ybuf[0] = jnp.zeros(ybuf.shape[1:], F32)

        def fill(wait):
            def body(k, carry):
                cp = pltpu.make_async_copy(ybuf.at[0], y_hbm.at[pl.ds(pl.multiple_of(k * MOE_BM, MOE_BM), MOE_BM)],
                                           ysem.at[0])
                if wait:
                    cp.wait()
                else:
                    cp.start()
                return carry
            lax.fori_loop(b0 + n, n_blocks, body, 0)
        fill(False)
        fill(True)


def _expert_ffn3(x_sorted, blk_start, nblk, w_gate_up, b_gate_up, w_down, b_down):
    n_rows, d = x_sorted.shape
    ne, _, dff2 = w_gate_up.shape
    dff = dff2 // 2
    any_spec = pl.BlockSpec(memory_space=pl.ANY)
    grid_spec = pltpu.PrefetchScalarGridSpec(
        num_scalar_prefetch=2,
        grid=(ne,),
        in_specs=[
            any_spec,
            pl.BlockSpec((1, d, dff2), lambda e, *_: (e, 0, 0)),
            pl.BlockSpec((1, 1, dff2), lambda e, *_: (e, 0, 0)),
            pl.BlockSpec((1, dff, d), lambda e, *_: (e, 0, 0)),
            pl.BlockSpec((1, 1, d), lambda e, *_: (e, 0, 0)),
        ],
        out_specs=any_spec,
        scratch_shapes=[
            pltpu.VMEM((2, MOE_BM, d), F32), pltpu.VMEM((2, MOE_BM, d), F32),
            pltpu.SemaphoreType.DMA((2,)), pltpu.SemaphoreType.DMA((2,)),
            pltpu.VMEM((d, dff2), BF16), pltpu.VMEM((dff, d), BF16),
        ],
    )
    return pl.pallas_call(
        functools.partial(_expert_ffn3_kernel, n_blocks=n_rows // MOE_BM),
        grid_spec=grid_spec,
        out_shape=jax.ShapeDtypeStruct((n_rows, d), F32),
        compiler_params=_cparams(("arbitrary",)),
        name="expert_ffn",
    )(blk_start, nblk, x_sorted, w_gate_up, b_gate_up.reshape(ne, 1, dff2), w_down, b_down.reshape(ne, 1, d))


def _combine2_kernel(d_ref, n_ref, o_ref, r_ref, ls_ref, gate_ref, xp_ref, xs_ref, g_ref, y_hbm, yp_ref, ys_ref,
                     buf, sem, *, n_tiles, n_prompt_tiles):
    j = pl.program_id(0)

    @pl.when(j == 0)
    def _():
        buf[...] = jnp.zeros_like(buf)

    @pl.when(j < n_tiles)
    def _():
        _tile_strips(j, j % 2, d_ref, n_ref, o_ref, buf, y_hbm, sem, to_hbm=False)

    @pl.when(j >= 1)
    def _():
        slot = (j - 1) % 2
        _tile_wait(r_ref[j - 1], slot, buf, y_hbm, sem)
        r_hi = buf[slot].astype(BF16)
        lane = lax.broadcasted_iota(jnp.int32, (TILE_TOKENS, TILE_ROWS), 1)
        pick = jnp.zeros((TILE_TOKENS, TILE_ROWS), F32)
        for k in range(TOP_K):
            pick = pick + jnp.where(lane == ls_ref[:, k:k + 1], gate_ref[:, k:k + 1], 0.0)
        p_hi = pick.astype(BF16)
        p_lo = (pick - p_hi.astype(F32)).astype(BF16)
        moe = jnp.dot(p_hi, r_hi, preferred_element_type=F32) + jnp.dot(p_lo, r_hi, preferred_element_type=F32)
        is_prompt = j - 1 < n_prompt_tiles
        y = _rms(jnp.where(is_prompt, xp_ref[...], xs_ref[...]) + moe, g_ref[...])

        @pl.when(is_prompt)
        def _():
            yp_ref[...] = y

        @pl.when(jnp.logical_not(is_prompt))
        def _():
            ys_ref[...] = y


def _combine2(y_sorted, x2p, x2s, ls, gates, tabs, g_final):
    tp, d = x2p.shape
    npt = tp // TILE_TOKENS
    assert x2s.shape[0] == TILE_TOKENS
    nt = npt + 1
    tile = lambda j: jnp.clip(j - 1, 0, nt - 1)
    ptile = lambda j: jnp.clip(j - 1, 0, npt - 1)
    grid_spec = pltpu.PrefetchScalarGridSpec(
        num_scalar_prefetch=4,
        grid=(nt + 1,),
        in_specs=[
            pl.BlockSpec((TILE_TOKENS, TOP_K), lambda j, *_: (tile(j), 0)),
            pl.BlockSpec((TILE_TOKENS, TOP_K), lambda j, *_: (tile(j), 0)),
            pl.BlockSpec((TILE_TOKENS, d), lambda j, *_: (ptile(j), 0)),
            pl.BlockSpec((TILE_TOKENS, d), lambda j, *_: (0, 0)),
            pl.BlockSpec((1, d), lambda j, *_: (0, 0)),
            pl.BlockSpec(memory_space=pl.ANY),
        ],
        out_specs=[
            pl.BlockSpec((TILE_TOKENS, d), lambda j, *_: (ptile(j), 0)),
            pl.BlockSpec((TILE_TOKENS, d), lambda j, *_: (0, 0)),
        ],
        scratch_shapes=[pltpu.VMEM((2, TILE_ROWS, d), F32), pltpu.SemaphoreType.DMA((2,))],
    )
    return pl.pallas_call(
        functools.partial(_combine2_kernel, n_tiles=nt, n_prompt_tiles=npt),
        grid_spec=grid_spec,
        out_shape=[jax.ShapeDtypeStruct((tp, d), F32), jax.ShapeDtypeStruct((TILE_TOKENS, d), F32)],
        compiler_params=_cparams(("arbitrary",)),
        name="moe_combine",
    )(tabs["d"], tabs["n"], tabs["o"], tabs["rows"], ls, gates, x2p, x2s, g_final.reshape(1, d), y_sorted)


def _route_tiles2(logits):
    t = logits.shape[0]
    nt = -(-t // TILE_TOKENS)
    tpad = nt * TILE_TOKENS
    up = lambda a, m: (a + m - 1) // m * m
    top_v, top_i = lax.top_k(logits, TOP_K)
    gates = jax.nn.softmax(top_v, axis=-1)
    onehot = (top_i[:, :, None] == jnp.arange(N_EXPERTS, dtype=top_i.dtype)[None, None, :])
    per_tok = jnp.pad(jnp.sum(onehot, axis=1).astype(jnp.int32), ((0, tpad - t), (0, 0)))
    before = jnp.cumsum(per_tok, axis=0) - per_tok
    n_tile = up(jnp.sum(per_tok.reshape(nt, TILE_TOKENS, N_EXPERTS), axis=1), SUBLANES)
    o_tile = jnp.cumsum(n_tile, axis=1) - n_tile
    rows_e = jnp.sum(n_tile, axis=0)
    nblk = up(rows_e, MOE_BM) // MOE_BM
    blk_end = jnp.cumsum(nblk)
    pstart = (blk_end - nblk) * MOE_BM
    d_tile = pstart[None, :] + jnp.cumsum(n_tile, axis=0) - n_tile
    local = (o_tile - before[::TILE_TOKENS])[:, None, :] + before.reshape(nt, TILE_TOKENS, N_EXPERTS)
    ls = jnp.sum(jnp.where(onehot, local.reshape(tpad, 1, N_EXPERTS)[:t], 0), axis=-1)
    ls = jnp.pad(ls.astype(jnp.int32), ((0, tpad - t), (0, 0)), constant_values=-1)
    gates = jnp.pad(gates, ((0, tpad - t), (0, 0)))
    n_used = blk_end[-1].astype(jnp.int32)
    nb = (t * TOP_K + N_EXPERTS * nt * (SUBLANES - 1) + N_EXPERTS * (MOE_BM - 1) + MOE_BM - 1) // MOE_BM
    blk = jnp.minimum(jnp.arange(nb, dtype=jnp.int32), n_used - 1)
    block_e = jnp.sum(blk_end[None, :] <= blk[:, None], axis=1).astype(jnp.int32)
    i32 = lambda a: a.astype(jnp.int32).reshape(-1)
    tabs = {"d": i32(d_tile), "n": i32(n_tile), "o": i32(o_tile), "rows": i32(jnp.sum(n_tile, axis=1)),
            "pad_start": i32(pstart + rows_e), "pad_len": i32(nblk * MOE_BM - rows_e),
            "blk_start": i32(blk_end - nblk), "nblk": i32(nblk)}
    return gates, ls, tabs, block_e, n_used.reshape(1), nb * MOE_BM


def kernel(x_prompt, x_sample, cache_k_win, cache_v_win, state_pool, cache_mem_k, cache_mem_v, mem_prompt,
           rel_bias, g_mix, w_in, w_pool, pool_scale, w_out, g_mem, g_x, w_xq, w_xk, w_xv, w_xo, g_ff,
           w_router, b_router, w_gate_up, b_gate_up, w_down, b_down, g_final):
    depth = g_mix.shape[0]
    assert depth == 1
    l = 0
    batch, seq, d = x_prompt.shape
    bd, n_new, _ = x_sample.shape
    n_mem = mem_prompt.shape[1]
    bw = d - A_WIDTH
    n_hist = cache_k_win.shape[2]
    n_pool = state_pool.shape[2]
    tp, ts = batch * seq, bd * n_new

    w_in_b = w_in[l].astype(BF16)
    w_pool_b = w_pool[l].astype(BF16)
    w_xkv_b = jnp.concatenate([w_xk[l], w_xv[l]], axis=1).astype(BF16)
    wts = {
        "w_out_a": w_out[l, :A_WIDTH].astype(BF16), "w_out_b": w_out[l, A_WIDTH:].astype(BF16),
        "g_x": g_x[l].reshape(1, d), "w_xq": w_xq[l].astype(BF16), "w_xo": w_xo[l].astype(BF16),
        "g_ff": g_ff[l].reshape(1, d), "w_router": w_router[l], "b_router": b_router[l].reshape(1, N_EXPERTS),
    }
    band_bias = _band_bias(rel_bias)
    step_bias = _step_bias(rel_bias)
    widths = (A_WIDTH, A_WIDTH, A_WIDTH, bw)
    scales = (A_HEAD_DIM ** -0.5, 1.0, 1.0, 1.0)

    xp = x_prompt.reshape(tp, d)
    w_kv_t = jnp.transpose(w_in[l][:, A_WIDTH:3 * A_WIDTH].reshape(d, 2, A_WIDTH), (1, 2, 0)).astype(BF16)
    q, k, v, u, k_t, v_t = _in_proj(xp, g_mix[l], w_in_b, w_kv_t, widths, scales, batch, seq, tm=512)
    o_a = _dil_attn(q, k, v, band_bias, batch, seq)
    o_b = _pool_prompt(u, w_pool_b, pool_scale[l], batch, seq, tm=512)
    mk, mv = _norm_proj(mem_prompt.reshape(batch * n_mem, d), g_mem[l], w_xkv_b, (d, d), (1.0, 1.0), tm=512,
                        heads=(X_HEADS, X_HEADS))
    x2p, hp, lgp = _mix_xattn(xp, o_a, o_b, mk, mv, wts, groups=batch, rows_per_group=seq, tm=512,
                              rows_per_sub=None, keys_per_sub=None)

    xs = x_sample.reshape(ts, d)
    qs, ks, vs, us = _norm_proj(xs, g_mix[l], w_in_b, widths, scales, tm=ts)
    r4 = lambda a: a.reshape(bd, n_new, A_HEADS, A_HEAD_DIM)
    ck, cv = cache_k_win[l], cache_v_win[l]
    new3 = lambda a: a.reshape(bd, n_new, A_WIDTH)
    to_t = lambda c: jnp.transpose(c, (0, 2, 3, 1)).reshape(bd, A_WIDTH, n_hist)
    from_t = lambda c: jnp.transpose(c.reshape(bd, A_HEADS, A_HEAD_DIM, n_hist), (0, 3, 1, 2))[None]
    bias_c, bias_n = _decode_bias(rel_bias, n_hist, n_new)
    o_as, ck_t, cv_t = _decode_attn(new3(qs), new3(ks), new3(vs), to_t(ck), to_t(cv), bias_c, bias_n,
                                    heads_per_step=4)
    o_as = o_as.reshape(ts, A_WIDTH)
    us3 = us.reshape(bd, n_new, bw)
    hist = jnp.concatenate([jnp.zeros((bd, POOL_HIST - n_pool, bw), F32), state_pool[l]], axis=1)
    cur = jnp.concatenate([us3, jnp.zeros((bd, SUBLANES - n_new, bw), F32)], axis=1)
    o_bs = _pool_dec(hist, cur, n_pool, w_pool_b, pool_scale[l])[:, :n_new].reshape(ts, bw)
    sub = 8
    xhd = d // X_HEADS
    x2s, hs, lgs = _mix_xattn(xs, o_as, o_bs, cache_mem_k[l].reshape(bd * n_mem, X_HEADS, xhd),
                              cache_mem_v[l].reshape(bd * n_mem, X_HEADS, xhd), wts, groups=bd // sub,
                              rows_per_group=sub * n_new, tm=sub * n_new, rows_per_sub=n_new, keys_per_sub=n_mem)

    assert ts <= TILE_TOKENS
    pad_s = lambda a: jnp.pad(a, ((0, TILE_TOKENS - ts), (0, 0)))
    gates, ls, tabs, block_e, n_used, n_rows = _route_tiles2(jnp.concatenate([lgp, lgs], axis=0))
    x_sorted = _dispatch2(hp, pad_s(hs), ls.T, tabs, n_rows)
    assert w_gate_up.shape[1] == N_EXPERTS
    y_sorted = _expert_ffn3(x_sorted, tabs["blk_start"], tabs["nblk"], w_gate_up[l], b_gate_up[l], w_down[l],
                            b_down[l])
    y_prompt, y_sample = _combine2(y_sorted, x2p, pad_s(x2s), ls, gates, tabs, g_final)
    y_prompt = y_prompt.reshape(batch, seq, d)
    y_sample = y_sample[:ts].reshape(bd, n_new, d)

    a5 = lambda a, b_: a.reshape(1, b_, -1, A_HEADS, A_HEAD_DIM)
    from_tp = lambda c: jnp.transpose(c.reshape(batch, A_HEADS, A_HEAD_DIM, seq), (0, 3, 1, 2))[None]
    k_win_prompt, v_win_prompt = from_tp(k_t), from_tp(v_t)
    pool_prompt = u.reshape(batch, seq, bw)[:, seq - n_pool:][None]
    mem_k_prompt = mk.reshape(1, batch, n_mem, X_HEADS, xhd)
    mem_v_prompt = mv.reshape(1, batch, n_mem, X_HEADS, xhd)
    k_win_sample, v_win_sample = from_t(ck_t), from_t(cv_t)
    pool_sample = jnp.concatenate([state_pool[l][:, n_new:], us3], axis=1)[None]
    return (y_prompt, y_sample, k_win_prompt, v_win_prompt, pool_prompt, mem_k_prompt, mem_v_prompt,
            k_win_sample, v_win_sample, pool_sample)
```

```python
import functools
import math

import numpy as np
import jax
import jax.numpy as jnp
from jax import lax
from jax.experimental import pallas as pl
from jax.experimental.pallas import tpu as pltpu

F32 = jnp.float32
BF16 = jnp.bfloat16

LANES = 128
SUBLANES = 8
VMEM_LIMIT_BYTES = 56 * 1024 * 1024

A_HEADS = 8
A_HEAD_DIM = 64
A_WIDTH = A_HEADS * A_HEAD_DIM
DILATED = ((128, 1), (512, 4), (2048, 16))
QB = 128
POOL_WINDOWS = (2, 4, 8, 16)
POOL_HIST = 16
X_HEADS = 4
N_EXPERTS = 32
TOP_K = 4
SWIGLU_LIMIT = 7.0
SWIGLU_ALPHA = 1.702
N_BUCKETS = 32
RMS_EPS = 1e-6
NEG = -1e30

MOE_BM = 256
COMBINE_TQ = 128


def _cparams(sem):
    return pltpu.CompilerParams(dimension_semantics=sem, vmem_limit_bytes=VMEM_LIMIT_BYTES)


def _rms(x, g):
    return x * lax.rsqrt(jnp.mean(x * x, axis=-1, keepdims=True) + RMS_EPS) * g


def _norm_proj_kernel(x_ref, g_ref, w_ref, *o_refs, widths, scales, heads):
    hb = _rms(x_ref[...], g_ref[...]).astype(BF16)
    off = 0
    for o_ref, width, scale, nh in zip(o_refs, widths, scales, heads):
        p = jnp.dot(hb, w_ref[:, off:off + width], preferred_element_type=F32)
        p = p if scale == 1.0 else p * scale
        if nh is None:
            o_ref[...] = p
        else:
            hd = width // nh
            for h in range(nh):
                o_ref[:, h, :] = p[:, h * hd:(h + 1) * hd]
        off += width


def _norm_proj(x, g, w_bf16, widths, scales, tm, heads=None):
    t, d = x.shape
    n = w_bf16.shape[1]
    heads = heads or (None,) * len(widths)
    assert sum(widths) == n and t % tm == 0
    shapes = [(wd,) if nh is None else (nh, wd // nh) for wd, nh in zip(widths, heads)]
    return pl.pallas_call(
        functools.partial(_norm_proj_kernel, widths=widths, scales=scales, heads=heads),
        grid=(t // tm,),
        in_specs=[
            pl.BlockSpec((tm, d), lambda i: (i, 0)),
            pl.BlockSpec((1, d), lambda i: (0, 0)),
            pl.BlockSpec((d, n), lambda i: (0, 0)),
        ],
        out_specs=[pl.BlockSpec((tm,) + sh, lambda i, nd=len(sh): (i,) + (0,) * nd) for sh in shapes],
        out_shape=[jax.ShapeDtypeStruct((t,) + sh, F32) for sh in shapes],
        compiler_params=_cparams(("arbitrary",)),
        name="norm_proj",
    )(x, g.reshape(1, d), w_bf16)


def _mem_kv_kernel(x_ref, g_ref, w_ref, k_ref, v_ref, kb_ref, vb_ref, *, heads):
    hb = _rms(x_ref[...], g_ref[...]).astype(BF16)
    d = x_ref.shape[1]
    hd = d // heads
    for j, (o_ref, ob_ref) in enumerate(((k_ref, kb_ref), (v_ref, vb_ref))):
        p = jnp.dot(hb, w_ref[:, j * d:(j + 1) * d], preferred_element_type=F32)
        ob_ref[...] = p.astype(BF16)
        for h in range(heads):
            o_ref[:, h, :] = p[:, h * hd:(h + 1) * hd]


def _mem_kv(x, g, w_bf16, heads, tm):
    t, d = x.shape
    assert w_bf16.shape == (d, 2 * d) and t % tm == 0
    hd = d // heads
    o3 = pl.BlockSpec((tm, heads, hd), lambda i: (i, 0, 0))
    o2 = pl.BlockSpec((tm, d), lambda i: (i, 0))
    return pl.pallas_call(
        functools.partial(_mem_kv_kernel, heads=heads),
        grid=(t // tm,),
        in_specs=[
            pl.BlockSpec((tm, d), lambda i: (i, 0)),
            pl.BlockSpec((1, d), lambda i: (0, 0)),
            pl.BlockSpec((d, 2 * d), lambda i: (0, 0)),
        ],
        out_specs=[o3, o3, o2, o2],
        out_shape=[jax.ShapeDtypeStruct((t, heads, hd), F32)] * 2 + [jax.ShapeDtypeStruct((t, d), BF16)] * 2,
        compiler_params=_cparams(("arbitrary",)),
        name="mem_kv",
    )(x, g.reshape(1, d), w_bf16)


def _in_proj_kernel(x_ref, g_ref, w_ref, wt_ref, *o_refs, widths, scales, n_t):
    hb = _rms(x_ref[...], g_ref[...]).astype(BF16)
    off = 0
    for o_ref, width, scale in zip(o_refs, widths, scales):
        p = jnp.dot(hb, w_ref[:, off:off + width], preferred_element_type=F32)
        o_ref[...] = p if scale == 1.0 else p * scale
        off += width
    for j in range(n_t):
        o_refs[len(widths) + j][0] = lax.dot_general(wt_ref[j], hb, (((1,), (1,)), ((), ())),
                                                    preferred_element_type=F32)


def _in_proj(x, g, w_bf16, wt_bf16, widths, scales, batch, seq, tm):
    t, d = x.shape
    n = w_bf16.shape[1]
    n_t, wt_width, _ = wt_bf16.shape
    nt = seq // tm
    assert sum(widths) == n and seq % tm == 0 and t == batch * seq
    return pl.pallas_call(
        functools.partial(_in_proj_kernel, widths=widths, scales=scales, n_t=n_t),
        grid=(t // tm,),
        in_specs=[
            pl.BlockSpec((tm, d), lambda i: (i, 0)),
            pl.BlockSpec((1, d), lambda i: (0, 0)),
            pl.BlockSpec((d, n), lambda i: (0, 0)),
            pl.BlockSpec(wt_bf16.shape, lambda i: (0, 0, 0)),
        ],
        out_specs=[pl.BlockSpec((tm, wd), lambda i: (i, 0)) for wd in widths]
        + [pl.BlockSpec((1, wt_width, tm), lambda i: (i // nt, 0, i % nt))] * n_t,
        out_shape=[jax.ShapeDtypeStruct((t, wd), F32) for wd in widths]
        + [jax.ShapeDtypeStruct((batch, wt_width, seq), F32)] * n_t,
        compiler_params=_cparams(("arbitrary",)),
        name="in_proj",
    )(x, g.reshape(1, d), w_bf16, wt_bf16)


def _t5_bucket_np(n, max_dist):
    max_exact = N_BUCKETS // 2
    nf = np.maximum(n, 1).astype(np.float32)
    large = max_exact + (
        np.log(nf / np.float32(max_exact)) / np.float32(math.log(max_dist / max_exact))
        * np.float32(N_BUCKETS - max_exact)
    ).astype(np.int32)
    return np.where(n < max_exact, n, np.minimum(large, N_BUCKETS - 1))


def _band_bias(rel_bias):
    max_dist = max(w for w, _ in DILATED)
    qi = np.arange(QB)[:, None]
    ki = np.arange(2 * QB)[None, :]
    j = qi + QB - ki
    tabs = []
    for window, dil in DILATED:
        steps = window // dil
        in_band = (j >= 0) & (j <= steps)
        bucket = _t5_bucket_np(np.clip(j, 0, steps) * dil, max_dist)
        onehot = (bucket[..., None] == np.arange(N_BUCKETS)).astype(np.float32)
        b = jnp.einsum("qkb,bh->hqk", onehot, rel_bias.astype(F32), precision=lax.Precision.HIGHEST)
        tabs.append(jnp.where(in_band[None], b, NEG))
    return jnp.stack(tabs)


def _step_bias(rel_bias):
    max_dist = max(w for w, _ in DILATED)
    tabs = []
    for window, dil in DILATED:
        steps = window // dil
        bucket = _t5_bucket_np(np.arange(steps, -1, -1) * dil, max_dist)
        tabs.append(rel_bias[bucket].astype(F32)[:, :, None])
    return jnp.stack(tabs)


def _decode_bias(rel_bias, n_hist, n_new):
    max_dist = max(w for w, _ in DILATED)
    t = np.arange(n_new)[:, None]
    tabs_c, tabs_n = [], []
    for window, dil in DILATED:
        out = []
        for dist in (n_hist + t - np.arange(n_hist)[None, :], t - np.arange(n_new)[None, :]):
            ok = (dist >= 0) & (dist % dil == 0) & (dist <= window)
            onehot = (_t5_bucket_np(np.clip(dist, 0, window), max_dist)[..., None] == np.arange(N_BUCKETS))
            b = jnp.einsum("tpb,bh->htp", onehot.astype(np.float32), rel_bias.astype(F32),
                           precision=lax.Precision.HIGHEST)
            out.append(jnp.where(ok[None], b, NEG))
        tabs_c.append(out[0])
        tabs_n.append(out[1])
    return jnp.stack(tabs_c), jnp.stack(tabs_n)


def _decode_attn_kernel(q_ref, kn_ref, vn_ref, knt_ref, vnt_ref, ck_ref, cv_ref, bc_ref, bn_ref,
                        o_ref, ok_ref, ov_ref, *, n_new, heads):
    n_hist = ck_ref.shape[2]
    hd = A_HEAD_DIM
    outs = []
    for h in range(heads):
        rows = slice(h * hd, (h + 1) * hd)
        q = q_ref[0, :, rows].astype(BF16)
        kt = ck_ref[0, rows, :].astype(BF16)
        vt = cv_ref[0, rows, :].astype(BF16)
        lc = jnp.dot(q, kt, preferred_element_type=F32)
        ln = lax.dot_general(q, kn_ref[0, :, rows].astype(BF16), (((1,), (1,)), ((), ())),
                             preferred_element_type=F32)
        vn = vn_ref[0, :, rows].astype(BF16)
        o_br, lse_br = [], []
        for br in range(len(DILATED)):
            bl = lc + bc_ref[br, h]
            bln = ln + bn_ref[br, h]
            m = jnp.maximum(jnp.max(bl, axis=-1, keepdims=True), jnp.max(bln, axis=-1, keepdims=True))
            p = jnp.exp(bl - m)
            pn = jnp.exp(bln - m)
            s = jnp.sum(p, axis=-1, keepdims=True) + jnp.sum(pn, axis=-1, keepdims=True)
            o = lax.dot_general(p.astype(BF16), vt, (((1,), (1,)), ((), ())), preferred_element_type=F32)
            o = o + jnp.dot(pn.astype(BF16), vn, preferred_element_type=F32)
            o_br.append(o / s)
            lse_br.append(m + jnp.log(s))
        m = jnp.maximum(jnp.maximum(lse_br[0], lse_br[1]), lse_br[2])
        es = [jnp.exp(l - m) for l in lse_br]
        outs.append((es[0] * o_br[0] + es[1] * o_br[1] + es[2] * o_br[2]) / (es[0] + es[1] + es[2]))
    o_ref[0] = jnp.concatenate(outs, axis=-1)

    lane = lax.broadcasted_iota(jnp.int32, (ck_ref.shape[1], LANES), 1)
    for c_ref, nt_ref, dst in ((ck_ref, knt_ref, ok_ref), (cv_ref, vnt_ref, ov_ref)):
        shifted = pltpu.roll(c_ref[0], n_hist - n_new, axis=1)
        tail = shifted[:, n_hist - LANES:]
        for j in range(n_new):
            tail = jnp.where(lane == LANES - n_new + j, nt_ref[0, :, j:j + 1], tail)
        dst[0, :, :n_hist - LANES] = shifted[:, :n_hist - LANES]
        dst[0, :, n_hist - LANES:] = tail


def _decode_attn(q, k_new, v_new, ck_t, cv_t, bias_c, bias_n, heads_per_step):
    bd, n_new, aw = q.shape
    n_hist = ck_t.shape[2]
    gw = heads_per_step * A_HEAD_DIM
    ng = aw // gw
    row = pl.BlockSpec((1, n_new, gw), lambda b, g: (b, 0, g))
    col = pl.BlockSpec((1, gw, n_new), lambda b, g: (b, g, 0))
    cache = pl.BlockSpec((1, gw, n_hist), lambda b, g: (b, g, 0))
    nbr = len(DILATED)
    return pl.pallas_call(
        functools.partial(_decode_attn_kernel, n_new=n_new, heads=heads_per_step),
        grid=(bd, ng),
        in_specs=[row, row, row, col, col, cache, cache,
                  pl.BlockSpec((nbr, heads_per_step, n_new, n_hist), lambda b, g: (0, g, 0, 0)),
                  pl.BlockSpec((nbr, heads_per_step, n_new, n_new), lambda b, g: (0, g, 0, 0))],
        out_specs=[row, cache, cache],
        out_shape=[jax.ShapeDtypeStruct(q.shape, F32), jax.ShapeDtypeStruct(ck_t.shape, F32),
                   jax.ShapeDtypeStruct(cv_t.shape, F32)],
        compiler_params=_cparams(("arbitrary", "arbitrary")),
        name="decode_attn",
    )(q, k_new, v_new, jnp.swapaxes(k_new, 1, 2), jnp.swapaxes(v_new, 1, 2), ck_t, cv_t, bias_c, bias_n)


def _dil_attn_kernel(q_ref, k_ref, v_ref, bias_ref, o_ref, obr_ref, lbr_ref, *, seq):
    lane = lax.broadcasted_iota(jnp.int32, (QB, LANES), 1)
    head0 = lane < A_HEAD_DIM

    def rows(ref, start, n, dil):
        if dil == 1:
            return ref[pl.ds(start, n), :]
        return ref[pl.ds(start, n, stride=dil), :]

    def block(br, dil, qstart, kstart, nk):
        qs = rows(q_ref, qstart, QB, dil)
        ks = rows(k_ref, kstart, nk, dil).astype(BF16)
        vs = rows(v_ref, kstart, nk, dil).astype(BF16)
        outs, lses = [], []
        for hh in range(2):
            keep = head0 if hh == 0 else jnp.logical_not(head0)
            qm = jnp.where(keep, qs, 0.0).astype(BF16)
            logits = lax.dot_general(qm, ks, (((1,), (1,)), ((), ())), preferred_element_type=F32)
            logits = logits + bias_ref[br, hh, :, 2 * QB - nk:]
            m = jnp.max(logits, axis=-1, keepdims=True)
            p = jnp.exp(logits - m)
            s = jnp.sum(p, axis=-1, keepdims=True)
            o = jnp.dot(p.astype(BF16), vs, preferred_element_type=F32)
            outs.append(o / s)
            lses.append(jnp.broadcast_to(m + jnp.log(s), (QB, LANES)))
        o = jnp.where(head0, outs[0], outs[1])
        lse = jnp.where(head0, lses[0], lses[1])
        if dil == 1:
            obr_ref[br, pl.ds(qstart, QB), :] = o
            lbr_ref[br, pl.ds(qstart, QB), :] = lse
        else:
            obr_ref[br, pl.ds(qstart, QB, stride=dil), :] = o
            lbr_ref[br, pl.ds(qstart, QB, stride=dil), :] = lse

    for br, (window, dil) in enumerate(DILATED):
        assert window // dil == QB
        nblk = seq // (dil * QB)
        for r in range(dil):
            block(br, dil, r, r, QB)
            for i in range(1, nblk):
                qstart = r + dil * QB * i
                block(br, dil, qstart, qstart - dil * QB, 2 * QB)

    l0, l1, l2 = lbr_ref[0], lbr_ref[1], lbr_ref[2]
    m = jnp.maximum(jnp.maximum(l0, l1), l2)
    e0, e1, e2 = jnp.exp(l0 - m), jnp.exp(l1 - m), jnp.exp(l2 - m)
    acc = e0 * obr_ref[0] + e1 * obr_ref[1] + e2 * obr_ref[2]
    o_ref[...] = acc / (e0 + e1 + e2)


def _dil_attn(q, k, v, band_bias, batch, seq):
    t, aw = q.shape
    npair = aw // LANES
    bias = band_bias.reshape(len(DILATED), npair, 2, QB, 2 * QB)
    spec = pl.BlockSpec((seq, LANES), lambda b, hp: (b, hp))
    return pl.pallas_call(
        functools.partial(_dil_attn_kernel, seq=seq),
        grid=(batch, npair),
        in_specs=[
            spec, spec, spec,
            pl.BlockSpec((len(DILATED), None, 2, QB, 2 * QB), lambda b, hp: (0, hp, 0, 0, 0)),
        ],
        out_specs=spec,
        out_shape=jax.ShapeDtypeStruct((t, aw), F32),
        scratch_shapes=[
            pltpu.VMEM((len(DILATED), seq, LANES), F32),
            pltpu.VMEM((len(DILATED), seq, LANES), F32),
        ],
        compiler_params=_cparams(("arbitrary", "arbitrary")),
        name="dil_attn",
    )(q, k, v, bias)


def _dil_attn_dec_kernel(q_ref, kn_ref, vn_ref, k1_ref, v1_ref, k2_ref, v2_ref, k3_ref, v3_ref,
                         rbias_ref, o_ref, *, n_new):
    kn = kn_ref[0]
    vn = vn_ref[0]
    for t in range(n_new):
        q = q_ref[0, t][None]
        outs, lses = [], []
        for br, (window, dil) in enumerate(DILATED):
            if dil == 1:
                kc, vc = k1_ref[0, t:], v1_ref[0, t:]
                bc = rbias_ref[br, 0:QB - t]
                kn_t, vn_t = kn[:t + 1], vn[:t + 1]
                bn = rbias_ref[br, QB - t:QB + 1]
            else:
                kref, vref = (k2_ref, v2_ref) if br == 1 else (k3_ref, v3_ref)
                kc, vc = kref[0, :, t], vref[0, :, t]
                bc = rbias_ref[br, 0:QB]
                kn_t, vn_t = kn[t:t + 1], vn[t:t + 1]
                bn = rbias_ref[br, QB:QB + 1]
            lc = jnp.sum(kc * q, axis=-1, keepdims=True) + bc
            ln = jnp.sum(kn_t * q, axis=-1, keepdims=True) + bn
            m = jnp.maximum(jnp.max(lc, axis=0, keepdims=True), jnp.max(ln, axis=0, keepdims=True))
            pc = jnp.exp(lc - m)
            pn = jnp.exp(ln - m)
            s = jnp.sum(pc, axis=0, keepdims=True) + jnp.sum(pn, axis=0, keepdims=True)
            o = jnp.sum(pc * vc, axis=0, keepdims=True) + jnp.sum(pn * vn_t, axis=0, keepdims=True)
            outs.append(o / s)
            lses.append(m + jnp.log(s))
        m = jnp.maximum(jnp.maximum(lses[0], lses[1]), lses[2])
        es = [jnp.exp(l - m) for l in lses]
        acc = es[0] * outs[0] + es[1] * outs[1] + es[2] * outs[2]
        o_ref[0, t] = (acc / (es[0] + es[1] + es[2]))[0]


def _dil_attn_dec(q, k_new, v_new, cache_k, cache_v, step_bias):
    bd, n_new, h, dh = q.shape
    n_hist = cache_k.shape[1]
    (w1, d1), (w2, d2), (w3, d3) = DILATED
    assert d1 == 1 and n_hist >= w3 and n_new <= d2 and n_hist % d3 == 0 and n_hist % d2 == 0
    assert w1 // d1 == QB and w2 // d2 == QB and w3 // d3 == QB

    def views(c):
        c2 = c.reshape(bd, n_hist // d2, d2, h, dh)
        c3 = c.reshape(bd, n_hist // d3, d3, h, dh)
        return c, c2, c3

    k1, k2, k3 = views(cache_k)
    v1, v2, v3 = views(cache_v)
    new_spec = pl.BlockSpec((1, n_new, h, dh), lambda b: (b, 0, 0, 0))
    s1 = pl.BlockSpec((1, QB, h, dh), lambda b: (b, n_hist // QB - 1, 0, 0))
    s2 = pl.BlockSpec((1, QB, n_new, h, dh), lambda b: (b, n_hist // d2 // QB - 1, 0, 0, 0))
    s3 = pl.BlockSpec((1, QB, n_new, h, dh), lambda b: (b, n_hist // d3 // QB - 1, 0, 0, 0))
    return pl.pallas_call(
        functools.partial(_dil_attn_dec_kernel, n_new=n_new),
        grid=(bd,),
        in_specs=[new_spec, new_spec, new_spec, s1, s1, s2, s2, s3, s3,
                  pl.BlockSpec(step_bias.shape, lambda b: (0, 0, 0, 0))],
        out_specs=new_spec,
        out_shape=jax.ShapeDtypeStruct(q.shape, F32),
        compiler_params=_cparams(("arbitrary",)),
        name="dil_attn_dec",
    )(q, k_new, v_new, k1, v1, k2, v2, k3, v3, step_bias)


def _pool_kernel(hist_ref, cur_ref, w_ref, scale_ref, o_ref, *, pos0_of_tile, tm):
    nb = cur_ref.shape[0]
    gdim = w_ref.shape[1]
    i = pl.program_id(1)
    pos0 = pos0_of_tile(i)
    have_hist = jnp.where(pos0 > 0, 1.0, 0.0).astype(F32)
    t = lax.broadcasted_iota(jnp.int32, (1, tm, 1), 1)
    for g, w in enumerate(POOL_WINDOWS):
        sl = slice(g * gdim, (g + 1) * gdim)
        cur = cur_ref[:, :, sl]
        ext = jnp.concatenate([hist_ref[:, :, sl] * have_hist, cur], axis=1)
        acc, span = ext, 1
        while span < w:
            n = acc.shape[1]
            acc = acc[:, span:n] + acc[:, 0:n - span]
            span *= 2
        wsum = acc[:, POOL_HIST + 1 - w:POOL_HIST + 1 - w + tm]
        cnt = jnp.minimum(pos0 + t + 1, w).astype(F32)
        d = (wsum / cnt - cur).astype(BF16).reshape(nb * tm, gdim)
        y = jnp.dot(d, w_ref[g], preferred_element_type=F32) * scale_ref[:, sl]
        o_ref[:, :, sl] = y.reshape(nb, tm, gdim)


def _pool_prompt(u, w_pool_bf16, pool_scale, batch, seq, tm):
    t, bw = u.shape
    nt = seq // tm
    hb = tm // POOL_HIST
    u3 = u.reshape(1, t, bw)
    out = pl.pallas_call(
        functools.partial(_pool_kernel, pos0_of_tile=lambda i: i * tm, tm=tm),
        grid=(batch, nt),
        in_specs=[
            pl.BlockSpec((1, POOL_HIST, bw), lambda b, i: (0, jnp.maximum((b * nt + i) * hb - 1, 0), 0)),
            pl.BlockSpec((1, tm, bw), lambda b, i: (0, b * nt + i, 0)),
            pl.BlockSpec(w_pool_bf16.shape, lambda b, i: (0, 0, 0)),
            pl.BlockSpec((1, bw), lambda b, i: (0, 0)),
        ],
        out_specs=pl.BlockSpec((1, tm, bw), lambda b, i: (0, b * nt + i, 0)),
        out_shape=jax.ShapeDtypeStruct((1, t, bw), F32),
        compiler_params=_cparams(("arbitrary", "arbitrary")),
        name="pool_prompt",
    )(u3, u3, w_pool_bf16, pool_scale.reshape(1, bw))
    return out.reshape(t, bw)


def _pool_dec(hist, cur, n_prev, w_pool_bf16, pool_scale):
    bd, tm, bw = cur.shape
    return pl.pallas_call(
        functools.partial(_pool_kernel, pos0_of_tile=lambda i: n_prev, tm=tm),
        grid=(1, 1),
        in_specs=[
            pl.BlockSpec(hist.shape, lambda b, i: (0, 0, 0)),
            pl.BlockSpec(cur.shape, lambda b, i: (0, 0, 0)),
            pl.BlockSpec(w_pool_bf16.shape, lambda b, i: (0, 0, 0)),
            pl.BlockSpec((1, bw), lambda b, i: (0, 0)),
        ],
        out_specs=pl.BlockSpec(cur.shape, lambda b, i: (0, 0, 0)),
        out_shape=jax.ShapeDtypeStruct(cur.shape, F32),
        compiler_params=_cparams(("arbitrary", "arbitrary")),
        name="pool_dec",
    )(hist, cur, w_pool_bf16, pool_scale.reshape(1, bw))


def _mix_xattn_kernel(x_ref, oa_ref, ob_ref, mk_ref, mv_ref, woa_ref, wob_ref, gx_ref, wq_ref, wo_ref,
                      gff_ref, wr_ref, br_ref, x2_ref, h_ref, lg_ref, *, rows_per_sub, keys_per_sub):
    tm = x_ref.shape[0]
    nkv = mk_ref.shape[0]
    xh = wq_ref.shape[1] // X_HEADS
    x = x_ref[...]
    x = x + jnp.dot(oa_ref[...].astype(BF16), woa_ref[...], preferred_element_type=F32)
    x = x + jnp.dot(ob_ref[...].astype(BF16), wob_ref[...], preferred_element_type=F32)
    hq = _rms(x, gx_ref[...]).astype(BF16)
    if rows_per_sub is not None:
        rsub = (pl.program_id(1) * tm + lax.broadcasted_iota(jnp.int32, (tm, nkv), 0)) // rows_per_sub
        ksub = lax.broadcasted_iota(jnp.int32, (tm, nkv), 1) // keys_per_sub
        same = rsub == ksub
    heads = []
    for h in range(X_HEADS):
        sl = slice(h * xh, (h + 1) * xh)
        q = jnp.dot(hq, wq_ref[:, sl], preferred_element_type=F32) * (xh ** -0.5)
        kh = (mk_ref[:, sl] if len(mk_ref.shape) == 2 else mk_ref[:, h, :]).astype(BF16)
        vh = (mv_ref[:, sl] if len(mv_ref.shape) == 2 else mv_ref[:, h, :]).astype(BF16)
        logits = lax.dot_general(q.astype(BF16), kh, (((1,), (1,)), ((), ())), preferred_element_type=F32)
        if rows_per_sub is not None:
            logits = jnp.where(same, logits, NEG)
        m = jnp.max(logits, axis=-1, keepdims=True)
        p = jnp.exp(logits - m)
        s = jnp.sum(p, axis=-1, keepdims=True)
        o = jnp.dot(p.astype(BF16), vh, preferred_element_type=F32) / s
        heads.append(o.astype(BF16))
    x = x + jnp.dot(jnp.concatenate(heads, axis=-1), wo_ref[...], preferred_element_type=F32)
    x2_ref[...] = x
    hf = _rms(x, gff_ref[...])
    h_ref[...] = hf
    lg_ref[...] = jnp.dot(hf.astype(BF16), wr_ref[...].astype(BF16), preferred_element_type=F32) + br_ref[...]


def _mix_xattn(x, oa, ob, mk, mv, w, groups, rows_per_group, tm, rows_per_sub, keys_per_sub):
    t, d = x.shape
    nt = rows_per_group // tm
    nkv = mk.shape[0] // groups
    ne = w["w_router"].shape[1]
    row = lambda width: pl.BlockSpec((tm, width), lambda g, i: (g * nt + i, 0))
    const = lambda a: pl.BlockSpec(a.shape, lambda g, i: (0,) * a.ndim)
    kv = pl.BlockSpec((nkv,) + mk.shape[1:], lambda g, i: (g,) + (0,) * (mk.ndim - 1))
    consts = [w["w_out_a"], w["w_out_b"], w["g_x"], w["w_xq"], w["w_xo"], w["g_ff"], w["w_router"], w["b_router"]]
    return pl.pallas_call(
        functools.partial(_mix_xattn_kernel, rows_per_sub=rows_per_sub, keys_per_sub=keys_per_sub),
        grid=(groups, nt),
        in_specs=[row(d), row(oa.shape[1]), row(ob.shape[1]), kv, kv] + [const(a) for a in consts],
        out_specs=[row(d), row(d), row(ne)],
        out_shape=[jax.ShapeDtypeStruct((t, d), F32), jax.ShapeDtypeStruct((t, d), F32),
                   jax.ShapeDtypeStruct((t, ne), F32)],
        compiler_params=_cparams(("arbitrary", "arbitrary")),
        name="mix_xattn",
    )(x, oa, ob, mk, mv, *consts)


def _moe_gather(h_hbm, tok_ref, buf, sem, slot):
    def body(i, carry):
        pltpu.make_async_copy(h_hbm.at[pl.ds(tok_ref[0, 0, i], 1)], buf.at[slot, pl.ds(i, 1)],
                              sem.at[slot]).start()
        return carry
    lax.fori_loop(0, MOE_BM, body, 0)


def _moe_ffn_kernel(be_ref, nu_ref, tok_ref, tokn_ref, h_hbm, wgu_ref, bgu_ref, wd_ref, bd_ref, o_ref,
                    buf, sem, wgu_bf, wd_bf):
    b = pl.program_id(0)
    n_used = nu_ref[0]
    slot = b % 2
    dff = wd_ref.shape[1]

    @pl.when(b == 0)
    def _():
        _moe_gather(h_hbm, tok_ref, buf, sem, 0)

    @pl.when(b + 1 < n_used)
    def _():
        _moe_gather(h_hbm, tokn_ref, buf, sem, 1 - slot)

    new_expert = jnp.logical_or(b == 0, be_ref[b] != be_ref[jnp.maximum(b - 1, 0)])

    @pl.when(jnp.logical_and(b < n_used, new_expert))
    def _():
        wgu_bf[...] = wgu_ref[0].astype(BF16)
        wd_bf[...] = wd_ref[0].astype(BF16)

    @pl.when(b < n_used)
    def _():
        pltpu.make_async_copy(h_hbm.at[pl.ds(0, MOE_BM)], buf.at[slot], sem.at[slot]).wait()
        xb = buf[slot].astype(BF16)
        hu = jnp.dot(xb, wgu_bf[...], preferred_element_type=F32) + bgu_ref[0]
        g = jnp.minimum(hu[:, :dff], SWIGLU_LIMIT)
        u = jnp.clip(hu[:, dff:], -SWIGLU_LIMIT, SWIGLU_LIMIT)
        a = g * jax.nn.sigmoid(SWIGLU_ALPHA * g) * (u + 1.0)
        o_ref[...] = jnp.dot(a.astype(BF16), wd_bf[...], preferred_element_type=F32) + bd_ref[0]

    @pl.when(b >= n_used)
    def _():
        o_ref[...] = jnp.zeros_like(o_ref)


def _moe_ffn(h, slot_tok, block_e, n_used, w_gate_up, b_gate_up, w_down, b_down):
    t, d = h.shape
    ne, _, dff2 = w_gate_up.shape
    dff = dff2 // 2
    nb = block_e.shape[0]
    tok3 = slot_tok.reshape(nb, 1, MOE_BM)
    grid_spec = pltpu.PrefetchScalarGridSpec(
        num_scalar_prefetch=2,
        grid=(nb,),
        in_specs=[
            pl.BlockSpec((1, 1, MOE_BM), lambda b, be, nu: (b, 0, 0), memory_space=pltpu.SMEM),
            pl.BlockSpec((1, 1, MOE_BM), lambda b, be, nu: (jnp.minimum(b + 1, nb - 1), 0, 0),
                         memory_space=pltpu.SMEM),
            pl.BlockSpec(memory_space=pl.ANY),
            pl.BlockSpec((1, d, dff2), lambda b, be, nu: (be[b], 0, 0)),
            pl.BlockSpec((1, 1, dff2), lambda b, be, nu: (be[b], 0, 0)),
            pl.BlockSpec((1, dff, d), lambda b, be, nu: (be[b], 0, 0)),
            pl.BlockSpec((1, 1, d), lambda b, be, nu: (be[b], 0, 0)),
        ],
        out_specs=pl.BlockSpec((MOE_BM, d), lambda b, be, nu: (b, 0)),
        scratch_shapes=[
            pltpu.VMEM((2, MOE_BM, d), F32),
            pltpu.SemaphoreType.DMA((2,)),
            pltpu.VMEM((d, dff2), BF16),
            pltpu.VMEM((dff, d), BF16),
        ],
    )
    return pl.pallas_call(
        _moe_ffn_kernel,
        grid_spec=grid_spec,
        out_shape=jax.ShapeDtypeStruct((nb * MOE_BM, d), F32),
        compiler_params=_cparams(("arbitrary",)),
        name="moe_ffn",
    )(block_e, n_used, tok3, tok3, h, w_gate_up, b_gate_up.reshape(ne, 1, dff2), w_down,
      b_down.reshape(ne, 1, d))


def _combine_gather(y_hbm, pos_ref, buf, sem, slot):
    def body(i, carry):
        pltpu.make_async_copy(y_hbm.at[pl.ds(pos_ref[0, 0, i], 1)], buf.at[slot, pl.ds(i, 1)],
                              sem.at[slot]).start()
        return carry
    lax.fori_loop(0, TOP_K * COMBINE_TQ, body, 0)


def _combine_kernel(pos_ref, posn_ref, y_hbm, x_ref, gate_ref, g_ref, o_ref, buf, sem, *, n_tiles):
    i = pl.program_id(0)
    slot = i % 2

    @pl.when(i == 0)
    def _():
        _combine_gather(y_hbm, pos_ref, buf, sem, 0)

    if n_tiles > 1:
        @pl.when(i + 1 < n_tiles)
        def _():
            _combine_gather(y_hbm, posn_ref, buf, sem, 1 - slot)

    pltpu.make_async_copy(y_hbm.at[pl.ds(0, TOP_K * COMBINE_TQ)], buf.at[slot], sem.at[slot]).wait()
    x = x_ref[...]
    gates = gate_ref[...]
    for k in range(TOP_K):
        x = x + gates[:, k:k + 1] * buf[slot, k * COMBINE_TQ:(k + 1) * COMBINE_TQ, :]
    o_ref[...] = _rms(x, g_ref[...])


def _combine(y_sorted, x2, pos, gates, g_final):
    t, d = x2.shape
    tq = COMBINE_TQ
    nt = t // tq
    pos3 = pos.reshape(nt, tq, TOP_K).transpose(0, 2, 1).reshape(nt, 1, TOP_K * tq)
    return pl.pallas_call(
        functools.partial(_combine_kernel, n_tiles=nt),
        grid=(nt,),
        in_specs=[
            pl.BlockSpec((1, 1, TOP_K * tq), lambda i: (i, 0, 0), memory_space=pltpu.SMEM),
            pl.BlockSpec((1, 1, TOP_K * tq), lambda i: (jnp.minimum(i + 1, nt - 1), 0, 0),
                         memory_space=pltpu.SMEM),
            pl.BlockSpec(memory_space=pl.ANY),
            pl.BlockSpec((tq, d), lambda i: (i, 0)),
            pl.BlockSpec((tq, TOP_K), lambda i: (i, 0)),
            pl.BlockSpec((1, d), lambda i: (0, 0)),
        ],
        out_specs=pl.BlockSpec((tq, d), lambda i: (i, 0)),
        out_shape=jax.ShapeDtypeStruct((t, d), F32),
        scratch_shapes=[pltpu.VMEM((2, TOP_K * tq, d), F32), pltpu.SemaphoreType.DMA((2,))],
        compiler_params=_cparams(("arbitrary",)),
        name="combine",
    )(pos3, pos3, y_sorted, x2, gates, g_final.reshape(1, d))


def _cache_shift_kernel(ck_ref, cv_ref, kn_ref, vn_ref, ok_ref, ov_ref, sem, *, n_new):
    bd, n_hist = ck_ref.shape[0], ck_ref.shape[1]
    keep = n_hist - n_new
    copies = []
    for src, new, dst in ((ck_ref, kn_ref, ok_ref), (cv_ref, vn_ref, ov_ref)):
        for b in range(bd):
            copies.append(pltpu.make_async_copy(src.at[b, pl.ds(n_new, keep)], dst.at[b, pl.ds(0, keep)], sem.at[0]))
        copies.append(pltpu.make_async_copy(new, dst.at[:, pl.ds(keep, n_new)], sem.at[1]))
    for c in copies:
        c.start()
    for c in copies:
        c.wait()


def _cache_shift(cache_k, cache_v, k_new, v_new):
    n_new = k_new.shape[1]
    any_spec = pl.BlockSpec(memory_space=pl.ANY)
    return pl.pallas_call(
        functools.partial(_cache_shift_kernel, n_new=n_new),
        in_specs=[any_spec] * 4,
        out_specs=[any_spec] * 2,
        out_shape=[jax.ShapeDtypeStruct(cache_k.shape, cache_k.dtype)] * 2,
        scratch_shapes=[pltpu.SemaphoreType.DMA((2,))],
        name="cache_shift",
    )(cache_k, cache_v, k_new, v_new)


def _route(logits):
    t = logits.shape[0]
    top_v, top_i = lax.top_k(logits, TOP_K)
    gates = jax.nn.softmax(top_v, axis=-1)
    onehot = (top_i[:, :, None] == jnp.arange(N_EXPERTS, dtype=top_i.dtype)[None, None, :])
    per_tok = jnp.sum(onehot, axis=1).astype(jnp.int32)
    before = jnp.cumsum(per_tok, axis=0) - per_tok
    counts = jnp.sum(per_tok, axis=0)
    nblk = (counts + MOE_BM - 1) // MOE_BM
    blk_end = jnp.cumsum(nblk)
    pstart = (blk_end - nblk) * MOE_BM
    row_of = (pstart[None, :] + before)[:, None, :]
    pos = jnp.sum(jnp.where(onehot, row_of, 0), axis=-1).astype(jnp.int32)
    nb = (t * TOP_K + N_EXPERTS * (MOE_BM - 1) + MOE_BM - 1) // MOE_BM
    n_used = blk_end[-1].astype(jnp.int32)
    tok = jnp.broadcast_to(jnp.arange(t, dtype=jnp.int32)[:, None], (t, TOP_K))
    slot_tok = jnp.zeros((nb * MOE_BM,), jnp.int32).at[pos.reshape(-1)].set(tok.reshape(-1))
    blk = jnp.minimum(jnp.arange(nb, dtype=jnp.int32), n_used - 1)
    block_e = jnp.sum(blk_end[None, :] <= blk[:, None], axis=1).astype(jnp.int32)
    return gates, pos, slot_tok, block_e, n_used.reshape(1)


ROUTE_TT = 128
STRIP_BITS = tuple(1 << j for j in range(ROUTE_TT.bit_length()))
PAD_BITS = tuple(1 << j for j in range((MOE_BM - 1).bit_length()))


def _strip_dmas(n, bits, make_copy, wait=False):
    for bit in bits:
        @pl.when((n & bit) != 0)
        def _(bit=bit):
            c = make_copy(n & (bit - 1), bit)
            if wait:
                c.wait()
            else:
                c.start()


def _dispatch_kernel(d_ref, n_ref, o_ref, ps_ref, pn_ref, lst_ref, hp_ref, hs_ref, xs_hbm, buf, zbuf, sem, psem,
                     *, n_tiles, n_prompt_tiles):
    i = pl.program_id(0)
    slot = i % 2
    rows = TOP_K * ROUTE_TT

    def wait_tile(s):
        pltpu.make_async_copy(xs_hbm.at[pl.ds(0, rows)], buf.at[s], sem.at[s]).wait()

    def put_rows(s, val):
        for c in range(val.shape[1] // LANES):
            buf[s, :, c, :] = val[:, c * LANES:(c + 1) * LANES]

    def pad_strips(wait):
        for e in range(N_EXPERTS):
            start = ps_ref[e]
            _strip_dmas(pn_ref[e], PAD_BITS, lambda off, size: pltpu.make_async_copy(
                zbuf.at[pl.ds(0, size)], xs_hbm.at[pl.ds(start + off, size)], psem.at[0]), wait)
        chunk = zbuf.shape[0]
        used_rows = ps_ref[N_EXPERTS - 1] + pn_ref[N_EXPERTS - 1]

        def body(c, carry):
            cp = pltpu.make_async_copy(zbuf, xs_hbm.at[pl.ds(c * chunk, chunk)], psem.at[0])
            if wait:
                cp.wait()
            else:
                cp.start()
            return carry
        lax.fori_loop(used_rows // chunk, xs_hbm.shape[0] // chunk, body, 0)

    @pl.when(i >= 2)
    def _():
        wait_tile(slot)

    @pl.when(i == 0)
    def _():
        zbuf[...] = jnp.zeros_like(zbuf)
        pad_strips(False)

    srow = lax.broadcasted_iota(jnp.int32, (rows, ROUTE_TT), 0)
    place = srow == lst_ref[0:1, :]
    for k in range(1, TOP_K):
        place = jnp.logical_or(place, srow == lst_ref[k:k + 1, :])
    x = jnp.where(i < n_prompt_tiles, hp_ref[...], hs_ref[...]).astype(BF16)
    put_rows(slot, jnp.dot(jnp.where(place, 1.0, 0.0).astype(BF16), x, preferred_element_type=F32))
    for e in range(N_EXPERTS):
        src0 = o_ref[i * N_EXPERTS + e]
        dst0 = d_ref[i * N_EXPERTS + e]
        _strip_dmas(n_ref[i * N_EXPERTS + e], STRIP_BITS, lambda off, size: pltpu.make_async_copy(
            buf.at[slot, pl.ds(src0 + off, size)], xs_hbm.at[pl.ds(dst0 + off, size)], sem.at[slot]))

    @pl.when(i == n_tiles - 1)
    def _():
        wait_tile(slot)
        if n_tiles > 1:
            wait_tile(1 - slot)
        pad_strips(True)


def _dispatch(hp, hs, ls_t, d_tile, n_tile, o_tile, pad_start, pad_len, n_rows):
    tp, d = hp.shape
    ts = hs.shape[0]
    npt, nst = tp // ROUTE_TT, ts // ROUTE_TT
    assert nst == 1 and tp % ROUTE_TT == 0 and ts % ROUTE_TT == 0
    nt = npt + nst
    grid_spec = pltpu.PrefetchScalarGridSpec(
        num_scalar_prefetch=5,
        grid=(nt,),
        in_specs=[
            pl.BlockSpec((TOP_K, ROUTE_TT), lambda i, *_: (0, i)),
            pl.BlockSpec((ROUTE_TT, d), lambda i, *_: (jnp.minimum(i, npt - 1), 0)),
            pl.BlockSpec((ROUTE_TT, d), lambda i, *_: (0, 0)),
        ],
        out_specs=pl.BlockSpec(memory_space=pl.ANY),
        scratch_shapes=[
            pltpu.VMEM((2, TOP_K * ROUTE_TT, d // LANES, LANES), F32),
            pltpu.VMEM((PAD_BITS[-1], d // LANES, LANES), F32),
            pltpu.SemaphoreType.DMA((2,)),
            pltpu.SemaphoreType.DMA((1,)),
        ],
    )
    return pl.pallas_call(
        functools.partial(_dispatch_kernel, n_tiles=nt, n_prompt_tiles=npt),
        grid_spec=grid_spec,
        out_shape=jax.ShapeDtypeStruct((n_rows, d // LANES, LANES), F32),
        compiler_params=_cparams(("arbitrary",)),
        name="moe_dispatch",
    )(d_tile.reshape(-1), n_tile.reshape(-1), o_tile.reshape(-1), pad_start, pad_len, ls_t, hp, hs)


def _expert_ffn_kernel(be_ref, nu_ref, x_ref, wgu_ref, bgu_ref, wd_ref, bd_ref, o_ref, wgu_bf, wd_bf):
    b = pl.program_id(0)
    dff = wd_ref.shape[1]
    new_expert = jnp.logical_or(b == 0, be_ref[b] != be_ref[jnp.maximum(b - 1, 0)])

    @pl.when(jnp.logical_and(b < nu_ref[0], new_expert))
    def _():
        wgu_bf[...] = wgu_ref[0].astype(BF16)
        wd_bf[...] = wd_ref[0].astype(BF16)

    @pl.when(b < nu_ref[0])
    def _():
        x = jnp.concatenate([x_ref[:, c, :] for c in range(x_ref.shape[1])], axis=-1).astype(BF16)
        hu = jnp.dot(x, wgu_bf[...], preferred_element_type=F32) + bgu_ref[0]
        g = jnp.minimum(hu[:, :dff], SWIGLU_LIMIT)
        u = jnp.clip(hu[:, dff:], -SWIGLU_LIMIT, SWIGLU_LIMIT)
        a = g * jax.nn.sigmoid(SWIGLU_ALPHA * g) * (u + 1.0)
        out = jnp.dot(a.astype(BF16), wd_bf[...], preferred_element_type=F32) + bd_ref[0]
        for c in range(o_ref.shape[1]):
            o_ref[:, c, :] = out[:, c * LANES:(c + 1) * LANES]

    @pl.when(b >= nu_ref[0])
    def _():
        o_ref[...] = jnp.zeros_like(o_ref)


def _expert_ffn(x_sorted, block_e, n_used, w_gate_up, b_gate_up, w_down, b_down):
    n_rows = x_sorted.shape[0]
    ne, d, dff2 = w_gate_up.shape
    dff = dff2 // 2
    nb = n_rows // MOE_BM
    used = lambda b, nu: jnp.minimum(b, nu[0] - 1)
    grid_spec = pltpu.PrefetchScalarGridSpec(
        num_scalar_prefetch=2,
        grid=(nb,),
        in_specs=[
            pl.BlockSpec((MOE_BM, d // LANES, LANES), lambda b, be, nu: (used(b, nu), 0, 0)),
            pl.BlockSpec((1, d, dff2), lambda b, be, nu: (be[b], 0, 0)),
            pl.BlockSpec((1, 1, dff2), lambda b, be, nu: (be[b], 0, 0)),
            pl.BlockSpec((1, dff, d), lambda b, be, nu: (be[b], 0, 0)),
            pl.BlockSpec((1, 1, d), lambda b, be, nu: (be[b], 0, 0)),
        ],
        out_specs=pl.BlockSpec((MOE_BM, d // LANES, LANES), lambda b, be, nu: (b, 0, 0)),
        scratch_shapes=[pltpu.VMEM((d, dff2), BF16), pltpu.VMEM((dff, d), BF16)],
    )
    return pl.pallas_call(
        _expert_ffn_kernel,
        grid_spec=grid_spec,
        out_shape=jax.ShapeDtypeStruct(x_sorted.shape, F32),
        compiler_params=_cparams(("arbitrary",)),
        name="expert_ffn",
    )(block_e, n_used, x_sorted, w_gate_up, b_gate_up.reshape(ne, 1, dff2), w_down, b_down.reshape(ne, 1, d))


def _moe_combine_kernel(d_ref, n_ref, o_ref, ls_ref, gate_ref, xp_ref, xs_ref, g_ref, y_hbm, yp_ref, ys_ref,
                        buf, sem, *, n_tiles, n_prompt_tiles):
    j = pl.program_id(0)
    rows = TOP_K * ROUTE_TT

    @pl.when(j < n_tiles)
    def _():
        slot = j % 2
        for e in range(N_EXPERTS):
            src0 = d_ref[j * N_EXPERTS + e]
            dst0 = o_ref[j * N_EXPERTS + e]
            _strip_dmas(n_ref[j * N_EXPERTS + e], STRIP_BITS, lambda off, size: pltpu.make_async_copy(
                y_hbm.at[pl.ds(src0 + off, size)], buf.at[slot, pl.ds(dst0 + off, size)], sem.at[slot]))

    @pl.when(j >= 1)
    def _():
        slot = (j - 1) % 2
        pltpu.make_async_copy(y_hbm.at[pl.ds(0, rows)], buf.at[slot], sem.at[slot]).wait()
        r = jnp.concatenate([buf[slot, :, c, :] for c in range(buf.shape[2])], axis=-1)
        r_hi = r.astype(BF16)
        r_lo = (r - r_hi.astype(F32)).astype(BF16)
        lane = lax.broadcasted_iota(jnp.int32, (ROUTE_TT, rows), 1)
        pick = jnp.zeros((ROUTE_TT, rows), F32)
        for k in range(TOP_K):
            pick = pick + jnp.where(lane == ls_ref[:, k:k + 1], gate_ref[:, k:k + 1], 0.0)
        p_hi = pick.astype(BF16)
        p_lo = (pick - p_hi.astype(F32)).astype(BF16)
        moe = (jnp.dot(p_hi, r_hi, preferred_element_type=F32) + jnp.dot(p_lo, r_hi, preferred_element_type=F32)
               + jnp.dot(p_hi, r_lo, preferred_element_type=F32))
        is_prompt = j - 1 < n_prompt_tiles
        y = _rms(jnp.where(is_prompt, xp_ref[...], xs_ref[...]) + moe, g_ref[...])

        @pl.when(is_prompt)
        def _():
            yp_ref[...] = y

        @pl.when(jnp.logical_not(is_prompt))
        def _():
            ys_ref[...] = y


def _moe_combine(y_sorted, x2p, x2s, ls, gates, d_tile, n_tile, o_tile, g_final):
    tp, d = x2p.shape
    ts = x2s.shape[0]
    npt, nst = tp // ROUTE_TT, ts // ROUTE_TT
    assert nst == 1
    nt = npt + nst
    tile = lambda j: jnp.clip(j - 1, 0, nt - 1)
    ptile = lambda j: jnp.clip(j - 1, 0, npt - 1)
    grid_spec = pltpu.PrefetchScalarGridSpec(
        num_scalar_prefetch=3,
        grid=(nt + 1,),
        in_specs=[
            pl.BlockSpec((ROUTE_TT, TOP_K), lambda j, *_: (tile(j), 0)),
            pl.BlockSpec((ROUTE_TT, TOP_K), lambda j, *_: (tile(j), 0)),
            pl.BlockSpec((ROUTE_TT, d), lambda j, *_: (ptile(j), 0)),
            pl.BlockSpec((ROUTE_TT, d), lambda j, *_: (0, 0)),
            pl.BlockSpec((1, d), lambda j, *_: (0, 0)),
            pl.BlockSpec(memory_space=pl.ANY),
        ],
        out_specs=[
            pl.BlockSpec((ROUTE_TT, d), lambda j, *_: (ptile(j), 0)),
            pl.BlockSpec((ROUTE_TT, d), lambda j, *_: (0, 0)),
        ],
        scratch_shapes=[pltpu.VMEM((2, TOP_K * ROUTE_TT, d // LANES, LANES), F32), pltpu.SemaphoreType.DMA((2,))],
    )
    return pl.pallas_call(
        functools.partial(_moe_combine_kernel, n_tiles=nt, n_prompt_tiles=npt),
        grid_spec=grid_spec,
        out_shape=[jax.ShapeDtypeStruct((tp, d), F32), jax.ShapeDtypeStruct((ts, d), F32)],
        compiler_params=_cparams(("arbitrary",)),
        name="moe_combine",
    )(d_tile.reshape(-1), n_tile.reshape(-1), o_tile.reshape(-1), ls, gates, x2p, x2s, g_final.reshape(1, d),
      y_sorted)


def _route_tiles(logits):
    t = logits.shape[0]
    nt = t // ROUTE_TT
    top_v, top_i = lax.top_k(logits, TOP_K)
    gates = jax.nn.softmax(top_v, axis=-1)
    onehot = (top_i[:, :, None] == jnp.arange(N_EXPERTS, dtype=top_i.dtype)[None, None, :])
    per_tok = jnp.sum(onehot, axis=1).astype(jnp.int32)
    before = jnp.cumsum(per_tok, axis=0) - per_tok
    counts = jnp.sum(per_tok, axis=0)
    nblk = (counts + MOE_BM - 1) // MOE_BM
    blk_end = jnp.cumsum(nblk)
    pstart = (blk_end - nblk) * MOE_BM
    n_used = blk_end[-1].astype(jnp.int32)
    nb = (t * TOP_K + N_EXPERTS * (MOE_BM - 1) + MOE_BM - 1) // MOE_BM
    blk = jnp.minimum(jnp.arange(nb, dtype=jnp.int32), n_used - 1)
    block_e = jnp.sum(blk_end[None, :] <= blk[:, None], axis=1).astype(jnp.int32)
    before_tile = before[::ROUTE_TT]
    n_tile = jnp.sum(per_tok.reshape(nt, ROUTE_TT, N_EXPERTS), axis=1)
    o_tile = jnp.cumsum(n_tile, axis=1) - n_tile
    d_tile = pstart[None, :] + before_tile
    local = (o_tile - before_tile)[:, None, :] + before.reshape(nt, ROUTE_TT, N_EXPERTS)
    ls = jnp.sum(jnp.where(onehot, local.reshape(t, 1, N_EXPERTS), 0), axis=-1).astype(jnp.int32)
    i32 = lambda a: a.astype(jnp.int32)
    return (gates, ls, i32(d_tile), i32(n_tile), i32(o_tile), i32(pstart + counts), i32(nblk * MOE_BM - counts),
            block_e, n_used.reshape(1), nb * MOE_BM)


TILE_TOKENS = 256
TILE_ROWS = TOP_K * TILE_TOKENS + N_EXPERTS * (SUBLANES - 1)
assert TILE_ROWS % SUBLANES == 0


def _bits(lo, hi):
    return tuple(lo << j for j in range((hi // lo).bit_length()))


STRIP_LEN_BITS = _bits(SUBLANES, TILE_TOKENS)
PAD_LEN_BITS = _bits(SUBLANES, MOE_BM - SUBLANES)
TILE_LEN_BITS = _bits(SUBLANES, TILE_ROWS)


def _tile_strips(i, slot, d_ref, n_ref, o_ref, local, hbm, sem, to_hbm):
    for e in range(N_EXPERTS):
        l0 = pl.multiple_of(o_ref[i * N_EXPERTS + e], SUBLANES)
        g0 = pl.multiple_of(d_ref[i * N_EXPERTS + e], SUBLANES)

        def make_copy(off, size, l0=l0, g0=g0):
            lo = local.at[slot, pl.ds(pl.multiple_of(l0 + off, SUBLANES), size)]
            gl = hbm.at[pl.ds(pl.multiple_of(g0 + off, SUBLANES), size)]
            return pltpu.make_async_copy(lo, gl, sem.at[slot]) if to_hbm else pltpu.make_async_copy(gl, lo, sem.at[slot])
        _strip_dmas(n_ref[i * N_EXPERTS + e], STRIP_LEN_BITS, make_copy)


def _tile_wait(rows, slot, local, hbm, sem):
    _strip_dmas(rows, TILE_LEN_BITS, lambda off, size: pltpu.make_async_copy(
        hbm.at[pl.ds(0, size)], local.at[slot, pl.ds(0, size)], sem.at[slot]), wait=True)


def _dispatch2_kernel(d_ref, n_ref, o_ref, r_ref, ps_ref, pn_ref, lst_ref, hp_ref, hs_ref, xs_hbm, buf, zbuf, sem,
                      psem, *, n_tiles, n_prompt_tiles):
    i = pl.program_id(0)
    slot = i % 2

    def pad_strips(wait):
        for e in range(N_EXPERTS):
            start = ps_ref[e]
            _strip_dmas(pn_ref[e], PAD_LEN_BITS, lambda off, size: pltpu.make_async_copy(
                zbuf.at[pl.ds(0, size)], xs_hbm.at[pl.ds(pl.multiple_of(start + off, SUBLANES), size)],
                psem.at[0]), wait)
        chunk = zbuf.shape[0]
        used_rows = ps_ref[N_EXPERTS - 1] + pn_ref[N_EXPERTS - 1]

        def body(c, carry):
            cp = pltpu.make_async_copy(zbuf, xs_hbm.at[pl.ds(pl.multiple_of(c * chunk, chunk), chunk)], psem.at[0])
            if wait:
                cp.wait()
            else:
                cp.start()
            return carry
        lax.fori_loop(used_rows // chunk, xs_hbm.shape[0] // chunk, body, 0)

    @pl.when(i >= 2)
    def _():
        _tile_wait(r_ref[i - 2], slot, buf, xs_hbm, sem)

    @pl.when(i == 0)
    def _():
        zbuf[...] = jnp.zeros_like(zbuf)
        pad_strips(False)

    srow = lax.broadcasted_iota(jnp.int32, (TILE_ROWS, TILE_TOKENS), 0)
    place = srow == lst_ref[0:1, :]
    for k in range(1, TOP_K):
        place = jnp.logical_or(place, srow == lst_ref[k:k + 1, :])
    x = jnp.where(i < n_prompt_tiles, hp_ref[...], hs_ref[...]).astype(BF16)
    buf[slot] = jnp.dot(jnp.where(place, 1.0, 0.0).astype(BF16), x, preferred_element_type=F32)
    _tile_strips(i, slot, d_ref, n_ref, o_ref, buf, xs_hbm, sem, to_hbm=True)

    @pl.when(i == n_tiles - 1)
    def _():
        _tile_wait(r_ref[i], slot, buf, xs_hbm, sem)
        if n_tiles > 1:
            _tile_wait(r_ref[jnp.maximum(i - 1, 0)], 1 - slot, buf, xs_hbm, sem)
        pad_strips(True)


def _dispatch2(hp, hs, ls_t, tabs, n_rows):
    tp, d = hp.shape
    npt = tp // TILE_TOKENS
    assert tp % TILE_TOKENS == 0 and hs.shape[0] == TILE_TOKENS
    nt = npt + 1
    grid_spec = pltpu.PrefetchScalarGridSpec(
        num_scalar_prefetch=6,
        grid=(nt,),
        in_specs=[
            pl.BlockSpec((TOP_K, TILE_TOKENS), lambda i, *_: (0, i)),
            pl.BlockSpec((TILE_TOKENS, d), lambda i, *_: (jnp.minimum(i, npt - 1), 0)),
            pl.BlockSpec((TILE_TOKENS, d), lambda i, *_: (0, 0)),
        ],
        out_specs=pl.BlockSpec(memory_space=pl.ANY),
        scratch_shapes=[
            pltpu.VMEM((2, TILE_ROWS, d), F32),
            pltpu.VMEM((PAD_LEN_BITS[-1], d), F32),
            pltpu.SemaphoreType.DMA((2,)),
            pltpu.SemaphoreType.DMA((1,)),
        ],
    )
    return pl.pallas_call(
        functools.partial(_dispatch2_kernel, n_tiles=nt, n_prompt_tiles=npt),
        grid_spec=grid_spec,
        out_shape=jax.ShapeDtypeStruct((n_rows, d), F32),
        compiler_params=_cparams(("arbitrary",)),
        name="moe_dispatch",
    )(tabs["d"], tabs["n"], tabs["o"], tabs["rows"], tabs["pad_start"], tabs["pad_len"], ls_t, hp, hs)


def _expert_ffn2_kernel(be_ref, nu_ref, x_ref, wgu_ref, bgu_ref, wd_ref, bd_ref, o_ref, wgu_bf, wd_bf):
    b = pl.program_id(0)
    dff = wd_ref.shape[1]
    new_expert = jnp.logical_or(b == 0, be_ref[b] != be_ref[jnp.maximum(b - 1, 0)])

    @pl.when(jnp.logical_and(b < nu_ref[0], new_expert))
    def _():
        wgu_bf[...] = wgu_ref[0].astype(BF16)
        wd_bf[...] = wd_ref[0].astype(BF16)

    @pl.when(b < nu_ref[0])
    def _():
        hu = jnp.dot(x_ref[...].astype(BF16), wgu_bf[...], preferred_element_type=F32) + bgu_ref[0]
        g = jnp.minimum(hu[:, :dff], SWIGLU_LIMIT)
        u = jnp.clip(hu[:, dff:], -SWIGLU_LIMIT, SWIGLU_LIMIT)
        a = g * jax.nn.sigmoid(SWIGLU_ALPHA * g) * (u + 1.0)
        o_ref[...] = jnp.dot(a.astype(BF16), wd_bf[...], preferred_element_type=F32) + bd_ref[0]

    @pl.when(b >= nu_ref[0])
    def _():
        o_ref[...] = jnp.zeros_like(o_ref)


def _expert_ffn2(x_sorted, block_e, n_used, w_gate_up, b_gate_up, w_down, b_down):
    n_rows, d = x_sorted.shape
    ne, _, dff2 = w_gate_up.shape
    dff = dff2 // 2
    nb = n_rows // MOE_BM
    used = lambda b, nu: jnp.minimum(b, nu[0] - 1)
    grid_spec = pltpu.PrefetchScalarGridSpec(
        num_scalar_prefetch=2,
        grid=(nb,),
        in_specs=[
            pl.BlockSpec((MOE_BM, d), lambda b, be, nu: (used(b, nu), 0)),
            pl.BlockSpec((1, d, dff2), lambda b, be, nu: (be[b], 0, 0)),
            pl.BlockSpec((1, 1, dff2), lambda b, be, nu: (be[b], 0, 0)),
            pl.BlockSpec((1, dff, d), lambda b, be, nu: (be[b], 0, 0)),
            pl.BlockSpec((1, 1, d), lambda b, be, nu: (be[b], 0, 0)),
        ],
        out_specs=pl.BlockSpec((MOE_BM, d), lambda b, be, nu: (b, 0)),
        scratch_shapes=[pltpu.VMEM((d, dff2), BF16), pltpu.VMEM((dff, d), BF16)],
    )
    return pl.pallas_call(
        _expert_ffn2_kernel,
        grid_spec=grid_spec,
        out_shape=jax.ShapeDtypeStruct((n_rows, d), F32),
        compiler_params=_cparams(("arbitrary",)),
        name="expert_ffn",
    )(block_e, n_used, x_sorted, w_gate_up, b_gate_up.reshape(ne, 1, dff2), w_down, b_down.reshape(ne, 1, d))


def _expert_ffn3_kernel(b0_ref, nb_ref, x_hbm, wgu_ref, bgu_ref, wd_ref, bd_ref, y_hbm, xbuf, ybuf, xsem, ysem,
                        wgu_bf, wd_bf, *, n_blocks):
    e = pl.program_id(0)
    b0 = b0_ref[e]
    n = nb_ref[e]
    dff = wd_ref.shape[1]

    def rows(k):
        return pl.ds(pl.multiple_of((b0 + k) * MOE_BM, MOE_BM), MOE_BM)

    def x_copy(k, slot):
        return pltpu.make_async_copy(x_hbm.at[rows(k)], xbuf.at[slot], xsem.at[slot])

    def y_copy(k, slot):
        return pltpu.make_async_copy(ybuf.at[slot], y_hbm.at[rows(k)], ysem.at[slot])

    @pl.when(n > 0)
    def _():
        x_copy(0, 0).start()
        wgu_bf[...] = wgu_ref[0].astype(BF16)
        wd_bf[...] = wd_ref[0].astype(BF16)

        def body(k, carry):
            slot = k % 2

            @pl.when(k + 1 < n)
            def _():
                x_copy(k + 1, 1 - slot).start()

            x_copy(k, slot).wait()

            @pl.when(k >= 2)
            def _():
                y_copy(k - 2, slot).wait()

            hu = jnp.dot(xbuf[slot].astype(BF16), wgu_bf[...], preferred_element_type=F32) + bgu_ref[0]
            g = jnp.minimum(hu[:, :dff], SWIGLU_LIMIT)
            u = jnp.clip(hu[:, dff:], -SWIGLU_LIMIT, SWIGLU_LIMIT)
            a = g * jax.nn.sigmoid(SWIGLU_ALPHA * g) * (u + 1.0)
            ybuf[slot] = jnp.dot(a.astype(BF16), wd_bf[...], preferred_element_type=F32) + bd_ref[0]
            y_copy(k, slot).start()
            return carry
        lax.fori_loop(0, n, body, 0)

        @pl.when(n >= 2)
        def _():
            y_copy(n - 2, n % 2).wait()
        y_copy(n - 1, (n - 1) % 2).wait()

    @pl.when(e == N_EXPERTS - 1)
    def _():
        ybuf[0] = jnp.zeros(ybuf.shape[1:], F32)

        def fill(wait):
            def body(k, carry):
                cp = pltpu.make_async_copy(ybuf.at[0], y_hbm.at[pl.ds(pl.multiple_of(k * MOE_BM, MOE_BM), MOE_BM)],
                                           ysem.at[0])
                if wait:
                    cp.wait()
                else:
                    cp.start()
                return carry
            lax.fori_loop(b0 + n, n_blocks, body, 0)
        fill(False)
        fill(True)


def _expert_ffn3(x_sorted, blk_start, nblk, w_gate_up, b_gate_up, w_down, b_down):
    n_rows, d = x_sorted.shape
    ne, _, dff2 = w_gate_up.shape
    dff = dff2 // 2
    any_spec = pl.BlockSpec(memory_space=pl.ANY)
    grid_spec = pltpu.PrefetchScalarGridSpec(
        num_scalar_prefetch=2,
        grid=(ne,),
        in_specs=[
            any_spec,
            pl.BlockSpec((1, d, dff2), lambda e, *_: (e, 0, 0)),
            pl.BlockSpec((1, 1, dff2), lambda e, *_: (e, 0, 0)),
            pl.BlockSpec((1, dff, d), lambda e, *_: (e, 0, 0)),
            pl.BlockSpec((1, 1, d), lambda e, *_: (e, 0, 0)),
        ],
        out_specs=any_spec,
        scratch_shapes=[
            pltpu.VMEM((2, MOE_BM, d), F32), pltpu.VMEM((2, MOE_BM, d), F32),
            pltpu.SemaphoreType.DMA((2,)), pltpu.SemaphoreType.DMA((2,)),
            pltpu.VMEM((d, dff2), BF16), pltpu.VMEM((dff, d), BF16),
        ],
    )
    return pl.pallas_call(
        functools.partial(_expert_ffn3_kernel, n_blocks=n_rows // MOE_BM),
        grid_spec=grid_spec,
        out_shape=jax.ShapeDtypeStruct((n_rows, d), F32),
        compiler_params=_cparams(("arbitrary",)),
        name="expert_ffn",
    )(blk_start, nblk, x_sorted, w_gate_up, b_gate_up.reshape(ne, 1, dff2), w_down, b_down.reshape(ne, 1, d))


FFN_PAIR = 2


def _expert_ffn4_kernel(b0_ref, nb_ref, x_hbm, wgu_ref, bgu_ref, wd_ref, bd_ref, y_hbm, xbuf, ybuf, xtail, ytail,
                        xsem, ysem, tsem, wgu_bf, wd_bf, *, n_blocks):
    e = pl.program_id(0)
    b0 = b0_ref[e]
    n = nb_ref[e]
    n2 = n // FFN_PAIR
    odd = n % FFN_PAIR
    dff = wd_ref.shape[1]
    big = FFN_PAIR * MOE_BM

    def rows(blk, size):
        return pl.ds(pl.multiple_of((b0 + blk) * MOE_BM, MOE_BM), size)

    def x_copy(kk, slot):
        return pltpu.make_async_copy(x_hbm.at[rows(FFN_PAIR * kk, big)], xbuf.at[slot], xsem.at[slot])

    def y_copy(kk, slot):
        return pltpu.make_async_copy(ybuf.at[slot], y_hbm.at[rows(FFN_PAIR * kk, big)], ysem.at[slot])

    def xt_copy():
        return pltpu.make_async_copy(x_hbm.at[rows(n - 1, MOE_BM)], xtail, tsem.at[0])

    def yt_copy():
        return pltpu.make_async_copy(ytail, y_hbm.at[rows(n - 1, MOE_BM)], tsem.at[1])

    def ffn(x):
        hu = jnp.dot(x.astype(BF16), wgu_bf[...], preferred_element_type=F32) + bgu_ref[0]
        g = jnp.minimum(hu[:, :dff], SWIGLU_LIMIT)
        u = jnp.clip(hu[:, dff:], -SWIGLU_LIMIT, SWIGLU_LIMIT)
        a = g * jax.nn.sigmoid(SWIGLU_ALPHA * g) * (u + 1.0)
        return jnp.dot(a.astype(BF16), wd_bf[...], preferred_element_type=F32) + bd_ref[0]

    @pl.when(n > 0)
    def _():
        @pl.when(odd == 1)
        def _():
            xt_copy().start()

        @pl.when(n2 > 0)
        def _():
            x_copy(0, 0).start()

        wgu_bf[...] = wgu_ref[0].astype(BF16)
        wd_bf[...] = wd_ref[0].astype(BF16)

        def body(kk, carry):
            slot = kk % 2

            @pl.when(kk + 1 < n2)
            def _():
                x_copy(kk + 1, 1 - slot).start()

            x_copy(kk, slot).wait()

            @pl.when(kk >= 2)
            def _():
                y_copy(kk - 2, slot).wait()

            ybuf[slot] = ffn(xbuf[slot])
            y_copy(kk, slot).start()
            return carry
        lax.fori_loop(0, n2, body, 0)

        @pl.when(odd == 1)
        def _():
            xt_copy().wait()
            ytail[...] = ffn(xtail[...])
            yt_copy().start()

        @pl.when(n2 >= 2)
        def _():
            y_copy(n2 - 2, n2 % 2).wait()

        @pl.when(n2 >= 1)
        def _():
            y_copy(n2 - 1, (n2 - 1) % 2).wait()

        @pl.when(odd == 1)
        def _():
            yt_copy().wait()

    @pl.when(e == N_EXPERTS - 1)
    def _():
        ytail[...] = jnp.zeros_like(ytail)

        def fill(wait):
            def body(k, carry):
                cp = pltpu.make_async_copy(ytail, y_hbm.at[pl.ds(pl.multiple_of(k * MOE_BM, MOE_BM), MOE_BM)],
                                           tsem.at[1])
                if wait:
                    cp.wait()
                else:
                    cp.start()
                return carry
            lax.fori_loop(b0 + n, n_blocks, body, 0)
        fill(False)
        fill(True)


def _expert_ffn4(x_sorted, blk_start, nblk, w_gate_up, b_gate_up, w_down, b_down):
    n_rows, d = x_sorted.shape
    ne, _, dff2 = w_gate_up.shape
    dff = dff2 // 2
    any_spec = pl.BlockSpec(memory_space=pl.ANY)
    big = FFN_PAIR * MOE_BM
    grid_spec = pltpu.PrefetchScalarGridSpec(
        num_scalar_prefetch=2,
        grid=(ne,),
        in_specs=[
            any_spec,
            pl.BlockSpec((1, d, dff2), lambda e, *_: (e, 0, 0)),
            pl.BlockSpec((1, 1, dff2), lambda e, *_: (e, 0, 0)),
            pl.BlockSpec((1, dff, d), lambda e, *_: (e, 0, 0)),
            pl.BlockSpec((1, 1, d), lambda e, *_: (e, 0, 0)),
        ],
        out_specs=any_spec,
        scratch_shapes=[
            pltpu.VMEM((2, big, d), F32), pltpu.VMEM((2, big, d), F32),
            pltpu.VMEM((MOE_BM, d), F32), pltpu.VMEM((MOE_BM, d), F32),
            pltpu.SemaphoreType.DMA((2,)), pltpu.SemaphoreType.DMA((2,)), pltpu.SemaphoreType.DMA((2,)),
            pltpu.VMEM((d, dff2), BF16), pltpu.VMEM((dff, d), BF16),
        ],
    )
    return pl.pallas_call(
        functools.partial(_expert_ffn4_kernel, n_blocks=n_rows // MOE_BM),
        grid_spec=grid_spec,
        out_shape=jax.ShapeDtypeStruct((n_rows, d), F32),
        compiler_params=_cparams(("arbitrary",)),
        name="expert_ffn",
    )(blk_start, nblk, x_sorted, w_gate_up, b_gate_up.reshape(ne, 1, dff2), w_down, b_down.reshape(ne, 1, d))


def _combine2_kernel(d_ref, n_ref, o_ref, r_ref, ls_ref, gate_ref, xp_ref, xs_ref, g_ref, y_hbm, yp_ref, ys_ref,
                     buf, sem, *, n_tiles, n_prompt_tiles):
    j = pl.program_id(0)

    @pl.when(j == 0)
    def _():
        buf[...] = jnp.zeros_like(buf)

    @pl.when(j < n_tiles)
    def _():
        _tile_strips(j, j % 2, d_ref, n_ref, o_ref, buf, y_hbm, sem, to_hbm=False)

    @pl.when(j >= 1)
    def _():
        slot = (j - 1) % 2
        _tile_wait(r_ref[j - 1], slot, buf, y_hbm, sem)
        r_hi = buf[slot].astype(BF16)
        lane = lax.broadcasted_iota(jnp.int32, (TILE_TOKENS, TILE_ROWS), 1)
        pick = jnp.zeros((TILE_TOKENS, TILE_ROWS), F32)
        for k in range(TOP_K):
            pick = pick + jnp.where(lane == ls_ref[:, k:k + 1], gate_ref[:, k:k + 1], 0.0)
        p_hi = pick.astype(BF16)
        p_lo = (pick - p_hi.astype(F32)).astype(BF16)
        moe = jnp.dot(p_hi, r_hi, preferred_element_type=F32) + jnp.dot(p_lo, r_hi, preferred_element_type=F32)
        is_prompt = j - 1 < n_prompt_tiles
        y = _rms(jnp.where(is_prompt, xp_ref[...], xs_ref[...]) + moe, g_ref[...])

        @pl.when(is_prompt)
        def _():
            yp_ref[...] = y

        @pl.when(jnp.logical_not(is_prompt))
        def _():
            ys_ref[...] = y


def _combine2(y_sorted, x2p, x2s, ls, gates, tabs, g_final):
    tp, d = x2p.shape
    npt = tp // TILE_TOKENS
    assert x2s.shape[0] == TILE_TOKENS
    nt = npt + 1
    tile = lambda j: jnp.clip(j - 1, 0, nt - 1)
    ptile = lambda j: jnp.clip(j - 1, 0, npt - 1)
    grid_spec = pltpu.PrefetchScalarGridSpec(
        num_scalar_prefetch=4,
        grid=(nt + 1,),
        in_specs=[
            pl.BlockSpec((TILE_TOKENS, TOP_K), lambda j, *_: (tile(j), 0)),
            pl.BlockSpec((TILE_TOKENS, TOP_K), lambda j, *_: (tile(j), 0)),
            pl.BlockSpec((TILE_TOKENS, d), lambda j, *_: (ptile(j), 0)),
            pl.BlockSpec((TILE_TOKENS, d), lambda j, *_: (0, 0)),
            pl.BlockSpec((1, d), lambda j, *_: (0, 0)),
            pl.BlockSpec(memory_space=pl.ANY),
        ],
        out_specs=[
            pl.BlockSpec((TILE_TOKENS, d), lambda j, *_: (ptile(j), 0)),
            pl.BlockSpec((TILE_TOKENS, d), lambda j, *_: (0, 0)),
        ],
        scratch_shapes=[pltpu.VMEM((2, TILE_ROWS, d), F32), pltpu.SemaphoreType.DMA((2,))],
    )
    return pl.pallas_call(
        functools.partial(_combine2_kernel, n_tiles=nt, n_prompt_tiles=npt),
        grid_spec=grid_spec,
        out_shape=[jax.ShapeDtypeStruct((tp, d), F32), jax.ShapeDtypeStruct((TILE_TOKENS, d), F32)],
        compiler_params=_cparams(("arbitrary",)),
        name="moe_combine",
    )(tabs["d"], tabs["n"], tabs["o"], tabs["rows"], ls, gates, x2p, x2s, g_final.reshape(1, d), y_sorted)


def _route_tiles2(logits):
    t = logits.shape[0]
    nt = -(-t // TILE_TOKENS)
    tpad = nt * TILE_TOKENS
    up = lambda a, m: (a + m - 1) // m * m
    top_v, top_i = lax.top_k(logits, TOP_K)
    gates = jax.nn.softmax(top_v, axis=-1)
    onehot = (top_i[:, :, None] == jnp.arange(N_EXPERTS, dtype=top_i.dtype)[None, None, :])
    per_tok = jnp.pad(jnp.sum(onehot, axis=1).astype(jnp.int32), ((0, tpad - t), (0, 0)))
    before = jnp.cumsum(per_tok, axis=0) - per_tok
    n_tile = up(jnp.sum(per_tok.reshape(nt, TILE_TOKENS, N_EXPERTS), axis=1), SUBLANES)
    o_tile = jnp.cumsum(n_tile, axis=1) - n_tile
    rows_e = jnp.sum(n_tile, axis=0)
    nblk = up(rows_e, MOE_BM) // MOE_BM
    blk_end = jnp.cumsum(nblk)
    pstart = (blk_end - nblk) * MOE_BM
    d_tile = pstart[None, :] + jnp.cumsum(n_tile, axis=0) - n_tile
    local = (o_tile - before[::TILE_TOKENS])[:, None, :] + before.reshape(nt, TILE_TOKENS, N_EXPERTS)
    ls = jnp.sum(jnp.where(onehot, local.reshape(tpad, 1, N_EXPERTS)[:t], 0), axis=-1)
    ls = jnp.pad(ls.astype(jnp.int32), ((0, tpad - t), (0, 0)), constant_values=-1)
    gates = jnp.pad(gates, ((0, tpad - t), (0, 0)))
    n_used = blk_end[-1].astype(jnp.int32)
    nb = (t * TOP_K + N_EXPERTS * nt * (SUBLANES - 1) + N_EXPERTS * (MOE_BM - 1) + MOE_BM - 1) // MOE_BM
    blk = jnp.minimum(jnp.arange(nb, dtype=jnp.int32), n_used - 1)
    block_e = jnp.sum(blk_end[None, :] <= blk[:, None], axis=1).astype(jnp.int32)
    i32 = lambda a: a.astype(jnp.int32).reshape(-1)
    tabs = {"d": i32(d_tile), "n": i32(n_tile), "o": i32(o_tile), "rows": i32(jnp.sum(n_tile, axis=1)),
            "pad_start": i32(pstart + rows_e), "pad_len": i32(nblk * MOE_BM - rows_e),
            "blk_start": i32(blk_end - nblk), "nblk": i32(nblk)}
    return gates, ls, tabs, block_e, n_used.reshape(1), nb * MOE_BM


def kernel(x_prompt, x_sample, cache_k_win, cache_v_win, state_pool, cache_mem_k, cache_mem_v, mem_prompt,
           rel_bias, g_mix, w_in, w_pool, pool_scale, w_out, g_mem, g_x, w_xq, w_xk, w_xv, w_xo, g_ff,
           w_router, b_router, w_gate_up, b_gate_up, w_down, b_down, g_final):
    depth = g_mix.shape[0]
    assert depth == 1
    l = 0
    batch, seq, d = x_prompt.shape
    bd, n_new, _ = x_sample.shape
    n_mem = mem_prompt.shape[1]
    bw = d - A_WIDTH
    n_hist = cache_k_win.shape[2]
    n_pool = state_pool.shape[2]
    tp, ts = batch * seq, bd * n_new

    w_in_b = w_in[l].astype(BF16)
    w_pool_b = w_pool[l].astype(BF16)
    w_xkv_b = jnp.concatenate([w_xk[l], w_xv[l]], axis=1).astype(BF16)
    wts = {
        "w_out_a": w_out[l, :A_WIDTH].astype(BF16), "w_out_b": w_out[l, A_WIDTH:].astype(BF16),
        "g_x": g_x[l].reshape(1, d), "w_xq": w_xq[l].astype(BF16), "w_xo": w_xo[l].astype(BF16),
        "g_ff": g_ff[l].reshape(1, d), "w_router": w_router[l], "b_router": b_router[l].reshape(1, N_EXPERTS),
    }
    band_bias = _band_bias(rel_bias)
    step_bias = _step_bias(rel_bias)
    widths = (A_WIDTH, A_WIDTH, A_WIDTH, bw)
    scales = (A_HEAD_DIM ** -0.5, 1.0, 1.0, 1.0)

    xp = x_prompt.reshape(tp, d)
    w_kv_t = jnp.transpose(w_in[l][:, A_WIDTH:3 * A_WIDTH].reshape(d, 2, A_WIDTH), (1, 2, 0)).astype(BF16)
    q, k, v, u, k_t, v_t = _in_proj(xp, g_mix[l], w_in_b, w_kv_t, widths, scales, batch, seq, tm=512)
    o_a = _dil_attn(q, k, v, band_bias, batch, seq)
    o_b = _pool_prompt(u, w_pool_b, pool_scale[l], batch, seq, tm=512)
    mk, mv, mk_b, mv_b = _mem_kv(mem_prompt.reshape(batch * n_mem, d), g_mem[l], w_xkv_b, X_HEADS, tm=512)
    x2p, hp, lgp = _mix_xattn(xp, o_a, o_b, mk_b, mv_b, wts, groups=batch, rows_per_group=seq, tm=512,
                              rows_per_sub=None, keys_per_sub=None)

    xs = x_sample.reshape(ts, d)
    qs, ks, vs, us = _norm_proj(xs, g_mix[l], w_in_b, widths, scales, tm=ts)
    r4 = lambda a: a.reshape(bd, n_new, A_HEADS, A_HEAD_DIM)
    ck, cv = cache_k_win[l], cache_v_win[l]
    new3 = lambda a: a.reshape(bd, n_new, A_WIDTH)
    to_t = lambda c: jnp.transpose(c, (0, 2, 3, 1)).reshape(bd, A_WIDTH, n_hist)
    from_t = lambda c: jnp.transpose(c.reshape(bd, A_HEADS, A_HEAD_DIM, n_hist), (0, 3, 1, 2))[None]
    bias_c, bias_n = _decode_bias(rel_bias, n_hist, n_new)
    o_as, ck_t, cv_t = _decode_attn(new3(qs), new3(ks), new3(vs), to_t(ck), to_t(cv), bias_c, bias_n,
                                    heads_per_step=4)
    o_as = o_as.reshape(ts, A_WIDTH)
    us3 = us.reshape(bd, n_new, bw)
    hist = jnp.concatenate([jnp.zeros((bd, POOL_HIST - n_pool, bw), F32), state_pool[l]], axis=1)
    cur = jnp.concatenate([us3, jnp.zeros((bd, SUBLANES - n_new, bw), F32)], axis=1)
    o_bs = _pool_dec(hist, cur, n_pool, w_pool_b, pool_scale[l])[:, :n_new].reshape(ts, bw)
    sub = 8
    xhd = d // X_HEADS
    x2s, hs, lgs = _mix_xattn(xs, o_as, o_bs, cache_mem_k[l].reshape(bd * n_mem, X_HEADS, xhd),
                              cache_mem_v[l].reshape(bd * n_mem, X_HEADS, xhd), wts, groups=bd // sub,
                              rows_per_group=sub * n_new, tm=sub * n_new, rows_per_sub=n_new, keys_per_sub=n_mem)

    assert ts <= TILE_TOKENS
    pad_s = lambda a: jnp.pad(a, ((0, TILE_TOKENS - ts), (0, 0)))
    gates, ls, tabs, block_e, n_used, n_rows = _route_tiles2(jnp.concatenate([lgp, lgs], axis=0))
    x_sorted = _dispatch2(hp, pad_s(hs), ls.T, tabs, n_rows)
    assert w_gate_up.shape[1] == N_EXPERTS
    y_sorted = _expert_ffn4(x_sorted, tabs["blk_start"], tabs["nblk"], w_gate_up[l], b_gate_up[l], w_down[l],
                            b_down[l])
    y_prompt, y_sample = _combine2(y_sorted, x2p, pad_s(x2s), ls, gates, tabs, g_final)
    y_prompt = y_prompt.reshape(batch, seq, d)
    y_sample = y_sample[:ts].reshape(bd, n_new, d)

    a5 = lambda a, b_: a.reshape(1, b_, -1, A_HEADS, A_HEAD_DIM)
    from_tp = lambda c: jnp.transpose(c.reshape(batch, A_HEADS, A_HEAD_DIM, seq), (0, 3, 1, 2))[None]
    k_win_prompt, v_win_prompt = from_tp(k_t), from_tp(v_t)
    pool_prompt = u.reshape(batch, seq, bw)[:, seq - n_pool:][None]
    mem_k_prompt = mk.reshape(1, batch, n_mem, X_HEADS, xhd)
    mem_v_prompt = mv.reshape(1, batch, n_mem, X_HEADS, xhd)
    k_win_sample, v_win_sample = from_t(ck_t), from_t(cv_t)
    pool_sample = jnp.concatenate([state_pool[l][:, n_new:], us3], axis=1)[None]
    return (y_prompt, y_sample, k_win_prompt, v_win_prompt, pool_prompt, mem_k_prompt, mem_v_prompt,
            k_win_sample, v_win_sample, pool_sample)
```

```python
import functools
import math

import numpy as np
import jax
import jax.numpy as jnp
from jax import lax
from jax.experimental import pallas as pl
from jax.experimental.pallas import tpu as pltpu

F32 = jnp.float32
BF16 = jnp.bfloat16

LANES = 128
SUBLANES = 8
VMEM_LIMIT_BYTES = 56 * 1024 * 1024

A_HEADS = 8
A_HEAD_DIM = 64
A_WIDTH = A_HEADS * A_HEAD_DIM
DILATED = ((128, 1), (512, 4), (2048, 16))
QB = 128
POOL_WINDOWS = (2, 4, 8, 16)
POOL_HIST = 16
X_HEADS = 4
N_EXPERTS = 32
TOP_K = 4
SWIGLU_LIMIT = 7.0
SWIGLU_ALPHA = 1.702
N_BUCKETS = 32
RMS_EPS = 1e-6
NEG = -1e30

MOE_BM = 256


def _cparams(sem):
    return pltpu.CompilerParams(dimension_semantics=sem, vmem_limit_bytes=VMEM_LIMIT_BYTES)


def _rms(x, g):
    return x * lax.rsqrt(jnp.mean(x * x, axis=-1, keepdims=True) + RMS_EPS) * g


def _norm_proj_kernel(x_ref, g_ref, w_ref, *o_refs, widths, scales):
    hb = _rms(x_ref[...], g_ref[...]).astype(BF16)
    off = 0
    for o_ref, width, scale in zip(o_refs, widths, scales):
        p = jnp.dot(hb, w_ref[:, off:off + width], preferred_element_type=F32)
        o_ref[...] = p if scale == 1.0 else p * scale
        off += width


def _norm_proj(x, g, w_bf16, widths, scales, tm):
    t, d = x.shape
    n = w_bf16.shape[1]
    assert sum(widths) == n and t % tm == 0
    return pl.pallas_call(
        functools.partial(_norm_proj_kernel, widths=widths, scales=scales),
        grid=(t // tm,),
        in_specs=[
            pl.BlockSpec((tm, d), lambda i: (i, 0)),
            pl.BlockSpec((1, d), lambda i: (0, 0)),
            pl.BlockSpec((d, n), lambda i: (0, 0)),
        ],
        out_specs=[pl.BlockSpec((tm, wd), lambda i: (i, 0)) for wd in widths],
        out_shape=[jax.ShapeDtypeStruct((t, wd), F32) for wd in widths],
        compiler_params=_cparams(("arbitrary",)),
        name="norm_proj",
    )(x, g.reshape(1, d), w_bf16)


def _mem_kv_kernel(x_ref, g_ref, w_ref, k_ref, v_ref, kb_ref, vb_ref, *, heads):
    hb = _rms(x_ref[...], g_ref[...]).astype(BF16)
    d = x_ref.shape[1]
    hd = d // heads
    for j, (o_ref, ob_ref) in enumerate(((k_ref, kb_ref), (v_ref, vb_ref))):
        p = jnp.dot(hb, w_ref[:, j * d:(j + 1) * d], preferred_element_type=F32)
        ob_ref[...] = p.astype(BF16)
        for h in range(heads):
            o_ref[:, h, :] = p[:, h * hd:(h + 1) * hd]


def _mem_kv(x, g, w_bf16, heads, tm):
    t, d = x.shape
    assert w_bf16.shape == (d, 2 * d) and t % tm == 0
    hd = d // heads
    o3 = pl.BlockSpec((tm, heads, hd), lambda i: (i, 0, 0))
    o2 = pl.BlockSpec((tm, d), lambda i: (i, 0))
    return pl.pallas_call(
        functools.partial(_mem_kv_kernel, heads=heads),
        grid=(t // tm,),
        in_specs=[
            pl.BlockSpec((tm, d), lambda i: (i, 0)),
            pl.BlockSpec((1, d), lambda i: (0, 0)),
            pl.BlockSpec((d, 2 * d), lambda i: (0, 0)),
        ],
        out_specs=[o3, o3, o2, o2],
        out_shape=[jax.ShapeDtypeStruct((t, heads, hd), F32)] * 2 + [jax.ShapeDtypeStruct((t, d), BF16)] * 2,
        compiler_params=_cparams(("arbitrary",)),
        name="mem_kv",
    )(x, g.reshape(1, d), w_bf16)


def _in_proj_kernel(x_ref, g_ref, w_ref, wt_ref, *o_refs, widths, scales, n_t):
    hb = _rms(x_ref[...], g_ref[...]).astype(BF16)
    off = 0
    for o_ref, width, scale in zip(o_refs, widths, scales):
        p = jnp.dot(hb, w_ref[:, off:off + width], preferred_element_type=F32)
        o_ref[...] = p if scale == 1.0 else p * scale
        off += width
    for j in range(n_t):
        o_refs[len(widths) + j][0] = lax.dot_general(wt_ref[j], hb, (((1,), (1,)), ((), ())),
                                                    preferred_element_type=F32)


def _in_proj(x, g, w_bf16, wt_bf16, widths, scales, batch, seq, tm):
    t, d = x.shape
    n = w_bf16.shape[1]
    n_t, wt_width, _ = wt_bf16.shape
    nt = seq // tm
    assert sum(widths) == n and seq % tm == 0 and t == batch * seq
    return pl.pallas_call(
        functools.partial(_in_proj_kernel, widths=widths, scales=scales, n_t=n_t),
        grid=(t // tm,),
        in_specs=[
            pl.BlockSpec((tm, d), lambda i: (i, 0)),
            pl.BlockSpec((1, d), lambda i: (0, 0)),
            pl.BlockSpec((d, n), lambda i: (0, 0)),
            pl.BlockSpec(wt_bf16.shape, lambda i: (0, 0, 0)),
        ],
        out_specs=[pl.BlockSpec((tm, wd), lambda i: (i, 0)) for wd in widths]
        + [pl.BlockSpec((1, wt_width, tm), lambda i: (i // nt, 0, i % nt))] * n_t,
        out_shape=[jax.ShapeDtypeStruct((t, wd), F32) for wd in widths]
        + [jax.ShapeDtypeStruct((batch, wt_width, seq), F32)] * n_t,
        compiler_params=_cparams(("arbitrary",)),
        name="in_proj",
    )(x, g.reshape(1, d), w_bf16, wt_bf16)


def _t5_bucket_np(n, max_dist):
    max_exact = N_BUCKETS // 2
    nf = np.maximum(n, 1).astype(np.float32)
    large = max_exact + (
        np.log(nf / np.float32(max_exact)) / np.float32(math.log(max_dist / max_exact))
        * np.float32(N_BUCKETS - max_exact)
    ).astype(np.int32)
    return np.where(n < max_exact, n, np.minimum(large, N_BUCKETS - 1))


def _band_bias(rel_bias):
    max_dist = max(w for w, _ in DILATED)
    qi = np.arange(QB)[:, None]
    ki = np.arange(2 * QB)[None, :]
    j = qi + QB - ki
    tabs = []
    for window, dil in DILATED:
        steps = window // dil
        in_band = (j >= 0) & (j <= steps)
        bucket = _t5_bucket_np(np.clip(j, 0, steps) * dil, max_dist)
        onehot = (bucket[..., None] == np.arange(N_BUCKETS)).astype(np.float32)
        b = jnp.einsum("qkb,bh->hqk", onehot, rel_bias.astype(F32), precision=lax.Precision.HIGHEST)
        tabs.append(jnp.where(in_band[None], b, NEG))
    return jnp.stack(tabs)


def _decode_bias(rel_bias, n_hist, n_new):
    max_dist = max(w for w, _ in DILATED)
    t = np.arange(n_new)[:, None]
    tabs_c, tabs_n = [], []
    for window, dil in DILATED:
        out = []
        for dist in (n_hist + t - np.arange(n_hist)[None, :], t - np.arange(n_new)[None, :]):
            ok = (dist >= 0) & (dist % dil == 0) & (dist <= window)
            onehot = (_t5_bucket_np(np.clip(dist, 0, window), max_dist)[..., None] == np.arange(N_BUCKETS))
            b = jnp.einsum("tpb,bh->htp", onehot.astype(np.float32), rel_bias.astype(F32),
                           precision=lax.Precision.HIGHEST)
            out.append(jnp.where(ok[None], b, NEG))
        tabs_c.append(out[0])
        tabs_n.append(out[1])
    return jnp.stack(tabs_c), jnp.stack(tabs_n)


def _decode_attn_kernel(q_ref, kn_ref, vn_ref, knt_ref, vnt_ref, ck_ref, cv_ref, bc_ref, bn_ref,
                        o_ref, ok_ref, ov_ref, *, n_new, heads):
    n_hist = ck_ref.shape[2]
    hd = A_HEAD_DIM
    outs = []
    for h in range(heads):
        rows = slice(h * hd, (h + 1) * hd)
        q = q_ref[0, :, rows].astype(BF16)
        kt = ck_ref[0, rows, :].astype(BF16)
        vt = cv_ref[0, rows, :].astype(BF16)
        lc = jnp.dot(q, kt, preferred_element_type=F32)
        ln = lax.dot_general(q, kn_ref[0, :, rows].astype(BF16), (((1,), (1,)), ((), ())),
                             preferred_element_type=F32)
        vn = vn_ref[0, :, rows].astype(BF16)
        o_br, lse_br = [], []
        for br in range(len(DILATED)):
            bl = lc + bc_ref[br, h]
            bln = ln + bn_ref[br, h]
            m = jnp.maximum(jnp.max(bl, axis=-1, keepdims=True), jnp.max(bln, axis=-1, keepdims=True))
            p = jnp.exp(bl - m)
            pn = jnp.exp(bln - m)
            s = jnp.sum(p, axis=-1, keepdims=True) + jnp.sum(pn, axis=-1, keepdims=True)
            o = lax.dot_general(p.astype(BF16), vt, (((1,), (1,)), ((), ())), preferred_element_type=F32)
            o = o + jnp.dot(pn.astype(BF16), vn, preferred_element_type=F32)
            o_br.append(o / s)
            lse_br.append(m + jnp.log(s))
        m = jnp.maximum(jnp.maximum(lse_br[0], lse_br[1]), lse_br[2])
        es = [jnp.exp(l - m) for l in lse_br]
        outs.append((es[0] * o_br[0] + es[1] * o_br[1] + es[2] * o_br[2]) / (es[0] + es[1] + es[2]))
    o_ref[0] = jnp.concatenate(outs, axis=-1)

    lane = lax.broadcasted_iota(jnp.int32, (ck_ref.shape[1], LANES), 1)
    for c_ref, nt_ref, dst in ((ck_ref, knt_ref, ok_ref), (cv_ref, vnt_ref, ov_ref)):
        shifted = pltpu.roll(c_ref[0], n_hist - n_new, axis=1)
        tail = shifted[:, n_hist - LANES:]
        for j in range(n_new):
            tail = jnp.where(lane == LANES - n_new + j, nt_ref[0, :, j:j + 1], tail)
        dst[0, :, :n_hist - LANES] = shifted[:, :n_hist - LANES]
        dst[0, :, n_hist - LANES:] = tail


def _decode_attn(q, k_new, v_new, ck_t, cv_t, bias_c, bias_n, heads_per_step):
    bd, n_new, aw = q.shape
    n_hist = ck_t.shape[2]
    gw = heads_per_step * A_HEAD_DIM
    ng = aw // gw
    row = pl.BlockSpec((1, n_new, gw), lambda b, g: (b, 0, g))
    col = pl.BlockSpec((1, gw, n_new), lambda b, g: (b, g, 0))
    cache = pl.BlockSpec((1, gw, n_hist), lambda b, g: (b, g, 0))
    nbr = len(DILATED)
    return pl.pallas_call(
        functools.partial(_decode_attn_kernel, n_new=n_new, heads=heads_per_step),
        grid=(bd, ng),
        in_specs=[row, row, row, col, col, cache, cache,
                  pl.BlockSpec((nbr, heads_per_step, n_new, n_hist), lambda b, g: (0, g, 0, 0)),
                  pl.BlockSpec((nbr, heads_per_step, n_new, n_new), lambda b, g: (0, g, 0, 0))],
        out_specs=[row, cache, cache],
        out_shape=[jax.ShapeDtypeStruct(q.shape, F32), jax.ShapeDtypeStruct(ck_t.shape, F32),
                   jax.ShapeDtypeStruct(cv_t.shape, F32)],
        compiler_params=_cparams(("arbitrary", "arbitrary")),
        name="decode_attn",
    )(q, k_new, v_new, jnp.swapaxes(k_new, 1, 2), jnp.swapaxes(v_new, 1, 2), ck_t, cv_t, bias_c, bias_n)


def _dil_attn_kernel(q_ref, k_ref, v_ref, bias_ref, o_ref, obr_ref, lbr_ref, *, seq):
    lane = lax.broadcasted_iota(jnp.int32, (QB, LANES), 1)
    head0 = lane < A_HEAD_DIM

    def rows(ref, start, n, dil):
        if dil == 1:
            return ref[pl.ds(start, n), :]
        return ref[pl.ds(start, n, stride=dil), :]

    def block(br, dil, qstart, kstart, nk):
        qs = rows(q_ref, qstart, QB, dil)
        ks = rows(k_ref, kstart, nk, dil).astype(BF16)
        vs = rows(v_ref, kstart, nk, dil).astype(BF16)
        outs, lses = [], []
        for hh in range(2):
            keep = head0 if hh == 0 else jnp.logical_not(head0)
            qm = jnp.where(keep, qs, 0.0).astype(BF16)
            logits = lax.dot_general(qm, ks, (((1,), (1,)), ((), ())), preferred_element_type=F32)
            logits = logits + bias_ref[br, hh, :, 2 * QB - nk:]
            m = jnp.max(logits, axis=-1, keepdims=True)
            p = jnp.exp(logits - m)
            s = jnp.sum(p, axis=-1, keepdims=True)
            o = jnp.dot(p.astype(BF16), vs, preferred_element_type=F32)
            outs.append(o / s)
            lses.append(jnp.broadcast_to(m + jnp.log(s), (QB, LANES)))
        o = jnp.where(head0, outs[0], outs[1])
        lse = jnp.where(head0, lses[0], lses[1])
        if dil == 1:
            obr_ref[br, pl.ds(qstart, QB), :] = o
            lbr_ref[br, pl.ds(qstart, QB), :] = lse
        else:
            obr_ref[br, pl.ds(qstart, QB, stride=dil), :] = o
            lbr_ref[br, pl.ds(qstart, QB, stride=dil), :] = lse

    for br, (window, dil) in enumerate(DILATED):
        assert window // dil == QB
        nblk = seq // (dil * QB)
        for r in range(dil):
            block(br, dil, r, r, QB)
            for i in range(1, nblk):
                qstart = r + dil * QB * i
                block(br, dil, qstart, qstart - dil * QB, 2 * QB)

    l0, l1, l2 = lbr_ref[0], lbr_ref[1], lbr_ref[2]
    m = jnp.maximum(jnp.maximum(l0, l1), l2)
    e0, e1, e2 = jnp.exp(l0 - m), jnp.exp(l1 - m), jnp.exp(l2 - m)
    acc = e0 * obr_ref[0] + e1 * obr_ref[1] + e2 * obr_ref[2]
    o_ref[...] = acc / (e0 + e1 + e2)


def _dil_attn(q, k, v, band_bias, batch, seq):
    t, aw = q.shape
    npair = aw // LANES
    bias = band_bias.reshape(len(DILATED), npair, 2, QB, 2 * QB)
    spec = pl.BlockSpec((seq, LANES), lambda b, hp: (b, hp))
    return pl.pallas_call(
        functools.partial(_dil_attn_kernel, seq=seq),
        grid=(batch, npair),
        in_specs=[
            spec, spec, spec,
            pl.BlockSpec((len(DILATED), None, 2, QB, 2 * QB), lambda b, hp: (0, hp, 0, 0, 0)),
        ],
        out_specs=spec,
        out_shape=jax.ShapeDtypeStruct((t, aw), F32),
        scratch_shapes=[
            pltpu.VMEM((len(DILATED), seq, LANES), F32),
            pltpu.VMEM((len(DILATED), seq, LANES), F32),
        ],
        compiler_params=_cparams(("arbitrary", "arbitrary")),
        name="dil_attn",
    )(q, k, v, bias)


def _pool_kernel(hist_ref, cur_ref, w_ref, scale_ref, o_ref, *, pos0_of_tile, tm):
    nb = cur_ref.shape[0]
    gdim = w_ref.shape[1]
    i = pl.program_id(1)
    pos0 = pos0_of_tile(i)
    have_hist = jnp.where(pos0 > 0, 1.0, 0.0).astype(F32)
    t = lax.broadcasted_iota(jnp.int32, (1, tm, 1), 1)
    for g, w in enumerate(POOL_WINDOWS):
        sl = slice(g * gdim, (g + 1) * gdim)
        cur = cur_ref[:, :, sl]
        ext = jnp.concatenate([hist_ref[:, :, sl] * have_hist, cur], axis=1)
        acc, span = ext, 1
        while span < w:
            n = acc.shape[1]
            acc = acc[:, span:n] + acc[:, 0:n - span]
            span *= 2
        wsum = acc[:, POOL_HIST + 1 - w:POOL_HIST + 1 - w + tm]
        cnt = jnp.minimum(pos0 + t + 1, w).astype(F32)
        d = (wsum / cnt - cur).astype(BF16).reshape(nb * tm, gdim)
        y = jnp.dot(d, w_ref[g], preferred_element_type=F32) * scale_ref[:, sl]
        o_ref[:, :, sl] = y.reshape(nb, tm, gdim)


def _pool_prompt(u, w_pool_bf16, pool_scale, batch, seq, tm):
    t, bw = u.shape
    nt = seq // tm
    hb = tm // POOL_HIST
    u3 = u.reshape(1, t, bw)
    out = pl.pallas_call(
        functools.partial(_pool_kernel, pos0_of_tile=lambda i: i * tm, tm=tm),
        grid=(batch, nt),
        in_specs=[
            pl.BlockSpec((1, POOL_HIST, bw), lambda b, i: (0, jnp.maximum((b * nt + i) * hb - 1, 0), 0)),
            pl.BlockSpec((1, tm, bw), lambda b, i: (0, b * nt + i, 0)),
            pl.BlockSpec(w_pool_bf16.shape, lambda b, i: (0, 0, 0)),
            pl.BlockSpec((1, bw), lambda b, i: (0, 0)),
        ],
        out_specs=pl.BlockSpec((1, tm, bw), lambda b, i: (0, b * nt + i, 0)),
        out_shape=jax.ShapeDtypeStruct((1, t, bw), F32),
        compiler_params=_cparams(("arbitrary", "arbitrary")),
        name="pool_prompt",
    )(u3, u3, w_pool_bf16, pool_scale.reshape(1, bw))
    return out.reshape(t, bw)


def _pool_dec(hist, cur, n_prev, w_pool_bf16, pool_scale):
    bd, tm, bw = cur.shape
    return pl.pallas_call(
        functools.partial(_pool_kernel, pos0_of_tile=lambda i: n_prev, tm=tm),
        grid=(1, 1),
        in_specs=[
            pl.BlockSpec(hist.shape, lambda b, i: (0, 0, 0)),
            pl.BlockSpec(cur.shape, lambda b, i: (0, 0, 0)),
            pl.BlockSpec(w_pool_bf16.shape, lambda b, i: (0, 0, 0)),
            pl.BlockSpec((1, bw), lambda b, i: (0, 0)),
        ],
        out_specs=pl.BlockSpec(cur.shape, lambda b, i: (0, 0, 0)),
        out_shape=jax.ShapeDtypeStruct(cur.shape, F32),
        compiler_params=_cparams(("arbitrary", "arbitrary")),
        name="pool_dec",
    )(hist, cur, w_pool_bf16, pool_scale.reshape(1, bw))


def _mix_xattn_kernel(x_ref, oa_ref, ob_ref, mk_ref, mv_ref, woa_ref, wob_ref, gx_ref, wq_ref, wo_ref,
                      gff_ref, wr_ref, br_ref, x2_ref, h_ref, lg_ref, *, rows_per_sub, keys_per_sub):
    tm = x_ref.shape[0]
    nkv = mk_ref.shape[0]
    xh = wq_ref.shape[1] // X_HEADS
    x = x_ref[...]
    x = x + jnp.dot(oa_ref[...].astype(BF16), woa_ref[...], preferred_element_type=F32)
    x = x + jnp.dot(ob_ref[...].astype(BF16), wob_ref[...], preferred_element_type=F32)
    hq = _rms(x, gx_ref[...]).astype(BF16)
    if rows_per_sub is not None:
        rsub = (pl.program_id(1) * tm + lax.broadcasted_iota(jnp.int32, (tm, nkv), 0)) // rows_per_sub
        ksub = lax.broadcasted_iota(jnp.int32, (tm, nkv), 1) // keys_per_sub
        same = rsub == ksub
    heads = []
    for h in range(X_HEADS):
        sl = slice(h * xh, (h + 1) * xh)
        q = jnp.dot(hq, wq_ref[:, sl], preferred_element_type=F32) * (xh ** -0.5)
        if len(mk_ref.shape) == 3:
            kh, vh = mk_ref[:, h, :].astype(BF16), mv_ref[:, h, :].astype(BF16)
        else:
            kh, vh = mk_ref[:, sl], mv_ref[:, sl]
        logits = lax.dot_general(q.astype(BF16), kh, (((1,), (1,)), ((), ())), preferred_element_type=F32)
        if rows_per_sub is not None:
            logits = jnp.where(same, logits, NEG)
        m = jnp.max(logits, axis=-1, keepdims=True)
        p = jnp.exp(logits - m)
        s = jnp.sum(p, axis=-1, keepdims=True)
        o = jnp.dot(p.astype(BF16), vh, preferred_element_type=F32) / s
        heads.append(o.astype(BF16))
    x = x + jnp.dot(jnp.concatenate(heads, axis=-1), wo_ref[...], preferred_element_type=F32)
    x2_ref[...] = x
    hf = _rms(x, gff_ref[...])
    h_ref[...] = hf
    lg_ref[...] = jnp.dot(hf.astype(BF16), wr_ref[...].astype(BF16), preferred_element_type=F32) + br_ref[...]


def _mix_xattn(x, oa, ob, mk, mv, w, groups, rows_per_group, tm, rows_per_sub, keys_per_sub):
    t, d = x.shape
    nt = rows_per_group // tm
    nkv = mk.shape[0] // groups
    ne = w["w_router"].shape[1]
    row = lambda width: pl.BlockSpec((tm, width), lambda g, i: (g * nt + i, 0))
    const = lambda a: pl.BlockSpec(a.shape, lambda g, i: (0,) * a.ndim)
    kv = pl.BlockSpec((nkv,) + mk.shape[1:], lambda g, i: (g,) + (0,) * (mk.ndim - 1))
    consts = [w["w_out_a"], w["w_out_b"], w["g_x"], w["w_xq"], w["w_xo"], w["g_ff"], w["w_router"], w["b_router"]]
    return pl.pallas_call(
        functools.partial(_mix_xattn_kernel, rows_per_sub=rows_per_sub, keys_per_sub=keys_per_sub),
        grid=(groups, nt),
        in_specs=[row(d), row(oa.shape[1]), row(ob.shape[1]), kv, kv] + [const(a) for a in consts],
        out_specs=[row(d), row(d), row(ne)],
        out_shape=[jax.ShapeDtypeStruct((t, d), F32), jax.ShapeDtypeStruct((t, d), F32),
                   jax.ShapeDtypeStruct((t, ne), F32)],
        compiler_params=_cparams(("arbitrary", "arbitrary")),
        name="mix_xattn",
    )(x, oa, ob, mk, mv, *consts)


TILE_TOKENS = 256
TILE_ROWS = TOP_K * TILE_TOKENS + N_EXPERTS * (SUBLANES - 1)
assert TILE_ROWS % SUBLANES == 0
FFN_PAIR = 2


def _bits(lo, hi):
    return tuple(lo << j for j in range((hi // lo).bit_length()))


STRIP_LEN_BITS = _bits(SUBLANES, TILE_TOKENS)
PAD_LEN_BITS = _bits(SUBLANES, MOE_BM - SUBLANES)
TILE_LEN_BITS = _bits(SUBLANES, TILE_ROWS)


def _strip_dmas(n, bits, make_copy, wait=False):
    for bit in bits:
        @pl.when((n & bit) != 0)
        def _(bit=bit):
            c = make_copy(n & (bit - 1), bit)
            if wait:
                c.wait()
            else:
                c.start()


def _tile_strips(i, slot, d_ref, n_ref, o_ref, local, hbm, sem, to_hbm):
    for e in range(N_EXPERTS):
        l0 = pl.multiple_of(o_ref[i * N_EXPERTS + e], SUBLANES)
        g0 = pl.multiple_of(d_ref[i * N_EXPERTS + e], SUBLANES)

        def make_copy(off, size, l0=l0, g0=g0):
            lo = local.at[slot, pl.ds(pl.multiple_of(l0 + off, SUBLANES), size)]
            gl = hbm.at[pl.ds(pl.multiple_of(g0 + off, SUBLANES), size)]
            src, dst = (lo, gl) if to_hbm else (gl, lo)
            return pltpu.make_async_copy(src, dst, sem.at[slot])
        _strip_dmas(n_ref[i * N_EXPERTS + e], STRIP_LEN_BITS, make_copy)


def _tile_wait(rows, slot, local, hbm, sem):
    _strip_dmas(rows, TILE_LEN_BITS, lambda off, size: pltpu.make_async_copy(
        hbm.at[pl.ds(0, size)], local.at[slot, pl.ds(0, size)], sem.at[slot]), wait=True)


def _dispatch_kernel(d_ref, n_ref, o_ref, r_ref, ps_ref, pn_ref, lst_ref, hp_ref, hs_ref, xs_hbm, buf, zbuf, sem,
                     psem, *, n_tiles, n_prompt_tiles):
    i = pl.program_id(0)
    slot = i % 2

    def pad_strips(wait):
        for e in range(N_EXPERTS):
            start = ps_ref[e]
            _strip_dmas(pn_ref[e], PAD_LEN_BITS, lambda off, size: pltpu.make_async_copy(
                zbuf.at[pl.ds(0, size)], xs_hbm.at[pl.ds(pl.multiple_of(start + off, SUBLANES), size)],
                psem.at[0]), wait)
        chunk = zbuf.shape[0]
        used_rows = ps_ref[N_EXPERTS - 1] + pn_ref[N_EXPERTS - 1]

        def body(c, carry):
            cp = pltpu.make_async_copy(zbuf, xs_hbm.at[pl.ds(pl.multiple_of(c * chunk, chunk), chunk)], psem.at[0])
            if wait:
                cp.wait()
            else:
                cp.start()
            return carry
        lax.fori_loop(used_rows // chunk, xs_hbm.shape[0] // chunk, body, 0)

    @pl.when(i >= 2)
    def _():
        _tile_wait(r_ref[i - 2], slot, buf, xs_hbm, sem)

    @pl.when(i == 0)
    def _():
        zbuf[...] = jnp.zeros_like(zbuf)
        pad_strips(False)

    srow = lax.broadcasted_iota(jnp.int32, (TILE_ROWS, TILE_TOKENS), 0)
    place = srow == lst_ref[0:1, :]
    for k in range(1, TOP_K):
        place = jnp.logical_or(place, srow == lst_ref[k:k + 1, :])
    x = jnp.where(i < n_prompt_tiles, hp_ref[...], hs_ref[...]).astype(BF16)
    buf[slot] = jnp.dot(jnp.where(place, 1.0, 0.0).astype(BF16), x, preferred_element_type=F32)
    _tile_strips(i, slot, d_ref, n_ref, o_ref, buf, xs_hbm, sem, to_hbm=True)

    @pl.when(i == n_tiles - 1)
    def _():
        _tile_wait(r_ref[i], slot, buf, xs_hbm, sem)
        if n_tiles > 1:
            _tile_wait(r_ref[jnp.maximum(i - 1, 0)], 1 - slot, buf, xs_hbm, sem)
        pad_strips(True)


def _dispatch(hp, hs, ls_t, tabs, n_rows):
    tp, d = hp.shape
    npt = tp // TILE_TOKENS
    assert tp % TILE_TOKENS == 0 and hs.shape[0] == TILE_TOKENS
    nt = npt + 1
    grid_spec = pltpu.PrefetchScalarGridSpec(
        num_scalar_prefetch=6,
        grid=(nt,),
        in_specs=[
            pl.BlockSpec((TOP_K, TILE_TOKENS), lambda i, *_: (0, i)),
            pl.BlockSpec((TILE_TOKENS, d), lambda i, *_: (jnp.minimum(i, npt - 1), 0)),
            pl.BlockSpec((TILE_TOKENS, d), lambda i, *_: (0, 0)),
        ],
        out_specs=pl.BlockSpec(memory_space=pl.ANY),
        scratch_shapes=[
            pltpu.VMEM((2, TILE_ROWS, d), F32),
            pltpu.VMEM((PAD_LEN_BITS[-1], d), F32),
            pltpu.SemaphoreType.DMA((2,)),
            pltpu.SemaphoreType.DMA((1,)),
        ],
    )
    return pl.pallas_call(
        functools.partial(_dispatch_kernel, n_tiles=nt, n_prompt_tiles=npt),
        grid_spec=grid_spec,
        out_shape=jax.ShapeDtypeStruct((n_rows, d), F32),
        compiler_params=_cparams(("arbitrary",)),
        name="moe_dispatch",
    )(tabs["d"], tabs["n"], tabs["o"], tabs["rows"], tabs["pad_start"], tabs["pad_len"], ls_t, hp, hs)


def _expert_ffn_kernel(b0_ref, nb_ref, x_hbm, wgu_ref, bgu_ref, wd_ref, bd_ref, y_hbm, xbuf, ybuf, xtail, ytail,
                       xsem, ysem, tsem, wgu_bf, wd_bf, *, n_blocks):
    e = pl.program_id(0)
    b0 = b0_ref[e]
    n = nb_ref[e]
    n2 = n // FFN_PAIR
    odd = n % FFN_PAIR
    dff = wd_ref.shape[1]
    big = FFN_PAIR * MOE_BM

    def rows(blk, size):
        return pl.ds(pl.multiple_of((b0 + blk) * MOE_BM, MOE_BM), size)

    def x_copy(kk, slot):
        return pltpu.make_async_copy(x_hbm.at[rows(FFN_PAIR * kk, big)], xbuf.at[slot], xsem.at[slot])

    def y_copy(kk, slot):
        return pltpu.make_async_copy(ybuf.at[slot], y_hbm.at[rows(FFN_PAIR * kk, big)], ysem.at[slot])

    def xt_copy():
        return pltpu.make_async_copy(x_hbm.at[rows(n - 1, MOE_BM)], xtail, tsem.at[0])

    def yt_copy():
        return pltpu.make_async_copy(ytail, y_hbm.at[rows(n - 1, MOE_BM)], tsem.at[1])

    def ffn(x):
        hu = jnp.dot(x.astype(BF16), wgu_bf[...], preferred_element_type=F32) + bgu_ref[0]
        g = jnp.minimum(hu[:, :dff], SWIGLU_LIMIT)
        u = jnp.clip(hu[:, dff:], -SWIGLU_LIMIT, SWIGLU_LIMIT)
        a = g * jax.nn.sigmoid(SWIGLU_ALPHA * g) * (u + 1.0)
        return jnp.dot(a.astype(BF16), wd_bf[...], preferred_element_type=F32) + bd_ref[0]

    @pl.when(n > 0)
    def _():
        @pl.when(odd == 1)
        def _():
            xt_copy().start()

        @pl.when(n2 > 0)
        def _():
            x_copy(0, 0).start()

        wgu_bf[...] = wgu_ref[0].astype(BF16)
        wd_bf[...] = wd_ref[0].astype(BF16)

        def body(kk, carry):
            slot = kk % 2

            @pl.when(kk + 1 < n2)
            def _():
                x_copy(kk + 1, 1 - slot).start()

            x_copy(kk, slot).wait()

            @pl.when(kk >= 2)
            def _():
                y_copy(kk - 2, slot).wait()

            ybuf[slot] = ffn(xbuf[slot])
            y_copy(kk, slot).start()
            return carry
        lax.fori_loop(0, n2, body, 0)

        @pl.when(odd == 1)
        def _():
            xt_copy().wait()
            ytail[...] = ffn(xtail[...])
            yt_copy().start()

        @pl.when(n2 >= 2)
        def _():
            y_copy(n2 - 2, n2 % 2).wait()

        @pl.when(n2 >= 1)
        def _():
            y_copy(n2 - 1, (n2 - 1) % 2).wait()

        @pl.when(odd == 1)
        def _():
            yt_copy().wait()

    @pl.when(e == N_EXPERTS - 1)
    def _():
        ytail[...] = jnp.zeros_like(ytail)

        def fill(wait):
            def body(k, carry):
                cp = pltpu.make_async_copy(ytail, y_hbm.at[pl.ds(pl.multiple_of(k * MOE_BM, MOE_BM), MOE_BM)],
                                           tsem.at[1])
                if wait:
                    cp.wait()
                else:
                    cp.start()
                return carry
            lax.fori_loop(b0 + n, n_blocks, body, 0)
        fill(False)
        fill(True)


def _expert_ffn(x_sorted, blk_start, nblk, w_gate_up, b_gate_up, w_down, b_down):
    n_rows, d = x_sorted.shape
    ne, _, dff2 = w_gate_up.shape
    dff = dff2 // 2
    any_spec = pl.BlockSpec(memory_space=pl.ANY)
    big = FFN_PAIR * MOE_BM
    grid_spec = pltpu.PrefetchScalarGridSpec(
        num_scalar_prefetch=2,
        grid=(ne,),
        in_specs=[
            any_spec,
            pl.BlockSpec((1, d, dff2), lambda e, *_: (e, 0, 0)),
            pl.BlockSpec((1, 1, dff2), lambda e, *_: (e, 0, 0)),
            pl.BlockSpec((1, dff, d), lambda e, *_: (e, 0, 0)),
            pl.BlockSpec((1, 1, d), lambda e, *_: (e, 0, 0)),
        ],
        out_specs=any_spec,
        scratch_shapes=[
            pltpu.VMEM((2, big, d), F32), pltpu.VMEM((2, big, d), F32),
            pltpu.VMEM((MOE_BM, d), F32), pltpu.VMEM((MOE_BM, d), F32),
            pltpu.SemaphoreType.DMA((2,)), pltpu.SemaphoreType.DMA((2,)), pltpu.SemaphoreType.DMA((2,)),
            pltpu.VMEM((d, dff2), BF16), pltpu.VMEM((dff, d), BF16),
        ],
    )
    return pl.pallas_call(
        functools.partial(_expert_ffn_kernel, n_blocks=n_rows // MOE_BM),
        grid_spec=grid_spec,
        out_shape=jax.ShapeDtypeStruct((n_rows, d), F32),
        compiler_params=_cparams(("arbitrary",)),
        name="expert_ffn",
    )(blk_start, nblk, x_sorted, w_gate_up, b_gate_up.reshape(ne, 1, dff2), w_down, b_down.reshape(ne, 1, d))


def _combine_kernel(d_ref, n_ref, o_ref, r_ref, ls_ref, gate_ref, xp_ref, xs_ref, g_ref, y_hbm, yp_ref, ys_ref,
                    buf, sem, *, n_tiles, n_prompt_tiles):
    j = pl.program_id(0)

    @pl.when(j == 0)
    def _():
        buf[...] = jnp.zeros_like(buf)

    @pl.when(j < n_tiles)
    def _():
        _tile_strips(j, j % 2, d_ref, n_ref, o_ref, buf, y_hbm, sem, to_hbm=False)

    @pl.when(j >= 1)
    def _():
        slot = (j - 1) % 2
        _tile_wait(r_ref[j - 1], slot, buf, y_hbm, sem)
        r_hi = buf[slot].astype(BF16)
        lane = lax.broadcasted_iota(jnp.int32, (TILE_TOKENS, TILE_ROWS), 1)
        pick = jnp.zeros((TILE_TOKENS, TILE_ROWS), F32)
        for k in range(TOP_K):
            pick = pick + jnp.where(lane == ls_ref[:, k:k + 1], gate_ref[:, k:k + 1], 0.0)
        p_hi = pick.astype(BF16)
        p_lo = (pick - p_hi.astype(F32)).astype(BF16)
        moe = jnp.dot(p_hi, r_hi, preferred_element_type=F32) + jnp.dot(p_lo, r_hi, preferred_element_type=F32)
        is_prompt = j - 1 < n_prompt_tiles
        y = _rms(jnp.where(is_prompt, xp_ref[...], xs_ref[...]) + moe, g_ref[...])

        @pl.when(is_prompt)
        def _():
            yp_ref[...] = y

        @pl.when(jnp.logical_not(is_prompt))
        def _():
            ys_ref[...] = y


def _combine(y_sorted, x2p, x2s, ls, gates, tabs, g_final):
    tp, d = x2p.shape
    npt = tp // TILE_TOKENS
    assert x2s.shape[0] == TILE_TOKENS
    nt = npt + 1
    tile = lambda j: jnp.clip(j - 1, 0, nt - 1)
    ptile = lambda j: jnp.clip(j - 1, 0, npt - 1)
    grid_spec = pltpu.PrefetchScalarGridSpec(
        num_scalar_prefetch=4,
        grid=(nt + 1,),
        in_specs=[
            pl.BlockSpec((TILE_TOKENS, TOP_K), lambda j, *_: (tile(j), 0)),
            pl.BlockSpec((TILE_TOKENS, TOP_K), lambda j, *_: (tile(j), 0)),
            pl.BlockSpec((TILE_TOKENS, d), lambda j, *_: (ptile(j), 0)),
            pl.BlockSpec((TILE_TOKENS, d), lambda j, *_: (0, 0)),
            pl.BlockSpec((1, d), lambda j, *_: (0, 0)),
            pl.BlockSpec(memory_space=pl.ANY),
        ],
        out_specs=[
            pl.BlockSpec((TILE_TOKENS, d), lambda j, *_: (ptile(j), 0)),
            pl.BlockSpec((TILE_TOKENS, d), lambda j, *_: (0, 0)),
        ],
        scratch_shapes=[pltpu.VMEM((2, TILE_ROWS, d), F32), pltpu.SemaphoreType.DMA((2,))],
    )
    return pl.pallas_call(
        functools.partial(_combine_kernel, n_tiles=nt, n_prompt_tiles=npt),
        grid_spec=grid_spec,
        out_shape=[jax.ShapeDtypeStruct((tp, d), F32), jax.ShapeDtypeStruct((TILE_TOKENS, d), F32)],
        compiler_params=_cparams(("arbitrary",)),
        name="moe_combine",
    )(tabs["d"], tabs["n"], tabs["o"], tabs["rows"], ls, gates, x2p, x2s, g_final.reshape(1, d), y_sorted)


def _route_tiles(logits):
    t = logits.shape[0]
    nt = -(-t // TILE_TOKENS)
    tpad = nt * TILE_TOKENS
    up = lambda a, m: (a + m - 1) // m * m
    top_v, top_i = lax.top_k(logits, TOP_K)
    gates = jax.nn.softmax(top_v, axis=-1)
    onehot = (top_i[:, :, None] == jnp.arange(N_EXPERTS, dtype=top_i.dtype)[None, None, :])
    per_tok = jnp.pad(jnp.sum(onehot, axis=1).astype(jnp.int32), ((0, tpad - t), (0, 0)))
    before = jnp.cumsum(per_tok, axis=0) - per_tok
    n_tile = up(jnp.sum(per_tok.reshape(nt, TILE_TOKENS, N_EXPERTS), axis=1), SUBLANES)
    o_tile = jnp.cumsum(n_tile, axis=1) - n_tile
    rows_e = jnp.sum(n_tile, axis=0)
    nblk = up(rows_e, MOE_BM) // MOE_BM
    blk_end = jnp.cumsum(nblk)
    pstart = (blk_end - nblk) * MOE_BM
    d_tile = pstart[None, :] + jnp.cumsum(n_tile, axis=0) - n_tile
    local = (o_tile - before[::TILE_TOKENS])[:, None, :] + before.reshape(nt, TILE_TOKENS, N_EXPERTS)
    ls = jnp.sum(jnp.where(onehot, local.reshape(tpad, 1, N_EXPERTS)[:t], 0), axis=-1)
    ls = jnp.pad(ls.astype(jnp.int32), ((0, tpad - t), (0, 0)), constant_values=-1)
    gates = jnp.pad(gates, ((0, tpad - t), (0, 0)))
    nb = (t * TOP_K + N_EXPERTS * nt * (SUBLANES - 1) + N_EXPERTS * (MOE_BM - 1) + MOE_BM - 1) // MOE_BM
    i32 = lambda a: a.astype(jnp.int32).reshape(-1)
    tabs = {"d": i32(d_tile), "n": i32(n_tile), "o": i32(o_tile), "rows": i32(jnp.sum(n_tile, axis=1)),
            "pad_start": i32(pstart + rows_e), "pad_len": i32(nblk * MOE_BM - rows_e),
            "blk_start": i32(blk_end - nblk), "nblk": i32(nblk)}
    return gates, ls, tabs, nb * MOE_BM


def kernel(x_prompt, x_sample, cache_k_win, cache_v_win, state_pool, cache_mem_k, cache_mem_v, mem_prompt,
           rel_bias, g_mix, w_in, w_pool, pool_scale, w_out, g_mem, g_x, w_xq, w_xk, w_xv, w_xo, g_ff,
           w_router, b_router, w_gate_up, b_gate_up, w_down, b_down, g_final):
    depth = g_mix.shape[0]
    assert depth == 1
    l = 0
    batch, seq, d = x_prompt.shape
    bd, n_new, _ = x_sample.shape
    n_mem = mem_prompt.shape[1]
    bw = d - A_WIDTH
    n_hist = cache_k_win.shape[2]
    n_pool = state_pool.shape[2]
    tp, ts = batch * seq, bd * n_new

    w_in_b = w_in[l].astype(BF16)
    w_pool_b = w_pool[l].astype(BF16)
    w_xkv_b = jnp.concatenate([w_xk[l], w_xv[l]], axis=1).astype(BF16)
    wts = {
        "w_out_a": w_out[l, :A_WIDTH].astype(BF16), "w_out_b": w_out[l, A_WIDTH:].astype(BF16),
        "g_x": g_x[l].reshape(1, d), "w_xq": w_xq[l].astype(BF16), "w_xo": w_xo[l].astype(BF16),
        "g_ff": g_ff[l].reshape(1, d), "w_router": w_router[l], "b_router": b_router[l].reshape(1, N_EXPERTS),
    }
    band_bias = _band_bias(rel_bias)
    widths = (A_WIDTH, A_WIDTH, A_WIDTH, bw)
    scales = (A_HEAD_DIM ** -0.5, 1.0, 1.0, 1.0)

    xp = x_prompt.reshape(tp, d)
    w_kv_t = jnp.transpose(w_in[l][:, A_WIDTH:3 * A_WIDTH].reshape(d, 2, A_WIDTH), (1, 2, 0)).astype(BF16)
    q, k, v, u, k_t, v_t = _in_proj(xp, g_mix[l], w_in_b, w_kv_t, widths, scales, batch, seq, tm=512)
    o_a = _dil_attn(q, k, v, band_bias, batch, seq)
    o_b = _pool_prompt(u, w_pool_b, pool_scale[l], batch, seq, tm=512)
    mk, mv, mk_b, mv_b = _mem_kv(mem_prompt.reshape(batch * n_mem, d), g_mem[l], w_xkv_b, X_HEADS, tm=512)
    x2p, hp, lgp = _mix_xattn(xp, o_a, o_b, mk_b, mv_b, wts, groups=batch, rows_per_group=seq, tm=512,
                              rows_per_sub=None, keys_per_sub=None)

    xs = x_sample.reshape(ts, d)
    qs, ks, vs, us = _norm_proj(xs, g_mix[l], w_in_b, widths, scales, tm=ts)
    ck, cv = cache_k_win[l], cache_v_win[l]
    new3 = lambda a: a.reshape(bd, n_new, A_WIDTH)
    to_t = lambda c: jnp.transpose(c, (0, 2, 3, 1)).reshape(bd, A_WIDTH, n_hist)
    from_t = lambda c: jnp.transpose(c.reshape(bd, A_HEADS, A_HEAD_DIM, n_hist), (0, 3, 1, 2))[None]
    bias_c, bias_n = _decode_bias(rel_bias, n_hist, n_new)
    o_as, ck_t, cv_t = _decode_attn(new3(qs), new3(ks), new3(vs), to_t(ck), to_t(cv), bias_c, bias_n,
                                    heads_per_step=A_HEADS)
    o_as = o_as.reshape(ts, A_WIDTH)
    us3 = us.reshape(bd, n_new, bw)
    hist = jnp.concatenate([jnp.zeros((bd, POOL_HIST - n_pool, bw), F32), state_pool[l]], axis=1)
    cur = jnp.concatenate([us3, jnp.zeros((bd, SUBLANES - n_new, bw), F32)], axis=1)
    o_bs = _pool_dec(hist, cur, n_pool, w_pool_b, pool_scale[l])[:, :n_new].reshape(ts, bw)
    sub = 8
    xhd = d // X_HEADS
    x2s, hs, lgs = _mix_xattn(xs, o_as, o_bs, cache_mem_k[l].reshape(bd * n_mem, X_HEADS, xhd),
                              cache_mem_v[l].reshape(bd * n_mem, X_HEADS, xhd), wts, groups=bd // sub,
                              rows_per_group=sub * n_new, tm=sub * n_new, rows_per_sub=n_new, keys_per_sub=n_mem)

    assert ts <= TILE_TOKENS and w_gate_up.shape[1] == N_EXPERTS
    pad_s = lambda a: jnp.pad(a, ((0, TILE_TOKENS - ts), (0, 0)))
    gates, ls, tabs, n_rows = _route_tiles(jnp.concatenate([lgp, lgs], axis=0))
    x_sorted = _dispatch(hp, pad_s(hs), ls.T, tabs, n_rows)
    y_sorted = _expert_ffn(x_sorted, tabs["blk_start"], tabs["nblk"], w_gate_up[l], b_gate_up[l], w_down[l],
                           b_down[l])
    y_prompt, y_sample = _combine(y_sorted, x2p, pad_s(x2s), ls, gates, tabs, g_final)
    y_prompt = y_prompt.reshape(batch, seq, d)
    y_sample = y_sample[:ts].reshape(bd, n_new, d)

    from_tp = lambda c: jnp.transpose(c.reshape(batch, A_HEADS, A_HEAD_DIM, seq), (0, 3, 1, 2))[None]
    k_win_prompt, v_win_prompt = from_tp(k_t), from_tp(v_t)
    pool_prompt = u.reshape(batch, seq, bw)[:, seq - n_pool:][None]
    mem_k_prompt = mk.reshape(1, batch, n_mem, X_HEADS, xhd)
    mem_v_prompt = mv.reshape(1, batch, n_mem, X_HEADS, xhd)
    k_win_sample, v_win_sample = from_t(ck_t), from_t(cv_t)
    pool_sample = jnp.concatenate([state_pool[l][:, n_new:], us3], axis=1)[None]
    return (y_prompt, y_sample, k_win_prompt, v_win_prompt, pool_prompt, mem_k_prompt, mem_v_prompt,
            k_win_sample, v_win_sample, pool_sample)
```

```python
import functools
import math

import numpy as np
import jax
import jax.numpy as jnp
from jax import lax
from jax.experimental import pallas as pl
from jax.experimental.pallas import tpu as pltpu

F32 = jnp.float32
BF16 = jnp.bfloat16

LANES = 128
SUBLANES = 8
VMEM_LIMIT_BYTES = 56 * 1024 * 1024

A_HEADS = 8
A_HEAD_DIM = 64
A_WIDTH = A_HEADS * A_HEAD_DIM
DILATED = ((128, 1), (512, 4), (2048, 16))
QB = 128
POOL_WINDOWS = (2, 4, 8, 16)
POOL_HIST = 16
X_HEADS = 4
N_EXPERTS = 32
TOP_K = 4
SWIGLU_LIMIT = 7.0
SWIGLU_ALPHA = 1.702
N_BUCKETS = 32
RMS_EPS = 1e-6
NEG = -1e30

MOE_BM = 256


def _cparams(sem):
    return pltpu.CompilerParams(dimension_semantics=sem, vmem_limit_bytes=VMEM_LIMIT_BYTES)


def _rms(x, g):
    return x * lax.rsqrt(jnp.mean(x * x, axis=-1, keepdims=True) + RMS_EPS) * g


def _norm_proj_kernel(x_ref, g_ref, w_ref, *o_refs, widths, scales):
    hb = _rms(x_ref[...], g_ref[...]).astype(BF16)
    off = 0
    for o_ref, width, scale in zip(o_refs, widths, scales):
        p = jnp.dot(hb, w_ref[:, off:off + width], preferred_element_type=F32)
        o_ref[...] = p if scale == 1.0 else p * scale
        off += width


def _norm_proj(x, g, w_bf16, widths, scales, tm):
    t, d = x.shape
    n = w_bf16.shape[1]
    assert sum(widths) == n and t % tm == 0
    return pl.pallas_call(
        functools.partial(_norm_proj_kernel, widths=widths, scales=scales),
        grid=(t // tm,),
        in_specs=[
            pl.BlockSpec((tm, d), lambda i: (i, 0)),
            pl.BlockSpec((1, d), lambda i: (0, 0)),
            pl.BlockSpec((d, n), lambda i: (0, 0)),
        ],
        out_specs=[pl.BlockSpec((tm, wd), lambda i: (i, 0)) for wd in widths],
        out_shape=[jax.ShapeDtypeStruct((t, wd), F32) for wd in widths],
        compiler_params=_cparams(("arbitrary",)),
        name="norm_proj",
    )(x, g.reshape(1, d), w_bf16)


def _mem_kv_kernel(x_ref, g_ref, w_ref, k_ref, v_ref, kb_ref, vb_ref, *, heads):
    hb = _rms(x_ref[...], g_ref[...]).astype(BF16)
    d = x_ref.shape[1]
    hd = d // heads
    for j, (o_ref, ob_ref) in enumerate(((k_ref, kb_ref), (v_ref, vb_ref))):
        p = jnp.dot(hb, w_ref[:, j * d:(j + 1) * d], preferred_element_type=F32)
        ob_ref[...] = p.astype(BF16)
        for h in range(heads):
            o_ref[:, h, :] = p[:, h * hd:(h + 1) * hd]


def _mem_kv(x, g, w_bf16, heads, tm):
    t, d = x.shape
    assert w_bf16.shape == (d, 2 * d) and t % tm == 0
    hd = d // heads
    o3 = pl.BlockSpec((tm, heads, hd), lambda i: (i, 0, 0))
    o2 = pl.BlockSpec((tm, d), lambda i: (i, 0))
    return pl.pallas_call(
        functools.partial(_mem_kv_kernel, heads=heads),
        grid=(t // tm,),
        in_specs=[
            pl.BlockSpec((tm, d), lambda i: (i, 0)),
            pl.BlockSpec((1, d), lambda i: (0, 0)),
            pl.BlockSpec((d, 2 * d), lambda i: (0, 0)),
        ],
        out_specs=[o3, o3, o2, o2],
        out_shape=[jax.ShapeDtypeStruct((t, heads, hd), F32)] * 2 + [jax.ShapeDtypeStruct((t, d), BF16)] * 2,
        compiler_params=_cparams(("arbitrary",)),
        name="mem_kv",
    )(x, g.reshape(1, d), w_bf16)


def _in_proj_kernel(x_ref, g_ref, w_ref, wt_ref, *o_refs, widths, scales, n_t):
    hb = _rms(x_ref[...], g_ref[...]).astype(BF16)
    off = 0
    for o_ref, width, scale in zip(o_refs, widths, scales):
        p = jnp.dot(hb, w_ref[:, off:off + width], preferred_element_type=F32)
        o_ref[...] = p if scale == 1.0 else p * scale
        off += width
    for j in range(n_t):
        o_refs[len(widths) + j][0] = lax.dot_general(wt_ref[j], hb, (((1,), (1,)), ((), ())),
                                                    preferred_element_type=F32)


def _in_proj(x, g, w_bf16, wt_bf16, widths, scales, batch, seq, tm):
    t, d = x.shape
    n = w_bf16.shape[1]
    n_t, wt_width, _ = wt_bf16.shape
    nt = seq // tm
    assert sum(widths) == n and seq % tm == 0 and t == batch * seq
    return pl.pallas_call(
        functools.partial(_in_proj_kernel, widths=widths, scales=scales, n_t=n_t),
        grid=(t // tm,),
        in_specs=[
            pl.BlockSpec((tm, d), lambda i: (i, 0)),
            pl.BlockSpec((1, d), lambda i: (0, 0)),
            pl.BlockSpec((d, n), lambda i: (0, 0)),
            pl.BlockSpec(wt_bf16.shape, lambda i: (0, 0, 0)),
        ],
        out_specs=[pl.BlockSpec((tm, wd), lambda i: (i, 0)) for wd in widths]
        + [pl.BlockSpec((1, wt_width, tm), lambda i: (i // nt, 0, i % nt))] * n_t,
        out_shape=[jax.ShapeDtypeStruct((t, wd), F32) for wd in widths]
        + [jax.ShapeDtypeStruct((batch, wt_width, seq), F32)] * n_t,
        compiler_params=_cparams(("arbitrary",)),
        name="in_proj",
    )(x, g.reshape(1, d), w_bf16, wt_bf16)


def _t5_bucket_np(n, max_dist):
    max_exact = N_BUCKETS // 2
    nf = np.maximum(n, 1).astype(np.float32)
    large = max_exact + (
        np.log(nf / np.float32(max_exact)) / np.float32(math.log(max_dist / max_exact))
        * np.float32(N_BUCKETS - max_exact)
    ).astype(np.int32)
    return np.where(n < max_exact, n, np.minimum(large, N_BUCKETS - 1))


def _band_bias(rel_bias):
    max_dist = max(w for w, _ in DILATED)
    qi = np.arange(QB)[:, None]
    ki = np.arange(2 * QB)[None, :]
    j = qi + QB - ki
    tabs = []
    for window, dil in DILATED:
        steps = window // dil
        in_band = (j >= 0) & (j <= steps)
        bucket = _t5_bucket_np(np.clip(j, 0, steps) * dil, max_dist)
        onehot = (bucket[..., None] == np.arange(N_BUCKETS)).astype(np.float32)
        b = jnp.einsum("qkb,bh->hqk", onehot, rel_bias.astype(F32), precision=lax.Precision.HIGHEST)
        tabs.append(jnp.where(in_band[None], b, NEG))
    return jnp.stack(tabs)


def _decode_bias(rel_bias, n_hist, n_new):
    max_dist = max(w for w, _ in DILATED)
    t = np.arange(n_new)[:, None]
    tabs_c, tabs_n = [], []
    for window, dil in DILATED:
        out = []
        for dist in (n_hist + t - np.arange(n_hist)[None, :], t - np.arange(n_new)[None, :]):
            ok = (dist >= 0) & (dist % dil == 0) & (dist <= window)
            onehot = (_t5_bucket_np(np.clip(dist, 0, window), max_dist)[..., None] == np.arange(N_BUCKETS))
            b = jnp.einsum("tpb,bh->htp", onehot.astype(np.float32), rel_bias.astype(F32),
                           precision=lax.Precision.HIGHEST)
            out.append(jnp.where(ok[None], b, NEG))
        tabs_c.append(out[0])
        tabs_n.append(out[1])
    return jnp.stack(tabs_c), jnp.stack(tabs_n)


def _decode_attn_kernel(q_ref, kn_ref, vn_ref, knt_ref, vnt_ref, ck_ref, cv_ref, bc_ref, bn_ref,
                        o_ref, ok_ref, ov_ref, *, n_new, heads):
    n_hist = ck_ref.shape[2]
    hd = A_HEAD_DIM
    outs = []
    for h in range(heads):
        rows = slice(h * hd, (h + 1) * hd)
        q = q_ref[0, :, rows].astype(BF16)
        kt = ck_ref[0, rows, :].astype(BF16)
        vt = cv_ref[0, rows, :].astype(BF16)
        lc = jnp.dot(q, kt, preferred_element_type=F32)
        ln = lax.dot_general(q, kn_ref[0, :, rows].astype(BF16), (((1,), (1,)), ((), ())),
                             preferred_element_type=F32)
        vn = vn_ref[0, :, rows].astype(BF16)
        o_br, lse_br = [], []
        for br in range(len(DILATED)):
            bl = lc + bc_ref[br, h]
            bln = ln + bn_ref[br, h]
            m = jnp.maximum(jnp.max(bl, axis=-1, keepdims=True), jnp.max(bln, axis=-1, keepdims=True))
            p = jnp.exp(bl - m)
            pn = jnp.exp(bln - m)
            s = jnp.sum(p, axis=-1, keepdims=True) + jnp.sum(pn, axis=-1, keepdims=True)
            o = lax.dot_general(p.astype(BF16), vt, (((1,), (1,)), ((), ())), preferred_element_type=F32)
            o = o + jnp.dot(pn.astype(BF16), vn, preferred_element_type=F32)
            o_br.append(o / s)
            lse_br.append(m + jnp.log(s))
        m = jnp.maximum(jnp.maximum(lse_br[0], lse_br[1]), lse_br[2])
        es = [jnp.exp(l - m) for l in lse_br]
        outs.append((es[0] * o_br[0] + es[1] * o_br[1] + es[2] * o_br[2]) / (es[0] + es[1] + es[2]))
    o_ref[0] = jnp.concatenate(outs, axis=-1)

    lane = lax.broadcasted_iota(jnp.int32, (ck_ref.shape[1], LANES), 1)
    for c_ref, nt_ref, dst in ((ck_ref, knt_ref, ok_ref), (cv_ref, vnt_ref, ov_ref)):
        shifted = pltpu.roll(c_ref[0], n_hist - n_new, axis=1)
        tail = shifted[:, n_hist - LANES:]
        for j in range(n_new):
            tail = jnp.where(lane == LANES - n_new + j, nt_ref[0, :, j:j + 1], tail)
        dst[0, :, :n_hist - LANES] = shifted[:, :n_hist - LANES]
        dst[0, :, n_hist - LANES:] = tail


def _decode_attn(q, k_new, v_new, ck_t, cv_t, bias_c, bias_n, heads_per_step):
    bd, n_new, aw = q.shape
    n_hist = ck_t.shape[2]
    gw = heads_per_step * A_HEAD_DIM
    ng = aw // gw
    row = pl.BlockSpec((1, n_new, gw), lambda b, g: (b, 0, g))
    col = pl.BlockSpec((1, gw, n_new), lambda b, g: (b, g, 0))
    cache = pl.BlockSpec((1, gw, n_hist), lambda b, g: (b, g, 0))
    nbr = len(DILATED)
    return pl.pallas_call(
        functools.partial(_decode_attn_kernel, n_new=n_new, heads=heads_per_step),
        grid=(bd, ng),
        in_specs=[row, row, row, col, col, cache, cache,
                  pl.BlockSpec((nbr, heads_per_step, n_new, n_hist), lambda b, g: (0, g, 0, 0)),
                  pl.BlockSpec((nbr, heads_per_step, n_new, n_new), lambda b, g: (0, g, 0, 0))],
        out_specs=[row, cache, cache],
        out_shape=[jax.ShapeDtypeStruct(q.shape, F32), jax.ShapeDtypeStruct(ck_t.shape, F32),
                   jax.ShapeDtypeStruct(cv_t.shape, F32)],
        compiler_params=_cparams(("arbitrary", "arbitrary")),
        name="decode_attn",
    )(q, k_new, v_new, jnp.swapaxes(k_new, 1, 2), jnp.swapaxes(v_new, 1, 2), ck_t, cv_t, bias_c, bias_n)


def _dil_attn_kernel(q_ref, k_ref, v_ref, bias_ref, o_ref, obr_ref, lbr_ref, *, seq):
    lane = lax.broadcasted_iota(jnp.int32, (QB, LANES), 1)
    head0 = lane < A_HEAD_DIM

    def rows(ref, start, n, dil):
        if dil == 1:
            return ref[pl.ds(start, n), :]
        return ref[pl.ds(start, n, stride=dil), :]

    def block(br, dil, qstart, kstart, nk):
        qs = rows(q_ref, qstart, QB, dil)
        ks = rows(k_ref, kstart, nk, dil).astype(BF16)
        vs = rows(v_ref, kstart, nk, dil).astype(BF16)
        outs, lses = [], []
        for hh in range(2):
            keep = head0 if hh == 0 else jnp.logical_not(head0)
            qm = jnp.where(keep, qs, 0.0).astype(BF16)
            logits = lax.dot_general(qm, ks, (((1,), (1,)), ((), ())), preferred_element_type=F32)
            logits = logits + bias_ref[br, hh, :, 2 * QB - nk:]
            m = jnp.max(logits, axis=-1, keepdims=True)
            p = jnp.exp(logits - m)
            s = jnp.sum(p, axis=-1, keepdims=True)
            o = jnp.dot(p.astype(BF16), vs, preferred_element_type=F32)
            outs.append(o / s)
            lses.append(jnp.broadcast_to(m + jnp.log(s), (QB, LANES)))
        o = jnp.where(head0, outs[0], outs[1])
        lse = jnp.where(head0, lses[0], lses[1])
        if dil == 1:
            obr_ref[br, pl.ds(qstart, QB), :] = o
            lbr_ref[br, pl.ds(qstart, QB), :] = lse
        else:
            obr_ref[br, pl.ds(qstart, QB, stride=dil), :] = o
            lbr_ref[br, pl.ds(qstart, QB, stride=dil), :] = lse

    for br, (window, dil) in enumerate(DILATED):
        assert window // dil == QB
        nblk = seq // (dil * QB)
        for r in range(dil):
            block(br, dil, r, r, QB)
            for i in range(1, nblk):
                qstart = r + dil * QB * i
                block(br, dil, qstart, qstart - dil * QB, 2 * QB)

    l0, l1, l2 = lbr_ref[0], lbr_ref[1], lbr_ref[2]
    m = jnp.maximum(jnp.maximum(l0, l1), l2)
    e0, e1, e2 = jnp.exp(l0 - m), jnp.exp(l1 - m), jnp.exp(l2 - m)
    acc = e0 * obr_ref[0] + e1 * obr_ref[1] + e2 * obr_ref[2]
    o_ref[...] = acc / (e0 + e1 + e2)


def _dil_attn(q, k, v, band_bias, batch, seq):
    t, aw = q.shape
    npair = aw // LANES
    bias = band_bias.reshape(len(DILATED), npair, 2, QB, 2 * QB)
    spec = pl.BlockSpec((seq, LANES), lambda b, hp: (b, hp))
    return pl.pallas_call(
        functools.partial(_dil_attn_kernel, seq=seq),
        grid=(batch, npair),
        in_specs=[
            spec, spec, spec,
            pl.BlockSpec((len(DILATED), None, 2, QB, 2 * QB), lambda b, hp: (0, hp, 0, 0, 0)),
        ],
        out_specs=spec,
        out_shape=jax.ShapeDtypeStruct((t, aw), F32),
        scratch_shapes=[
            pltpu.VMEM((len(DILATED), seq, LANES), F32),
            pltpu.VMEM((len(DILATED), seq, LANES), F32),
        ],
        compiler_params=_cparams(("arbitrary", "arbitrary")),
        name="dil_attn",
    )(q, k, v, bias)


def _pool_kernel(hist_ref, cur_ref, w_ref, scale_ref, o_ref, *, pos0_of_tile, tm):
    nb = cur_ref.shape[0]
    gdim = w_ref.shape[1]
    i = pl.program_id(1)
    pos0 = pos0_of_tile(i)
    have_hist = jnp.where(pos0 > 0, 1.0, 0.0).astype(F32)
    t = lax.broadcasted_iota(jnp.int32, (1, tm, 1), 1)
    for g, w in enumerate(POOL_WINDOWS):
        sl = slice(g * gdim, (g + 1) * gdim)
        cur = cur_ref[:, :, sl]
        ext = jnp.concatenate([hist_ref[:, :, sl] * have_hist, cur], axis=1)
        acc, span = ext, 1
        while span < w:
            n = acc.shape[1]
            acc = acc[:, span:n] + acc[:, 0:n - span]
            span *= 2
        wsum = acc[:, POOL_HIST + 1 - w:POOL_HIST + 1 - w + tm]
        cnt = jnp.minimum(pos0 + t + 1, w).astype(F32)
        d = (wsum / cnt - cur).astype(BF16).reshape(nb * tm, gdim)
        y = jnp.dot(d, w_ref[g], preferred_element_type=F32) * scale_ref[:, sl]
        o_ref[:, :, sl] = y.reshape(nb, tm, gdim)


def _pool_prompt(u, w_pool_bf16, pool_scale, batch, seq, tm):
    t, bw = u.shape
    nt = seq // tm
    hb = tm // POOL_HIST
    u3 = u.reshape(1, t, bw)
    out = pl.pallas_call(
        functools.partial(_pool_kernel, pos0_of_tile=lambda i: i * tm, tm=tm),
        grid=(batch, nt),
        in_specs=[
            pl.BlockSpec((1, POOL_HIST, bw), lambda b, i: (0, jnp.maximum((b * nt + i) * hb - 1, 0), 0)),
            pl.BlockSpec((1, tm, bw), lambda b, i: (0, b * nt + i, 0)),
            pl.BlockSpec(w_pool_bf16.shape, lambda b, i: (0, 0, 0)),
            pl.BlockSpec((1, bw), lambda b, i: (0, 0)),
        ],
        out_specs=pl.BlockSpec((1, tm, bw), lambda b, i: (0, b * nt + i, 0)),
        out_shape=jax.ShapeDtypeStruct((1, t, bw), F32),
        compiler_params=_cparams(("arbitrary", "arbitrary")),
        name="pool_prompt",
    )(u3, u3, w_pool_bf16, pool_scale.reshape(1, bw))
    return out.reshape(t, bw)


def _pool_dec(hist, cur, n_prev, w_pool_bf16, pool_scale):
    bd, tm, bw = cur.shape
    return pl.pallas_call(
        functools.partial(_pool_kernel, pos0_of_tile=lambda i: n_prev, tm=tm),
        grid=(1, 1),
        in_specs=[
            pl.BlockSpec(hist.shape, lambda b, i: (0, 0, 0)),
            pl.BlockSpec(cur.shape, lambda b, i: (0, 0, 0)),
            pl.BlockSpec(w_pool_bf16.shape, lambda b, i: (0, 0, 0)),
            pl.BlockSpec((1, bw), lambda b, i: (0, 0)),
        ],
        out_specs=pl.BlockSpec(cur.shape, lambda b, i: (0, 0, 0)),
        out_shape=jax.ShapeDtypeStruct(cur.shape, F32),
        compiler_params=_cparams(("arbitrary", "arbitrary")),
        name="pool_dec",
    )(hist, cur, w_pool_bf16, pool_scale.reshape(1, bw))


def _mix_xattn_kernel(x_ref, oa_ref, ob_ref, mk_ref, mv_ref, woa_ref, wob_ref, gx_ref, wq_ref, wo_ref,
                      gff_ref, wr_ref, br_ref, x2_ref, h_ref, lg_ref, *, rows_per_sub, keys_per_sub):
    tm = x_ref.shape[0]
    nkv = mk_ref.shape[0]
    xh = wq_ref.shape[1] // X_HEADS
    x = x_ref[...]
    x = x + jnp.dot(oa_ref[...].astype(BF16), woa_ref[...], preferred_element_type=F32)
    x = x + jnp.dot(ob_ref[...].astype(BF16), wob_ref[...], preferred_element_type=F32)
    hq = _rms(x, gx_ref[...]).astype(BF16)
    if rows_per_sub is not None:
        rsub = (pl.program_id(1) * tm + lax.broadcasted_iota(jnp.int32, (tm, nkv), 0)) // rows_per_sub
        ksub = lax.broadcasted_iota(jnp.int32, (tm, nkv), 1) // keys_per_sub
        same = rsub == ksub
    heads = []
    for h in range(X_HEADS):
        sl = slice(h * xh, (h + 1) * xh)
        q = jnp.dot(hq, wq_ref[:, sl], preferred_element_type=F32) * (xh ** -0.5)
        if len(mk_ref.shape) == 3:
            kh, vh = mk_ref[:, h, :].astype(BF16), mv_ref[:, h, :].astype(BF16)
        else:
            kh, vh = mk_ref[:, sl], mv_ref[:, sl]
        logits = lax.dot_general(q.astype(BF16), kh, (((1,), (1,)), ((), ())), preferred_element_type=F32)
        if rows_per_sub is not None:
            logits = jnp.where(same, logits, NEG)
        m = jnp.max(logits, axis=-1, keepdims=True)
        p = jnp.exp(logits - m)
        s = jnp.sum(p, axis=-1, keepdims=True)
        o = jnp.dot(p.astype(BF16), vh, preferred_element_type=F32) / s
        heads.append(o.astype(BF16))
    x = x + jnp.dot(jnp.concatenate(heads, axis=-1), wo_ref[...], preferred_element_type=F32)
    x2_ref[...] = x
    hf = _rms(x, gff_ref[...])
    h_ref[...] = hf
    lg_ref[...] = jnp.dot(hf.astype(BF16), wr_ref[...].astype(BF16), preferred_element_type=F32) + br_ref[...]


def _mix_xattn(x, oa, ob, mk, mv, w, groups, rows_per_group, tm, rows_per_sub, keys_per_sub):
    t, d = x.shape
    nt = rows_per_group // tm
    nkv = mk.shape[0] // groups
    ne = w["w_router"].shape[1]
    row = lambda width: pl.BlockSpec((tm, width), lambda g, i: (g * nt + i, 0))
    const = lambda a: pl.BlockSpec(a.shape, lambda g, i: (0,) * a.ndim)
    kv = pl.BlockSpec((nkv,) + mk.shape[1:], lambda g, i: (g,) + (0,) * (mk.ndim - 1))
    consts = [w["w_out_a"], w["w_out_b"], w["g_x"], w["w_xq"], w["w_xo"], w["g_ff"], w["w_router"], w["b_router"]]
    return pl.pallas_call(
        functools.partial(_mix_xattn_kernel, rows_per_sub=rows_per_sub, keys_per_sub=keys_per_sub),
        grid=(groups, nt),
        in_specs=[row(d), row(oa.shape[1]), row(ob.shape[1]), kv, kv] + [const(a) for a in consts],
        out_specs=[row(d), row(d), row(ne)],
        out_shape=[jax.ShapeDtypeStruct((t, d), F32), jax.ShapeDtypeStruct((t, d), F32),
                   jax.ShapeDtypeStruct((t, ne), F32)],
        compiler_params=_cparams(("arbitrary", "arbitrary")),
        name="mix_xattn",
    )(x, oa, ob, mk, mv, *consts)


TILE_TOKENS = 256
TILE_ROWS = TOP_K * TILE_TOKENS + N_EXPERTS * (SUBLANES - 1)
assert TILE_ROWS % SUBLANES == 0
FFN_PAIR = 2


def _bits(lo, hi):
    return tuple(lo << j for j in range((hi // lo).bit_length()))


STRIP_LEN_BITS = _bits(SUBLANES, TILE_TOKENS)
PAD_LEN_BITS = _bits(SUBLANES, MOE_BM - SUBLANES)
TILE_LEN_BITS = _bits(SUBLANES, TILE_ROWS)


def _strip_dmas(n, bits, make_copy, wait=False):
    for bit in bits:
        @pl.when((n & bit) != 0)
        def _(bit=bit):
            c = make_copy(n & (bit - 1), bit)
            if wait:
                c.wait()
            else:
                c.start()


def _tile_strips(i, slot, d_ref, n_ref, o_ref, local, hbm, sem, to_hbm):
    for e in range(N_EXPERTS):
        l0 = pl.multiple_of(o_ref[i * N_EXPERTS + e], SUBLANES)
        g0 = pl.multiple_of(d_ref[i * N_EXPERTS + e], SUBLANES)

        def make_copy(off, size, l0=l0, g0=g0):
            lo = local.at[slot, pl.ds(pl.multiple_of(l0 + off, SUBLANES), size)]
            gl = hbm.at[pl.ds(pl.multiple_of(g0 + off, SUBLANES), size)]
            src, dst = (lo, gl) if to_hbm else (gl, lo)
            return pltpu.make_async_copy(src, dst, sem.at[slot])
        _strip_dmas(n_ref[i * N_EXPERTS + e], STRIP_LEN_BITS, make_copy)


def _tile_wait(rows, slot, local, hbm, sem):
    _strip_dmas(rows, TILE_LEN_BITS, lambda off, size: pltpu.make_async_copy(
        hbm.at[pl.ds(0, size)], local.at[slot, pl.ds(0, size)], sem.at[slot]), wait=True)


def _dispatch_kernel(d_ref, n_ref, o_ref, r_ref, ps_ref, pn_ref, lst_ref, hp_ref, hs_ref, xs_hbm, buf, zbuf, sem,
                     psem, *, n_tiles, n_prompt_tiles):
    i = pl.program_id(0)
    slot = i % 2

    def pad_strips(wait):
        for e in range(N_EXPERTS):
            start = ps_ref[e]
            _strip_dmas(pn_ref[e], PAD_LEN_BITS, lambda off, size: pltpu.make_async_copy(
                zbuf.at[pl.ds(0, size)], xs_hbm.at[pl.ds(pl.multiple_of(start + off, SUBLANES), size)],
                psem.at[0]), wait)
        chunk = zbuf.shape[0]
        used_rows = ps_ref[N_EXPERTS - 1] + pn_ref[N_EXPERTS - 1]

        def body(c, carry):
            cp = pltpu.make_async_copy(zbuf, xs_hbm.at[pl.ds(pl.multiple_of(c * chunk, chunk), chunk)], psem.at[0])
            if wait:
                cp.wait()
            else:
                cp.start()
            return carry
        lax.fori_loop(used_rows // chunk, xs_hbm.shape[0] // chunk, body, 0)

    @pl.when(i >= 2)
    def _():
        _tile_wait(r_ref[i - 2], slot, buf, xs_hbm, sem)

    @pl.when(i == 0)
    def _():
        zbuf[...] = jnp.zeros_like(zbuf)
        pad_strips(False)

    srow = lax.broadcasted_iota(jnp.int32, (TILE_ROWS, TILE_TOKENS), 0)
    place = srow == lst_ref[0:1, :]
    for k in range(1, TOP_K):
        place = jnp.logical_or(place, srow == lst_ref[k:k + 1, :])
    x = jnp.where(i < n_prompt_tiles, hp_ref[...], hs_ref[...]).astype(BF16)
    buf[slot] = jnp.dot(jnp.where(place, 1.0, 0.0).astype(BF16), x, preferred_element_type=F32)
    _tile_strips(i, slot, d_ref, n_ref, o_ref, buf, xs_hbm, sem, to_hbm=True)

    @pl.when(i == n_tiles - 1)
    def _():
        _tile_wait(r_ref[i], slot, buf, xs_hbm, sem)
        if n_tiles > 1:
            _tile_wait(r_ref[jnp.maximum(i - 1, 0)], 1 - slot, buf, xs_hbm, sem)
        pad_strips(True)


def _dispatch(hp, hs, ls_t, tabs, n_rows):
    tp, d = hp.shape
    npt = tp // TILE_TOKENS
    assert tp % TILE_TOKENS == 0 and hs.shape[0] == TILE_TOKENS
    nt = npt + 1
    grid_spec = pltpu.PrefetchScalarGridSpec(
        num_scalar_prefetch=6,
        grid=(nt,),
        in_specs=[
            pl.BlockSpec((TOP_K, TILE_TOKENS), lambda i, *_: (0, i)),
            pl.BlockSpec((TILE_TOKENS, d), lambda i, *_: (jnp.minimum(i, npt - 1), 0)),
            pl.BlockSpec((TILE_TOKENS, d), lambda i, *_: (0, 0)),
        ],
        out_specs=pl.BlockSpec(memory_space=pl.ANY),
        scratch_shapes=[
            pltpu.VMEM((2, TILE_ROWS, d), F32),
            pltpu.VMEM((PAD_LEN_BITS[-1], d), F32),
            pltpu.SemaphoreType.DMA((2,)),
            pltpu.SemaphoreType.DMA((1,)),
        ],
    )
    return pl.pallas_call(
        functools.partial(_dispatch_kernel, n_tiles=nt, n_prompt_tiles=npt),
        grid_spec=grid_spec,
        out_shape=jax.ShapeDtypeStruct((n_rows, d), F32),
        compiler_params=_cparams(("arbitrary",)),
        name="moe_dispatch",
    )(tabs["d"], tabs["n"], tabs["o"], tabs["rows"], tabs["pad_start"], tabs["pad_len"], ls_t, hp, hs)


def _expert_ffn_kernel(b0_ref, nb_ref, x_hbm, wgu_hbm, bgu_ref, wd_hbm, bd_ref, y_hbm, xbuf, ybuf, xtail, ytail,
                       xsem, ysem, tsem, wgu_st, wd_st, wsem, wgu_bf, wd_bf, *, n_blocks):
    e = pl.program_id(0)
    b0 = b0_ref[e]
    n = nb_ref[e]
    n2 = n // FFN_PAIR
    odd = n % FFN_PAIR
    dff = wd_st.shape[0]
    big = FFN_PAIR * MOE_BM

    def w_copies(ex):
        return (pltpu.make_async_copy(wgu_hbm.at[ex], wgu_st, wsem.at[0]),
                pltpu.make_async_copy(wd_hbm.at[ex], wd_st, wsem.at[1]))

    def rows(blk, size):
        return pl.ds(pl.multiple_of((b0 + blk) * MOE_BM, MOE_BM), size)

    def x_copy(kk, slot):
        return pltpu.make_async_copy(x_hbm.at[rows(FFN_PAIR * kk, big)], xbuf.at[slot], xsem.at[slot])

    def y_copy(kk, slot):
        return pltpu.make_async_copy(ybuf.at[slot], y_hbm.at[rows(FFN_PAIR * kk, big)], ysem.at[slot])

    def xt_copy():
        return pltpu.make_async_copy(x_hbm.at[rows(n - 1, MOE_BM)], xtail, tsem.at[0])

    def yt_copy():
        return pltpu.make_async_copy(ytail, y_hbm.at[rows(n - 1, MOE_BM)], tsem.at[1])

    def ffn(x):
        hu = jnp.dot(x.astype(BF16), wgu_bf[...], preferred_element_type=F32) + bgu_ref[0]
        g = jnp.minimum(hu[:, :dff], SWIGLU_LIMIT)
        u = jnp.clip(hu[:, dff:], -SWIGLU_LIMIT, SWIGLU_LIMIT)
        a = g * jax.nn.sigmoid(SWIGLU_ALPHA * g) * (u + 1.0)
        return jnp.dot(a.astype(BF16), wd_bf[...], preferred_element_type=F32) + bd_ref[0]

    @pl.when(e == 0)
    def _():
        for c in w_copies(0):
            c.start()

    @pl.when(jnp.logical_and(n > 0, odd == 1))
    def _():
        xt_copy().start()

    @pl.when(n2 > 0)
    def _():
        x_copy(0, 0).start()

    @pl.when(n2 > 1)
    def _():
        x_copy(1, 1).start()

    for c in w_copies(e):
        c.wait()

    @pl.when(n > 0)
    def _():
        wgu_bf[...] = wgu_st[...].astype(BF16)
        wd_bf[...] = wd_st[...].astype(BF16)

    @pl.when(e + 1 < N_EXPERTS)
    def _():
        for c in w_copies(e + 1):
            c.start()

    @pl.when(n > 0)
    def _():
        def body(kk, carry):
            slot = kk % 2

            @pl.when(jnp.logical_and(kk >= 1, kk + 1 < n2))
            def _():
                x_copy(kk + 1, 1 - slot).start()

            x_copy(kk, slot).wait()

            @pl.when(kk >= 2)
            def _():
                y_copy(kk - 2, slot).wait()

            ybuf[slot] = ffn(xbuf[slot])
            y_copy(kk, slot).start()
            return carry
        lax.fori_loop(0, n2, body, 0)

        @pl.when(odd == 1)
        def _():
            xt_copy().wait()
            ytail[...] = ffn(xtail[...])
            yt_copy().start()

        @pl.when(n2 >= 2)
        def _():
            y_copy(n2 - 2, n2 % 2).wait()

        @pl.when(n2 >= 1)
        def _():
            y_copy(n2 - 1, (n2 - 1) % 2).wait()

        @pl.when(odd == 1)
        def _():
            yt_copy().wait()

    @pl.when(e == N_EXPERTS - 1)
    def _():
        ytail[...] = jnp.zeros_like(ytail)

        def fill(wait):
            def body(k, carry):
                cp = pltpu.make_async_copy(ytail, y_hbm.at[pl.ds(pl.multiple_of(k * MOE_BM, MOE_BM), MOE_BM)],
                                           tsem.at[1])
                if wait:
                    cp.wait()
                else:
                    cp.start()
                return carry
            lax.fori_loop(b0 + n, n_blocks, body, 0)
        fill(False)
        fill(True)


def _expert_ffn(x_sorted, blk_start, nblk, w_gate_up, b_gate_up, w_down, b_down):
    n_rows, d = x_sorted.shape
    ne, _, dff2 = w_gate_up.shape
    dff = dff2 // 2
    any_spec = pl.BlockSpec(memory_space=pl.ANY)
    big = FFN_PAIR * MOE_BM
    grid_spec = pltpu.PrefetchScalarGridSpec(
        num_scalar_prefetch=2,
        grid=(ne,),
        in_specs=[
            any_spec,
            any_spec,
            pl.BlockSpec((1, 1, dff2), lambda e, *_: (e, 0, 0)),
            any_spec,
            pl.BlockSpec((1, 1, d), lambda e, *_: (e, 0, 0)),
        ],
        out_specs=any_spec,
        scratch_shapes=[
            pltpu.VMEM((2, big, d), F32), pltpu.VMEM((2, big, d), F32),
            pltpu.VMEM((MOE_BM, d), F32), pltpu.VMEM((MOE_BM, d), F32),
            pltpu.SemaphoreType.DMA((2,)), pltpu.SemaphoreType.DMA((2,)), pltpu.SemaphoreType.DMA((2,)),
            pltpu.VMEM((d, dff2), F32), pltpu.VMEM((dff, d), F32), pltpu.SemaphoreType.DMA((2,)),
            pltpu.VMEM((d, dff2), BF16), pltpu.VMEM((dff, d), BF16),
        ],
    )
    return pl.pallas_call(
        functools.partial(_expert_ffn_kernel, n_blocks=n_rows // MOE_BM),
        grid_spec=grid_spec,
        out_shape=jax.ShapeDtypeStruct((n_rows, d), F32),
        compiler_params=_cparams(("arbitrary",)),
        name="expert_ffn",
    )(blk_start, nblk, x_sorted, w_gate_up, b_gate_up.reshape(ne, 1, dff2), w_down, b_down.reshape(ne, 1, d))


def _combine_kernel(d_ref, n_ref, o_ref, r_ref, ls_ref, gate_ref, xp_ref, xs_ref, g_ref, y_hbm, yp_ref, ys_ref,
                    buf, sem, *, n_tiles, n_prompt_tiles):
    j = pl.program_id(0)

    @pl.when(j == 0)
    def _():
        buf[...] = jnp.zeros_like(buf)

    @pl.when(j < n_tiles)
    def _():
        _tile_strips(j, j % 2, d_ref, n_ref, o_ref, buf, y_hbm, sem, to_hbm=False)

    @pl.when(j >= 1)
    def _():
        slot = (j - 1) % 2
        _tile_wait(r_ref[j - 1], slot, buf, y_hbm, sem)
        r_hi = buf[slot].astype(BF16)
        lane = lax.broadcasted_iota(jnp.int32, (TILE_TOKENS, TILE_ROWS), 1)
        pick = jnp.zeros((TILE_TOKENS, TILE_ROWS), F32)
        for k in range(TOP_K):
            pick = pick + jnp.where(lane == ls_ref[:, k:k + 1], gate_ref[:, k:k + 1], 0.0)
        p_hi = pick.astype(BF16)
        p_lo = (pick - p_hi.astype(F32)).astype(BF16)
        moe = jnp.dot(p_hi, r_hi, preferred_element_type=F32) + jnp.dot(p_lo, r_hi, preferred_element_type=F32)
        is_prompt = j - 1 < n_prompt_tiles
        y = _rms(jnp.where(is_prompt, xp_ref[...], xs_ref[...]) + moe, g_ref[...])

        @pl.when(is_prompt)
        def _():
            yp_ref[...] = y

        @pl.when(jnp.logical_not(is_prompt))
        def _():
            ys_ref[...] = y


def _combine(y_sorted, x2p, x2s, ls, gates, tabs, g_final):
    tp, d = x2p.shape
    npt = tp // TILE_TOKENS
    assert x2s.shape[0] == TILE_TOKENS
    nt = npt + 1
    tile = lambda j: jnp.clip(j - 1, 0, nt - 1)
    ptile = lambda j: jnp.clip(j - 1, 0, npt - 1)
    grid_spec = pltpu.PrefetchScalarGridSpec(
        num_scalar_prefetch=4,
        grid=(nt + 1,),
        in_specs=[
            pl.BlockSpec((TILE_TOKENS, TOP_K), lambda j, *_: (tile(j), 0)),
            pl.BlockSpec((TILE_TOKENS, TOP_K), lambda j, *_: (tile(j), 0)),
            pl.BlockSpec((TILE_TOKENS, d), lambda j, *_: (ptile(j), 0)),
            pl.BlockSpec((TILE_TOKENS, d), lambda j, *_: (0, 0)),
            pl.BlockSpec((1, d), lambda j, *_: (0, 0)),
            pl.BlockSpec(memory_space=pl.ANY),
        ],
        out_specs=[
            pl.BlockSpec((TILE_TOKENS, d), lambda j, *_: (ptile(j), 0)),
            pl.BlockSpec((TILE_TOKENS, d), lambda j, *_: (0, 0)),
        ],
        scratch_shapes=[pltpu.VMEM((2, TILE_ROWS, d), F32), pltpu.SemaphoreType.DMA((2,))],
    )
    return pl.pallas_call(
        functools.partial(_combine_kernel, n_tiles=nt, n_prompt_tiles=npt),
        grid_spec=grid_spec,
        out_shape=[jax.ShapeDtypeStruct((tp, d), F32), jax.ShapeDtypeStruct((TILE_TOKENS, d), F32)],
        compiler_params=_cparams(("arbitrary",)),
        name="moe_combine",
    )(tabs["d"], tabs["n"], tabs["o"], tabs["rows"], ls, gates, x2p, x2s, g_final.reshape(1, d), y_sorted)


def _route_tiles(logits):
    t = logits.shape[0]
    nt = -(-t // TILE_TOKENS)
    tpad = nt * TILE_TOKENS
    up = lambda a, m: (a + m - 1) // m * m
    top_v, top_i = lax.top_k(logits, TOP_K)
    gates = jax.nn.softmax(top_v, axis=-1)
    onehot = (top_i[:, :, None] == jnp.arange(N_EXPERTS, dtype=top_i.dtype)[None, None, :])
    per_tok = jnp.pad(jnp.sum(onehot, axis=1).astype(jnp.int32), ((0, tpad - t), (0, 0)))
    before = jnp.cumsum(per_tok, axis=0) - per_tok
    n_tile = up(jnp.sum(per_tok.reshape(nt, TILE_TOKENS, N_EXPERTS), axis=1), SUBLANES)
    o_tile = jnp.cumsum(n_tile, axis=1) - n_tile
    rows_e = jnp.sum(n_tile, axis=0)
    nblk = up(rows_e, MOE_BM) // MOE_BM
    blk_end = jnp.cumsum(nblk)
    pstart = (blk_end - nblk) * MOE_BM
    d_tile = pstart[None, :] + jnp.cumsum(n_tile, axis=0) - n_tile
    local = (o_tile - before[::TILE_TOKENS])[:, None, :] + before.reshape(nt, TILE_TOKENS, N_EXPERTS)
    ls = jnp.sum(jnp.where(onehot, local.reshape(tpad, 1, N_EXPERTS)[:t], 0), axis=-1)
    ls = jnp.pad(ls.astype(jnp.int32), ((0, tpad - t), (0, 0)), constant_values=-1)
    gates = jnp.pad(gates, ((0, tpad - t), (0, 0)))
    nb = (t * TOP_K + N_EXPERTS * nt * (SUBLANES - 1) + N_EXPERTS * (MOE_BM - 1) + MOE_BM - 1) // MOE_BM
    i32 = lambda a: a.astype(jnp.int32).reshape(-1)
    tabs = {"d": i32(d_tile), "n": i32(n_tile), "o": i32(o_tile), "rows": i32(jnp.sum(n_tile, axis=1)),
            "pad_start": i32(pstart + rows_e), "pad_len": i32(nblk * MOE_BM - rows_e),
            "blk_start": i32(blk_end - nblk), "nblk": i32(nblk)}
    return gates, ls, tabs, nb * MOE_BM


def kernel(x_prompt, x_sample, cache_k_win, cache_v_win, state_pool, cache_mem_k, cache_mem_v, mem_prompt,
           rel_bias, g_mix, w_in, w_pool, pool_scale, w_out, g_mem, g_x, w_xq, w_xk, w_xv, w_xo, g_ff,
           w_router, b_router, w_gate_up, b_gate_up, w_down, b_down, g_final):
    depth = g_mix.shape[0]
    assert depth == 1
    l = 0
    batch, seq, d = x_prompt.shape
    bd, n_new, _ = x_sample.shape
    n_mem = mem_prompt.shape[1]
    bw = d - A_WIDTH
    n_hist = cache_k_win.shape[2]
    n_pool = state_pool.shape[2]
    tp, ts = batch * seq, bd * n_new

    w_in_b = w_in[l].astype(BF16)
    w_pool_b = w_pool[l].astype(BF16)
    w_xkv_b = jnp.concatenate([w_xk[l], w_xv[l]], axis=1).astype(BF16)
    wts = {
        "w_out_a": w_out[l, :A_WIDTH].astype(BF16), "w_out_b": w_out[l, A_WIDTH:].astype(BF16),
        "g_x": g_x[l].reshape(1, d), "w_xq": w_xq[l].astype(BF16), "w_xo": w_xo[l].astype(BF16),
        "g_ff": g_ff[l].reshape(1, d), "w_router": w_router[l], "b_router": b_router[l].reshape(1, N_EXPERTS),
    }
    band_bias = _band_bias(rel_bias)
    widths = (A_WIDTH, A_WIDTH, A_WIDTH, bw)
    scales = (A_HEAD_DIM ** -0.5, 1.0, 1.0, 1.0)

    xp = x_prompt.reshape(tp, d)
    w_kv_t = jnp.transpose(w_in[l][:, A_WIDTH:3 * A_WIDTH].reshape(d, 2, A_WIDTH), (1, 2, 0)).astype(BF16)
    q, k, v, u, k_t, v_t = _in_proj(xp, g_mix[l], w_in_b, w_kv_t, widths, scales, batch, seq, tm=512)
    o_a = _dil_attn(q, k, v, band_bias, batch, seq)
    o_b = _pool_prompt(u, w_pool_b, pool_scale[l], batch, seq, tm=512)
    mk, mv, mk_b, mv_b = _mem_kv(mem_prompt.reshape(batch * n_mem, d), g_mem[l], w_xkv_b, X_HEADS, tm=512)
    x2p, hp, lgp = _mix_xattn(xp, o_a, o_b, mk_b, mv_b, wts, groups=batch, rows_per_group=seq, tm=512,
                              rows_per_sub=None, keys_per_sub=None)

    xs = x_sample.reshape(ts, d)
    qs, ks, vs, us = _norm_proj(xs, g_mix[l], w_in_b, widths, scales, tm=ts)
    ck, cv = cache_k_win[l], cache_v_win[l]
    new3 = lambda a: a.reshape(bd, n_new, A_WIDTH)
    to_t = lambda c: jnp.transpose(c, (0, 2, 3, 1)).reshape(bd, A_WIDTH, n_hist)
    from_t = lambda c: jnp.transpose(c.reshape(bd, A_HEADS, A_HEAD_DIM, n_hist), (0, 3, 1, 2))[None]
    bias_c, bias_n = _decode_bias(rel_bias, n_hist, n_new)
    o_as, ck_t, cv_t = _decode_attn(new3(qs), new3(ks), new3(vs), to_t(ck), to_t(cv), bias_c, bias_n,
                                    heads_per_step=A_HEADS)
    o_as = o_as.reshape(ts, A_WIDTH)
    us3 = us.reshape(bd, n_new, bw)
    hist = jnp.concatenate([jnp.zeros((bd, POOL_HIST - n_pool, bw), F32), state_pool[l]], axis=1)
    cur = jnp.concatenate([us3, jnp.zeros((bd, SUBLANES - n_new, bw), F32)], axis=1)
    o_bs = _pool_dec(hist, cur, n_pool, w_pool_b, pool_scale[l])[:, :n_new].reshape(ts, bw)
    sub = 8
    xhd = d // X_HEADS
    x2s, hs, lgs = _mix_xattn(xs, o_as, o_bs, cache_mem_k[l].reshape(bd * n_mem, X_HEADS, xhd),
                              cache_mem_v[l].reshape(bd * n_mem, X_HEADS, xhd), wts, groups=bd // sub,
                              rows_per_group=sub * n_new, tm=sub * n_new, rows_per_sub=n_new, keys_per_sub=n_mem)

    assert ts <= TILE_TOKENS and w_gate_up.shape[1] == N_EXPERTS
    pad_s = lambda a: jnp.pad(a, ((0, TILE_TOKENS - ts), (0, 0)))
    gates, ls, tabs, n_rows = _route_tiles(jnp.concatenate([lgp, lgs], axis=0))
    x_sorted = _dispatch(hp, pad_s(hs), ls.T, tabs, n_rows)
    y_sorted = _expert_ffn(x_sorted, tabs["blk_start"], tabs["nblk"], w_gate_up[l], b_gate_up[l], w_down[l],
                           b_down[l])
    y_prompt, y_sample = _combine(y_sorted, x2p, pad_s(x2s), ls, gates, tabs, g_final)
    y_prompt = y_prompt.reshape(batch, seq, d)
    y_sample = y_sample[:ts].reshape(bd, n_new, d)

    from_tp = lambda c: jnp.transpose(c.reshape(batch, A_HEADS, A_HEAD_DIM, seq), (0, 3, 1, 2))[None]
    k_win_prompt, v_win_prompt = from_tp(k_t), from_tp(v_t)
    pool_prompt = u.reshape(batch, seq, bw)[:, seq - n_pool:][None]
    mem_k_prompt = mk.reshape(1, batch, n_mem, X_HEADS, xhd)
    mem_v_prompt = mv.reshape(1, batch, n_mem, X_HEADS, xhd)
    k_win_sample, v_win_sample = from_t(ck_t), from_t(cv_t)
    pool_sample = jnp.concatenate([state_pool[l][:, n_new:], us3], axis=1)[None]
    return (y_prompt, y_sample, k_win_prompt, v_win_prompt, pool_prompt, mem_k_prompt, mem_v_prompt,
            k_win_sample, v_win_sample, pool_sample)
```

```python
import functools
import math

import numpy as np
import jax
import jax.numpy as jnp
from jax import lax
from jax.experimental import pallas as pl
from jax.experimental.pallas import tpu as pltpu

F32 = jnp.float32
BF16 = jnp.bfloat16

LANES = 128
SUBLANES = 8
VMEM_LIMIT_BYTES = 56 * 1024 * 1024

A_HEADS = 8
A_HEAD_DIM = 64
A_WIDTH = A_HEADS * A_HEAD_DIM
DILATED = ((128, 1), (512, 4), (2048, 16))
QB = 128
POOL_WINDOWS = (2, 4, 8, 16)
POOL_HIST = 16
X_HEADS = 4
N_EXPERTS = 32
TOP_K = 4
SWIGLU_LIMIT = 7.0
SWIGLU_ALPHA = 1.702
N_BUCKETS = 32
RMS_EPS = 1e-6
NEG = -1e30

MOE_BM = 256


def _cparams(sem):
    return pltpu.CompilerParams(dimension_semantics=sem, vmem_limit_bytes=VMEM_LIMIT_BYTES)


def _rms(x, g):
    return x * lax.rsqrt(jnp.mean(x * x, axis=-1, keepdims=True) + RMS_EPS) * g


def _norm_proj_kernel(x_ref, g_ref, w_ref, *o_refs, widths, scales):
    hb = _rms(x_ref[...], g_ref[...]).astype(BF16)
    off = 0
    for o_ref, width, scale in zip(o_refs, widths, scales):
        p = jnp.dot(hb, w_ref[:, off:off + width], preferred_element_type=F32)
        o_ref[...] = p if scale == 1.0 else p * scale
        off += width


def _norm_proj(x, g, w_bf16, widths, scales, tm):
    t, d = x.shape
    n = w_bf16.shape[1]
    assert sum(widths) == n and t % tm == 0
    return pl.pallas_call(
        functools.partial(_norm_proj_kernel, widths=widths, scales=scales),
        grid=(t // tm,),
        in_specs=[
            pl.BlockSpec((tm, d), lambda i: (i, 0)),
            pl.BlockSpec((1, d), lambda i: (0, 0)),
            pl.BlockSpec((d, n), lambda i: (0, 0)),
        ],
        out_specs=[pl.BlockSpec((tm, wd), lambda i: (i, 0)) for wd in widths],
        out_shape=[jax.ShapeDtypeStruct((t, wd), F32) for wd in widths],
        compiler_params=_cparams(("arbitrary",)),
        name="norm_proj",
    )(x, g.reshape(1, d), w_bf16)


def _mem_kv_kernel(x_ref, g_ref, w_ref, k_ref, v_ref, kb_ref, vb_ref, *, heads):
    hb = _rms(x_ref[...], g_ref[...]).astype(BF16)
    d = x_ref.shape[1]
    hd = d // heads
    for j, (o_ref, ob_ref) in enumerate(((k_ref, kb_ref), (v_ref, vb_ref))):
        p = jnp.dot(hb, w_ref[:, j * d:(j + 1) * d], preferred_element_type=F32)
        ob_ref[...] = p.astype(BF16)
        for h in range(heads):
            o_ref[:, h, :] = p[:, h * hd:(h + 1) * hd]


def _mem_kv(x, g, w_bf16, heads, tm):
    t, d = x.shape
    assert w_bf16.shape == (d, 2 * d) and t % tm == 0
    hd = d // heads
    o3 = pl.BlockSpec((tm, heads, hd), lambda i: (i, 0, 0))
    o2 = pl.BlockSpec((tm, d), lambda i: (i, 0))
    return pl.pallas_call(
        functools.partial(_mem_kv_kernel, heads=heads),
        grid=(t // tm,),
        in_specs=[
            pl.BlockSpec((tm, d), lambda i: (i, 0)),
            pl.BlockSpec((1, d), lambda i: (0, 0)),
            pl.BlockSpec((d, 2 * d), lambda i: (0, 0)),
        ],
        out_specs=[o3, o3, o2, o2],
        out_shape=[jax.ShapeDtypeStruct((t, heads, hd), F32)] * 2 + [jax.ShapeDtypeStruct((t, d), BF16)] * 2,
        compiler_params=_cparams(("arbitrary",)),
        name="mem_kv",
    )(x, g.reshape(1, d), w_bf16)


def _in_proj_kernel(x_ref, g_ref, w_ref, wt_ref, *o_refs, widths, scales, n_t):
    hb = _rms(x_ref[...], g_ref[...]).astype(BF16)
    off = 0
    for o_ref, width, scale in zip(o_refs, widths, scales):
        p = jnp.dot(hb, w_ref[:, off:off + width], preferred_element_type=F32)
        o_ref[...] = p if scale == 1.0 else p * scale
        off += width
    for j in range(n_t):
        o_refs[len(widths) + j][0] = lax.dot_general(wt_ref[j], hb, (((1,), (1,)), ((), ())),
                                                    preferred_element_type=F32)


def _in_proj(x, g, w_bf16, wt_bf16, widths, scales, batch, seq, tm):
    t, d = x.shape
    n = w_bf16.shape[1]
    n_t, wt_width, _ = wt_bf16.shape
    nt = seq // tm
    assert sum(widths) == n and seq % tm == 0 and t == batch * seq
    return pl.pallas_call(
        functools.partial(_in_proj_kernel, widths=widths, scales=scales, n_t=n_t),
        grid=(t // tm,),
        in_specs=[
            pl.BlockSpec((tm, d), lambda i: (i, 0)),
            pl.BlockSpec((1, d), lambda i: (0, 0)),
            pl.BlockSpec((d, n), lambda i: (0, 0)),
            pl.BlockSpec(wt_bf16.shape, lambda i: (0, 0, 0)),
        ],
        out_specs=[pl.BlockSpec((tm, wd), lambda i: (i, 0)) for wd in widths]
        + [pl.BlockSpec((1, wt_width, tm), lambda i: (i // nt, 0, i % nt))] * n_t,
        out_shape=[jax.ShapeDtypeStruct((t, wd), F32) for wd in widths]
        + [jax.ShapeDtypeStruct((batch, wt_width, seq), F32)] * n_t,
        compiler_params=_cparams(("arbitrary",)),
        name="in_proj",
    )(x, g.reshape(1, d), w_bf16, wt_bf16)


def _t5_bucket_np(n, max_dist):
    max_exact = N_BUCKETS // 2
    nf = np.maximum(n, 1).astype(np.float32)
    large = max_exact + (
        np.log(nf / np.float32(max_exact)) / np.float32(math.log(max_dist / max_exact))
        * np.float32(N_BUCKETS - max_exact)
    ).astype(np.int32)
    return np.where(n < max_exact, n, np.minimum(large, N_BUCKETS - 1))


def _band_bias(rel_bias):
    max_dist = max(w for w, _ in DILATED)
    qi = np.arange(QB)[:, None]
    ki = np.arange(2 * QB)[None, :]
    j = qi + QB - ki
    tabs = []
    for window, dil in DILATED:
        steps = window // dil
        in_band = (j >= 0) & (j <= steps)
        bucket = _t5_bucket_np(np.clip(j, 0, steps) * dil, max_dist)
        onehot = (bucket[..., None] == np.arange(N_BUCKETS)).astype(np.float32)
        b = jnp.einsum("qkb,bh->hqk", onehot, rel_bias.astype(F32), precision=lax.Precision.HIGHEST)
        tabs.append(jnp.where(in_band[None], b, NEG))
    return jnp.stack(tabs)


def _decode_bias(rel_bias, n_hist, n_new):
    max_dist = max(w for w, _ in DILATED)
    t = np.arange(n_new)[:, None]
    tabs_c, tabs_n = [], []
    for window, dil in DILATED:
        out = []
        for dist in (n_hist + t - np.arange(n_hist)[None, :], t - np.arange(n_new)[None, :]):
            ok = (dist >= 0) & (dist % dil == 0) & (dist <= window)
            onehot = (_t5_bucket_np(np.clip(dist, 0, window), max_dist)[..., None] == np.arange(N_BUCKETS))
            b = jnp.einsum("tpb,bh->htp", onehot.astype(np.float32), rel_bias.astype(F32),
                           precision=lax.Precision.HIGHEST)
            out.append(jnp.where(ok[None], b, NEG))
        tabs_c.append(out[0])
        tabs_n.append(out[1])
    return jnp.stack(tabs_c), jnp.stack(tabs_n)


def _decode_attn_kernel(q_ref, kn_ref, vn_ref, knt_ref, vnt_ref, ck_ref, cv_ref, bc_ref, bn_ref,
                        o_ref, ok_ref, ov_ref, *, n_new, heads):
    n_hist = ck_ref.shape[2]
    hd = A_HEAD_DIM
    outs = []
    for h in range(heads):
        rows = slice(h * hd, (h + 1) * hd)
        q = q_ref[0, :, rows].astype(BF16)
        kt = ck_ref[0, rows, :].astype(BF16)
        vt = cv_ref[0, rows, :].astype(BF16)
        lc = jnp.dot(q, kt, preferred_element_type=F32)
        ln = lax.dot_general(q, kn_ref[0, :, rows].astype(BF16), (((1,), (1,)), ((), ())),
                             preferred_element_type=F32)
        vn = vn_ref[0, :, rows].astype(BF16)
        o_br, lse_br = [], []
        for br in range(len(DILATED)):
            bl = lc + bc_ref[br, h]
            bln = ln + bn_ref[br, h]
            m = jnp.maximum(jnp.max(bl, axis=-1, keepdims=True), jnp.max(bln, axis=-1, keepdims=True))
            p = jnp.exp(bl - m)
            pn = jnp.exp(bln - m)
            s = jnp.sum(p, axis=-1, keepdims=True) + jnp.sum(pn, axis=-1, keepdims=True)
            o = lax.dot_general(p.astype(BF16), vt, (((1,), (1,)), ((), ())), preferred_element_type=F32)
            o = o + jnp.dot(pn.astype(BF16), vn, preferred_element_type=F32)
            o_br.append(o / s)
            lse_br.append(m + jnp.log(s))
        m = jnp.maximum(jnp.maximum(lse_br[0], lse_br[1]), lse_br[2])
        es = [jnp.exp(l - m) for l in lse_br]
        outs.append((es[0] * o_br[0] + es[1] * o_br[1] + es[2] * o_br[2]) / (es[0] + es[1] + es[2]))
    o_ref[0] = jnp.concatenate(outs, axis=-1)

    lane = lax.broadcasted_iota(jnp.int32, (ck_ref.shape[1], LANES), 1)
    for c_ref, nt_ref, dst in ((ck_ref, knt_ref, ok_ref), (cv_ref, vnt_ref, ov_ref)):
        shifted = pltpu.roll(c_ref[0], n_hist - n_new, axis=1)
        tail = shifted[:, n_hist - LANES:]
        for j in range(n_new):
            tail = jnp.where(lane == LANES - n_new + j, nt_ref[0, :, j:j + 1], tail)
        dst[0, :, :n_hist - LANES] = shifted[:, :n_hist - LANES]
        dst[0, :, n_hist - LANES:] = tail


def _decode_attn(q, k_new, v_new, ck_t, cv_t, bias_c, bias_n, heads_per_step):
    bd, n_new, aw = q.shape
    n_hist = ck_t.shape[2]
    gw = heads_per_step * A_HEAD_DIM
    ng = aw // gw
    row = pl.BlockSpec((1, n_new, gw), lambda b, g: (b, 0, g))
    col = pl.BlockSpec((1, gw, n_new), lambda b, g: (b, g, 0))
    cache = pl.BlockSpec((1, gw, n_hist), lambda b, g: (b, g, 0))
    nbr = len(DILATED)
    return pl.pallas_call(
        functools.partial(_decode_attn_kernel, n_new=n_new, heads=heads_per_step),
        grid=(bd, ng),
        in_specs=[row, row, row, col, col, cache, cache,
                  pl.BlockSpec((nbr, heads_per_step, n_new, n_hist), lambda b, g: (0, g, 0, 0)),
                  pl.BlockSpec((nbr, heads_per_step, n_new, n_new), lambda b, g: (0, g, 0, 0))],
        out_specs=[row, cache, cache],
        out_shape=[jax.ShapeDtypeStruct(q.shape, F32), jax.ShapeDtypeStruct(ck_t.shape, F32),
                   jax.ShapeDtypeStruct(cv_t.shape, F32)],
        compiler_params=_cparams(("arbitrary", "arbitrary")),
        name="decode_attn",
    )(q, k_new, v_new, jnp.swapaxes(k_new, 1, 2), jnp.swapaxes(v_new, 1, 2), ck_t, cv_t, bias_c, bias_n)


def _dil_attn_kernel(q_ref, k_ref, v_ref, bias_ref, o_ref, obr_ref, lbr_ref, *, seq):
    lane = lax.broadcasted_iota(jnp.int32, (QB, LANES), 1)
    head0 = lane < A_HEAD_DIM

    def rows(ref, start, n, dil):
        if dil == 1:
            return ref[pl.ds(start, n), :]
        return ref[pl.ds(start, n, stride=dil), :]

    def block(br, dil, qstart, kstart, nk):
        qs = rows(q_ref, qstart, QB, dil)
        ks = rows(k_ref, kstart, nk, dil).astype(BF16)
        vs = rows(v_ref, kstart, nk, dil).astype(BF16)
        outs, lses = [], []
        for hh in range(2):
            keep = head0 if hh == 0 else jnp.logical_not(head0)
            qm = jnp.where(keep, qs, 0.0).astype(BF16)
            logits = lax.dot_general(qm, ks, (((1,), (1,)), ((), ())), preferred_element_type=F32)
            logits = logits + bias_ref[br, hh, :, 2 * QB - nk:]
            m = jnp.max(logits, axis=-1, keepdims=True)
            p = jnp.exp(logits - m)
            s = jnp.sum(p, axis=-1, keepdims=True)
            o = jnp.dot(p.astype(BF16), vs, preferred_element_type=F32)
            outs.append(o / s)
            lses.append(jnp.broadcast_to(m + jnp.log(s), (QB, LANES)))
        o = jnp.where(head0, outs[0], outs[1])
        lse = jnp.where(head0, lses[0], lses[1])
        if dil == 1:
            obr_ref[br, pl.ds(qstart, QB), :] = o
            lbr_ref[br, pl.ds(qstart, QB), :] = lse
        else:
            obr_ref[br, pl.ds(qstart, QB, stride=dil), :] = o
            lbr_ref[br, pl.ds(qstart, QB, stride=dil), :] = lse

    for br, (window, dil) in enumerate(DILATED):
        assert window // dil == QB
        nblk = seq // (dil * QB)
        for r in range(dil):
            block(br, dil, r, r, QB)
            for i in range(1, nblk):
                qstart = r + dil * QB * i
                block(br, dil, qstart, qstart - dil * QB, 2 * QB)

    l0, l1, l2 = lbr_ref[0], lbr_ref[1], lbr_ref[2]
    m = jnp.maximum(jnp.maximum(l0, l1), l2)
    e0, e1, e2 = jnp.exp(l0 - m), jnp.exp(l1 - m), jnp.exp(l2 - m)
    acc = e0 * obr_ref[0] + e1 * obr_ref[1] + e2 * obr_ref[2]
    o_ref[...] = acc / (e0 + e1 + e2)


def _dil_attn(q, k, v, band_bias, batch, seq):
    t, aw = q.shape
    npair = aw // LANES
    bias = band_bias.reshape(len(DILATED), npair, 2, QB, 2 * QB)
    spec = pl.BlockSpec((seq, LANES), lambda b, hp: (b, hp))
    return pl.pallas_call(
        functools.partial(_dil_attn_kernel, seq=seq),
        grid=(batch, npair),
        in_specs=[
            spec, spec, spec,
            pl.BlockSpec((len(DILATED), None, 2, QB, 2 * QB), lambda b, hp: (0, hp, 0, 0, 0)),
        ],
        out_specs=spec,
        out_shape=jax.ShapeDtypeStruct((t, aw), F32),
        scratch_shapes=[
            pltpu.VMEM((len(DILATED), seq, LANES), F32),
            pltpu.VMEM((len(DILATED), seq, LANES), F32),
        ],
        compiler_params=_cparams(("arbitrary", "arbitrary")),
        name="dil_attn",
    )(q, k, v, bias)


def _pool_kernel(hist_ref, cur_ref, w_ref, scale_ref, o_ref, *, pos0_of_tile, tm):
    nb = cur_ref.shape[0]
    gdim = w_ref.shape[1]
    i = pl.program_id(1)
    pos0 = pos0_of_tile(i)
    have_hist = jnp.where(pos0 > 0, 1.0, 0.0).astype(F32)
    t = lax.broadcasted_iota(jnp.int32, (1, tm, 1), 1)
    for g, w in enumerate(POOL_WINDOWS):
        sl = slice(g * gdim, (g + 1) * gdim)
        cur = cur_ref[:, :, sl]
        ext = jnp.concatenate([hist_ref[:, :, sl] * have_hist, cur], axis=1)
        acc, span = ext, 1
        while span < w:
            n = acc.shape[1]
            acc = acc[:, span:n] + acc[:, 0:n - span]
            span *= 2
        wsum = acc[:, POOL_HIST + 1 - w:POOL_HIST + 1 - w + tm]
        cnt = jnp.minimum(pos0 + t + 1, w).astype(F32)
        d = (wsum / cnt - cur).astype(BF16).reshape(nb * tm, gdim)
        y = jnp.dot(d, w_ref[g], preferred_element_type=F32) * scale_ref[:, sl]
        o_ref[:, :, sl] = y.reshape(nb, tm, gdim)


def _pool_prompt(u, w_pool_bf16, pool_scale, batch, seq, tm):
    t, bw = u.shape
    nt = seq // tm
    hb = tm // POOL_HIST
    u3 = u.reshape(1, t, bw)
    out = pl.pallas_call(
        functools.partial(_pool_kernel, pos0_of_tile=lambda i: i * tm, tm=tm),
        grid=(batch, nt),
        in_specs=[
            pl.BlockSpec((1, POOL_HIST, bw), lambda b, i: (0, jnp.maximum((b * nt + i) * hb - 1, 0), 0)),
            pl.BlockSpec((1, tm, bw), lambda b, i: (0, b * nt + i, 0)),
            pl.BlockSpec(w_pool_bf16.shape, lambda b, i: (0, 0, 0)),
            pl.BlockSpec((1, bw), lambda b, i: (0, 0)),
        ],
        out_specs=pl.BlockSpec((1, tm, bw), lambda b, i: (0, b * nt + i, 0)),
        out_shape=jax.ShapeDtypeStruct((1, t, bw), F32),
        compiler_params=_cparams(("arbitrary", "arbitrary")),
        name="pool_prompt",
    )(u3, u3, w_pool_bf16, pool_scale.reshape(1, bw))
    return out.reshape(t, bw)


def _pool_dec(hist, cur, n_prev, w_pool_bf16, pool_scale):
    bd, tm, bw = cur.shape
    return pl.pallas_call(
        functools.partial(_pool_kernel, pos0_of_tile=lambda i: n_prev, tm=tm),
        grid=(1, 1),
        in_specs=[
            pl.BlockSpec(hist.shape, lambda b, i: (0, 0, 0)),
            pl.BlockSpec(cur.shape, lambda b, i: (0, 0, 0)),
            pl.BlockSpec(w_pool_bf16.shape, lambda b, i: (0, 0, 0)),
            pl.BlockSpec((1, bw), lambda b, i: (0, 0)),
        ],
        out_specs=pl.BlockSpec(cur.shape, lambda b, i: (0, 0, 0)),
        out_shape=jax.ShapeDtypeStruct(cur.shape, F32),
        compiler_params=_cparams(("arbitrary", "arbitrary")),
        name="pool_dec",
    )(hist, cur, w_pool_bf16, pool_scale.reshape(1, bw))


def _mix_xattn_kernel(x_ref, oa_ref, ob_ref, mk_ref, mv_ref, woa_ref, wob_ref, gx_ref, wq_ref, wo_ref,
                      gff_ref, wr_ref, br_ref, x2_ref, h_ref, lg_ref, *, rows_per_sub, keys_per_sub, logits_t):
    tm = x_ref.shape[0]
    nkv = mk_ref.shape[0]
    xh = wq_ref.shape[1] // X_HEADS
    x = x_ref[...]
    x = x + jnp.dot(oa_ref[...].astype(BF16), woa_ref[...], preferred_element_type=F32)
    x = x + jnp.dot(ob_ref[...].astype(BF16), wob_ref[...], preferred_element_type=F32)
    hq = _rms(x, gx_ref[...]).astype(BF16)
    if rows_per_sub is not None:
        rsub = (pl.program_id(1) * tm + lax.broadcasted_iota(jnp.int32, (tm, nkv), 0)) // rows_per_sub
        ksub = lax.broadcasted_iota(jnp.int32, (tm, nkv), 1) // keys_per_sub
        same = rsub == ksub
    heads = []
    for h in range(X_HEADS):
        sl = slice(h * xh, (h + 1) * xh)
        q = jnp.dot(hq, wq_ref[:, sl], preferred_element_type=F32) * (xh ** -0.5)
        if len(mk_ref.shape) == 3:
            kh, vh = mk_ref[:, h, :].astype(BF16), mv_ref[:, h, :].astype(BF16)
        else:
            kh, vh = mk_ref[:, sl], mv_ref[:, sl]
        logits = lax.dot_general(q.astype(BF16), kh, (((1,), (1,)), ((), ())), preferred_element_type=F32)
        if rows_per_sub is not None:
            logits = jnp.where(same, logits, NEG)
        m = jnp.max(logits, axis=-1, keepdims=True)
        p = jnp.exp(logits - m)
        s = jnp.sum(p, axis=-1, keepdims=True)
        o = jnp.dot(p.astype(BF16), vh, preferred_element_type=F32) / s
        heads.append(o.astype(BF16))
    x = x + jnp.dot(jnp.concatenate(heads, axis=-1), wo_ref[...], preferred_element_type=F32)
    x2_ref[...] = x
    hf = _rms(x, gff_ref[...])
    h_ref[...] = hf
    if logits_t:
        lg_ref[...] = lax.dot_general(wr_ref[...].astype(BF16), hf.astype(BF16), (((1,), (1,)), ((), ())),
                                      preferred_element_type=F32) + br_ref[...]
    else:
        lg_ref[...] = jnp.dot(hf.astype(BF16), wr_ref[...].astype(BF16), preferred_element_type=F32) + br_ref[...]


def _mix_xattn(x, oa, ob, mk, mv, w, groups, rows_per_group, tm, rows_per_sub, keys_per_sub, logits_t):
    t, d = x.shape
    nt = rows_per_group // tm
    nkv = mk.shape[0] // groups
    ne = w["w_router"].shape[1]
    row = lambda width: pl.BlockSpec((tm, width), lambda g, i: (g * nt + i, 0))
    const = lambda a: pl.BlockSpec(a.shape, lambda g, i: (0,) * a.ndim)
    kv = pl.BlockSpec((nkv,) + mk.shape[1:], lambda g, i: (g,) + (0,) * (mk.ndim - 1))
    w_r, b_r = (w["w_router"].T, w["b_router"].T) if logits_t else (w["w_router"], w["b_router"])
    consts = [w["w_out_a"], w["w_out_b"], w["g_x"], w["w_xq"], w["w_xo"], w["g_ff"], w_r, b_r]
    lg_spec = pl.BlockSpec((ne, tm), lambda g, i: (0, g * nt + i)) if logits_t else row(ne)
    return pl.pallas_call(
        functools.partial(_mix_xattn_kernel, rows_per_sub=rows_per_sub, keys_per_sub=keys_per_sub,
                          logits_t=logits_t),
        grid=(groups, nt),
        in_specs=[row(d), row(oa.shape[1]), row(ob.shape[1]), kv, kv] + [const(a) for a in consts],
        out_specs=[row(d), row(d), lg_spec],
        out_shape=[jax.ShapeDtypeStruct((t, d), F32), jax.ShapeDtypeStruct((t, d), F32),
                   jax.ShapeDtypeStruct((ne, t) if logits_t else (t, ne), F32)],
        compiler_params=_cparams(("arbitrary", "arbitrary")),
        name="mix_xattn",
    )(x, oa, ob, mk, mv, *consts)


TILE_TOKENS = 256
TILE_ROWS = TOP_K * TILE_TOKENS + N_EXPERTS * (SUBLANES - 1)
assert TILE_ROWS % SUBLANES == 0
FFN_PAIR = 2


def _bits(lo, hi):
    return tuple(lo << j for j in range((hi // lo).bit_length()))


STRIP_LEN_BITS = _bits(SUBLANES, TILE_TOKENS)
PAD_LEN_BITS = _bits(SUBLANES, MOE_BM - SUBLANES)
TILE_LEN_BITS = _bits(SUBLANES, TILE_ROWS)


def _strip_dmas(n, bits, make_copy, wait=False):
    for bit in bits:
        @pl.when((n & bit) != 0)
        def _(bit=bit):
            c = make_copy(n & (bit - 1), bit)
            if wait:
                c.wait()
            else:
                c.start()


def _tile_strips(i, slot, d_ref, n_ref, o_ref, local, hbm, sem, to_hbm):
    for e in range(N_EXPERTS):
        l0 = pl.multiple_of(o_ref[i * N_EXPERTS + e], SUBLANES)
        g0 = pl.multiple_of(d_ref[i * N_EXPERTS + e], SUBLANES)

        def make_copy(off, size, l0=l0, g0=g0):
            lo = local.at[slot, pl.ds(pl.multiple_of(l0 + off, SUBLANES), size)]
            gl = hbm.at[pl.ds(pl.multiple_of(g0 + off, SUBLANES), size)]
            src, dst = (lo, gl) if to_hbm else (gl, lo)
            return pltpu.make_async_copy(src, dst, sem.at[slot])
        _strip_dmas(n_ref[i * N_EXPERTS + e], STRIP_LEN_BITS, make_copy)


def _tile_wait(rows, slot, local, hbm, sem):
    _strip_dmas(rows, TILE_LEN_BITS, lambda off, size: pltpu.make_async_copy(
        hbm.at[pl.ds(0, size)], local.at[slot, pl.ds(0, size)], sem.at[slot]), wait=True)


def _dispatch_kernel(d_ref, n_ref, o_ref, r_ref, ps_ref, pn_ref, lst_ref, hp_ref, hs_ref, xs_hbm, buf, zbuf, sem,
                     psem, *, n_tiles, n_prompt_tiles):
    i = pl.program_id(0)
    slot = i % 2

    def pad_strips(wait):
        for e in range(N_EXPERTS):
            start = ps_ref[e]
            _strip_dmas(pn_ref[e], PAD_LEN_BITS, lambda off, size: pltpu.make_async_copy(
                zbuf.at[pl.ds(0, size)], xs_hbm.at[pl.ds(pl.multiple_of(start + off, SUBLANES), size)],
                psem.at[0]), wait)
        chunk = zbuf.shape[0]
        used_rows = ps_ref[N_EXPERTS - 1] + pn_ref[N_EXPERTS - 1]

        def body(c, carry):
            cp = pltpu.make_async_copy(zbuf, xs_hbm.at[pl.ds(pl.multiple_of(c * chunk, chunk), chunk)], psem.at[0])
            if wait:
                cp.wait()
            else:
                cp.start()
            return carry
        lax.fori_loop(used_rows // chunk, xs_hbm.shape[0] // chunk, body, 0)

    @pl.when(i >= 2)
    def _():
        _tile_wait(r_ref[i - 2], slot, buf, xs_hbm, sem)

    @pl.when(i == 0)
    def _():
        zbuf[...] = jnp.zeros_like(zbuf)
        pad_strips(False)

    srow = lax.broadcasted_iota(jnp.int32, (TILE_ROWS, TILE_TOKENS), 0)
    place = srow == lst_ref[0:1, :]
    for k in range(1, TOP_K):
        place = jnp.logical_or(place, srow == lst_ref[k:k + 1, :])
    x = jnp.where(i < n_prompt_tiles, hp_ref[...], hs_ref[...]).astype(BF16)
    buf[slot] = jnp.dot(jnp.where(place, 1.0, 0.0).astype(BF16), x, preferred_element_type=F32)
    _tile_strips(i, slot, d_ref, n_ref, o_ref, buf, xs_hbm, sem, to_hbm=True)

    @pl.when(i == n_tiles - 1)
    def _():
        _tile_wait(r_ref[i], slot, buf, xs_hbm, sem)
        if n_tiles > 1:
            _tile_wait(r_ref[jnp.maximum(i - 1, 0)], 1 - slot, buf, xs_hbm, sem)
        pad_strips(True)


def _dispatch(hp, hs, ls_t, tabs, n_rows):
    tp, d = hp.shape
    npt = tp // TILE_TOKENS
    assert tp % TILE_TOKENS == 0 and hs.shape[0] == TILE_TOKENS
    nt = npt + 1
    grid_spec = pltpu.PrefetchScalarGridSpec(
        num_scalar_prefetch=6,
        grid=(nt,),
        in_specs=[
            pl.BlockSpec((TOP_K, TILE_TOKENS), lambda i, *_: (0, i)),
            pl.BlockSpec((TILE_TOKENS, d), lambda i, *_: (jnp.minimum(i, npt - 1), 0)),
            pl.BlockSpec((TILE_TOKENS, d), lambda i, *_: (0, 0)),
        ],
        out_specs=pl.BlockSpec(memory_space=pl.ANY),
        scratch_shapes=[
            pltpu.VMEM((2, TILE_ROWS, d), F32),
            pltpu.VMEM((PAD_LEN_BITS[-1], d), F32),
            pltpu.SemaphoreType.DMA((2,)),
            pltpu.SemaphoreType.DMA((1,)),
        ],
    )
    return pl.pallas_call(
        functools.partial(_dispatch_kernel, n_tiles=nt, n_prompt_tiles=npt),
        grid_spec=grid_spec,
        out_shape=jax.ShapeDtypeStruct((n_rows, d), F32),
        compiler_params=_cparams(("arbitrary",)),
        name="moe_dispatch",
    )(tabs["d"], tabs["n"], tabs["o"], tabs["rows"], tabs["pad_start"], tabs["pad_len"], ls_t, hp, hs)


def _expert_ffn_kernel(b0_ref, nb_ref, x_hbm, wgu_hbm, bgu_ref, wd_hbm, bd_ref, y_hbm, xbuf, ybuf, xtail, ytail,
                       xsem, ysem, tsem, wgu_st, wd_st, wsem, wgu_bf, wd_bf, *, n_blocks):
    e = pl.program_id(0)
    b0 = b0_ref[e]
    n = nb_ref[e]
    n2 = n // FFN_PAIR
    odd = n % FFN_PAIR
    dff = wd_st.shape[0]
    big = FFN_PAIR * MOE_BM

    def w_copies(ex):
        return (pltpu.make_async_copy(wgu_hbm.at[ex], wgu_st, wsem.at[0]),
                pltpu.make_async_copy(wd_hbm.at[ex], wd_st, wsem.at[1]))

    def rows(blk, size):
        return pl.ds(pl.multiple_of((b0 + blk) * MOE_BM, MOE_BM), size)

    def x_copy(kk, slot):
        return pltpu.make_async_copy(x_hbm.at[rows(FFN_PAIR * kk, big)], xbuf.at[slot], xsem.at[slot])

    def y_copy(kk, slot):
        return pltpu.make_async_copy(ybuf.at[slot], y_hbm.at[rows(FFN_PAIR * kk, big)], ysem.at[slot])

    def xt_copy():
        return pltpu.make_async_copy(x_hbm.at[rows(n - 1, MOE_BM)], xtail, tsem.at[0])

    def yt_copy():
        return pltpu.make_async_copy(ytail, y_hbm.at[rows(n - 1, MOE_BM)], tsem.at[1])

    def ffn(x):
        hu = jnp.dot(x.astype(BF16), wgu_bf[...], preferred_element_type=F32) + bgu_ref[0]
        g = jnp.minimum(hu[:, :dff], SWIGLU_LIMIT)
        u = jnp.clip(hu[:, dff:], -SWIGLU_LIMIT, SWIGLU_LIMIT)
        a = g * jax.nn.sigmoid(SWIGLU_ALPHA * g) * (u + 1.0)
        return jnp.dot(a.astype(BF16), wd_bf[...], preferred_element_type=F32) + bd_ref[0]

    @pl.when(e == 0)
    def _():
        for c in w_copies(0):
            c.start()

    @pl.when(jnp.logical_and(n > 0, odd == 1))
    def _():
        xt_copy().start()

    @pl.when(n2 > 0)
    def _():
        x_copy(0, 0).start()

    @pl.when(n2 > 1)
    def _():
        x_copy(1, 1).start()

    for c in w_copies(e):
        c.wait()

    @pl.when(n > 0)
    def _():
        wgu_bf[...] = wgu_st[...].astype(BF16)
        wd_bf[...] = wd_st[...].astype(BF16)

    @pl.when(e + 1 < N_EXPERTS)
    def _():
        for c in w_copies(e + 1):
            c.start()

    @pl.when(n > 0)
    def _():
        def body(kk, carry):
            slot = kk % 2

            @pl.when(jnp.logical_and(kk >= 1, kk + 1 < n2))
            def _():
                x_copy(kk + 1, 1 - slot).start()

            x_copy(kk, slot).wait()

            @pl.when(kk >= 2)
            def _():
                y_copy(kk - 2, slot).wait()

            ybuf[slot] = ffn(xbuf[slot])
            y_copy(kk, slot).start()
            return carry
        lax.fori_loop(0, n2, body, 0)

        @pl.when(odd == 1)
        def _():
            xt_copy().wait()
            ytail[...] = ffn(xtail[...])
            yt_copy().start()

        @pl.when(n2 >= 2)
        def _():
            y_copy(n2 - 2, n2 % 2).wait()

        @pl.when(n2 >= 1)
        def _():
            y_copy(n2 - 1, (n2 - 1) % 2).wait()

        @pl.when(odd == 1)
        def _():
            yt_copy().wait()

    @pl.when(e == N_EXPERTS - 1)
    def _():
        ytail[...] = jnp.zeros_like(ytail)

        def fill(wait):
            def body(k, carry):
                cp = pltpu.make_async_copy(ytail, y_hbm.at[pl.ds(pl.multiple_of(k * MOE_BM, MOE_BM), MOE_BM)],
                                           tsem.at[1])
                if wait:
                    cp.wait()
                else:
                    cp.start()
                return carry
            lax.fori_loop(b0 + n, n_blocks, body, 0)
        fill(False)
        fill(True)


def _expert_ffn(x_sorted, blk_start, nblk, w_gate_up, b_gate_up, w_down, b_down):
    n_rows, d = x_sorted.shape
    ne, _, dff2 = w_gate_up.shape
    dff = dff2 // 2
    any_spec = pl.BlockSpec(memory_space=pl.ANY)
    big = FFN_PAIR * MOE_BM
    grid_spec = pltpu.PrefetchScalarGridSpec(
        num_scalar_prefetch=2,
        grid=(ne,),
        in_specs=[
            any_spec,
            any_spec,
            pl.BlockSpec((1, 1, dff2), lambda e, *_: (e, 0, 0)),
            any_spec,
            pl.BlockSpec((1, 1, d), lambda e, *_: (e, 0, 0)),
        ],
        out_specs=any_spec,
        scratch_shapes=[
            pltpu.VMEM((2, big, d), F32), pltpu.VMEM((2, big, d), F32),
            pltpu.VMEM((MOE_BM, d), F32), pltpu.VMEM((MOE_BM, d), F32),
            pltpu.SemaphoreType.DMA((2,)), pltpu.SemaphoreType.DMA((2,)), pltpu.SemaphoreType.DMA((2,)),
            pltpu.VMEM((d, dff2), F32), pltpu.VMEM((dff, d), F32), pltpu.SemaphoreType.DMA((2,)),
            pltpu.VMEM((d, dff2), BF16), pltpu.VMEM((dff, d), BF16),
        ],
    )
    return pl.pallas_call(
        functools.partial(_expert_ffn_kernel, n_blocks=n_rows // MOE_BM),
        grid_spec=grid_spec,
        out_shape=jax.ShapeDtypeStruct((n_rows, d), F32),
        compiler_params=_cparams(("arbitrary",)),
        name="expert_ffn",
    )(blk_start, nblk, x_sorted, w_gate_up, b_gate_up.reshape(ne, 1, dff2), w_down, b_down.reshape(ne, 1, d))


def _combine_kernel(d_ref, n_ref, o_ref, r_ref, ls_ref, gate_ref, xp_ref, xs_ref, g_ref, y_hbm, yp_ref, ys_ref,
                    buf, sem, *, n_tiles, n_prompt_tiles):
    j = pl.program_id(0)

    @pl.when(j == 0)
    def _():
        buf[...] = jnp.zeros_like(buf)

    @pl.when(j < n_tiles)
    def _():
        _tile_strips(j, j % 2, d_ref, n_ref, o_ref, buf, y_hbm, sem, to_hbm=False)

    @pl.when(j >= 1)
    def _():
        slot = (j - 1) % 2
        _tile_wait(r_ref[j - 1], slot, buf, y_hbm, sem)
        r_hi = buf[slot].astype(BF16)
        lane = lax.broadcasted_iota(jnp.int32, (TILE_TOKENS, TILE_ROWS), 1)
        pick = jnp.zeros((TILE_TOKENS, TILE_ROWS), F32)
        for k in range(TOP_K):
            pick = pick + jnp.where(lane == ls_ref[:, k:k + 1], gate_ref[:, k:k + 1], 0.0)
        p_hi = pick.astype(BF16)
        p_lo = (pick - p_hi.astype(F32)).astype(BF16)
        moe = jnp.dot(p_hi, r_hi, preferred_element_type=F32) + jnp.dot(p_lo, r_hi, preferred_element_type=F32)
        is_prompt = j - 1 < n_prompt_tiles
        y = _rms(jnp.where(is_prompt, xp_ref[...], xs_ref[...]) + moe, g_ref[...])

        @pl.when(is_prompt)
        def _():
            yp_ref[...] = y

        @pl.when(jnp.logical_not(is_prompt))
        def _():
            ys_ref[...] = y


def _combine(y_sorted, x2p, x2s, ls, gates, tabs, g_final):
    tp, d = x2p.shape
    npt = tp // TILE_TOKENS
    assert x2s.shape[0] == TILE_TOKENS
    nt = npt + 1
    tile = lambda j: jnp.clip(j - 1, 0, nt - 1)
    ptile = lambda j: jnp.clip(j - 1, 0, npt - 1)
    grid_spec = pltpu.PrefetchScalarGridSpec(
        num_scalar_prefetch=4,
        grid=(nt + 1,),
        in_specs=[
            pl.BlockSpec((TILE_TOKENS, TOP_K), lambda j, *_: (tile(j), 0)),
            pl.BlockSpec((TILE_TOKENS, TOP_K), lambda j, *_: (tile(j), 0)),
            pl.BlockSpec((TILE_TOKENS, d), lambda j, *_: (ptile(j), 0)),
            pl.BlockSpec((TILE_TOKENS, d), lambda j, *_: (0, 0)),
            pl.BlockSpec((1, d), lambda j, *_: (0, 0)),
            pl.BlockSpec(memory_space=pl.ANY),
        ],
        out_specs=[
            pl.BlockSpec((TILE_TOKENS, d), lambda j, *_: (ptile(j), 0)),
            pl.BlockSpec((TILE_TOKENS, d), lambda j, *_: (0, 0)),
        ],
        scratch_shapes=[pltpu.VMEM((2, TILE_ROWS, d), F32), pltpu.SemaphoreType.DMA((2,))],
    )
    return pl.pallas_call(
        functools.partial(_combine_kernel, n_tiles=nt, n_prompt_tiles=npt),
        grid_spec=grid_spec,
        out_shape=[jax.ShapeDtypeStruct((tp, d), F32), jax.ShapeDtypeStruct((TILE_TOKENS, d), F32)],
        compiler_params=_cparams(("arbitrary",)),
        name="moe_combine",
    )(tabs["d"], tabs["n"], tabs["o"], tabs["rows"], ls, gates, x2p, x2s, g_final.reshape(1, d), y_sorted)


def _route_kernel(lp_ref, ls_in_ref, gate_ref, row_ref, cnt_ref, *, n_prompt_tiles, n_real_last):
    i = pl.program_id(0)
    is_prompt = i < n_prompt_tiles
    logits = jnp.where(is_prompt, lp_ref[...], ls_in_ref[...])
    tok = lax.broadcasted_iota(jnp.int32, (1, TILE_TOKENS), 1)
    real = jnp.logical_or(is_prompt, tok < n_real_last)
    eid = lax.broadcasted_iota(jnp.int32, logits.shape, 0)
    work = logits
    vals, sels = [], []
    for _ in range(TOP_K):
        v = jnp.max(work, axis=0, keepdims=True)
        idx = jnp.min(jnp.where(work == v, eid, N_EXPERTS), axis=0, keepdims=True)
        vals.append(v)
        sels.append(jnp.logical_and(eid == idx, real))
        work = jnp.where(eid == idx, -jnp.inf, work)
    es = [jnp.exp(v - vals[0]) for v in vals]
    denom = es[0] + es[1] + es[2] + es[3]
    chosen = jnp.zeros(logits.shape, F32)
    for sel in sels:
        chosen = chosen + jnp.where(sel, 1.0, 0.0)
    r = lax.broadcasted_iota(jnp.int32, (TILE_TOKENS, TILE_TOKENS), 0)
    c = lax.broadcasted_iota(jnp.int32, (TILE_TOKENS, TILE_TOKENS), 1)
    rank = jnp.dot(chosen.astype(BF16), jnp.where(r < c, 1.0, 0.0).astype(BF16), preferred_element_type=F32)
    count = jnp.sum(chosen, axis=1, keepdims=True)
    strip = jnp.floor((count + (SUBLANES - 1)) * (1.0 / SUBLANES)) * SUBLANES
    er = lax.broadcasted_iota(jnp.int32, (N_EXPERTS, N_EXPERTS), 0)
    ec = lax.broadcasted_iota(jnp.int32, (N_EXPERTS, N_EXPERTS), 1)
    start = jnp.dot(jnp.where(ec < er, 1.0, 0.0).astype(BF16),
                    jnp.broadcast_to(strip, (N_EXPERTS, LANES)).astype(BF16), preferred_element_type=F32)[:, 0:1]
    place = start + rank
    for k in range(TOP_K):
        row_k = jnp.sum(jnp.where(sels[k], place, 0.0), axis=0, keepdims=True)
        row_ref[k:k + 1, :] = jnp.where(real, row_k, -1.0).astype(jnp.int32)
        gate_ref[k:k + 1, :] = jnp.where(real, es[k] / denom, 0.0)
    cnt_ref[0] = strip.astype(jnp.int32)


def _route(lgp_t, lgs):
    ne, tp = lgp_t.shape
    ts = lgs.shape[0]
    assert tp % TILE_TOKENS == 0 and ts <= TILE_TOKENS and ne == N_EXPERTS
    npt = tp // TILE_TOKENS
    nt = npt + 1
    lgs_t = jnp.pad(lgs.T, ((0, 0), (0, TILE_TOKENS - ts)))
    tile4 = pl.BlockSpec((TOP_K, TILE_TOKENS), lambda i: (0, i))
    gates_t, ls_t, n_tile = pl.pallas_call(
        functools.partial(_route_kernel, n_prompt_tiles=npt, n_real_last=ts),
        grid=(nt,),
        in_specs=[pl.BlockSpec((ne, TILE_TOKENS), lambda i: (0, jnp.minimum(i, npt - 1))),
                  pl.BlockSpec((ne, TILE_TOKENS), lambda i: (0, 0))],
        out_specs=[tile4, tile4, pl.BlockSpec((1, ne, 1), lambda i: (i, 0, 0))],
        out_shape=[jax.ShapeDtypeStruct((TOP_K, nt * TILE_TOKENS), F32),
                   jax.ShapeDtypeStruct((TOP_K, nt * TILE_TOKENS), jnp.int32),
                   jax.ShapeDtypeStruct((nt, ne, 1), jnp.int32)],
        compiler_params=_cparams(("arbitrary",)),
        name="moe_route",
    )(lgp_t, lgs_t)
    n_tile = n_tile.reshape(nt, ne)
    up = lambda a, m: (a + m - 1) // m * m
    o_tile = jnp.cumsum(n_tile, axis=1) - n_tile
    rows_e = jnp.sum(n_tile, axis=0)
    nblk = up(rows_e, MOE_BM) // MOE_BM
    blk_end = jnp.cumsum(nblk)
    pstart = (blk_end - nblk) * MOE_BM
    d_tile = pstart[None, :] + jnp.cumsum(n_tile, axis=0) - n_tile
    t = tp + ts
    nb = (t * TOP_K + N_EXPERTS * nt * (SUBLANES - 1) + N_EXPERTS * (MOE_BM - 1) + MOE_BM - 1) // MOE_BM
    i32 = lambda a: a.astype(jnp.int32).reshape(-1)
    tabs = {"d": i32(d_tile), "n": i32(n_tile), "o": i32(o_tile), "rows": i32(jnp.sum(n_tile, axis=1)),
            "pad_start": i32(pstart + rows_e), "pad_len": i32(nblk * MOE_BM - rows_e),
            "blk_start": i32(blk_end - nblk), "nblk": i32(nblk)}
    return gates_t, ls_t, tabs, nb * MOE_BM


def kernel(x_prompt, x_sample, cache_k_win, cache_v_win, state_pool, cache_mem_k, cache_mem_v, mem_prompt,
           rel_bias, g_mix, w_in, w_pool, pool_scale, w_out, g_mem, g_x, w_xq, w_xk, w_xv, w_xo, g_ff,
           w_router, b_router, w_gate_up, b_gate_up, w_down, b_down, g_final):
    depth = g_mix.shape[0]
    assert depth == 1
    l = 0
    batch, seq, d = x_prompt.shape
    bd, n_new, _ = x_sample.shape
    n_mem = mem_prompt.shape[1]
    bw = d - A_WIDTH
    n_hist = cache_k_win.shape[2]
    n_pool = state_pool.shape[2]
    tp, ts = batch * seq, bd * n_new

    w_in_b = w_in[l].astype(BF16)
    w_pool_b = w_pool[l].astype(BF16)
    w_xkv_b = jnp.concatenate([w_xk[l], w_xv[l]], axis=1).astype(BF16)
    wts = {
        "w_out_a": w_out[l, :A_WIDTH].astype(BF16), "w_out_b": w_out[l, A_WIDTH:].astype(BF16),
        "g_x": g_x[l].reshape(1, d), "w_xq": w_xq[l].astype(BF16), "w_xo": w_xo[l].astype(BF16),
        "g_ff": g_ff[l].reshape(1, d), "w_router": w_router[l], "b_router": b_router[l].reshape(1, N_EXPERTS),
    }
    band_bias = _band_bias(rel_bias)
    widths = (A_WIDTH, A_WIDTH, A_WIDTH, bw)
    scales = (A_HEAD_DIM ** -0.5, 1.0, 1.0, 1.0)

    xp = x_prompt.reshape(tp, d)
    w_kv_t = jnp.transpose(w_in[l][:, A_WIDTH:3 * A_WIDTH].reshape(d, 2, A_WIDTH), (1, 2, 0)).astype(BF16)
    q, k, v, u, k_t, v_t = _in_proj(xp, g_mix[l], w_in_b, w_kv_t, widths, scales, batch, seq, tm=512)
    o_a = _dil_attn(q, k, v, band_bias, batch, seq)
    o_b = _pool_prompt(u, w_pool_b, pool_scale[l], batch, seq, tm=512)
    mk, mv, mk_b, mv_b = _mem_kv(mem_prompt.reshape(batch * n_mem, d), g_mem[l], w_xkv_b, X_HEADS, tm=512)
    x2p, hp, lgp = _mix_xattn(xp, o_a, o_b, mk_b, mv_b, wts, groups=batch, rows_per_group=seq, tm=512,
                              rows_per_sub=None, keys_per_sub=None, logits_t=True)

    xs = x_sample.reshape(ts, d)
    qs, ks, vs, us = _norm_proj(xs, g_mix[l], w_in_b, widths, scales, tm=ts)
    ck, cv = cache_k_win[l], cache_v_win[l]
    new3 = lambda a: a.reshape(bd, n_new, A_WIDTH)
    to_t = lambda c: jnp.transpose(c, (0, 2, 3, 1)).reshape(bd, A_WIDTH, n_hist)
    from_t = lambda c: jnp.transpose(c.reshape(bd, A_HEADS, A_HEAD_DIM, n_hist), (0, 3, 1, 2))[None]
    bias_c, bias_n = _decode_bias(rel_bias, n_hist, n_new)
    o_as, ck_t, cv_t = _decode_attn(new3(qs), new3(ks), new3(vs), to_t(ck), to_t(cv), bias_c, bias_n,
                                    heads_per_step=A_HEADS)
    o_as = o_as.reshape(ts, A_WIDTH)
    us3 = us.reshape(bd, n_new, bw)
    hist = jnp.concatenate([jnp.zeros((bd, POOL_HIST - n_pool, bw), F32), state_pool[l]], axis=1)
    cur = jnp.concatenate([us3, jnp.zeros((bd, SUBLANES - n_new, bw), F32)], axis=1)
    o_bs = _pool_dec(hist, cur, n_pool, w_pool_b, pool_scale[l])[:, :n_new].reshape(ts, bw)
    sub = 8
    xhd = d // X_HEADS
    x2s, hs, lgs = _mix_xattn(xs, o_as, o_bs, cache_mem_k[l].reshape(bd * n_mem, X_HEADS, xhd),
                              cache_mem_v[l].reshape(bd * n_mem, X_HEADS, xhd), wts, groups=bd // sub,
                              rows_per_group=sub * n_new, tm=sub * n_new, rows_per_sub=n_new, keys_per_sub=n_mem,
                              logits_t=False)

    assert ts <= TILE_TOKENS and w_gate_up.shape[1] == N_EXPERTS
    pad_s = lambda a: jnp.pad(a, ((0, TILE_TOKENS - ts), (0, 0)))
    gates_t, ls_t, tabs, n_rows = _route(lgp, lgs)
    gates, ls = gates_t.T, ls_t.T
    x_sorted = _dispatch(hp, pad_s(hs), ls_t, tabs, n_rows)
    y_sorted = _expert_ffn(x_sorted, tabs["blk_start"], tabs["nblk"], w_gate_up[l], b_gate_up[l], w_down[l],
                           b_down[l])
    y_prompt, y_sample = _combine(y_sorted, x2p, pad_s(x2s), ls, gates, tabs, g_final)
    y_prompt = y_prompt.reshape(batch, seq, d)
    y_sample = y_sample[:ts].reshape(bd, n_new, d)

    from_tp = lambda c: jnp.transpose(c.reshape(batch, A_HEADS, A_HEAD_DIM, seq), (0, 3, 1, 2))[None]
    k_win_prompt, v_win_prompt = from_tp(k_t), from_tp(v_t)
    pool_prompt = u.reshape(batch, seq, bw)[:, seq - n_pool:][None]
    mem_k_prompt = mk.reshape(1, batch, n_mem, X_HEADS, xhd)
    mem_v_prompt = mv.reshape(1, batch, n_mem, X_HEADS, xhd)
    k_win_sample, v_win_sample = from_t(ck_t), from_t(cv_t)
    pool_sample = jnp.concatenate([state_pool[l][:, n_new:], us3], axis=1)[None]
    return (y_prompt, y_sample, k_win_prompt, v_win_prompt, pool_prompt, mem_k_prompt, mem_v_prompt,
            k_win_sample, v_win_sample, pool_sample)
```

```python
import functools
import math

import numpy as np
import jax
import jax.numpy as jnp
from jax import lax
from jax.experimental import pallas as pl
from jax.experimental.pallas import tpu as pltpu

F32 = jnp.float32
BF16 = jnp.bfloat16

LANES = 128
SUBLANES = 8
VMEM_LIMIT_BYTES = 56 * 1024 * 1024

A_HEADS = 8
A_HEAD_DIM = 64
A_WIDTH = A_HEADS * A_HEAD_DIM
DILATED = ((128, 1), (512, 4), (2048, 16))
QB = 128
POOL_WINDOWS = (2, 4, 8, 16)
POOL_HIST = 16
X_HEADS = 4
N_EXPERTS = 32
TOP_K = 4
SWIGLU_LIMIT = 7.0
SWIGLU_ALPHA = 1.702
N_BUCKETS = 32
RMS_EPS = 1e-6
NEG = -1e30

MOE_BM = 256


def _cparams(sem):
    return pltpu.CompilerParams(dimension_semantics=sem, vmem_limit_bytes=VMEM_LIMIT_BYTES)


def _rms(x, g):
    return x * lax.rsqrt(jnp.mean(x * x, axis=-1, keepdims=True) + RMS_EPS) * g


def _norm_proj_kernel(x_ref, g_ref, w_ref, *o_refs, widths, scales):
    hb = _rms(x_ref[...], g_ref[...]).astype(BF16)
    off = 0
    for o_ref, width, scale in zip(o_refs, widths, scales):
        p = jnp.dot(hb, w_ref[:, off:off + width], preferred_element_type=F32)
        o_ref[...] = p if scale == 1.0 else p * scale
        off += width


def _norm_proj(x, g, w_bf16, widths, scales, tm):
    t, d = x.shape
    n = w_bf16.shape[1]
    assert sum(widths) == n and t % tm == 0
    return pl.pallas_call(
        functools.partial(_norm_proj_kernel, widths=widths, scales=scales),
        grid=(t // tm,),
        in_specs=[
            pl.BlockSpec((tm, d), lambda i: (i, 0)),
            pl.BlockSpec((1, d), lambda i: (0, 0)),
            pl.BlockSpec((d, n), lambda i: (0, 0)),
        ],
        out_specs=[pl.BlockSpec((tm, wd), lambda i: (i, 0)) for wd in widths],
        out_shape=[jax.ShapeDtypeStruct((t, wd), F32) for wd in widths],
        compiler_params=_cparams(("arbitrary",)),
        name="norm_proj",
    )(x, g.reshape(1, d), w_bf16)


def _mem_kv_kernel(x_ref, g_ref, w_ref, k_ref, v_ref, kb_ref, vb_ref, *, heads):
    hb = _rms(x_ref[...], g_ref[...]).astype(BF16)
    d = x_ref.shape[1]
    hd = d // heads
    for j, (o_ref, ob_ref) in enumerate(((k_ref, kb_ref), (v_ref, vb_ref))):
        p = jnp.dot(hb, w_ref[:, j * d:(j + 1) * d], preferred_element_type=F32)
        ob_ref[...] = p.astype(BF16)
        for h in range(heads):
            o_ref[:, h, :] = p[:, h * hd:(h + 1) * hd]


def _mem_kv(x, g, w_bf16, heads, tm):
    t, d = x.shape
    assert w_bf16.shape == (d, 2 * d) and t % tm == 0
    hd = d // heads
    o3 = pl.BlockSpec((tm, heads, hd), lambda i: (i, 0, 0))
    o2 = pl.BlockSpec((tm, d), lambda i: (i, 0))
    return pl.pallas_call(
        functools.partial(_mem_kv_kernel, heads=heads),
        grid=(t // tm,),
        in_specs=[
            pl.BlockSpec((tm, d), lambda i: (i, 0)),
            pl.BlockSpec((1, d), lambda i: (0, 0)),
            pl.BlockSpec((d, 2 * d), lambda i: (0, 0)),
        ],
        out_specs=[o3, o3, o2, o2],
        out_shape=[jax.ShapeDtypeStruct((t, heads, hd), F32)] * 2 + [jax.ShapeDtypeStruct((t, d), BF16)] * 2,
        compiler_params=_cparams(("arbitrary",)),
        name="mem_kv",
    )(x, g.reshape(1, d), w_bf16)


def _in_proj_kernel(x_ref, g_ref, w_ref, wt_ref, *o_refs, widths, scales, n_t):
    hb = _rms(x_ref[...], g_ref[...]).astype(BF16)
    off = 0
    for o_ref, width, scale in zip(o_refs, widths, scales):
        p = jnp.dot(hb, w_ref[:, off:off + width], preferred_element_type=F32)
        o_ref[...] = p if scale == 1.0 else p * scale
        off += width
    for j in range(n_t):
        o_refs[len(widths) + j][0] = lax.dot_general(wt_ref[j], hb, (((1,), (1,)), ((), ())),
                                                    preferred_element_type=F32)


def _in_proj(x, g, w_bf16, wt_bf16, widths, scales, batch, seq, tm):
    t, d = x.shape
    n = w_bf16.shape[1]
    n_t, wt_width, _ = wt_bf16.shape
    nt = seq // tm
    assert sum(widths) == n and seq % tm == 0 and t == batch * seq
    return pl.pallas_call(
        functools.partial(_in_proj_kernel, widths=widths, scales=scales, n_t=n_t),
        grid=(t // tm,),
        in_specs=[
            pl.BlockSpec((tm, d), lambda i: (i, 0)),
            pl.BlockSpec((1, d), lambda i: (0, 0)),
            pl.BlockSpec((d, n), lambda i: (0, 0)),
            pl.BlockSpec(wt_bf16.shape, lambda i: (0, 0, 0)),
        ],
        out_specs=[pl.BlockSpec((tm, wd), lambda i: (i, 0)) for wd in widths]
        + [pl.BlockSpec((1, wt_width, tm), lambda i: (i // nt, 0, i % nt))] * n_t,
        out_shape=[jax.ShapeDtypeStruct((t, wd), F32) for wd in widths]
        + [jax.ShapeDtypeStruct((batch, wt_width, seq), F32)] * n_t,
        compiler_params=_cparams(("arbitrary",)),
        name="in_proj",
    )(x, g.reshape(1, d), w_bf16, wt_bf16)


def _t5_bucket_np(n, max_dist):
    max_exact = N_BUCKETS // 2
    nf = np.maximum(n, 1).astype(np.float32)
    large = max_exact + (
        np.log(nf / np.float32(max_exact)) / np.float32(math.log(max_dist / max_exact))
        * np.float32(N_BUCKETS - max_exact)
    ).astype(np.int32)
    return np.where(n < max_exact, n, np.minimum(large, N_BUCKETS - 1))


def _band_bias(rel_bias):
    max_dist = max(w for w, _ in DILATED)
    qi = np.arange(QB)[:, None]
    ki = np.arange(2 * QB)[None, :]
    j = qi + QB - ki
    tabs = []
    for window, dil in DILATED:
        steps = window // dil
        in_band = (j >= 0) & (j <= steps)
        bucket = _t5_bucket_np(np.clip(j, 0, steps) * dil, max_dist)
        onehot = (bucket[..., None] == np.arange(N_BUCKETS)).astype(np.float32)
        b = jnp.einsum("qkb,bh->hqk", onehot, rel_bias.astype(F32), precision=lax.Precision.HIGHEST)
        tabs.append(jnp.where(in_band[None], b, NEG))
    return jnp.stack(tabs)


def _decode_bias(rel_bias, n_hist, n_new):
    max_dist = max(w for w, _ in DILATED)
    t = np.arange(n_new)[:, None]
    tabs_c, tabs_n = [], []
    for window, dil in DILATED:
        out = []
        for dist in (n_hist + t - np.arange(n_hist)[None, :], t - np.arange(n_new)[None, :]):
            ok = (dist >= 0) & (dist % dil == 0) & (dist <= window)
            onehot = (_t5_bucket_np(np.clip(dist, 0, window), max_dist)[..., None] == np.arange(N_BUCKETS))
            b = jnp.einsum("tpb,bh->htp", onehot.astype(np.float32), rel_bias.astype(F32),
                           precision=lax.Precision.HIGHEST)
            out.append(jnp.where(ok[None], b, NEG))
        tabs_c.append(out[0])
        tabs_n.append(out[1])
    return jnp.stack(tabs_c), jnp.stack(tabs_n)


def _decode_attn_kernel(q_ref, kn_ref, vn_ref, knt_ref, vnt_ref, ck_ref, cv_ref, bc_ref, bn_ref,
                        o_ref, ok_ref, ov_ref, *, n_new, heads):
    n_hist = ck_ref.shape[2]
    hd = A_HEAD_DIM
    outs = []
    for h in range(heads):
        rows = slice(h * hd, (h + 1) * hd)
        q = q_ref[0, :, rows].astype(BF16)
        kt = ck_ref[0, rows, :].astype(BF16)
        vt = cv_ref[0, rows, :].astype(BF16)
        lc = jnp.dot(q, kt, preferred_element_type=F32)
        ln = lax.dot_general(q, kn_ref[0, :, rows].astype(BF16), (((1,), (1,)), ((), ())),
                             preferred_element_type=F32)
        vn = vn_ref[0, :, rows].astype(BF16)
        o_br, lse_br = [], []
        for br in range(len(DILATED)):
            bl = lc + bc_ref[br, h]
            bln = ln + bn_ref[br, h]
            m = jnp.maximum(jnp.max(bl, axis=-1, keepdims=True), jnp.max(bln, axis=-1, keepdims=True))
            p = jnp.exp(bl - m)
            pn = jnp.exp(bln - m)
            s = jnp.sum(p, axis=-1, keepdims=True) + jnp.sum(pn, axis=-1, keepdims=True)
            o = lax.dot_general(p.astype(BF16), vt, (((1,), (1,)), ((), ())), preferred_element_type=F32)
            o = o + jnp.dot(pn.astype(BF16), vn, preferred_element_type=F32)
            o_br.append(o / s)
            lse_br.append(m + jnp.log(s))
        m = jnp.maximum(jnp.maximum(lse_br[0], lse_br[1]), lse_br[2])
        es = [jnp.exp(l - m) for l in lse_br]
        outs.append((es[0] * o_br[0] + es[1] * o_br[1] + es[2] * o_br[2]) / (es[0] + es[1] + es[2]))
    o_ref[0] = jnp.concatenate(outs, axis=-1)

    lane = lax.broadcasted_iota(jnp.int32, (ck_ref.shape[1], LANES), 1)
    for c_ref, nt_ref, dst in ((ck_ref, knt_ref, ok_ref), (cv_ref, vnt_ref, ov_ref)):
        shifted = pltpu.roll(c_ref[0], n_hist - n_new, axis=1)
        tail = shifted[:, n_hist - LANES:]
        for j in range(n_new):
            tail = jnp.where(lane == LANES - n_new + j, nt_ref[0, :, j:j + 1], tail)
        dst[0, :, :n_hist - LANES] = shifted[:, :n_hist - LANES]
        dst[0, :, n_hist - LANES:] = tail


def _decode_attn(q, k_new, v_new, ck_t, cv_t, bias_c, bias_n, heads_per_step):
    bd, n_new, aw = q.shape
    n_hist = ck_t.shape[2]
    gw = heads_per_step * A_HEAD_DIM
    ng = aw // gw
    row = pl.BlockSpec((1, n_new, gw), lambda b, g: (b, 0, g))
    col = pl.BlockSpec((1, gw, n_new), lambda b, g: (b, g, 0))
    cache = pl.BlockSpec((1, gw, n_hist), lambda b, g: (b, g, 0))
    nbr = len(DILATED)
    return pl.pallas_call(
        functools.partial(_decode_attn_kernel, n_new=n_new, heads=heads_per_step),
        grid=(bd, ng),
        in_specs=[row, row, row, col, col, cache, cache,
                  pl.BlockSpec((nbr, heads_per_step, n_new, n_hist), lambda b, g: (0, g, 0, 0)),
                  pl.BlockSpec((nbr, heads_per_step, n_new, n_new), lambda b, g: (0, g, 0, 0))],
        out_specs=[row, cache, cache],
        out_shape=[jax.ShapeDtypeStruct(q.shape, F32), jax.ShapeDtypeStruct(ck_t.shape, F32),
                   jax.ShapeDtypeStruct(cv_t.shape, F32)],
        compiler_params=_cparams(("arbitrary", "arbitrary")),
        name="decode_attn",
    )(q, k_new, v_new, jnp.swapaxes(k_new, 1, 2), jnp.swapaxes(v_new, 1, 2), ck_t, cv_t, bias_c, bias_n)


def _dil_attn_kernel(q_ref, k_ref, v_ref, bias_ref, o_ref, obr_ref, lbr_ref, *, seq):
    lane = lax.broadcasted_iota(jnp.int32, (QB, LANES), 1)
    head0 = lane < A_HEAD_DIM

    def rows(ref, start, n, dil):
        if dil == 1:
            return ref[pl.ds(start, n), :]
        return ref[pl.ds(start, n, stride=dil), :]

    def block(br, dil, qstart, kstart, nk):
        qs = rows(q_ref, qstart, QB, dil)
        ks = rows(k_ref, kstart, nk, dil).astype(BF16)
        vs = rows(v_ref, kstart, nk, dil).astype(BF16)
        outs, lses = [], []
        for hh in range(2):
            keep = head0 if hh == 0 else jnp.logical_not(head0)
            qm = jnp.where(keep, qs, 0.0).astype(BF16)
            logits = lax.dot_general(qm, ks, (((1,), (1,)), ((), ())), preferred_element_type=F32)
            logits = logits + bias_ref[br, hh, :, 2 * QB - nk:]
            m = jnp.max(logits, axis=-1, keepdims=True)
            p = jnp.exp(logits - m)
            s = jnp.sum(p, axis=-1, keepdims=True)
            o = jnp.dot(p.astype(BF16), vs, preferred_element_type=F32)
            outs.append(o / s)
            lses.append(jnp.broadcast_to(m + jnp.log(s), (QB, LANES)))
        o = jnp.where(head0, outs[0], outs[1])
        lse = jnp.where(head0, lses[0], lses[1])
        if dil == 1:
            obr_ref[br, pl.ds(qstart, QB), :] = o
            lbr_ref[br, pl.ds(qstart, QB), :] = lse
        else:
            obr_ref[br, pl.ds(qstart, QB, stride=dil), :] = o
            lbr_ref[br, pl.ds(qstart, QB, stride=dil), :] = lse

    for br, (window, dil) in enumerate(DILATED):
        assert window // dil == QB
        nblk = seq // (dil * QB)
        for r in range(dil):
            block(br, dil, r, r, QB)
            for i in range(1, nblk):
                qstart = r + dil * QB * i
                block(br, dil, qstart, qstart - dil * QB, 2 * QB)

    l0, l1, l2 = lbr_ref[0], lbr_ref[1], lbr_ref[2]
    m = jnp.maximum(jnp.maximum(l0, l1), l2)
    e0, e1, e2 = jnp.exp(l0 - m), jnp.exp(l1 - m), jnp.exp(l2 - m)
    acc = e0 * obr_ref[0] + e1 * obr_ref[1] + e2 * obr_ref[2]
    o_ref[...] = acc / (e0 + e1 + e2)


def _dil_attn(q, k, v, band_bias, batch, seq):
    t, aw = q.shape
    npair = aw // LANES
    bias = band_bias.reshape(len(DILATED), npair, 2, QB, 2 * QB)
    spec = pl.BlockSpec((seq, LANES), lambda b, hp: (b, hp))
    return pl.pallas_call(
        functools.partial(_dil_attn_kernel, seq=seq),
        grid=(batch, npair),
        in_specs=[
            spec, spec, spec,
            pl.BlockSpec((len(DILATED), None, 2, QB, 2 * QB), lambda b, hp: (0, hp, 0, 0, 0)),
        ],
        out_specs=spec,
        out_shape=jax.ShapeDtypeStruct((t, aw), F32),
        scratch_shapes=[
            pltpu.VMEM((len(DILATED), seq, LANES), F32),
            pltpu.VMEM((len(DILATED), seq, LANES), F32),
        ],
        compiler_params=_cparams(("arbitrary", "arbitrary")),
        name="dil_attn",
    )(q, k, v, bias)


def _pool_kernel(hist_ref, cur_ref, w_ref, scale_ref, o_ref, *, pos0_of_tile, tm):
    nb = cur_ref.shape[0]
    gdim = w_ref.shape[1]
    i = pl.program_id(1)
    pos0 = pos0_of_tile(i)
    have_hist = jnp.where(pos0 > 0, 1.0, 0.0).astype(F32)
    t = lax.broadcasted_iota(jnp.int32, (1, tm, 1), 1)
    for g, w in enumerate(POOL_WINDOWS):
        sl = slice(g * gdim, (g + 1) * gdim)
        cur = cur_ref[:, :, sl]
        ext = jnp.concatenate([hist_ref[:, :, sl] * have_hist, cur], axis=1)
        acc, span = ext, 1
        while span < w:
            n = acc.shape[1]
            acc = acc[:, span:n] + acc[:, 0:n - span]
            span *= 2
        wsum = acc[:, POOL_HIST + 1 - w:POOL_HIST + 1 - w + tm]
        cnt = jnp.minimum(pos0 + t + 1, w).astype(F32)
        d = (wsum / cnt - cur).astype(BF16).reshape(nb * tm, gdim)
        y = jnp.dot(d, w_ref[g], preferred_element_type=F32) * scale_ref[:, sl]
        o_ref[:, :, sl] = y.reshape(nb, tm, gdim)


def _pool_prompt(u, w_pool_bf16, pool_scale, batch, seq, tm):
    t, bw = u.shape
    nt = seq // tm
    hb = tm // POOL_HIST
    u3 = u.reshape(1, t, bw)
    out = pl.pallas_call(
        functools.partial(_pool_kernel, pos0_of_tile=lambda i: i * tm, tm=tm),
        grid=(batch, nt),
        in_specs=[
            pl.BlockSpec((1, POOL_HIST, bw), lambda b, i: (0, jnp.maximum((b * nt + i) * hb - 1, 0), 0)),
            pl.BlockSpec((1, tm, bw), lambda b, i: (0, b * nt + i, 0)),
            pl.BlockSpec(w_pool_bf16.shape, lambda b, i: (0, 0, 0)),
            pl.BlockSpec((1, bw), lambda b, i: (0, 0)),
        ],
        out_specs=pl.BlockSpec((1, tm, bw), lambda b, i: (0, b * nt + i, 0)),
        out_shape=jax.ShapeDtypeStruct((1, t, bw), F32),
        compiler_params=_cparams(("arbitrary", "arbitrary")),
        name="pool_prompt",
    )(u3, u3, w_pool_bf16, pool_scale.reshape(1, bw))
    return out.reshape(t, bw)


def _pool_dec(hist, cur, n_prev, w_pool_bf16, pool_scale):
    bd, tm, bw = cur.shape
    return pl.pallas_call(
        functools.partial(_pool_kernel, pos0_of_tile=lambda i: n_prev, tm=tm),
        grid=(1, 1),
        in_specs=[
            pl.BlockSpec(hist.shape, lambda b, i: (0, 0, 0)),
            pl.BlockSpec(cur.shape, lambda b, i: (0, 0, 0)),
            pl.BlockSpec(w_pool_bf16.shape, lambda b, i: (0, 0, 0)),
            pl.BlockSpec((1, bw), lambda b, i: (0, 0)),
        ],
        out_specs=pl.BlockSpec(cur.shape, lambda b, i: (0, 0, 0)),
        out_shape=jax.ShapeDtypeStruct(cur.shape, F32),
        compiler_params=_cparams(("arbitrary", "arbitrary")),
        name="pool_dec",
    )(hist, cur, w_pool_bf16, pool_scale.reshape(1, bw))


def _mix_xattn_kernel(x_ref, oa_ref, ob_ref, mk_ref, mv_ref, woa_ref, wob_ref, gx_ref, wq_ref, wo_ref,
                      gff_ref, wr_ref, br_ref, x2_ref, h_ref, lg_ref, *, rows_per_sub, keys_per_sub, logits_t, chains):
    tm_all = x_ref.shape[0]
    for ch in range(chains):
        rs = slice(ch * tm_all // chains, (ch + 1) * tm_all // chains)
        _mix_xattn_rows(rs, x_ref, oa_ref, ob_ref, mk_ref, mv_ref, woa_ref, wob_ref, gx_ref, wq_ref, wo_ref,
                        gff_ref, wr_ref, br_ref, x2_ref, h_ref, lg_ref, rows_per_sub, keys_per_sub, logits_t)


def _mix_xattn_rows(rs, x_ref, oa_ref, ob_ref, mk_ref, mv_ref, woa_ref, wob_ref, gx_ref, wq_ref, wo_ref,
                    gff_ref, wr_ref, br_ref, x2_ref, h_ref, lg_ref, rows_per_sub, keys_per_sub, logits_t):
    tm = rs.stop - rs.start
    nkv = mk_ref.shape[0]
    xh = wq_ref.shape[1] // X_HEADS
    x = x_ref[rs, :]
    x = x + jnp.dot(oa_ref[rs, :].astype(BF16), woa_ref[...], preferred_element_type=F32)
    x = x + jnp.dot(ob_ref[rs, :].astype(BF16), wob_ref[...], preferred_element_type=F32)
    hq = _rms(x, gx_ref[...]).astype(BF16)
    if rows_per_sub is not None:
        row0 = pl.program_id(1) * x_ref.shape[0] + rs.start
        rsub = (row0 + lax.broadcasted_iota(jnp.int32, (tm, nkv), 0)) // rows_per_sub
        ksub = lax.broadcasted_iota(jnp.int32, (tm, nkv), 1) // keys_per_sub
        same = rsub == ksub
    heads = []
    for h in range(X_HEADS):
        sl = slice(h * xh, (h + 1) * xh)
        q = jnp.dot(hq, wq_ref[:, sl], preferred_element_type=F32) * (xh ** -0.5)
        if len(mk_ref.shape) == 3:
            kh, vh = mk_ref[:, h, :].astype(BF16), mv_ref[:, h, :].astype(BF16)
        else:
            kh, vh = mk_ref[:, sl], mv_ref[:, sl]
        logits = lax.dot_general(q.astype(BF16), kh, (((1,), (1,)), ((), ())), preferred_element_type=F32)
        if rows_per_sub is not None:
            logits = jnp.where(same, logits, NEG)
        m = jnp.max(logits, axis=-1, keepdims=True)
        p = jnp.exp(logits - m)
        s = jnp.sum(p, axis=-1, keepdims=True)
        o = jnp.dot(p.astype(BF16), vh, preferred_element_type=F32) / s
        heads.append(o.astype(BF16))
    x = x + jnp.dot(jnp.concatenate(heads, axis=-1), wo_ref[...], preferred_element_type=F32)
    x2_ref[rs, :] = x
    hf = _rms(x, gff_ref[...])
    h_ref[rs, :] = hf
    if logits_t:
        lg_ref[:, rs] = lax.dot_general(wr_ref[...].astype(BF16), hf.astype(BF16), (((1,), (1,)), ((), ())),
                                      preferred_element_type=F32) + br_ref[...]
    else:
        lg_ref[rs, :] = jnp.dot(hf.astype(BF16), wr_ref[...].astype(BF16), preferred_element_type=F32) + br_ref[...]


def _mix_xattn(x, oa, ob, mk, mv, w, groups, rows_per_group, tm, rows_per_sub, keys_per_sub, logits_t, chains=1):
    t, d = x.shape
    nt = rows_per_group // tm
    nkv = mk.shape[0] // groups
    ne = w["w_router"].shape[1]
    row = lambda width: pl.BlockSpec((tm, width), lambda g, i: (g * nt + i, 0))
    const = lambda a: pl.BlockSpec(a.shape, lambda g, i: (0,) * a.ndim)
    kv = pl.BlockSpec((nkv,) + mk.shape[1:], lambda g, i: (g,) + (0,) * (mk.ndim - 1))
    w_r, b_r = (w["w_router"].T, w["b_router"].T) if logits_t else (w["w_router"], w["b_router"])
    consts = [w["w_out_a"], w["w_out_b"], w["g_x"], w["w_xq"], w["w_xo"], w["g_ff"], w_r, b_r]
    lg_spec = pl.BlockSpec((ne, tm), lambda g, i: (0, g * nt + i)) if logits_t else row(ne)
    return pl.pallas_call(
        functools.partial(_mix_xattn_kernel, rows_per_sub=rows_per_sub, keys_per_sub=keys_per_sub,
                          logits_t=logits_t, chains=chains),
        grid=(groups, nt),
        in_specs=[row(d), row(oa.shape[1]), row(ob.shape[1]), kv, kv] + [const(a) for a in consts],
        out_specs=[row(d), row(d), lg_spec],
        out_shape=[jax.ShapeDtypeStruct((t, d), F32), jax.ShapeDtypeStruct((t, d), F32),
                   jax.ShapeDtypeStruct((ne, t) if logits_t else (t, ne), F32)],
        compiler_params=_cparams(("arbitrary", "arbitrary")),
        name="mix_xattn",
    )(x, oa, ob, mk, mv, *consts)


TILE_TOKENS = 256
TILE_ROWS = TOP_K * TILE_TOKENS + N_EXPERTS * (SUBLANES - 1)
assert TILE_ROWS % SUBLANES == 0
FFN_PAIR = 2


def _bits(lo, hi):
    return tuple(lo << j for j in range((hi // lo).bit_length()))


STRIP_LEN_BITS = _bits(SUBLANES, TILE_TOKENS)
PAD_LEN_BITS = _bits(SUBLANES, MOE_BM - SUBLANES)
TILE_LEN_BITS = _bits(SUBLANES, TILE_ROWS)


def _strip_dmas(n, bits, make_copy, wait=False):
    for bit in bits:
        @pl.when((n & bit) != 0)
        def _(bit=bit):
            c = make_copy(n & (bit - 1), bit)
            if wait:
                c.wait()
            else:
                c.start()


def _tile_strips(i, slot, d_ref, n_ref, o_ref, local, hbm, sem, to_hbm):
    for e in range(N_EXPERTS):
        l0 = pl.multiple_of(o_ref[i * N_EXPERTS + e], SUBLANES)
        g0 = pl.multiple_of(d_ref[i * N_EXPERTS + e], SUBLANES)

        def make_copy(off, size, l0=l0, g0=g0):
            lo = local.at[slot, pl.ds(pl.multiple_of(l0 + off, SUBLANES), size)]
            gl = hbm.at[pl.ds(pl.multiple_of(g0 + off, SUBLANES), size)]
            src, dst = (lo, gl) if to_hbm else (gl, lo)
            return pltpu.make_async_copy(src, dst, sem.at[slot])
        _strip_dmas(n_ref[i * N_EXPERTS + e], STRIP_LEN_BITS, make_copy)


def _tile_wait(rows, slot, local, hbm, sem):
    _strip_dmas(rows, TILE_LEN_BITS, lambda off, size: pltpu.make_async_copy(
        hbm.at[pl.ds(0, size)], local.at[slot, pl.ds(0, size)], sem.at[slot]), wait=True)


def _dispatch_kernel(d_ref, n_ref, o_ref, r_ref, ps_ref, pn_ref, lst_ref, hp_ref, hs_ref, xs_hbm, buf, zbuf, sem,
                     psem, *, n_tiles, n_prompt_tiles):
    i = pl.program_id(0)
    slot = i % 2

    def pad_strips(wait):
        for e in range(N_EXPERTS):
            start = ps_ref[e]
            _strip_dmas(pn_ref[e], PAD_LEN_BITS, lambda off, size: pltpu.make_async_copy(
                zbuf.at[pl.ds(0, size)], xs_hbm.at[pl.ds(pl.multiple_of(start + off, SUBLANES), size)],
                psem.at[0]), wait)
        chunk = zbuf.shape[0]
        used_rows = ps_ref[N_EXPERTS - 1] + pn_ref[N_EXPERTS - 1]

        def body(c, carry):
            cp = pltpu.make_async_copy(zbuf, xs_hbm.at[pl.ds(pl.multiple_of(c * chunk, chunk), chunk)], psem.at[0])
            if wait:
                cp.wait()
            else:
                cp.start()
            return carry
        lax.fori_loop(used_rows // chunk, xs_hbm.shape[0] // chunk, body, 0)

    @pl.when(i >= 2)
    def _():
        _tile_wait(r_ref[i - 2], slot, buf, xs_hbm, sem)

    @pl.when(i == 0)
    def _():
        zbuf[...] = jnp.zeros_like(zbuf)
        pad_strips(False)

    srow = lax.broadcasted_iota(jnp.int32, (TILE_ROWS, TILE_TOKENS), 0)
    place = srow == lst_ref[0:1, :]
    for k in range(1, TOP_K):
        place = jnp.logical_or(place, srow == lst_ref[k:k + 1, :])
    x = jnp.where(i < n_prompt_tiles, hp_ref[...], hs_ref[...]).astype(BF16)
    buf[slot] = jnp.dot(jnp.where(place, 1.0, 0.0).astype(BF16), x, preferred_element_type=F32)
    _tile_strips(i, slot, d_ref, n_ref, o_ref, buf, xs_hbm, sem, to_hbm=True)

    @pl.when(i == n_tiles - 1)
    def _():
        _tile_wait(r_ref[i], slot, buf, xs_hbm, sem)
        if n_tiles > 1:
            _tile_wait(r_ref[jnp.maximum(i - 1, 0)], 1 - slot, buf, xs_hbm, sem)
        pad_strips(True)


def _dispatch(hp, hs, ls_t, tabs, n_rows):
    tp, d = hp.shape
    npt = tp // TILE_TOKENS
    assert tp % TILE_TOKENS == 0 and hs.shape[0] == TILE_TOKENS
    nt = npt + 1
    grid_spec = pltpu.PrefetchScalarGridSpec(
        num_scalar_prefetch=6,
        grid=(nt,),
        in_specs=[
            pl.BlockSpec((TOP_K, TILE_TOKENS), lambda i, *_: (0, i)),
            pl.BlockSpec((TILE_TOKENS, d), lambda i, *_: (jnp.minimum(i, npt - 1), 0)),
            pl.BlockSpec((TILE_TOKENS, d), lambda i, *_: (0, 0)),
        ],
        out_specs=pl.BlockSpec(memory_space=pl.ANY),
        scratch_shapes=[
            pltpu.VMEM((2, TILE_ROWS, d), F32),
            pltpu.VMEM((PAD_LEN_BITS[-1], d), F32),
            pltpu.SemaphoreType.DMA((2,)),
            pltpu.SemaphoreType.DMA((1,)),
        ],
    )
    return pl.pallas_call(
        functools.partial(_dispatch_kernel, n_tiles=nt, n_prompt_tiles=npt),
        grid_spec=grid_spec,
        out_shape=jax.ShapeDtypeStruct((n_rows, d), F32),
        compiler_params=_cparams(("arbitrary",)),
        name="moe_dispatch",
    )(tabs["d"], tabs["n"], tabs["o"], tabs["rows"], tabs["pad_start"], tabs["pad_len"], ls_t, hp, hs)


def _expert_ffn_kernel(b0_ref, nb_ref, x_hbm, wgu_hbm, bgu_ref, wd_hbm, bd_ref, y_hbm, xbuf, ybuf, xtail, ytail,
                       xsem, ysem, tsem, wgu_st, wd_st, wsem, wgu_bf, wd_bf, *, n_blocks):
    e = pl.program_id(0)
    b0 = b0_ref[e]
    n = nb_ref[e]
    n2 = n // FFN_PAIR
    odd = n % FFN_PAIR
    dff = wd_st.shape[0]
    big = FFN_PAIR * MOE_BM

    def w_copies(ex):
        return (pltpu.make_async_copy(wgu_hbm.at[ex], wgu_st, wsem.at[0]),
                pltpu.make_async_copy(wd_hbm.at[ex], wd_st, wsem.at[1]))

    def rows(blk, size):
        return pl.ds(pl.multiple_of((b0 + blk) * MOE_BM, MOE_BM), size)

    def x_copy(kk, slot):
        return pltpu.make_async_copy(x_hbm.at[rows(FFN_PAIR * kk, big)], xbuf.at[slot], xsem.at[slot])

    def y_copy(kk, slot):
        return pltpu.make_async_copy(ybuf.at[slot], y_hbm.at[rows(FFN_PAIR * kk, big)], ysem.at[slot])

    def xt_copy():
        return pltpu.make_async_copy(x_hbm.at[rows(n - 1, MOE_BM)], xtail, tsem.at[0])

    def yt_copy():
        return pltpu.make_async_copy(ytail, y_hbm.at[rows(n - 1, MOE_BM)], tsem.at[1])

    def ffn(x):
        hu = jnp.dot(x.astype(BF16), wgu_bf[...], preferred_element_type=F32) + bgu_ref[0]
        g = jnp.minimum(hu[:, :dff], SWIGLU_LIMIT)
        u = jnp.clip(hu[:, dff:], -SWIGLU_LIMIT, SWIGLU_LIMIT)
        a = g * jax.nn.sigmoid(SWIGLU_ALPHA * g) * (u + 1.0)
        return jnp.dot(a.astype(BF16), wd_bf[...], preferred_element_type=F32) + bd_ref[0]

    @pl.when(e == 0)
    def _():
        for c in w_copies(0):
            c.start()

    @pl.when(jnp.logical_and(n > 0, odd == 1))
    def _():
        xt_copy().start()

    @pl.when(n2 > 0)
    def _():
        x_copy(0, 0).start()

    @pl.when(n2 > 1)
    def _():
        x_copy(1, 1).start()

    for c in w_copies(e):
        c.wait()

    @pl.when(n > 0)
    def _():
        wgu_bf[...] = wgu_st[...].astype(BF16)
        wd_bf[...] = wd_st[...].astype(BF16)

    @pl.when(e + 1 < N_EXPERTS)
    def _():
        for c in w_copies(e + 1):
            c.start()

    def drain(nblocks):
        m2 = nblocks // FFN_PAIR

        def y_wait(slot):
            pltpu.make_async_copy(ybuf.at[slot], y_hbm.at[pl.ds(0, big)], ysem.at[slot]).wait()

        @pl.when(m2 >= 2)
        def _():
            y_wait(m2 % 2)

        @pl.when(m2 >= 1)
        def _():
            y_wait((m2 - 1) % 2)

        @pl.when(nblocks % FFN_PAIR == 1)
        def _():
            pltpu.make_async_copy(ytail, y_hbm.at[pl.ds(0, MOE_BM)], tsem.at[1]).wait()

    @pl.when(e >= 1)
    def _():
        drain(nb_ref[jnp.maximum(e - 1, 0)])

    @pl.when(n > 0)
    def _():
        def body(kk, carry):
            slot = kk % 2

            @pl.when(jnp.logical_and(kk >= 1, kk + 1 < n2))
            def _():
                x_copy(kk + 1, 1 - slot).start()

            x_copy(kk, slot).wait()

            @pl.when(kk >= 2)
            def _():
                y_copy(kk - 2, slot).wait()

            ybuf[slot] = ffn(xbuf[slot])
            y_copy(kk, slot).start()
            return carry
        lax.fori_loop(0, n2, body, 0)

        @pl.when(odd == 1)
        def _():
            xt_copy().wait()
            ytail[...] = ffn(xtail[...])
            yt_copy().start()

    @pl.when(e == N_EXPERTS - 1)
    def _():
        drain(n)
        ytail[...] = jnp.zeros_like(ytail)

        def fill(wait):
            def body(k, carry):
                cp = pltpu.make_async_copy(ytail, y_hbm.at[pl.ds(pl.multiple_of(k * MOE_BM, MOE_BM), MOE_BM)],
                                           tsem.at[1])
                if wait:
                    cp.wait()
                else:
                    cp.start()
                return carry
            lax.fori_loop(b0 + n, n_blocks, body, 0)
        fill(False)
        fill(True)


def _expert_ffn(x_sorted, blk_start, nblk, w_gate_up, b_gate_up, w_down, b_down):
    n_rows, d = x_sorted.shape
    ne, _, dff2 = w_gate_up.shape
    dff = dff2 // 2
    any_spec = pl.BlockSpec(memory_space=pl.ANY)
    big = FFN_PAIR * MOE_BM
    grid_spec = pltpu.PrefetchScalarGridSpec(
        num_scalar_prefetch=2,
        grid=(ne,),
        in_specs=[
            any_spec,
            any_spec,
            pl.BlockSpec((1, 1, dff2), lambda e, *_: (e, 0, 0)),
            any_spec,
            pl.BlockSpec((1, 1, d), lambda e, *_: (e, 0, 0)),
        ],
        out_specs=any_spec,
        scratch_shapes=[
            pltpu.VMEM((2, big, d), F32), pltpu.VMEM((2, big, d), F32),
            pltpu.VMEM((MOE_BM, d), F32), pltpu.VMEM((MOE_BM, d), F32),
            pltpu.SemaphoreType.DMA((2,)), pltpu.SemaphoreType.DMA((2,)), pltpu.SemaphoreType.DMA((2,)),
            pltpu.VMEM((d, dff2), F32), pltpu.VMEM((dff, d), F32), pltpu.SemaphoreType.DMA((2,)),
            pltpu.VMEM((d, dff2), BF16), pltpu.VMEM((dff, d), BF16),
        ],
    )
    return pl.pallas_call(
        functools.partial(_expert_ffn_kernel, n_blocks=n_rows // MOE_BM),
        grid_spec=grid_spec,
        out_shape=jax.ShapeDtypeStruct((n_rows, d), F32),
        compiler_params=_cparams(("arbitrary",)),
        name="expert_ffn",
    )(blk_start, nblk, x_sorted, w_gate_up, b_gate_up.reshape(ne, 1, dff2), w_down, b_down.reshape(ne, 1, d))


def _combine_kernel(d_ref, n_ref, o_ref, r_ref, ls_ref, gate_ref, xp_ref, xs_ref, g_ref, y_hbm, yp_ref, ys_ref,
                    buf, sem, *, n_tiles, n_prompt_tiles):
    j = pl.program_id(0)

    @pl.when(j == 0)
    def _():
        buf[...] = jnp.zeros_like(buf)

    @pl.when(j < n_tiles)
    def _():
        _tile_strips(j, j % 2, d_ref, n_ref, o_ref, buf, y_hbm, sem, to_hbm=False)

    @pl.when(j >= 1)
    def _():
        slot = (j - 1) % 2
        _tile_wait(r_ref[j - 1], slot, buf, y_hbm, sem)
        r_hi = buf[slot].astype(BF16)
        lane = lax.broadcasted_iota(jnp.int32, (TILE_TOKENS, TILE_ROWS), 1)
        pick = jnp.zeros((TILE_TOKENS, TILE_ROWS), F32)
        for k in range(TOP_K):
            pick = pick + jnp.where(lane == ls_ref[:, k:k + 1], gate_ref[:, k:k + 1], 0.0)
        p_hi = pick.astype(BF16)
        p_lo = (pick - p_hi.astype(F32)).astype(BF16)
        moe = jnp.dot(p_hi, r_hi, preferred_element_type=F32) + jnp.dot(p_lo, r_hi, preferred_element_type=F32)
        is_prompt = j - 1 < n_prompt_tiles
        y = _rms(jnp.where(is_prompt, xp_ref[...], xs_ref[...]) + moe, g_ref[...])

        @pl.when(is_prompt)
        def _():
            yp_ref[...] = y

        @pl.when(jnp.logical_not(is_prompt))
        def _():
            ys_ref[...] = y


def _combine(y_sorted, x2p, x2s, ls, gates, tabs, g_final):
    tp, d = x2p.shape
    npt = tp // TILE_TOKENS
    assert x2s.shape[0] == TILE_TOKENS
    nt = npt + 1
    tile = lambda j: jnp.clip(j - 1, 0, nt - 1)
    ptile = lambda j: jnp.clip(j - 1, 0, npt - 1)
    grid_spec = pltpu.PrefetchScalarGridSpec(
        num_scalar_prefetch=4,
        grid=(nt + 1,),
        in_specs=[
            pl.BlockSpec((TILE_TOKENS, TOP_K), lambda j, *_: (tile(j), 0)),
            pl.BlockSpec((TILE_TOKENS, TOP_K), lambda j, *_: (tile(j), 0)),
            pl.BlockSpec((TILE_TOKENS, d), lambda j, *_: (ptile(j), 0)),
            pl.BlockSpec((TILE_TOKENS, d), lambda j, *_: (0, 0)),
            pl.BlockSpec((1, d), lambda j, *_: (0, 0)),
            pl.BlockSpec(memory_space=pl.ANY),
        ],
        out_specs=[
            pl.BlockSpec((TILE_TOKENS, d), lambda j, *_: (ptile(j), 0)),
            pl.BlockSpec((TILE_TOKENS, d), lambda j, *_: (0, 0)),
        ],
        scratch_shapes=[pltpu.VMEM((2, TILE_ROWS, d), F32), pltpu.SemaphoreType.DMA((2,))],
    )
    return pl.pallas_call(
        functools.partial(_combine_kernel, n_tiles=nt, n_prompt_tiles=npt),
        grid_spec=grid_spec,
        out_shape=[jax.ShapeDtypeStruct((tp, d), F32), jax.ShapeDtypeStruct((TILE_TOKENS, d), F32)],
        compiler_params=_cparams(("arbitrary",)),
        name="moe_combine",
    )(tabs["d"], tabs["n"], tabs["o"], tabs["rows"], ls, gates, x2p, x2s, g_final.reshape(1, d), y_sorted)


def _route_kernel(lp_ref, ls_in_ref, gate_ref, row_ref, cnt_ref, *, n_prompt_tiles, n_real_last):
    i = pl.program_id(0)
    is_prompt = i < n_prompt_tiles
    logits = jnp.where(is_prompt, lp_ref[...], ls_in_ref[...])
    tok = lax.broadcasted_iota(jnp.int32, (1, TILE_TOKENS), 1)
    real = jnp.logical_or(is_prompt, tok < n_real_last)
    eid = lax.broadcasted_iota(jnp.int32, logits.shape, 0)
    work = logits
    vals, sels = [], []
    for _ in range(TOP_K):
        v = jnp.max(work, axis=0, keepdims=True)
        idx = jnp.min(jnp.where(work == v, eid, N_EXPERTS), axis=0, keepdims=True)
        vals.append(v)
        sels.append(jnp.logical_and(eid == idx, real))
        work = jnp.where(eid == idx, -jnp.inf, work)
    es = [jnp.exp(v - vals[0]) for v in vals]
    denom = es[0] + es[1] + es[2] + es[3]
    chosen = jnp.zeros(logits.shape, F32)
    for sel in sels:
        chosen = chosen + jnp.where(sel, 1.0, 0.0)
    r = lax.broadcasted_iota(jnp.int32, (TILE_TOKENS, TILE_TOKENS), 0)
    c = lax.broadcasted_iota(jnp.int32, (TILE_TOKENS, TILE_TOKENS), 1)
    rank = jnp.dot(chosen.astype(BF16), jnp.where(r < c, 1.0, 0.0).astype(BF16), preferred_element_type=F32)
    count = jnp.sum(chosen, axis=1, keepdims=True)
    strip = jnp.floor((count + (SUBLANES - 1)) * (1.0 / SUBLANES)) * SUBLANES
    er = lax.broadcasted_iota(jnp.int32, (N_EXPERTS, N_EXPERTS), 0)
    ec = lax.broadcasted_iota(jnp.int32, (N_EXPERTS, N_EXPERTS), 1)
    start = jnp.dot(jnp.where(ec < er, 1.0, 0.0).astype(BF16),
                    jnp.broadcast_to(strip, (N_EXPERTS, LANES)).astype(BF16), preferred_element_type=F32)[:, 0:1]
    place = start + rank
    for k in range(TOP_K):
        row_k = jnp.sum(jnp.where(sels[k], place, 0.0), axis=0, keepdims=True)
        row_ref[k:k + 1, :] = jnp.where(real, row_k, -1.0).astype(jnp.int32)
        gate_ref[k:k + 1, :] = jnp.where(real, es[k] / denom, 0.0)
    cnt_ref[0] = strip.astype(jnp.int32)


def _route(lgp_t, lgs):
    ne, tp = lgp_t.shape
    ts = lgs.shape[0]
    assert tp % TILE_TOKENS == 0 and ts <= TILE_TOKENS and ne == N_EXPERTS
    npt = tp // TILE_TOKENS
    nt = npt + 1
    lgs_t = jnp.pad(lgs.T, ((0, 0), (0, TILE_TOKENS - ts)))
    tile4 = pl.BlockSpec((TOP_K, TILE_TOKENS), lambda i: (0, i))
    gates_t, ls_t, n_tile = pl.pallas_call(
        functools.partial(_route_kernel, n_prompt_tiles=npt, n_real_last=ts),
        grid=(nt,),
        in_specs=[pl.BlockSpec((ne, TILE_TOKENS), lambda i: (0, jnp.minimum(i, npt - 1))),
                  pl.BlockSpec((ne, TILE_TOKENS), lambda i: (0, 0))],
        out_specs=[tile4, tile4, pl.BlockSpec((1, ne, 1), lambda i: (i, 0, 0))],
        out_shape=[jax.ShapeDtypeStruct((TOP_K, nt * TILE_TOKENS), F32),
                   jax.ShapeDtypeStruct((TOP_K, nt * TILE_TOKENS), jnp.int32),
                   jax.ShapeDtypeStruct((nt, ne, 1), jnp.int32)],
        compiler_params=_cparams(("arbitrary",)),
        name="moe_route",
    )(lgp_t, lgs_t)
    n_tile = n_tile.reshape(nt, ne)
    up = lambda a, m: (a + m - 1) // m * m
    o_tile = jnp.cumsum(n_tile, axis=1) - n_tile
    rows_e = jnp.sum(n_tile, axis=0)
    nblk = up(rows_e, MOE_BM) // MOE_BM
    blk_end = jnp.cumsum(nblk)
    pstart = (blk_end - nblk) * MOE_BM
    d_tile = pstart[None, :] + jnp.cumsum(n_tile, axis=0) - n_tile
    t = tp + ts
    nb = (t * TOP_K + N_EXPERTS * nt * (SUBLANES - 1) + N_EXPERTS * (MOE_BM - 1) + MOE_BM - 1) // MOE_BM
    i32 = lambda a: a.astype(jnp.int32).reshape(-1)
    tabs = {"d": i32(d_tile), "n": i32(n_tile), "o": i32(o_tile), "rows": i32(jnp.sum(n_tile, axis=1)),
            "pad_start": i32(pstart + rows_e), "pad_len": i32(nblk * MOE_BM - rows_e),
            "blk_start": i32(blk_end - nblk), "nblk": i32(nblk)}
    return gates_t, ls_t, tabs, nb * MOE_BM


def kernel(x_prompt, x_sample, cache_k_win, cache_v_win, state_pool, cache_mem_k, cache_mem_v, mem_prompt,
           rel_bias, g_mix, w_in, w_pool, pool_scale, w_out, g_mem, g_x, w_xq, w_xk, w_xv, w_xo, g_ff,
           w_router, b_router, w_gate_up, b_gate_up, w_down, b_down, g_final):
    depth = g_mix.shape[0]
    assert depth == 1
    l = 0
    batch, seq, d = x_prompt.shape
    bd, n_new, _ = x_sample.shape
    n_mem = mem_prompt.shape[1]
    bw = d - A_WIDTH
    n_hist = cache_k_win.shape[2]
    n_pool = state_pool.shape[2]
    tp, ts = batch * seq, bd * n_new

    w_in_b = w_in[l].astype(BF16)
    w_pool_b = w_pool[l].astype(BF16)
    w_xkv_b = jnp.concatenate([w_xk[l], w_xv[l]], axis=1).astype(BF16)
    wts = {
        "w_out_a": w_out[l, :A_WIDTH].astype(BF16), "w_out_b": w_out[l, A_WIDTH:].astype(BF16),
        "g_x": g_x[l].reshape(1, d), "w_xq": w_xq[l].astype(BF16), "w_xo": w_xo[l].astype(BF16),
        "g_ff": g_ff[l].reshape(1, d), "w_router": w_router[l], "b_router": b_router[l].reshape(1, N_EXPERTS),
    }
    band_bias = _band_bias(rel_bias)
    widths = (A_WIDTH, A_WIDTH, A_WIDTH, bw)
    scales = (A_HEAD_DIM ** -0.5, 1.0, 1.0, 1.0)

    xp = x_prompt.reshape(tp, d)
    w_kv_t = jnp.transpose(w_in[l][:, A_WIDTH:3 * A_WIDTH].reshape(d, 2, A_WIDTH), (1, 2, 0)).astype(BF16)
    q, k, v, u, k_t, v_t = _in_proj(xp, g_mix[l], w_in_b, w_kv_t, widths, scales, batch, seq, tm=512)
    o_a = _dil_attn(q, k, v, band_bias, batch, seq)
    o_b = _pool_prompt(u, w_pool_b, pool_scale[l], batch, seq, tm=512)
    mk, mv, mk_b, mv_b = _mem_kv(mem_prompt.reshape(batch * n_mem, d), g_mem[l], w_xkv_b, X_HEADS, tm=512)
    x2p, hp, lgp = _mix_xattn(xp, o_a, o_b, mk_b, mv_b, wts, groups=batch, rows_per_group=seq, tm=1024,
                              rows_per_sub=None, keys_per_sub=None, logits_t=True, chains=2)

    xs = x_sample.reshape(ts, d)
    qs, ks, vs, us = _norm_proj(xs, g_mix[l], w_in_b, widths, scales, tm=ts)
    ck, cv = cache_k_win[l], cache_v_win[l]
    new3 = lambda a: a.reshape(bd, n_new, A_WIDTH)
    to_t = lambda c: jnp.transpose(c, (0, 2, 3, 1)).reshape(bd, A_WIDTH, n_hist)
    from_t = lambda c: jnp.transpose(c.reshape(bd, A_HEADS, A_HEAD_DIM, n_hist), (0, 3, 1, 2))[None]
    bias_c, bias_n = _decode_bias(rel_bias, n_hist, n_new)
    o_as, ck_t, cv_t = _decode_attn(new3(qs), new3(ks), new3(vs), to_t(ck), to_t(cv), bias_c, bias_n,
                                    heads_per_step=A_HEADS)
    o_as = o_as.reshape(ts, A_WIDTH)
    us3 = us.reshape(bd, n_new, bw)
    hist = jnp.concatenate([jnp.zeros((bd, POOL_HIST - n_pool, bw), F32), state_pool[l]], axis=1)
    cur = jnp.concatenate([us3, jnp.zeros((bd, SUBLANES - n_new, bw), F32)], axis=1)
    o_bs = _pool_dec(hist, cur, n_pool, w_pool_b, pool_scale[l])[:, :n_new].reshape(ts, bw)
    sub = 8
    xhd = d // X_HEADS
    x2s, hs, lgs = _mix_xattn(xs, o_as, o_bs, cache_mem_k[l].reshape(bd * n_mem, X_HEADS, xhd),
                              cache_mem_v[l].reshape(bd * n_mem, X_HEADS, xhd), wts, groups=bd // sub,
                              rows_per_group=sub * n_new, tm=sub * n_new, rows_per_sub=n_new, keys_per_sub=n_mem,
                              logits_t=False)

    assert ts <= TILE_TOKENS and w_gate_up.shape[1] == N_EXPERTS
    pad_s = lambda a: jnp.pad(a, ((0, TILE_TOKENS - ts), (0, 0)))
    gates_t, ls_t, tabs, n_rows = _route(lgp, lgs)
    gates, ls = gates_t.T, ls_t.T
    x_sorted = _dispatch(hp, pad_s(hs), ls_t, tabs, n_rows)
    y_sorted = _expert_ffn(x_sorted, tabs["blk_start"], tabs["nblk"], w_gate_up[l], b_gate_up[l], w_down[l],
                           b_down[l])
    y_prompt, y_sample = _combine(y_sorted, x2p, pad_s(x2s), ls, gates, tabs, g_final)
    y_prompt = y_prompt.reshape(batch, seq, d)
    y_sample = y_sample[:ts].reshape(bd, n_new, d)

    from_tp = lambda c: jnp.transpose(c.reshape(batch, A_HEADS, A_HEAD_DIM, seq), (0, 3, 1, 2))[None]
    k_win_prompt, v_win_prompt = from_tp(k_t), from_tp(v_t)
    pool_prompt = u.reshape(batch, seq, bw)[:, seq - n_pool:][None]
    mem_k_prompt = mk.reshape(1, batch, n_mem, X_HEADS, xhd)
    mem_v_prompt = mv.reshape(1, batch, n_mem, X_HEADS, xhd)
    k_win_sample, v_win_sample = from_t(ck_t), from_t(cv_t)
    pool_sample = jnp.concatenate([state_pool[l][:, n_new:], us3], axis=1)[None]
    return (y_prompt, y_sample, k_win_prompt, v_win_prompt, pool_prompt, mem_k_prompt, mem_v_prompt,
            k_win_sample, v_win_sample, pool_sample)
```

```python
import functools
import math

import numpy as np
import jax
import jax.numpy as jnp
from jax import lax
from jax.experimental import pallas as pl
from jax.experimental.pallas import tpu as pltpu

F32 = jnp.float32
BF16 = jnp.bfloat16

LANES = 128
SUBLANES = 8
VMEM_LIMIT_BYTES = 56 * 1024 * 1024

A_HEADS = 8
A_HEAD_DIM = 64
A_WIDTH = A_HEADS * A_HEAD_DIM
DILATED = ((128, 1), (512, 4), (2048, 16))
QB = 128
POOL_WINDOWS = (2, 4, 8, 16)
POOL_HIST = 16
X_HEADS = 4
N_EXPERTS = 32
TOP_K = 4
SWIGLU_LIMIT = 7.0
SWIGLU_ALPHA = 1.702
N_BUCKETS = 32
RMS_EPS = 1e-6
NEG = -1e30

MOE_BM = 256


def _cparams(sem):
    return pltpu.CompilerParams(dimension_semantics=sem, vmem_limit_bytes=VMEM_LIMIT_BYTES)


def _rms(x, g):
    return x * lax.rsqrt(jnp.mean(x * x, axis=-1, keepdims=True) + RMS_EPS) * g


def _norm_proj_kernel(x_ref, g_ref, w_ref, *o_refs, widths, scales):
    hb = _rms(x_ref[...], g_ref[...]).astype(BF16)
    off = 0
    for o_ref, width, scale in zip(o_refs, widths, scales):
        p = jnp.dot(hb, w_ref[:, off:off + width], preferred_element_type=F32)
        o_ref[...] = p if scale == 1.0 else p * scale
        off += width


def _norm_proj(x, g, w_bf16, widths, scales, tm):
    t, d = x.shape
    n = w_bf16.shape[1]
    assert sum(widths) == n and t % tm == 0
    return pl.pallas_call(
        functools.partial(_norm_proj_kernel, widths=widths, scales=scales),
        grid=(t // tm,),
        in_specs=[
            pl.BlockSpec((tm, d), lambda i: (i, 0)),
            pl.BlockSpec((1, d), lambda i: (0, 0)),
            pl.BlockSpec((d, n), lambda i: (0, 0)),
        ],
        out_specs=[pl.BlockSpec((tm, wd), lambda i: (i, 0)) for wd in widths],
        out_shape=[jax.ShapeDtypeStruct((t, wd), F32) for wd in widths],
        compiler_params=_cparams(("arbitrary",)),
        name="norm_proj",
    )(x, g.reshape(1, d), w_bf16)


def _mem_kv_kernel(x_ref, g_ref, w_ref, k_ref, v_ref, kb_ref, vb_ref, *, heads):
    hb = _rms(x_ref[...], g_ref[...]).astype(BF16)
    d = x_ref.shape[1]
    hd = d // heads
    for j, (o_ref, ob_ref) in enumerate(((k_ref, kb_ref), (v_ref, vb_ref))):
        p = jnp.dot(hb, w_ref[:, j * d:(j + 1) * d], preferred_element_type=F32)
        ob_ref[...] = p.astype(BF16)
        for h in range(heads):
            o_ref[:, h, :] = p[:, h * hd:(h + 1) * hd]


def _mem_kv(x, g, w_bf16, heads, tm):
    t, d = x.shape
    assert w_bf16.shape == (d, 2 * d) and t % tm == 0
    hd = d // heads
    o3 = pl.BlockSpec((tm, heads, hd), lambda i: (i, 0, 0))
    o2 = pl.BlockSpec((tm, d), lambda i: (i, 0))
    return pl.pallas_call(
        functools.partial(_mem_kv_kernel, heads=heads),
        grid=(t // tm,),
        in_specs=[
            pl.BlockSpec((tm, d), lambda i: (i, 0)),
            pl.BlockSpec((1, d), lambda i: (0, 0)),
            pl.BlockSpec((d, 2 * d), lambda i: (0, 0)),
        ],
        out_specs=[o3, o3, o2, o2],
        out_shape=[jax.ShapeDtypeStruct((t, heads, hd), F32)] * 2 + [jax.ShapeDtypeStruct((t, d), BF16)] * 2,
        compiler_params=_cparams(("arbitrary",)),
        name="mem_kv",
    )(x, g.reshape(1, d), w_bf16)


def _in_proj_kernel(x_ref, g_ref, w_ref, wt_ref, *o_refs, widths, scales, n_t):
    hb = _rms(x_ref[...], g_ref[...]).astype(BF16)
    off = 0
    for o_ref, width, scale in zip(o_refs, widths, scales):
        p = jnp.dot(hb, w_ref[:, off:off + width], preferred_element_type=F32)
        o_ref[...] = p if scale == 1.0 else p * scale
        off += width
    for j in range(n_t):
        o_refs[len(widths) + j][0] = lax.dot_general(wt_ref[j], hb, (((1,), (1,)), ((), ())),
                                                    preferred_element_type=F32)


def _in_proj(x, g, w_bf16, wt_bf16, widths, scales, batch, seq, tm):
    t, d = x.shape
    n = w_bf16.shape[1]
    n_t, wt_width, _ = wt_bf16.shape
    nt = seq // tm
    assert sum(widths) == n and seq % tm == 0 and t == batch * seq
    return pl.pallas_call(
        functools.partial(_in_proj_kernel, widths=widths, scales=scales, n_t=n_t),
        grid=(t // tm,),
        in_specs=[
            pl.BlockSpec((tm, d), lambda i: (i, 0)),
            pl.BlockSpec((1, d), lambda i: (0, 0)),
            pl.BlockSpec((d, n), lambda i: (0, 0)),
            pl.BlockSpec(wt_bf16.shape, lambda i: (0, 0, 0)),
        ],
        out_specs=[pl.BlockSpec((tm, wd), lambda i: (i, 0)) for wd in widths]
        + [pl.BlockSpec((1, wt_width, tm), lambda i: (i // nt, 0, i % nt))] * n_t,
        out_shape=[jax.ShapeDtypeStruct((t, wd), F32) for wd in widths]
        + [jax.ShapeDtypeStruct((batch, wt_width, seq), F32)] * n_t,
        compiler_params=_cparams(("arbitrary",)),
        name="in_proj",
    )(x, g.reshape(1, d), w_bf16, wt_bf16)


def _t5_bucket_np(n, max_dist):
    max_exact = N_BUCKETS // 2
    nf = np.maximum(n, 1).astype(np.float32)
    large = max_exact + (
        np.log(nf / np.float32(max_exact)) / np.float32(math.log(max_dist / max_exact))
        * np.float32(N_BUCKETS - max_exact)
    ).astype(np.int32)
    return np.where(n < max_exact, n, np.minimum(large, N_BUCKETS - 1))


def _band_bias(rel_bias):
    max_dist = max(w for w, _ in DILATED)
    qi = np.arange(QB)[:, None]
    ki = np.arange(2 * QB)[None, :]
    j = qi + QB - ki
    tabs = []
    for window, dil in DILATED:
        steps = window // dil
        in_band = (j >= 0) & (j <= steps)
        bucket = _t5_bucket_np(np.clip(j, 0, steps) * dil, max_dist)
        onehot = (bucket[..., None] == np.arange(N_BUCKETS)).astype(np.float32)
        b = jnp.einsum("qkb,bh->hqk", onehot, rel_bias.astype(F32), precision=lax.Precision.HIGHEST)
        tabs.append(jnp.where(in_band[None], b, NEG))
    return jnp.stack(tabs)


def _decode_bias(rel_bias, n_hist, n_new):
    max_dist = max(w for w, _ in DILATED)
    t = np.arange(n_new)[:, None]
    tabs_c, tabs_n = [], []
    for window, dil in DILATED:
        out = []
        for dist in (n_hist + t - np.arange(n_hist)[None, :], t - np.arange(n_new)[None, :]):
            ok = (dist >= 0) & (dist % dil == 0) & (dist <= window)
            onehot = (_t5_bucket_np(np.clip(dist, 0, window), max_dist)[..., None] == np.arange(N_BUCKETS))
            b = jnp.einsum("tpb,bh->htp", onehot.astype(np.float32), rel_bias.astype(F32),
                           precision=lax.Precision.HIGHEST)
            out.append(jnp.where(ok[None], b, NEG))
        tabs_c.append(out[0])
        tabs_n.append(out[1])
    return jnp.stack(tabs_c), jnp.stack(tabs_n)


def _decode_attn_kernel(q_ref, kn_ref, vn_ref, knt_ref, vnt_ref, ck_ref, cv_ref, bc_ref, bn_ref,
                        o_ref, ok_ref, ov_ref, *, n_new, heads):
    n_hist = ck_ref.shape[2]
    hd = A_HEAD_DIM
    outs = []
    for h in range(heads):
        rows = slice(h * hd, (h + 1) * hd)
        q = q_ref[0, :, rows].astype(BF16)
        kt = ck_ref[0, rows, :].astype(BF16)
        vt = cv_ref[0, rows, :].astype(BF16)
        lc = jnp.dot(q, kt, preferred_element_type=F32)
        ln = lax.dot_general(q, kn_ref[0, :, rows].astype(BF16), (((1,), (1,)), ((), ())),
                             preferred_element_type=F32)
        vn = vn_ref[0, :, rows].astype(BF16)
        o_br, lse_br = [], []
        for br in range(len(DILATED)):
            bl = lc + bc_ref[br, h]
            bln = ln + bn_ref[br, h]
            m = jnp.maximum(jnp.max(bl, axis=-1, keepdims=True), jnp.max(bln, axis=-1, keepdims=True))
            p = jnp.exp(bl - m)
            pn = jnp.exp(bln - m)
            s = jnp.sum(p, axis=-1, keepdims=True) + jnp.sum(pn, axis=-1, keepdims=True)
            o = lax.dot_general(p.astype(BF16), vt, (((1,), (1,)), ((), ())), preferred_element_type=F32)
            o = o + jnp.dot(pn.astype(BF16), vn, preferred_element_type=F32)
            o_br.append(o / s)
            lse_br.append(m + jnp.log(s))
        m = jnp.maximum(jnp.maximum(lse_br[0], lse_br[1]), lse_br[2])
        es = [jnp.exp(l - m) for l in lse_br]
        outs.append((es[0] * o_br[0] + es[1] * o_br[1] + es[2] * o_br[2]) / (es[0] + es[1] + es[2]))
    o_ref[0] = jnp.concatenate(outs, axis=-1)

    lane = lax.broadcasted_iota(jnp.int32, (ck_ref.shape[1], LANES), 1)
    for c_ref, nt_ref, dst in ((ck_ref, knt_ref, ok_ref), (cv_ref, vnt_ref, ov_ref)):
        shifted = pltpu.roll(c_ref[0], n_hist - n_new, axis=1)
        tail = shifted[:, n_hist - LANES:]
        for j in range(n_new):
            tail = jnp.where(lane == LANES - n_new + j, nt_ref[0, :, j:j + 1], tail)
        dst[0, :, :n_hist - LANES] = shifted[:, :n_hist - LANES]
        dst[0, :, n_hist - LANES:] = tail


def _decode_attn(q, k_new, v_new, ck_t, cv_t, bias_c, bias_n, heads_per_step):
    bd, n_new, aw = q.shape
    n_hist = ck_t.shape[2]
    gw = heads_per_step * A_HEAD_DIM
    ng = aw // gw
    row = pl.BlockSpec((1, n_new, gw), lambda b, g: (b, 0, g))
    col = pl.BlockSpec((1, gw, n_new), lambda b, g: (b, g, 0))
    cache = pl.BlockSpec((1, gw, n_hist), lambda b, g: (b, g, 0))
    nbr = len(DILATED)
    return pl.pallas_call(
        functools.partial(_decode_attn_kernel, n_new=n_new, heads=heads_per_step),
        grid=(bd, ng),
        in_specs=[row, row, row, col, col, cache, cache,
                  pl.BlockSpec((nbr, heads_per_step, n_new, n_hist), lambda b, g: (0, g, 0, 0)),
                  pl.BlockSpec((nbr, heads_per_step, n_new, n_new), lambda b, g: (0, g, 0, 0))],
        out_specs=[row, cache, cache],
        out_shape=[jax.ShapeDtypeStruct(q.shape, F32), jax.ShapeDtypeStruct(ck_t.shape, F32),
                   jax.ShapeDtypeStruct(cv_t.shape, F32)],
        compiler_params=_cparams(("arbitrary", "arbitrary")),
        name="decode_attn",
    )(q, k_new, v_new, jnp.swapaxes(k_new, 1, 2), jnp.swapaxes(v_new, 1, 2), ck_t, cv_t, bias_c, bias_n)


def _dil_attn_kernel(q_ref, k_ref, v_ref, bias_ref, o_ref, obr_ref, lbr_ref, *, seq):
    lane = lax.broadcasted_iota(jnp.int32, (QB, LANES), 1)
    head0 = lane < A_HEAD_DIM

    def rows(ref, start, n, dil):
        if dil == 1:
            return ref[pl.ds(start, n), :]
        return ref[pl.ds(start, n, stride=dil), :]

    def block(br, dil, qstart, kstart, nk):
        qs = rows(q_ref, qstart, QB, dil)
        ks = rows(k_ref, kstart, nk, dil).astype(BF16)
        vs = rows(v_ref, kstart, nk, dil).astype(BF16)
        outs, lses = [], []
        for hh in range(2):
            keep = head0 if hh == 0 else jnp.logical_not(head0)
            qm = jnp.where(keep, qs, 0.0).astype(BF16)
            logits = lax.dot_general(qm, ks, (((1,), (1,)), ((), ())), preferred_element_type=F32)
            logits = logits + bias_ref[br, hh, :, 2 * QB - nk:]
            m = jnp.max(logits, axis=-1, keepdims=True)
            p = jnp.exp(logits - m)
            s = jnp.sum(p, axis=-1, keepdims=True)
            o = jnp.dot(p.astype(BF16), vs, preferred_element_type=F32)
            outs.append(o / s)
            lses.append(jnp.broadcast_to(m + jnp.log(s), (QB, LANES)))
        o = jnp.where(head0, outs[0], outs[1])
        lse = jnp.where(head0, lses[0], lses[1])
        if dil == 1:
            obr_ref[br, pl.ds(qstart, QB), :] = o
            lbr_ref[br, pl.ds(qstart, QB), :] = lse
        else:
            obr_ref[br, pl.ds(qstart, QB, stride=dil), :] = o
            lbr_ref[br, pl.ds(qstart, QB, stride=dil), :] = lse

    for br, (window, dil) in enumerate(DILATED):
        assert window // dil == QB
        nblk = seq // (dil * QB)
        for r in range(dil):
            block(br, dil, r, r, QB)
            for i in range(1, nblk):
                qstart = r + dil * QB * i
                block(br, dil, qstart, qstart - dil * QB, 2 * QB)

    l0, l1, l2 = lbr_ref[0], lbr_ref[1], lbr_ref[2]
    m = jnp.maximum(jnp.maximum(l0, l1), l2)
    e0, e1, e2 = jnp.exp(l0 - m), jnp.exp(l1 - m), jnp.exp(l2 - m)
    acc = e0 * obr_ref[0] + e1 * obr_ref[1] + e2 * obr_ref[2]
    o_ref[...] = acc / (e0 + e1 + e2)


def _dil_attn(q, k, v, band_bias, batch, seq):
    t, aw = q.shape
    npair = aw // LANES
    bias = band_bias.reshape(len(DILATED), npair, 2, QB, 2 * QB)
    spec = pl.BlockSpec((seq, LANES), lambda b, hp: (b, hp))
    return pl.pallas_call(
        functools.partial(_dil_attn_kernel, seq=seq),
        grid=(batch, npair),
        in_specs=[
            spec, spec, spec,
            pl.BlockSpec((len(DILATED), None, 2, QB, 2 * QB), lambda b, hp: (0, hp, 0, 0, 0)),
        ],
        out_specs=spec,
        out_shape=jax.ShapeDtypeStruct((t, aw), F32),
        scratch_shapes=[
            pltpu.VMEM((len(DILATED), seq, LANES), F32),
            pltpu.VMEM((len(DILATED), seq, LANES), F32),
        ],
        compiler_params=_cparams(("arbitrary", "arbitrary")),
        name="dil_attn",
    )(q, k, v, bias)


def _pool_kernel(hist_ref, cur_ref, w_ref, scale_ref, o_ref, *, pos0_of_tile, tm):
    nb = cur_ref.shape[0]
    gdim = w_ref.shape[1]
    i = pl.program_id(1)
    pos0 = pos0_of_tile(i)
    have_hist = jnp.where(pos0 > 0, 1.0, 0.0).astype(F32)
    t = lax.broadcasted_iota(jnp.int32, (1, tm, 1), 1)
    for g, w in enumerate(POOL_WINDOWS):
        sl = slice(g * gdim, (g + 1) * gdim)
        cur = cur_ref[:, :, sl]
        ext = jnp.concatenate([hist_ref[:, :, sl] * have_hist, cur], axis=1)
        acc, span = ext, 1
        while span < w:
            n = acc.shape[1]
            acc = acc[:, span:n] + acc[:, 0:n - span]
            span *= 2
        wsum = acc[:, POOL_HIST + 1 - w:POOL_HIST + 1 - w + tm]
        cnt = jnp.minimum(pos0 + t + 1, w).astype(F32)
        d = (wsum / cnt - cur).astype(BF16).reshape(nb * tm, gdim)
        y = jnp.dot(d, w_ref[g], preferred_element_type=F32) * scale_ref[:, sl]
        o_ref[:, :, sl] = y.reshape(nb, tm, gdim)


def _pool_prompt(u, w_pool_bf16, pool_scale, batch, seq, tm):
    t, bw = u.shape
    nt = seq // tm
    hb = tm // POOL_HIST
    u3 = u.reshape(1, t, bw)
    out = pl.pallas_call(
        functools.partial(_pool_kernel, pos0_of_tile=lambda i: i * tm, tm=tm),
        grid=(batch, nt),
        in_specs=[
            pl.BlockSpec((1, POOL_HIST, bw), lambda b, i: (0, jnp.maximum((b * nt + i) * hb - 1, 0), 0)),
            pl.BlockSpec((1, tm, bw), lambda b, i: (0, b * nt + i, 0)),
            pl.BlockSpec(w_pool_bf16.shape, lambda b, i: (0, 0, 0)),
            pl.BlockSpec((1, bw), lambda b, i: (0, 0)),
        ],
        out_specs=pl.BlockSpec((1, tm, bw), lambda b, i: (0, b * nt + i, 0)),
        out_shape=jax.ShapeDtypeStruct((1, t, bw), F32),
        compiler_params=_cparams(("arbitrary", "arbitrary")),
        name="pool_prompt",
    )(u3, u3, w_pool_bf16, pool_scale.reshape(1, bw))
    return out.reshape(t, bw)


def _pool_dec(hist, cur, n_prev, w_pool_bf16, pool_scale):
    bd, tm, bw = cur.shape
    return pl.pallas_call(
        functools.partial(_pool_kernel, pos0_of_tile=lambda i: n_prev, tm=tm),
        grid=(1, 1),
        in_specs=[
            pl.BlockSpec(hist.shape, lambda b, i: (0, 0, 0)),
            pl.BlockSpec(cur.shape, lambda b, i: (0, 0, 0)),
            pl.BlockSpec(w_pool_bf16.shape, lambda b, i: (0, 0, 0)),
            pl.BlockSpec((1, bw), lambda b, i: (0, 0)),
        ],
        out_specs=pl.BlockSpec(cur.shape, lambda b, i: (0, 0, 0)),
        out_shape=jax.ShapeDtypeStruct(cur.shape, F32),
        compiler_params=_cparams(("arbitrary", "arbitrary")),
        name="pool_dec",
    )(hist, cur, w_pool_bf16, pool_scale.reshape(1, bw))


def _mix_xattn_kernel(x_ref, oa_ref, ob_ref, mk_ref, mv_ref, woa_ref, wob_ref, gx_ref, wq_ref, wo_ref,
                      gff_ref, wr_ref, br_ref, x2_ref, h_ref, lg_ref, *, rows_per_sub, keys_per_sub, logits_t):
    tm = x_ref.shape[0]
    nkv = mk_ref.shape[0]
    xh = wq_ref.shape[1] // X_HEADS
    x = x_ref[...]
    x = x + jnp.dot(oa_ref[...].astype(BF16), woa_ref[...], preferred_element_type=F32)
    x = x + jnp.dot(ob_ref[...].astype(BF16), wob_ref[...], preferred_element_type=F32)
    hq = _rms(x, gx_ref[...]).astype(BF16)
    if rows_per_sub is not None:
        rsub = (pl.program_id(1) * tm + lax.broadcasted_iota(jnp.int32, (tm, nkv), 0)) // rows_per_sub
        ksub = lax.broadcasted_iota(jnp.int32, (tm, nkv), 1) // keys_per_sub
        same = rsub == ksub
    heads = []
    for h in range(X_HEADS):
        sl = slice(h * xh, (h + 1) * xh)
        q = jnp.dot(hq, wq_ref[:, sl], preferred_element_type=F32) * (xh ** -0.5)
        if len(mk_ref.shape) == 3:
            kh, vh = mk_ref[:, h, :].astype(BF16), mv_ref[:, h, :].astype(BF16)
        else:
            kh, vh = mk_ref[:, sl], mv_ref[:, sl]
        logits = lax.dot_general(q.astype(BF16), kh, (((1,), (1,)), ((), ())), preferred_element_type=F32)
        if rows_per_sub is not None:
            logits = jnp.where(same, logits, NEG)
        m = jnp.max(logits, axis=-1, keepdims=True)
        p = jnp.exp(logits - m)
        s = jnp.sum(p, axis=-1, keepdims=True)
        o = jnp.dot(p.astype(BF16), vh, preferred_element_type=F32) / s
        heads.append(o.astype(BF16))
    x = x + jnp.dot(jnp.concatenate(heads, axis=-1), wo_ref[...], preferred_element_type=F32)
    x2_ref[...] = x
    hf = _rms(x, gff_ref[...])
    h_ref[...] = hf
    if logits_t:
        lg_ref[...] = lax.dot_general(wr_ref[...].astype(BF16), hf.astype(BF16), (((1,), (1,)), ((), ())),
                                      preferred_element_type=F32) + br_ref[...]
    else:
        lg_ref[...] = jnp.dot(hf.astype(BF16), wr_ref[...].astype(BF16), preferred_element_type=F32) + br_ref[...]


def _mix_xattn(x, oa, ob, mk, mv, w, groups, rows_per_group, tm, rows_per_sub, keys_per_sub, logits_t):
    t, d = x.shape
    nt = rows_per_group // tm
    nkv = mk.shape[0] // groups
    ne = w["w_router"].shape[1]
    row = lambda width: pl.BlockSpec((tm, width), lambda g, i: (g * nt + i, 0))
    const = lambda a: pl.BlockSpec(a.shape, lambda g, i: (0,) * a.ndim)
    kv = pl.BlockSpec((nkv,) + mk.shape[1:], lambda g, i: (g,) + (0,) * (mk.ndim - 1))
    w_r, b_r = (w["w_router"].T, w["b_router"].T) if logits_t else (w["w_router"], w["b_router"])
    consts = [w["w_out_a"], w["w_out_b"], w["g_x"], w["w_xq"], w["w_xo"], w["g_ff"], w_r, b_r]
    lg_spec = pl.BlockSpec((ne, tm), lambda g, i: (0, g * nt + i)) if logits_t else row(ne)
    return pl.pallas_call(
        functools.partial(_mix_xattn_kernel, rows_per_sub=rows_per_sub, keys_per_sub=keys_per_sub,
                          logits_t=logits_t),
        grid=(groups, nt),
        in_specs=[row(d), row(oa.shape[1]), row(ob.shape[1]), kv, kv] + [const(a) for a in consts],
        out_specs=[row(d), row(d), lg_spec],
        out_shape=[jax.ShapeDtypeStruct((t, d), F32), jax.ShapeDtypeStruct((t, d), F32),
                   jax.ShapeDtypeStruct((ne, t) if logits_t else (t, ne), F32)],
        compiler_params=_cparams(("arbitrary", "arbitrary")),
        name="mix_xattn",
    )(x, oa, ob, mk, mv, *consts)


TILE_TOKENS = 256
TILE_ROWS = TOP_K * TILE_TOKENS + N_EXPERTS * (SUBLANES - 1)
assert TILE_ROWS % SUBLANES == 0
FFN_PAIR = 2


def _bits(lo, hi):
    return tuple(lo << j for j in range((hi // lo).bit_length()))


STRIP_LEN_BITS = _bits(SUBLANES, TILE_TOKENS)
PAD_LEN_BITS = _bits(SUBLANES, MOE_BM - SUBLANES)
TILE_LEN_BITS = _bits(SUBLANES, TILE_ROWS)


def _strip_dmas(n, bits, make_copy, wait=False):
    for bit in bits:
        @pl.when((n & bit) != 0)
        def _(bit=bit):
            c = make_copy(n & (bit - 1), bit)
            if wait:
                c.wait()
            else:
                c.start()


def _tile_strips(i, slot, d_ref, n_ref, o_ref, local, hbm, sem, to_hbm):
    for e in range(N_EXPERTS):
        l0 = pl.multiple_of(o_ref[i * N_EXPERTS + e], SUBLANES)
        g0 = pl.multiple_of(d_ref[i * N_EXPERTS + e], SUBLANES)

        def make_copy(off, size, l0=l0, g0=g0):
            lo = local.at[slot, pl.ds(pl.multiple_of(l0 + off, SUBLANES), size)]
            gl = hbm.at[pl.ds(pl.multiple_of(g0 + off, SUBLANES), size)]
            src, dst = (lo, gl) if to_hbm else (gl, lo)
            return pltpu.make_async_copy(src, dst, sem.at[slot])
        _strip_dmas(n_ref[i * N_EXPERTS + e], STRIP_LEN_BITS, make_copy)


def _tile_wait(rows, slot, local, hbm, sem):
    _strip_dmas(rows, TILE_LEN_BITS, lambda off, size: pltpu.make_async_copy(
        hbm.at[pl.ds(0, size)], local.at[slot, pl.ds(0, size)], sem.at[slot]), wait=True)


def _dispatch_kernel(d_ref, n_ref, o_ref, r_ref, ps_ref, pn_ref, lst_ref, hp_ref, hs_ref, xs_hbm, buf, zbuf, sem,
                     psem, *, n_tiles, n_prompt_tiles):
    i = pl.program_id(0)
    slot = i % 2

    def pad_strips(wait):
        for e in range(N_EXPERTS):
            start = ps_ref[e]
            _strip_dmas(pn_ref[e], PAD_LEN_BITS, lambda off, size: pltpu.make_async_copy(
                zbuf.at[pl.ds(0, size)], xs_hbm.at[pl.ds(pl.multiple_of(start + off, SUBLANES), size)],
                psem.at[0]), wait)
        chunk = zbuf.shape[0]
        used_rows = ps_ref[N_EXPERTS - 1] + pn_ref[N_EXPERTS - 1]

        def body(c, carry):
            cp = pltpu.make_async_copy(zbuf, xs_hbm.at[pl.ds(pl.multiple_of(c * chunk, chunk), chunk)], psem.at[0])
            if wait:
                cp.wait()
            else:
                cp.start()
            return carry
        lax.fori_loop(used_rows // chunk, xs_hbm.shape[0] // chunk, body, 0)

    @pl.when(i >= 2)
    def _():
        _tile_wait(r_ref[i - 2], slot, buf, xs_hbm, sem)

    @pl.when(i == 0)
    def _():
        zbuf[...] = jnp.zeros_like(zbuf)
        pad_strips(False)

    srow = lax.broadcasted_iota(jnp.int32, (TILE_ROWS, TILE_TOKENS), 0)
    place = srow == lst_ref[0:1, :]
    for k in range(1, TOP_K):
        place = jnp.logical_or(place, srow == lst_ref[k:k + 1, :])
    x = jnp.where(i < n_prompt_tiles, hp_ref[...], hs_ref[...]).astype(BF16)
    buf[slot] = jnp.dot(jnp.where(place, 1.0, 0.0).astype(BF16), x, preferred_element_type=F32)
    _tile_strips(i, slot, d_ref, n_ref, o_ref, buf, xs_hbm, sem, to_hbm=True)

    @pl.when(i == n_tiles - 1)
    def _():
        _tile_wait(r_ref[i], slot, buf, xs_hbm, sem)
        if n_tiles > 1:
            _tile_wait(r_ref[jnp.maximum(i - 1, 0)], 1 - slot, buf, xs_hbm, sem)
        pad_strips(True)


def _dispatch(hp, hs, ls_t, tabs, n_rows):
    tp, d = hp.shape
    npt = tp // TILE_TOKENS
    assert tp % TILE_TOKENS == 0 and hs.shape[0] == TILE_TOKENS
    nt = npt + 1
    grid_spec = pltpu.PrefetchScalarGridSpec(
        num_scalar_prefetch=6,
        grid=(nt,),
        in_specs=[
            pl.BlockSpec((TOP_K, TILE_TOKENS), lambda i, *_: (0, i)),
            pl.BlockSpec((TILE_TOKENS, d), lambda i, *_: (jnp.minimum(i, npt - 1), 0)),
            pl.BlockSpec((TILE_TOKENS, d), lambda i, *_: (0, 0)),
        ],
        out_specs=pl.BlockSpec(memory_space=pl.ANY),
        scratch_shapes=[
            pltpu.VMEM((2, TILE_ROWS, d), F32),
            pltpu.VMEM((PAD_LEN_BITS[-1], d), F32),
            pltpu.SemaphoreType.DMA((2,)),
            pltpu.SemaphoreType.DMA((1,)),
        ],
    )
    return pl.pallas_call(
        functools.partial(_dispatch_kernel, n_tiles=nt, n_prompt_tiles=npt),
        grid_spec=grid_spec,
        out_shape=jax.ShapeDtypeStruct((n_rows, d), F32),
        compiler_params=_cparams(("arbitrary",)),
        name="moe_dispatch",
    )(tabs["d"], tabs["n"], tabs["o"], tabs["rows"], tabs["pad_start"], tabs["pad_len"], ls_t, hp, hs)


def _expert_ffn_kernel(b0_ref, nb_ref, x_hbm, wgu_hbm, bgu_ref, wd_hbm, bd_ref, y_hbm, xbuf, ybuf, xtail, ytail,
                       xsem, ysem, tsem, wgu_st, wd_st, wsem, wgu_bf, wd_bf, *, n_blocks):
    e = pl.program_id(0)
    b0 = b0_ref[e]
    n = nb_ref[e]
    n2 = n // FFN_PAIR
    odd = n % FFN_PAIR
    dff = wd_st.shape[0]
    big = FFN_PAIR * MOE_BM

    def w_copies(ex):
        return (pltpu.make_async_copy(wgu_hbm.at[ex], wgu_st, wsem.at[0]),
                pltpu.make_async_copy(wd_hbm.at[ex], wd_st, wsem.at[1]))

    def rows(blk, size):
        return pl.ds(pl.multiple_of((b0 + blk) * MOE_BM, MOE_BM), size)

    def x_copy(kk, slot):
        return pltpu.make_async_copy(x_hbm.at[rows(FFN_PAIR * kk, big)], xbuf.at[slot], xsem.at[slot])

    def y_copy(kk, slot):
        return pltpu.make_async_copy(ybuf.at[slot], y_hbm.at[rows(FFN_PAIR * kk, big)], ysem.at[slot])

    def xt_copy():
        return pltpu.make_async_copy(x_hbm.at[rows(n - 1, MOE_BM)], xtail, tsem.at[0])

    def yt_copy():
        return pltpu.make_async_copy(ytail, y_hbm.at[rows(n - 1, MOE_BM)], tsem.at[1])

    def ffn(x):
        hu = jnp.dot(x.astype(BF16), wgu_bf[...], preferred_element_type=F32) + bgu_ref[0]
        g = jnp.minimum(hu[:, :dff], SWIGLU_LIMIT)
        u = jnp.clip(hu[:, dff:], -SWIGLU_LIMIT, SWIGLU_LIMIT)
        a = g * jax.nn.sigmoid(SWIGLU_ALPHA * g) * (u + 1.0)
        return jnp.dot(a.astype(BF16), wd_bf[...], preferred_element_type=F32) + bd_ref[0]

    @pl.when(e == 0)
    def _():
        for c in w_copies(0):
            c.start()

    @pl.when(jnp.logical_and(n > 0, odd == 1))
    def _():
        xt_copy().start()

    @pl.when(n2 > 0)
    def _():
        x_copy(0, 0).start()

    @pl.when(n2 > 1)
    def _():
        x_copy(1, 1).start()

    for c in w_copies(e):
        c.wait()

    @pl.when(n > 0)
    def _():
        wgu_bf[...] = wgu_st[...].astype(BF16)
        wd_bf[...] = wd_st[...].astype(BF16)

    @pl.when(e + 1 < N_EXPERTS)
    def _():
        for c in w_copies(e + 1):
            c.start()

    def drain(nblocks):
        m2 = nblocks // FFN_PAIR

        def y_wait(slot):
            pltpu.make_async_copy(ybuf.at[slot], y_hbm.at[pl.ds(0, big)], ysem.at[slot]).wait()

        @pl.when(m2 >= 2)
        def _():
            y_wait(m2 % 2)

        @pl.when(m2 >= 1)
        def _():
            y_wait((m2 - 1) % 2)

        @pl.when(nblocks % FFN_PAIR == 1)
        def _():
            pltpu.make_async_copy(ytail, y_hbm.at[pl.ds(0, MOE_BM)], tsem.at[1]).wait()

    @pl.when(e >= 1)
    def _():
        drain(nb_ref[jnp.maximum(e - 1, 0)])

    @pl.when(n > 0)
    def _():
        def body(kk, carry):
            slot = kk % 2

            @pl.when(jnp.logical_and(kk >= 1, kk + 1 < n2))
            def _():
                x_copy(kk + 1, 1 - slot).start()

            x_copy(kk, slot).wait()

            @pl.when(kk >= 2)
            def _():
                y_copy(kk - 2, slot).wait()

            ybuf[slot] = ffn(xbuf[slot])
            y_copy(kk, slot).start()
            return carry
        lax.fori_loop(0, n2, body, 0)

        @pl.when(odd == 1)
        def _():
            xt_copy().wait()
            ytail[...] = ffn(xtail[...])
            yt_copy().start()

    @pl.when(e == N_EXPERTS - 1)
    def _():
        drain(n)
        ytail[...] = jnp.zeros_like(ytail)

        def fill(wait):
            def body(k, carry):
                cp = pltpu.make_async_copy(ytail, y_hbm.at[pl.ds(pl.multiple_of(k * MOE_BM, MOE_BM), MOE_BM)],
                                           tsem.at[1])
                if wait:
                    cp.wait()
                else:
                    cp.start()
                return carry
            lax.fori_loop(b0 + n, n_blocks, body, 0)
        fill(False)
        fill(True)


def _expert_ffn(x_sorted, blk_start, nblk, w_gate_up, b_gate_up, w_down, b_down):
    n_rows, d = x_sorted.shape
    ne, _, dff2 = w_gate_up.shape
    dff = dff2 // 2
    any_spec = pl.BlockSpec(memory_space=pl.ANY)
    big = FFN_PAIR * MOE_BM
    grid_spec = pltpu.PrefetchScalarGridSpec(
        num_scalar_prefetch=2,
        grid=(ne,),
        in_specs=[
            any_spec,
            any_spec,
            pl.BlockSpec((1, 1, dff2), lambda e, *_: (e, 0, 0)),
            any_spec,
            pl.BlockSpec((1, 1, d), lambda e, *_: (e, 0, 0)),
        ],
        out_specs=any_spec,
        scratch_shapes=[
            pltpu.VMEM((2, big, d), F32), pltpu.VMEM((2, big, d), F32),
            pltpu.VMEM((MOE_BM, d), F32), pltpu.VMEM((MOE_BM, d), F32),
            pltpu.SemaphoreType.DMA((2,)), pltpu.SemaphoreType.DMA((2,)), pltpu.SemaphoreType.DMA((2,)),
            pltpu.VMEM((d, dff2), F32), pltpu.VMEM((dff, d), F32), pltpu.SemaphoreType.DMA((2,)),
            pltpu.VMEM((d, dff2), BF16), pltpu.VMEM((dff, d), BF16),
        ],
    )
    return pl.pallas_call(
        functools.partial(_expert_ffn_kernel, n_blocks=n_rows // MOE_BM),
        grid_spec=grid_spec,
        out_shape=jax.ShapeDtypeStruct((n_rows, d), F32),
        compiler_params=_cparams(("arbitrary",)),
        name="expert_ffn",
    )(blk_start, nblk, x_sorted, w_gate_up, b_gate_up.reshape(ne, 1, dff2), w_down, b_down.reshape(ne, 1, d))


def _combine_kernel(d_ref, n_ref, o_ref, r_ref, ls_ref, gate_ref, xp_ref, xs_ref, g_ref, y_hbm, yp_ref, ys_ref,
                    buf, sem, *, n_tiles, n_prompt_tiles):
    j = pl.program_id(0)

    @pl.when(j == 0)
    def _():
        buf[...] = jnp.zeros_like(buf)

    @pl.when(j < n_tiles)
    def _():
        _tile_strips(j, j % 2, d_ref, n_ref, o_ref, buf, y_hbm, sem, to_hbm=False)

    @pl.when(j >= 1)
    def _():
        slot = (j - 1) % 2
        _tile_wait(r_ref[j - 1], slot, buf, y_hbm, sem)
        r_hi = buf[slot].astype(BF16)
        lane = lax.broadcasted_iota(jnp.int32, (TILE_TOKENS, TILE_ROWS), 1)
        pick = jnp.zeros((TILE_TOKENS, TILE_ROWS), F32)
        for k in range(TOP_K):
            pick = pick + jnp.where(lane == ls_ref[:, k:k + 1], gate_ref[:, k:k + 1], 0.0)
        p_hi = pick.astype(BF16)
        p_lo = (pick - p_hi.astype(F32)).astype(BF16)
        moe = jnp.dot(p_hi, r_hi, preferred_element_type=F32) + jnp.dot(p_lo, r_hi, preferred_element_type=F32)
        is_prompt = j - 1 < n_prompt_tiles
        y = _rms(jnp.where(is_prompt, xp_ref[...], xs_ref[...]) + moe, g_ref[...])

        @pl.when(is_prompt)
        def _():
            yp_ref[...] = y

        @pl.when(jnp.logical_not(is_prompt))
        def _():
            ys_ref[...] = y


def _combine(y_sorted, x2p, x2s, ls, gates, tabs, g_final):
    tp, d = x2p.shape
    npt = tp // TILE_TOKENS
    assert x2s.shape[0] == TILE_TOKENS
    nt = npt + 1
    tile = lambda j: jnp.clip(j - 1, 0, nt - 1)
    ptile = lambda j: jnp.clip(j - 1, 0, npt - 1)
    grid_spec = pltpu.PrefetchScalarGridSpec(
        num_scalar_prefetch=4,
        grid=(nt + 1,),
        in_specs=[
            pl.BlockSpec((TILE_TOKENS, TOP_K), lambda j, *_: (tile(j), 0)),
            pl.BlockSpec((TILE_TOKENS, TOP_K), lambda j, *_: (tile(j), 0)),
            pl.BlockSpec((TILE_TOKENS, d), lambda j, *_: (ptile(j), 0)),
            pl.BlockSpec((TILE_TOKENS, d), lambda j, *_: (0, 0)),
            pl.BlockSpec((1, d), lambda j, *_: (0, 0)),
            pl.BlockSpec(memory_space=pl.ANY),
        ],
        out_specs=[
            pl.BlockSpec((TILE_TOKENS, d), lambda j, *_: (ptile(j), 0)),
            pl.BlockSpec((TILE_TOKENS, d), lambda j, *_: (0, 0)),
        ],
        scratch_shapes=[pltpu.VMEM((2, TILE_ROWS, d), F32), pltpu.SemaphoreType.DMA((2,))],
    )
    return pl.pallas_call(
        functools.partial(_combine_kernel, n_tiles=nt, n_prompt_tiles=npt),
        grid_spec=grid_spec,
        out_shape=[jax.ShapeDtypeStruct((tp, d), F32), jax.ShapeDtypeStruct((TILE_TOKENS, d), F32)],
        compiler_params=_cparams(("arbitrary",)),
        name="moe_combine",
    )(tabs["d"], tabs["n"], tabs["o"], tabs["rows"], ls, gates, x2p, x2s, g_final.reshape(1, d), y_sorted)


def _route_kernel(lp_ref, ls_in_ref, gate_ref, row_ref, cnt_ref, *, n_prompt_tiles, n_real_last):
    i = pl.program_id(0)
    is_prompt = i < n_prompt_tiles
    logits = jnp.where(is_prompt, lp_ref[...], ls_in_ref[...])
    tok = lax.broadcasted_iota(jnp.int32, (1, TILE_TOKENS), 1)
    real = jnp.logical_or(is_prompt, tok < n_real_last)
    eid = lax.broadcasted_iota(jnp.int32, logits.shape, 0)
    work = logits
    vals, sels = [], []
    for _ in range(TOP_K):
        v = jnp.max(work, axis=0, keepdims=True)
        idx = jnp.min(jnp.where(work == v, eid, N_EXPERTS), axis=0, keepdims=True)
        vals.append(v)
        sels.append(jnp.logical_and(eid == idx, real))
        work = jnp.where(eid == idx, -jnp.inf, work)
    es = [jnp.exp(v - vals[0]) for v in vals]
    denom = es[0] + es[1] + es[2] + es[3]
    chosen = jnp.zeros(logits.shape, F32)
    for sel in sels:
        chosen = chosen + jnp.where(sel, 1.0, 0.0)
    r = lax.broadcasted_iota(jnp.int32, (TILE_TOKENS, TILE_TOKENS), 0)
    c = lax.broadcasted_iota(jnp.int32, (TILE_TOKENS, TILE_TOKENS), 1)
    rank = jnp.dot(chosen.astype(BF16), jnp.where(r < c, 1.0, 0.0).astype(BF16), preferred_element_type=F32)
    count = jnp.sum(chosen, axis=1, keepdims=True)
    strip = jnp.floor((count + (SUBLANES - 1)) * (1.0 / SUBLANES)) * SUBLANES
    er = lax.broadcasted_iota(jnp.int32, (N_EXPERTS, N_EXPERTS), 0)
    ec = lax.broadcasted_iota(jnp.int32, (N_EXPERTS, N_EXPERTS), 1)
    start = jnp.dot(jnp.where(ec < er, 1.0, 0.0).astype(BF16),
                    jnp.broadcast_to(strip, (N_EXPERTS, LANES)).astype(BF16), preferred_element_type=F32)[:, 0:1]
    place = start + rank
    for k in range(TOP_K):
        row_k = jnp.sum(jnp.where(sels[k], place, 0.0), axis=0, keepdims=True)
        row_ref[k:k + 1, :] = jnp.where(real, row_k, -1.0).astype(jnp.int32)
        gate_ref[k:k + 1, :] = jnp.where(real, es[k] / denom, 0.0)
    cnt_ref[0] = strip.astype(jnp.int32)


def _route(lgp_t, lgs):
    ne, tp = lgp_t.shape
    ts = lgs.shape[0]
    assert tp % TILE_TOKENS == 0 and ts <= TILE_TOKENS and ne == N_EXPERTS
    npt = tp // TILE_TOKENS
    nt = npt + 1
    lgs_t = jnp.pad(lgs.T, ((0, 0), (0, TILE_TOKENS - ts)))
    tile4 = pl.BlockSpec((TOP_K, TILE_TOKENS), lambda i: (0, i))
    gates_t, ls_t, n_tile = pl.pallas_call(
        functools.partial(_route_kernel, n_prompt_tiles=npt, n_real_last=ts),
        grid=(nt,),
        in_specs=[pl.BlockSpec((ne, TILE_TOKENS), lambda i: (0, jnp.minimum(i, npt - 1))),
                  pl.BlockSpec((ne, TILE_TOKENS), lambda i: (0, 0))],
        out_specs=[tile4, tile4, pl.BlockSpec((1, ne, 1), lambda i: (i, 0, 0))],
        out_shape=[jax.ShapeDtypeStruct((TOP_K, nt * TILE_TOKENS), F32),
                   jax.ShapeDtypeStruct((TOP_K, nt * TILE_TOKENS), jnp.int32),
                   jax.ShapeDtypeStruct((nt, ne, 1), jnp.int32)],
        compiler_params=_cparams(("arbitrary",)),
        name="moe_route",
    )(lgp_t, lgs_t)
    n_tile = n_tile.reshape(nt, ne)
    up = lambda a, m: (a + m - 1) // m * m
    o_tile = jnp.cumsum(n_tile, axis=1) - n_tile
    rows_e = jnp.sum(n_tile, axis=0)
    nblk = up(rows_e, MOE_BM) // MOE_BM
    blk_end = jnp.cumsum(nblk)
    pstart = (blk_end - nblk) * MOE_BM
    d_tile = pstart[None, :] + jnp.cumsum(n_tile, axis=0) - n_tile
    t = tp + ts
    nb = (t * TOP_K + N_EXPERTS * nt * (SUBLANES - 1) + N_EXPERTS * (MOE_BM - 1) + MOE_BM - 1) // MOE_BM
    i32 = lambda a: a.astype(jnp.int32).reshape(-1)
    tabs = {"d": i32(d_tile), "n": i32(n_tile), "o": i32(o_tile), "rows": i32(jnp.sum(n_tile, axis=1)),
            "pad_start": i32(pstart + rows_e), "pad_len": i32(nblk * MOE_BM - rows_e),
            "blk_start": i32(blk_end - nblk), "nblk": i32(nblk)}
    return gates_t, ls_t, tabs, nb * MOE_BM


def kernel(x_prompt, x_sample, cache_k_win, cache_v_win, state_pool, cache_mem_k, cache_mem_v, mem_prompt,
           rel_bias, g_mix, w_in, w_pool, pool_scale, w_out, g_mem, g_x, w_xq, w_xk, w_xv, w_xo, g_ff,
           w_router, b_router, w_gate_up, b_gate_up, w_down, b_down, g_final):
    depth = g_mix.shape[0]
    assert depth == 1
    l = 0
    batch, seq, d = x_prompt.shape
    bd, n_new, _ = x_sample.shape
    n_mem = mem_prompt.shape[1]
    bw = d - A_WIDTH
    n_hist = cache_k_win.shape[2]
    n_pool = state_pool.shape[2]
    tp, ts = batch * seq, bd * n_new

    w_in_b = w_in[l].astype(BF16)
    w_pool_b = w_pool[l].astype(BF16)
    w_xkv_b = jnp.concatenate([w_xk[l], w_xv[l]], axis=1).astype(BF16)
    wts = {
        "w_out_a": w_out[l, :A_WIDTH].astype(BF16), "w_out_b": w_out[l, A_WIDTH:].astype(BF16),
        "g_x": g_x[l].reshape(1, d), "w_xq": w_xq[l].astype(BF16), "w_xo": w_xo[l].astype(BF16),
        "g_ff": g_ff[l].reshape(1, d), "w_router": w_router[l], "b_router": b_router[l].reshape(1, N_EXPERTS),
    }
    band_bias = _band_bias(rel_bias)
    widths = (A_WIDTH, A_WIDTH, A_WIDTH, bw)
    scales = (A_HEAD_DIM ** -0.5, 1.0, 1.0, 1.0)

    xp = x_prompt.reshape(tp, d)
    w_kv_t = jnp.transpose(w_in[l][:, A_WIDTH:3 * A_WIDTH].reshape(d, 2, A_WIDTH), (1, 2, 0)).astype(BF16)
    q, k, v, u, k_t, v_t = _in_proj(xp, g_mix[l], w_in_b, w_kv_t, widths, scales, batch, seq, tm=512)
    o_a = _dil_attn(q, k, v, band_bias, batch, seq)
    o_b = _pool_prompt(u, w_pool_b, pool_scale[l], batch, seq, tm=512)
    mk, mv, mk_b, mv_b = _mem_kv(mem_prompt.reshape(batch * n_mem, d), g_mem[l], w_xkv_b, X_HEADS, tm=512)
    x2p, hp, lgp = _mix_xattn(xp, o_a, o_b, mk_b, mv_b, wts, groups=batch, rows_per_group=seq, tm=512,
                              rows_per_sub=None, keys_per_sub=None, logits_t=True)

    xs = x_sample.reshape(ts, d)
    qs, ks, vs, us = _norm_proj(xs, g_mix[l], w_in_b, widths, scales, tm=ts)
    ck, cv = cache_k_win[l], cache_v_win[l]
    new3 = lambda a: a.reshape(bd, n_new, A_WIDTH)
    to_t = lambda c: jnp.transpose(c, (0, 2, 3, 1)).reshape(bd, A_WIDTH, n_hist)
    from_t = lambda c: jnp.transpose(c.reshape(bd, A_HEADS, A_HEAD_DIM, n_hist), (0, 3, 1, 2))[None]
    bias_c, bias_n = _decode_bias(rel_bias, n_hist, n_new)
    o_as, ck_t, cv_t = _decode_attn(new3(qs), new3(ks), new3(vs), to_t(ck), to_t(cv), bias_c, bias_n,
                                    heads_per_step=A_HEADS)
    o_as = o_as.reshape(ts, A_WIDTH)
    us3 = us.reshape(bd, n_new, bw)
    hist = jnp.concatenate([jnp.zeros((bd, POOL_HIST - n_pool, bw), F32), state_pool[l]], axis=1)
    cur = jnp.concatenate([us3, jnp.zeros((bd, SUBLANES - n_new, bw), F32)], axis=1)
    o_bs = _pool_dec(hist, cur, n_pool, w_pool_b, pool_scale[l])[:, :n_new].reshape(ts, bw)
    sub = 8
    xhd = d // X_HEADS
    x2s, hs, lgs = _mix_xattn(xs, o_as, o_bs, cache_mem_k[l].reshape(bd * n_mem, X_HEADS, xhd),
                              cache_mem_v[l].reshape(bd * n_mem, X_HEADS, xhd), wts, groups=bd // sub,
                              rows_per_group=sub * n_new, tm=sub * n_new, rows_per_sub=n_new, keys_per_sub=n_mem,
                              logits_t=False)

    assert ts <= TILE_TOKENS and w_gate_up.shape[1] == N_EXPERTS
    pad_s = lambda a: jnp.pad(a, ((0, TILE_TOKENS - ts), (0, 0)))
    gates_t, ls_t, tabs, n_rows = _route(lgp, lgs)
    gates, ls = gates_t.T, ls_t.T
    x_sorted = _dispatch(hp, pad_s(hs), ls_t, tabs, n_rows)
    y_sorted = _expert_ffn(x_sorted, tabs["blk_start"], tabs["nblk"], w_gate_up[l], b_gate_up[l], w_down[l],
                           b_down[l])
    y_prompt, y_sample = _combine(y_sorted, x2p, pad_s(x2s), ls, gates, tabs, g_final)
    y_prompt = y_prompt.reshape(batch, seq, d)
    y_sample = y_sample[:ts].reshape(bd, n_new, d)

    from_tp = lambda c: jnp.transpose(c.reshape(batch, A_HEADS, A_HEAD_DIM, seq), (0, 3, 1, 2))[None]
    k_win_prompt, v_win_prompt = from_tp(k_t), from_tp(v_t)
    pool_prompt = u.reshape(batch, seq, bw)[:, seq - n_pool:][None]
    mem_k_prompt = mk.reshape(1, batch, n_mem, X_HEADS, xhd)
    mem_v_prompt = mv.reshape(1, batch, n_mem, X_HEADS, xhd)
    k_win_sample, v_win_sample = from_t(ck_t), from_t(cv_t)
    pool_sample = jnp.concatenate([state_pool[l][:, n_new:], us3], axis=1)[None]
    return (y_prompt, y_sample, k_win_prompt, v_win_prompt, pool_prompt, mem_k_prompt, mem_v_prompt,
            k_win_sample, v_win_sample, pool_sample)
```

```python
import functools
import math

import numpy as np
import jax
import jax.numpy as jnp
from jax import lax
from jax.experimental import pallas as pl
from jax.experimental.pallas import tpu as pltpu

F32 = jnp.float32
BF16 = jnp.bfloat16

LANES = 128
SUBLANES = 8
VMEM_LIMIT_BYTES = 56 * 1024 * 1024

A_HEADS = 8
A_HEAD_DIM = 64
A_WIDTH = A_HEADS * A_HEAD_DIM
DILATED = ((128, 1), (512, 4), (2048, 16))
QB = 128
POOL_WINDOWS = (2, 4, 8, 16)
POOL_HIST = 16
X_HEADS = 4
N_EXPERTS = 32
TOP_K = 4
SWIGLU_LIMIT = 7.0
SWIGLU_ALPHA = 1.702
N_BUCKETS = 32
RMS_EPS = 1e-6
NEG = -1e30

MOE_BM = 256


def _cparams(sem):
    return pltpu.CompilerParams(dimension_semantics=sem, vmem_limit_bytes=VMEM_LIMIT_BYTES)


def _rms(x, g):
    return x * lax.rsqrt(jnp.mean(x * x, axis=-1, keepdims=True) + RMS_EPS) * g


def _norm_proj_kernel(x_ref, g_ref, w_ref, *o_refs, widths, scales):
    hb = _rms(x_ref[...], g_ref[...]).astype(BF16)
    off = 0
    for o_ref, width, scale in zip(o_refs, widths, scales):
        p = jnp.dot(hb, w_ref[:, off:off + width], preferred_element_type=F32)
        o_ref[...] = p if scale == 1.0 else p * scale
        off += width


def _norm_proj(x, g, w_bf16, widths, scales, tm):
    t, d = x.shape
    n = w_bf16.shape[1]
    assert sum(widths) == n and t % tm == 0
    return pl.pallas_call(
        functools.partial(_norm_proj_kernel, widths=widths, scales=scales),
        grid=(t // tm,),
        in_specs=[
            pl.BlockSpec((tm, d), lambda i: (i, 0)),
            pl.BlockSpec((1, d), lambda i: (0, 0)),
            pl.BlockSpec((d, n), lambda i: (0, 0)),
        ],
        out_specs=[pl.BlockSpec((tm, wd), lambda i: (i, 0)) for wd in widths],
        out_shape=[jax.ShapeDtypeStruct((t, wd), F32) for wd in widths],
        compiler_params=_cparams(("arbitrary",)),
        name="norm_proj",
    )(x, g.reshape(1, d), w_bf16)


def _mem_kv_kernel(x_ref, g_ref, w_ref, k_ref, v_ref, kb_ref, vb_ref, *, heads):
    hb = _rms(x_ref[...], g_ref[...]).astype(BF16)
    d = x_ref.shape[1]
    hd = d // heads
    for j, (o_ref, ob_ref) in enumerate(((k_ref, kb_ref), (v_ref, vb_ref))):
        p = jnp.dot(hb, w_ref[:, j * d:(j + 1) * d], preferred_element_type=F32)
        ob_ref[...] = p.astype(BF16)
        for h in range(heads):
            o_ref[:, h, :] = p[:, h * hd:(h + 1) * hd]


def _mem_kv(x, g, w_bf16, heads, tm):
    t, d = x.shape
    assert w_bf16.shape == (d, 2 * d) and t % tm == 0
    hd = d // heads
    o3 = pl.BlockSpec((tm, heads, hd), lambda i: (i, 0, 0))
    o2 = pl.BlockSpec((tm, d), lambda i: (i, 0))
    return pl.pallas_call(
        functools.partial(_mem_kv_kernel, heads=heads),
        grid=(t // tm,),
        in_specs=[
            pl.BlockSpec((tm, d), lambda i: (i, 0)),
            pl.BlockSpec((1, d), lambda i: (0, 0)),
            pl.BlockSpec((d, 2 * d), lambda i: (0, 0)),
        ],
        out_specs=[o3, o3, o2, o2],
        out_shape=[jax.ShapeDtypeStruct((t, heads, hd), F32)] * 2 + [jax.ShapeDtypeStruct((t, d), BF16)] * 2,
        compiler_params=_cparams(("arbitrary",)),
        name="mem_kv",
    )(x, g.reshape(1, d), w_bf16)


def _in_proj_kernel(x_ref, g_ref, w_ref, wt_ref, *o_refs, widths, scales, n_t):
    hb = _rms(x_ref[...], g_ref[...]).astype(BF16)
    off = 0
    for o_ref, width, scale in zip(o_refs, widths, scales):
        p = jnp.dot(hb, w_ref[:, off:off + width], preferred_element_type=F32)
        o_ref[...] = p if scale == 1.0 else p * scale
        off += width
    for j in range(n_t):
        o_refs[len(widths) + j][0] = lax.dot_general(wt_ref[j], hb, (((1,), (1,)), ((), ())),
                                                    preferred_element_type=F32)


def _in_proj(x, g, w_bf16, wt_bf16, widths, scales, batch, seq, tm):
    t, d = x.shape
    n = w_bf16.shape[1]
    n_t, wt_width, _ = wt_bf16.shape
    nt = seq // tm
    assert sum(widths) == n and seq % tm == 0 and t == batch * seq
    return pl.pallas_call(
        functools.partial(_in_proj_kernel, widths=widths, scales=scales, n_t=n_t),
        grid=(t // tm,),
        in_specs=[
            pl.BlockSpec((tm, d), lambda i: (i, 0)),
            pl.BlockSpec((1, d), lambda i: (0, 0)),
            pl.BlockSpec((d, n), lambda i: (0, 0)),
            pl.BlockSpec(wt_bf16.shape, lambda i: (0, 0, 0)),
        ],
        out_specs=[pl.BlockSpec((tm, wd), lambda i: (i, 0)) for wd in widths]
        + [pl.BlockSpec((1, wt_width, tm), lambda i: (i // nt, 0, i % nt))] * n_t,
        out_shape=[jax.ShapeDtypeStruct((t, wd), F32) for wd in widths]
        + [jax.ShapeDtypeStruct((batch, wt_width, seq), F32)] * n_t,
        compiler_params=_cparams(("arbitrary",)),
        name="in_proj",
    )(x, g.reshape(1, d), w_bf16, wt_bf16)


def _t5_bucket_np(n, max_dist):
    max_exact = N_BUCKETS // 2
    nf = np.maximum(n, 1).astype(np.float32)
    large = max_exact + (
        np.log(nf / np.float32(max_exact)) / np.float32(math.log(max_dist / max_exact))
        * np.float32(N_BUCKETS - max_exact)
    ).astype(np.int32)
    return np.where(n < max_exact, n, np.minimum(large, N_BUCKETS - 1))


def _band_bias(rel_bias):
    max_dist = max(w for w, _ in DILATED)
    qi = np.arange(QB)[:, None]
    ki = np.arange(2 * QB)[None, :]
    j = qi + QB - ki
    tabs = []
    for window, dil in DILATED:
        steps = window // dil
        in_band = (j >= 0) & (j <= steps)
        bucket = _t5_bucket_np(np.clip(j, 0, steps) * dil, max_dist)
        onehot = (bucket[..., None] == np.arange(N_BUCKETS)).astype(np.float32)
        b = jnp.einsum("qkb,bh->hqk", onehot, rel_bias.astype(F32), precision=lax.Precision.HIGHEST)
        tabs.append(jnp.where(in_band[None], b, NEG))
    return jnp.stack(tabs)


def _decode_bias(rel_bias, n_hist, n_new):
    max_dist = max(w for w, _ in DILATED)
    t = np.arange(n_new)[:, None]
    tabs_c, tabs_n = [], []
    for window, dil in DILATED:
        out = []
        for dist in (n_hist + t - np.arange(n_hist)[None, :], t - np.arange(n_new)[None, :]):
            ok = (dist >= 0) & (dist % dil == 0) & (dist <= window)
            onehot = (_t5_bucket_np(np.clip(dist, 0, window), max_dist)[..., None] == np.arange(N_BUCKETS))
            b = jnp.einsum("tpb,bh->htp", onehot.astype(np.float32), rel_bias.astype(F32),
                           precision=lax.Precision.HIGHEST)
            out.append(jnp.where(ok[None], b, NEG))
        tabs_c.append(out[0])
        tabs_n.append(out[1])
    return jnp.stack(tabs_c), jnp.stack(tabs_n)


def _decode_attn_kernel(q_ref, kn_ref, vn_ref, knt_ref, vnt_ref, ck_ref, cv_ref, bc_ref, bn_ref,
                        o_ref, ok_ref, ov_ref, *, n_new, heads):
    n_hist = ck_ref.shape[2]
    hd = A_HEAD_DIM
    outs = []
    for h in range(heads):
        rows = slice(h * hd, (h + 1) * hd)
        q = q_ref[0, :, rows].astype(BF16)
        kt = ck_ref[0, rows, :].astype(BF16)
        vt = cv_ref[0, rows, :].astype(BF16)
        lc = jnp.dot(q, kt, preferred_element_type=F32)
        ln = lax.dot_general(q, kn_ref[0, :, rows].astype(BF16), (((1,), (1,)), ((), ())),
                             preferred_element_type=F32)
        vn = vn_ref[0, :, rows].astype(BF16)
        o_br, lse_br = [], []
        for br in range(len(DILATED)):
            bl = lc + bc_ref[br, h]
            bln = ln + bn_ref[br, h]
            m = jnp.maximum(jnp.max(bl, axis=-1, keepdims=True), jnp.max(bln, axis=-1, keepdims=True))
            p = jnp.exp(bl - m)
            pn = jnp.exp(bln - m)
            s = jnp.sum(p, axis=-1, keepdims=True) + jnp.sum(pn, axis=-1, keepdims=True)
            o = lax.dot_general(p.astype(BF16), vt, (((1,), (1,)), ((), ())), preferred_element_type=F32)
            o = o + jnp.dot(pn.astype(BF16), vn, preferred_element_type=F32)
            o_br.append(o / s)
            lse_br.append(m + jnp.log(s))
        m = jnp.maximum(jnp.maximum(lse_br[0], lse_br[1]), lse_br[2])
        es = [jnp.exp(l - m) for l in lse_br]
        outs.append((es[0] * o_br[0] + es[1] * o_br[1] + es[2] * o_br[2]) / (es[0] + es[1] + es[2]))
    o_ref[0] = jnp.concatenate(outs, axis=-1)

    lane = lax.broadcasted_iota(jnp.int32, (ck_ref.shape[1], LANES), 1)
    for c_ref, nt_ref, dst in ((ck_ref, knt_ref, ok_ref), (cv_ref, vnt_ref, ov_ref)):
        shifted = pltpu.roll(c_ref[0], n_hist - n_new, axis=1)
        tail = shifted[:, n_hist - LANES:]
        for j in range(n_new):
            tail = jnp.where(lane == LANES - n_new + j, nt_ref[0, :, j:j + 1], tail)
        dst[0, :, :n_hist - LANES] = shifted[:, :n_hist - LANES]
        dst[0, :, n_hist - LANES:] = tail


def _decode_attn(q, k_new, v_new, ck_t, cv_t, bias_c, bias_n, heads_per_step):
    bd, n_new, aw = q.shape
    n_hist = ck_t.shape[2]
    gw = heads_per_step * A_HEAD_DIM
    ng = aw // gw
    row = pl.BlockSpec((1, n_new, gw), lambda b, g: (b, 0, g))
    col = pl.BlockSpec((1, gw, n_new), lambda b, g: (b, g, 0))
    cache = pl.BlockSpec((1, gw, n_hist), lambda b, g: (b, g, 0))
    nbr = len(DILATED)
    return pl.pallas_call(
        functools.partial(_decode_attn_kernel, n_new=n_new, heads=heads_per_step),
        grid=(bd, ng),
        in_specs=[row, row, row, col, col, cache, cache,
                  pl.BlockSpec((nbr, heads_per_step, n_new, n_hist), lambda b, g: (0, g, 0, 0)),
                  pl.BlockSpec((nbr, heads_per_step, n_new, n_new), lambda b, g: (0, g, 0, 0))],
        out_specs=[row, cache, cache],
        out_shape=[jax.ShapeDtypeStruct(q.shape, F32), jax.ShapeDtypeStruct(ck_t.shape, F32),
                   jax.ShapeDtypeStruct(cv_t.shape, F32)],
        compiler_params=_cparams(("arbitrary", "arbitrary")),
        name="decode_attn",
    )(q, k_new, v_new, jnp.swapaxes(k_new, 1, 2), jnp.swapaxes(v_new, 1, 2), ck_t, cv_t, bias_c, bias_n)


def _dil_attn_kernel(q_ref, k_ref, v_ref, bias_ref, o_ref, obr_ref, lbr_ref, *, seq):
    lane = lax.broadcasted_iota(jnp.int32, (QB, LANES), 1)
    head0 = lane < A_HEAD_DIM

    def rows(ref, start, n, dil):
        if dil == 1:
            return ref[pl.ds(start, n), :]
        return ref[pl.ds(start, n, stride=dil), :]

    def block(br, dil, qstart, kstart, nk):
        qs = rows(q_ref, qstart, QB, dil)
        ks = rows(k_ref, kstart, nk, dil).astype(BF16)
        vs = rows(v_ref, kstart, nk, dil).astype(BF16)
        outs, lses = [], []
        for hh in range(2):
            keep = head0 if hh == 0 else jnp.logical_not(head0)
            qm = jnp.where(keep, qs, 0.0).astype(BF16)
            logits = lax.dot_general(qm, ks, (((1,), (1,)), ((), ())), preferred_element_type=F32)
            logits = logits + bias_ref[br, hh, :, 2 * QB - nk:]
            m = jnp.max(logits, axis=-1, keepdims=True)
            p = jnp.exp(logits - m)
            s = jnp.sum(p, axis=-1, keepdims=True)
            o = jnp.dot(p.astype(BF16), vs, preferred_element_type=F32)
            outs.append(o / s)
            lses.append(jnp.broadcast_to(m + jnp.log(s), (QB, LANES)))
        o = jnp.where(head0, outs[0], outs[1])
        lse = jnp.where(head0, lses[0], lses[1])
        if dil == 1:
            obr_ref[br, pl.ds(qstart, QB), :] = o
            lbr_ref[br, pl.ds(qstart, QB), :] = lse
        else:
            obr_ref[br, pl.ds(qstart, QB, stride=dil), :] = o
            lbr_ref[br, pl.ds(qstart, QB, stride=dil), :] = lse

    for br, (window, dil) in enumerate(DILATED):
        assert window // dil == QB
        nblk = seq // (dil * QB)
        for r in range(dil):
            block(br, dil, r, r, QB)
            for i in range(1, nblk):
                qstart = r + dil * QB * i
                block(br, dil, qstart, qstart - dil * QB, 2 * QB)

    l0, l1, l2 = lbr_ref[0], lbr_ref[1], lbr_ref[2]
    m = jnp.maximum(jnp.maximum(l0, l1), l2)
    e0, e1, e2 = jnp.exp(l0 - m), jnp.exp(l1 - m), jnp.exp(l2 - m)
    acc = e0 * obr_ref[0] + e1 * obr_ref[1] + e2 * obr_ref[2]
    o_ref[...] = acc / (e0 + e1 + e2)


def _dil_attn(q, k, v, band_bias, batch, seq):
    t, aw = q.shape
    npair = aw // LANES
    bias = band_bias.reshape(len(DILATED), npair, 2, QB, 2 * QB)
    spec = pl.BlockSpec((seq, LANES), lambda b, hp: (b, hp))
    return pl.pallas_call(
        functools.partial(_dil_attn_kernel, seq=seq),
        grid=(batch, npair),
        in_specs=[
            spec, spec, spec,
            pl.BlockSpec((len(DILATED), None, 2, QB, 2 * QB), lambda b, hp: (0, hp, 0, 0, 0)),
        ],
        out_specs=spec,
        out_shape=jax.ShapeDtypeStruct((t, aw), F32),
        scratch_shapes=[
            pltpu.VMEM((len(DILATED), seq, LANES), F32),
            pltpu.VMEM((len(DILATED), seq, LANES), F32),
        ],
        compiler_params=_cparams(("arbitrary", "arbitrary")),
        name="dil_attn",
    )(q, k, v, bias)


def _pool_kernel(hist_ref, cur_ref, w_ref, scale_ref, o_ref, *, pos0_of_tile, tm):
    nb = cur_ref.shape[0]
    gdim = w_ref.shape[1]
    i = pl.program_id(1)
    pos0 = pos0_of_tile(i)
    have_hist = jnp.where(pos0 > 0, 1.0, 0.0).astype(F32)
    t = lax.broadcasted_iota(jnp.int32, (1, tm, 1), 1)
    for g, w in enumerate(POOL_WINDOWS):
        sl = slice(g * gdim, (g + 1) * gdim)
        cur = cur_ref[:, :, sl]
        ext = jnp.concatenate([hist_ref[:, :, sl] * have_hist, cur], axis=1)
        acc, span = ext, 1
        while span < w:
            n = acc.shape[1]
            acc = acc[:, span:n] + acc[:, 0:n - span]
            span *= 2
        wsum = acc[:, POOL_HIST + 1 - w:POOL_HIST + 1 - w + tm]
        cnt = jnp.minimum(pos0 + t + 1, w).astype(F32)
        d = (wsum / cnt - cur).astype(BF16).reshape(nb * tm, gdim)
        y = jnp.dot(d, w_ref[g], preferred_element_type=F32) * scale_ref[:, sl]
        o_ref[:, :, sl] = y.reshape(nb, tm, gdim)


def _pool_prompt(u, w_pool_bf16, pool_scale, batch, seq, tm):
    t, bw = u.shape
    nt = seq // tm
    hb = tm // POOL_HIST
    u3 = u.reshape(1, t, bw)
    out = pl.pallas_call(
        functools.partial(_pool_kernel, pos0_of_tile=lambda i: i * tm, tm=tm),
        grid=(batch, nt),
        in_specs=[
            pl.BlockSpec((1, POOL_HIST, bw), lambda b, i: (0, jnp.maximum((b * nt + i) * hb - 1, 0), 0)),
            pl.BlockSpec((1, tm, bw), lambda b, i: (0, b * nt + i, 0)),
            pl.BlockSpec(w_pool_bf16.shape, lambda b, i: (0, 0, 0)),
            pl.BlockSpec((1, bw), lambda b, i: (0, 0)),
        ],
        out_specs=pl.BlockSpec((1, tm, bw), lambda b, i: (0, b * nt + i, 0)),
        out_shape=jax.ShapeDtypeStruct((1, t, bw), F32),
        compiler_params=_cparams(("arbitrary", "arbitrary")),
        name="pool_prompt",
    )(u3, u3, w_pool_bf16, pool_scale.reshape(1, bw))
    return out.reshape(t, bw)


def _pool_dec(hist, cur, n_prev, w_pool_bf16, pool_scale):
    bd, tm, bw = cur.shape
    return pl.pallas_call(
        functools.partial(_pool_kernel, pos0_of_tile=lambda i: n_prev, tm=tm),
        grid=(1, 1),
        in_specs=[
            pl.BlockSpec(hist.shape, lambda b, i: (0, 0, 0)),
            pl.BlockSpec(cur.shape, lambda b, i: (0, 0, 0)),
            pl.BlockSpec(w_pool_bf16.shape, lambda b, i: (0, 0, 0)),
            pl.BlockSpec((1, bw), lambda b, i: (0, 0)),
        ],
        out_specs=pl.BlockSpec(cur.shape, lambda b, i: (0, 0, 0)),
        out_shape=jax.ShapeDtypeStruct(cur.shape, F32),
        compiler_params=_cparams(("arbitrary", "arbitrary")),
        name="pool_dec",
    )(hist, cur, w_pool_bf16, pool_scale.reshape(1, bw))


def _mix_xattn_kernel(x_ref, oa_ref, ob_ref, mk_ref, mv_ref, woa_ref, wob_ref, gx_ref, wq_ref, wo_ref,
                      gff_ref, wr_ref, br_ref, x2_ref, h_ref, lg_ref, *, rows_per_sub, keys_per_sub, logits_t, heads_in_rows):
    tm = x_ref.shape[0]
    nkv = mk_ref.shape[0]
    xh = wq_ref.shape[1] // X_HEADS
    x = x_ref[...]
    x = x + jnp.dot(oa_ref[...].astype(BF16), woa_ref[...], preferred_element_type=F32)
    x = x + jnp.dot(ob_ref[...].astype(BF16), wob_ref[...], preferred_element_type=F32)
    hq = _rms(x, gx_ref[...]).astype(BF16)
    if rows_per_sub is not None:
        rsub = (pl.program_id(1) * tm + lax.broadcasted_iota(jnp.int32, (tm, nkv), 0)) // rows_per_sub
        col = lax.broadcasted_iota(jnp.int32, (tm, nkv), 1)
        if heads_in_rows:
            same = rsub == (col // X_HEADS) // keys_per_sub
            col_head = col % X_HEADS
            k_all, v_all = mk_ref[...].astype(BF16), mv_ref[...].astype(BF16)
        else:
            same = rsub == col // keys_per_sub
    heads = []
    for h in range(X_HEADS):
        sl = slice(h * xh, (h + 1) * xh)
        q = jnp.dot(hq, wq_ref[:, sl], preferred_element_type=F32) * (xh ** -0.5)
        if heads_in_rows:
            kh, vh = k_all, v_all
        else:
            kh, vh = mk_ref[:, sl], mv_ref[:, sl]
        logits = lax.dot_general(q.astype(BF16), kh, (((1,), (1,)), ((), ())), preferred_element_type=F32)
        if rows_per_sub is not None:
            keep = jnp.logical_and(same, col_head == h) if heads_in_rows else same
            logits = jnp.where(keep, logits, NEG)
        m = jnp.max(logits, axis=-1, keepdims=True)
        p = jnp.exp(logits - m)
        s = jnp.sum(p, axis=-1, keepdims=True)
        o = jnp.dot(p.astype(BF16), vh, preferred_element_type=F32) / s
        heads.append(o.astype(BF16))
    x = x + jnp.dot(jnp.concatenate(heads, axis=-1), wo_ref[...], preferred_element_type=F32)
    x2_ref[...] = x
    hf = _rms(x, gff_ref[...])
    h_ref[...] = hf
    if logits_t:
        lg_ref[...] = lax.dot_general(wr_ref[...].astype(BF16), hf.astype(BF16), (((1,), (1,)), ((), ())),
                                      preferred_element_type=F32) + br_ref[...]
    else:
        lg_ref[...] = jnp.dot(hf.astype(BF16), wr_ref[...].astype(BF16), preferred_element_type=F32) + br_ref[...]


def _mix_xattn(x, oa, ob, mk, mv, w, groups, rows_per_group, tm, rows_per_sub, keys_per_sub, logits_t,
               heads_in_rows=False):
    t, d = x.shape
    nt = rows_per_group // tm
    nkv = mk.shape[0] // groups
    ne = w["w_router"].shape[1]
    row = lambda width: pl.BlockSpec((tm, width), lambda g, i: (g * nt + i, 0))
    const = lambda a: pl.BlockSpec(a.shape, lambda g, i: (0,) * a.ndim)
    kv = pl.BlockSpec((nkv, mk.shape[1]), lambda g, i: (g, 0))
    w_r, b_r = (w["w_router"].T, w["b_router"].T) if logits_t else (w["w_router"], w["b_router"])
    consts = [w["w_out_a"], w["w_out_b"], w["g_x"], w["w_xq"], w["w_xo"], w["g_ff"], w_r, b_r]
    lg_spec = pl.BlockSpec((ne, tm), lambda g, i: (0, g * nt + i)) if logits_t else row(ne)
    return pl.pallas_call(
        functools.partial(_mix_xattn_kernel, rows_per_sub=rows_per_sub, keys_per_sub=keys_per_sub,
                          logits_t=logits_t, heads_in_rows=heads_in_rows),
        grid=(groups, nt),
        in_specs=[row(d), row(oa.shape[1]), row(ob.shape[1]), kv, kv] + [const(a) for a in consts],
        out_specs=[row(d), row(d), lg_spec],
        out_shape=[jax.ShapeDtypeStruct((t, d), F32), jax.ShapeDtypeStruct((t, d), F32),
                   jax.ShapeDtypeStruct((ne, t) if logits_t else (t, ne), F32)],
        compiler_params=_cparams(("arbitrary", "arbitrary")),
        name="mix_xattn",
    )(x, oa, ob, mk, mv, *consts)


TILE_TOKENS = 256
TILE_ROWS = TOP_K * TILE_TOKENS + N_EXPERTS * (SUBLANES - 1)
assert TILE_ROWS % SUBLANES == 0
FFN_PAIR = 2


def _bits(lo, hi):
    return tuple(lo << j for j in range((hi // lo).bit_length()))


STRIP_LEN_BITS = _bits(SUBLANES, TILE_TOKENS)
PAD_LEN_BITS = _bits(SUBLANES, MOE_BM - SUBLANES)
TILE_LEN_BITS = _bits(SUBLANES, TILE_ROWS)


def _strip_dmas(n, bits, make_copy, wait=False):
    for bit in bits:
        @pl.when((n & bit) != 0)
        def _(bit=bit):
            c = make_copy(n & (bit - 1), bit)
            if wait:
                c.wait()
            else:
                c.start()


def _tile_strips(i, slot, d_ref, n_ref, o_ref, local, hbm, sem, to_hbm):
    for e in range(N_EXPERTS):
        l0 = pl.multiple_of(o_ref[i * N_EXPERTS + e], SUBLANES)
        g0 = pl.multiple_of(d_ref[i * N_EXPERTS + e], SUBLANES)

        def make_copy(off, size, l0=l0, g0=g0):
            lo = local.at[slot, pl.ds(pl.multiple_of(l0 + off, SUBLANES), size)]
            gl = hbm.at[pl.ds(pl.multiple_of(g0 + off, SUBLANES), size)]
            src, dst = (lo, gl) if to_hbm else (gl, lo)
            return pltpu.make_async_copy(src, dst, sem.at[slot])
        _strip_dmas(n_ref[i * N_EXPERTS + e], STRIP_LEN_BITS, make_copy)


def _tile_wait(rows, slot, local, hbm, sem):
    _strip_dmas(rows, TILE_LEN_BITS, lambda off, size: pltpu.make_async_copy(
        hbm.at[pl.ds(0, size)], local.at[slot, pl.ds(0, size)], sem.at[slot]), wait=True)


def _dispatch_kernel(d_ref, n_ref, o_ref, r_ref, ps_ref, pn_ref, lst_ref, hp_ref, hs_ref, xs_hbm, buf, zbuf, sem,
                     psem, *, n_tiles, n_prompt_tiles):
    i = pl.program_id(0)
    slot = i % 2

    def pad_strips(wait):
        for e in range(N_EXPERTS):
            start = ps_ref[e]
            _strip_dmas(pn_ref[e], PAD_LEN_BITS, lambda off, size: pltpu.make_async_copy(
                zbuf.at[pl.ds(0, size)], xs_hbm.at[pl.ds(pl.multiple_of(start + off, SUBLANES), size)],
                psem.at[0]), wait)
        chunk = zbuf.shape[0]
        used_rows = ps_ref[N_EXPERTS - 1] + pn_ref[N_EXPERTS - 1]

        def body(c, carry):
            cp = pltpu.make_async_copy(zbuf, xs_hbm.at[pl.ds(pl.multiple_of(c * chunk, chunk), chunk)], psem.at[0])
            if wait:
                cp.wait()
            else:
                cp.start()
            return carry
        lax.fori_loop(used_rows // chunk, xs_hbm.shape[0] // chunk, body, 0)

    @pl.when(i >= 2)
    def _():
        _tile_wait(r_ref[i - 2], slot, buf, xs_hbm, sem)

    @pl.when(i == 0)
    def _():
        zbuf[...] = jnp.zeros_like(zbuf)
        pad_strips(False)

    srow = lax.broadcasted_iota(jnp.int32, (TILE_ROWS, TILE_TOKENS), 0)
    place = srow == lst_ref[0:1, :]
    for k in range(1, TOP_K):
        place = jnp.logical_or(place, srow == lst_ref[k:k + 1, :])
    x = jnp.where(i < n_prompt_tiles, hp_ref[...], hs_ref[...]).astype(BF16)
    buf[slot] = jnp.dot(jnp.where(place, 1.0, 0.0).astype(BF16), x, preferred_element_type=F32)
    _tile_strips(i, slot, d_ref, n_ref, o_ref, buf, xs_hbm, sem, to_hbm=True)

    @pl.when(i == n_tiles - 1)
    def _():
        _tile_wait(r_ref[i], slot, buf, xs_hbm, sem)
        if n_tiles > 1:
            _tile_wait(r_ref[jnp.maximum(i - 1, 0)], 1 - slot, buf, xs_hbm, sem)
        pad_strips(True)


def _dispatch(hp, hs, ls_t, tabs, n_rows):
    tp, d = hp.shape
    npt = tp // TILE_TOKENS
    assert tp % TILE_TOKENS == 0 and hs.shape[0] == TILE_TOKENS
    nt = npt + 1
    grid_spec = pltpu.PrefetchScalarGridSpec(
        num_scalar_prefetch=6,
        grid=(nt,),
        in_specs=[
            pl.BlockSpec((TOP_K, TILE_TOKENS), lambda i, *_: (0, i)),
            pl.BlockSpec((TILE_TOKENS, d), lambda i, *_: (jnp.minimum(i, npt - 1), 0)),
            pl.BlockSpec((TILE_TOKENS, d), lambda i, *_: (0, 0)),
        ],
        out_specs=pl.BlockSpec(memory_space=pl.ANY),
        scratch_shapes=[
            pltpu.VMEM((2, TILE_ROWS, d), F32),
            pltpu.VMEM((PAD_LEN_BITS[-1], d), F32),
            pltpu.SemaphoreType.DMA((2,)),
            pltpu.SemaphoreType.DMA((1,)),
        ],
    )
    return pl.pallas_call(
        functools.partial(_dispatch_kernel, n_tiles=nt, n_prompt_tiles=npt),
        grid_spec=grid_spec,
        out_shape=jax.ShapeDtypeStruct((n_rows, d), F32),
        compiler_params=_cparams(("arbitrary",)),
        name="moe_dispatch",
    )(tabs["d"], tabs["n"], tabs["o"], tabs["rows"], tabs["pad_start"], tabs["pad_len"], ls_t, hp, hs)


def _expert_ffn_kernel(b0_ref, nb_ref, x_hbm, wgu_hbm, bgu_ref, wd_hbm, bd_ref, y_hbm, xbuf, ybuf, xtail, ytail,
                       xsem, ysem, tsem, wgu_st, wd_st, wsem, wgu_bf, wd_bf, *, n_blocks):
    e = pl.program_id(0)
    b0 = b0_ref[e]
    n = nb_ref[e]
    n2 = n // FFN_PAIR
    odd = n % FFN_PAIR
    dff = wd_st.shape[0]
    big = FFN_PAIR * MOE_BM

    def w_copies(ex):
        return (pltpu.make_async_copy(wgu_hbm.at[ex], wgu_st, wsem.at[0]),
                pltpu.make_async_copy(wd_hbm.at[ex], wd_st, wsem.at[1]))

    def rows(blk, size):
        return pl.ds(pl.multiple_of((b0 + blk) * MOE_BM, MOE_BM), size)

    def x_copy(kk, slot):
        return pltpu.make_async_copy(x_hbm.at[rows(FFN_PAIR * kk, big)], xbuf.at[slot], xsem.at[slot])

    def y_copy(kk, slot):
        return pltpu.make_async_copy(ybuf.at[slot], y_hbm.at[rows(FFN_PAIR * kk, big)], ysem.at[slot])

    def xt_copy():
        return pltpu.make_async_copy(x_hbm.at[rows(n - 1, MOE_BM)], xtail, tsem.at[0])

    def yt_copy():
        return pltpu.make_async_copy(ytail, y_hbm.at[rows(n - 1, MOE_BM)], tsem.at[1])

    def ffn(x):
        hu = jnp.dot(x.astype(BF16), wgu_bf[...], preferred_element_type=F32) + bgu_ref[0]
        g = jnp.minimum(hu[:, :dff], SWIGLU_LIMIT)
        u = jnp.clip(hu[:, dff:], -SWIGLU_LIMIT, SWIGLU_LIMIT)
        a = g * jax.nn.sigmoid(SWIGLU_ALPHA * g) * (u + 1.0)
        return jnp.dot(a.astype(BF16), wd_bf[...], preferred_element_type=F32) + bd_ref[0]

    @pl.when(e == 0)
    def _():
        for c in w_copies(0):
            c.start()

    @pl.when(jnp.logical_and(n > 0, odd == 1))
    def _():
        xt_copy().start()

    @pl.when(n2 > 0)
    def _():
        x_copy(0, 0).start()

    @pl.when(n2 > 1)
    def _():
        x_copy(1, 1).start()

    for c in w_copies(e):
        c.wait()

    @pl.when(n > 0)
    def _():
        wgu_bf[...] = wgu_st[...].astype(BF16)
        wd_bf[...] = wd_st[...].astype(BF16)

    @pl.when(e + 1 < N_EXPERTS)
    def _():
        for c in w_copies(e + 1):
            c.start()

    def drain(nblocks):
        m2 = nblocks // FFN_PAIR

        def y_wait(slot):
            pltpu.make_async_copy(ybuf.at[slot], y_hbm.at[pl.ds(0, big)], ysem.at[slot]).wait()

        @pl.when(m2 >= 2)
        def _():
            y_wait(m2 % 2)

        @pl.when(m2 >= 1)
        def _():
            y_wait((m2 - 1) % 2)

        @pl.when(nblocks % FFN_PAIR == 1)
        def _():
            pltpu.make_async_copy(ytail, y_hbm.at[pl.ds(0, MOE_BM)], tsem.at[1]).wait()

    @pl.when(e >= 1)
    def _():
        drain(nb_ref[jnp.maximum(e - 1, 0)])

    @pl.when(n > 0)
    def _():
        def body(kk, carry):
            slot = kk % 2

            @pl.when(jnp.logical_and(kk >= 1, kk + 1 < n2))
            def _():
                x_copy(kk + 1, 1 - slot).start()

            x_copy(kk, slot).wait()

            @pl.when(kk >= 2)
            def _():
                y_copy(kk - 2, slot).wait()

            ybuf[slot] = ffn(xbuf[slot])
            y_copy(kk, slot).start()
            return carry
        lax.fori_loop(0, n2, body, 0)

        @pl.when(odd == 1)
        def _():
            xt_copy().wait()
            ytail[...] = ffn(xtail[...])
            yt_copy().start()

    @pl.when(e == N_EXPERTS - 1)
    def _():
        drain(n)
        ytail[...] = jnp.zeros_like(ytail)

        def fill(wait):
            def body(k, carry):
                cp = pltpu.make_async_copy(ytail, y_hbm.at[pl.ds(pl.multiple_of(k * MOE_BM, MOE_BM), MOE_BM)],
                                           tsem.at[1])
                if wait:
                    cp.wait()
                else:
                    cp.start()
                return carry
            lax.fori_loop(b0 + n, n_blocks, body, 0)
        fill(False)
        fill(True)


def _expert_ffn(x_sorted, blk_start, nblk, w_gate_up, b_gate_up, w_down, b_down):
    n_rows, d = x_sorted.shape
    ne, _, dff2 = w_gate_up.shape
    dff = dff2 // 2
    any_spec = pl.BlockSpec(memory_space=pl.ANY)
    big = FFN_PAIR * MOE_BM
    grid_spec = pltpu.PrefetchScalarGridSpec(
        num_scalar_prefetch=2,
        grid=(ne,),
        in_specs=[
            any_spec,
            any_spec,
            pl.BlockSpec((1, 1, dff2), lambda e, *_: (e, 0, 0)),
            any_spec,
            pl.BlockSpec((1, 1, d), lambda e, *_: (e, 0, 0)),
        ],
        out_specs=any_spec,
        scratch_shapes=[
            pltpu.VMEM((2, big, d), F32), pltpu.VMEM((2, big, d), F32),
            pltpu.VMEM((MOE_BM, d), F32), pltpu.VMEM((MOE_BM, d), F32),
            pltpu.SemaphoreType.DMA((2,)), pltpu.SemaphoreType.DMA((2,)), pltpu.SemaphoreType.DMA((2,)),
            pltpu.VMEM((d, dff2), F32), pltpu.VMEM((dff, d), F32), pltpu.SemaphoreType.DMA((2,)),
            pltpu.VMEM((d, dff2), BF16), pltpu.VMEM((dff, d), BF16),
        ],
    )
    return pl.pallas_call(
        functools.partial(_expert_ffn_kernel, n_blocks=n_rows // MOE_BM),
        grid_spec=grid_spec,
        out_shape=jax.ShapeDtypeStruct((n_rows, d), F32),
        compiler_params=_cparams(("arbitrary",)),
        name="expert_ffn",
    )(blk_start, nblk, x_sorted, w_gate_up, b_gate_up.reshape(ne, 1, dff2), w_down, b_down.reshape(ne, 1, d))


def _combine_kernel(d_ref, n_ref, o_ref, r_ref, ls_ref, gate_ref, xp_ref, xs_ref, g_ref, y_hbm, yp_ref, ys_ref,
                    buf, sem, *, n_tiles, n_prompt_tiles):
    j = pl.program_id(0)

    @pl.when(j == 0)
    def _():
        buf[...] = jnp.zeros_like(buf)

    @pl.when(j < n_tiles)
    def _():
        _tile_strips(j, j % 2, d_ref, n_ref, o_ref, buf, y_hbm, sem, to_hbm=False)

    @pl.when(j >= 1)
    def _():
        slot = (j - 1) % 2
        _tile_wait(r_ref[j - 1], slot, buf, y_hbm, sem)
        r_hi = buf[slot].astype(BF16)
        lane = lax.broadcasted_iota(jnp.int32, (TILE_TOKENS, TILE_ROWS), 1)
        pick = jnp.zeros((TILE_TOKENS, TILE_ROWS), F32)
        for k in range(TOP_K):
            pick = pick + jnp.where(lane == ls_ref[:, k:k + 1], gate_ref[:, k:k + 1], 0.0)
        p_hi = pick.astype(BF16)
        p_lo = (pick - p_hi.astype(F32)).astype(BF16)
        moe = jnp.dot(p_hi, r_hi, preferred_element_type=F32) + jnp.dot(p_lo, r_hi, preferred_element_type=F32)
        is_prompt = j - 1 < n_prompt_tiles
        y = _rms(jnp.where(is_prompt, xp_ref[...], xs_ref[...]) + moe, g_ref[...])

        @pl.when(is_prompt)
        def _():
            yp_ref[...] = y

        @pl.when(jnp.logical_not(is_prompt))
        def _():
            ys_ref[...] = y


def _combine(y_sorted, x2p, x2s, ls, gates, tabs, g_final):
    tp, d = x2p.shape
    npt = tp // TILE_TOKENS
    assert x2s.shape[0] == TILE_TOKENS
    nt = npt + 1
    tile = lambda j: jnp.clip(j - 1, 0, nt - 1)
    ptile = lambda j: jnp.clip(j - 1, 0, npt - 1)
    grid_spec = pltpu.PrefetchScalarGridSpec(
        num_scalar_prefetch=4,
        grid=(nt + 1,),
        in_specs=[
            pl.BlockSpec((TILE_TOKENS, TOP_K), lambda j, *_: (tile(j), 0)),
            pl.BlockSpec((TILE_TOKENS, TOP_K), lambda j, *_: (tile(j), 0)),
            pl.BlockSpec((TILE_TOKENS, d), lambda j, *_: (ptile(j), 0)),
            pl.BlockSpec((TILE_TOKENS, d), lambda j, *_: (0, 0)),
            pl.BlockSpec((1, d), lambda j, *_: (0, 0)),
            pl.BlockSpec(memory_space=pl.ANY),
        ],
        out_specs=[
            pl.BlockSpec((TILE_TOKENS, d), lambda j, *_: (ptile(j), 0)),
            pl.BlockSpec((TILE_TOKENS, d), lambda j, *_: (0, 0)),
        ],
        scratch_shapes=[pltpu.VMEM((2, TILE_ROWS, d), F32), pltpu.SemaphoreType.DMA((2,))],
    )
    return pl.pallas_call(
        functools.partial(_combine_kernel, n_tiles=nt, n_prompt_tiles=npt),
        grid_spec=grid_spec,
        out_shape=[jax.ShapeDtypeStruct((tp, d), F32), jax.ShapeDtypeStruct((TILE_TOKENS, d), F32)],
        compiler_params=_cparams(("arbitrary",)),
        name="moe_combine",
    )(tabs["d"], tabs["n"], tabs["o"], tabs["rows"], ls, gates, x2p, x2s, g_final.reshape(1, d), y_sorted)


def _route_kernel(lp_ref, ls_in_ref, gate_ref, row_ref, cnt_ref, *, n_prompt_tiles, n_real_last):
    i = pl.program_id(0)
    is_prompt = i < n_prompt_tiles
    logits = jnp.where(is_prompt, lp_ref[...], ls_in_ref[...])
    tok = lax.broadcasted_iota(jnp.int32, (1, TILE_TOKENS), 1)
    real = jnp.logical_or(is_prompt, tok < n_real_last)
    eid = lax.broadcasted_iota(jnp.int32, logits.shape, 0)
    work = logits
    vals, sels = [], []
    for _ in range(TOP_K):
        v = jnp.max(work, axis=0, keepdims=True)
        idx = jnp.min(jnp.where(work == v, eid, N_EXPERTS), axis=0, keepdims=True)
        vals.append(v)
        sels.append(jnp.logical_and(eid == idx, real))
        work = jnp.where(eid == idx, -jnp.inf, work)
    es = [jnp.exp(v - vals[0]) for v in vals]
    denom = es[0] + es[1] + es[2] + es[3]
    chosen = jnp.zeros(logits.shape, F32)
    for sel in sels:
        chosen = chosen + jnp.where(sel, 1.0, 0.0)
    r = lax.broadcasted_iota(jnp.int32, (TILE_TOKENS, TILE_TOKENS), 0)
    c = lax.broadcasted_iota(jnp.int32, (TILE_TOKENS, TILE_TOKENS), 1)
    rank = jnp.dot(chosen.astype(BF16), jnp.where(r < c, 1.0, 0.0).astype(BF16), preferred_element_type=F32)
    count = jnp.sum(chosen, axis=1, keepdims=True)
    strip = jnp.floor((count + (SUBLANES - 1)) * (1.0 / SUBLANES)) * SUBLANES
    er = lax.broadcasted_iota(jnp.int32, (N_EXPERTS, N_EXPERTS), 0)
    ec = lax.broadcasted_iota(jnp.int32, (N_EXPERTS, N_EXPERTS), 1)
    start = jnp.dot(jnp.where(ec < er, 1.0, 0.0).astype(BF16),
                    jnp.broadcast_to(strip, (N_EXPERTS, LANES)).astype(BF16), preferred_element_type=F32)[:, 0:1]
    place = start + rank
    for k in range(TOP_K):
        row_k = jnp.sum(jnp.where(sels[k], place, 0.0), axis=0, keepdims=True)
        row_ref[k:k + 1, :] = jnp.where(real, row_k, -1.0).astype(jnp.int32)
        gate_ref[k:k + 1, :] = jnp.where(real, es[k] / denom, 0.0)
    cnt_ref[0] = strip.astype(jnp.int32)


def _route(lgp_t, lgs):
    ne, tp = lgp_t.shape
    ts = lgs.shape[0]
    assert tp % TILE_TOKENS == 0 and ts <= TILE_TOKENS and ne == N_EXPERTS
    npt = tp // TILE_TOKENS
    nt = npt + 1
    lgs_t = jnp.pad(lgs.T, ((0, 0), (0, TILE_TOKENS - ts)))
    tile4 = pl.BlockSpec((TOP_K, TILE_TOKENS), lambda i: (0, i))
    gates_t, ls_t, n_tile = pl.pallas_call(
        functools.partial(_route_kernel, n_prompt_tiles=npt, n_real_last=ts),
        grid=(nt,),
        in_specs=[pl.BlockSpec((ne, TILE_TOKENS), lambda i: (0, jnp.minimum(i, npt - 1))),
                  pl.BlockSpec((ne, TILE_TOKENS), lambda i: (0, 0))],
        out_specs=[tile4, tile4, pl.BlockSpec((1, ne, 1), lambda i: (i, 0, 0))],
        out_shape=[jax.ShapeDtypeStruct((TOP_K, nt * TILE_TOKENS), F32),
                   jax.ShapeDtypeStruct((TOP_K, nt * TILE_TOKENS), jnp.int32),
                   jax.ShapeDtypeStruct((nt, ne, 1), jnp.int32)],
        compiler_params=_cparams(("arbitrary",)),
        name="moe_route",
    )(lgp_t, lgs_t)
    n_tile = n_tile.reshape(nt, ne)
    up = lambda a, m: (a + m - 1) // m * m
    o_tile = jnp.cumsum(n_tile, axis=1) - n_tile
    rows_e = jnp.sum(n_tile, axis=0)
    nblk = up(rows_e, MOE_BM) // MOE_BM
    blk_end = jnp.cumsum(nblk)
    pstart = (blk_end - nblk) * MOE_BM
    d_tile = pstart[None, :] + jnp.cumsum(n_tile, axis=0) - n_tile
    t = tp + ts
    nb = (t * TOP_K + N_EXPERTS * nt * (SUBLANES - 1) + N_EXPERTS * (MOE_BM - 1) + MOE_BM - 1) // MOE_BM
    i32 = lambda a: a.astype(jnp.int32).reshape(-1)
    tabs = {"d": i32(d_tile), "n": i32(n_tile), "o": i32(o_tile), "rows": i32(jnp.sum(n_tile, axis=1)),
            "pad_start": i32(pstart + rows_e), "pad_len": i32(nblk * MOE_BM - rows_e),
            "blk_start": i32(blk_end - nblk), "nblk": i32(nblk)}
    return gates_t, ls_t, tabs, nb * MOE_BM


def kernel(x_prompt, x_sample, cache_k_win, cache_v_win, state_pool, cache_mem_k, cache_mem_v, mem_prompt,
           rel_bias, g_mix, w_in, w_pool, pool_scale, w_out, g_mem, g_x, w_xq, w_xk, w_xv, w_xo, g_ff,
           w_router, b_router, w_gate_up, b_gate_up, w_down, b_down, g_final):
    depth = g_mix.shape[0]
    assert depth == 1
    l = 0
    batch, seq, d = x_prompt.shape
    bd, n_new, _ = x_sample.shape
    n_mem = mem_prompt.shape[1]
    bw = d - A_WIDTH
    n_hist = cache_k_win.shape[2]
    n_pool = state_pool.shape[2]
    tp, ts = batch * seq, bd * n_new

    w_in_b = w_in[l].astype(BF16)
    w_pool_b = w_pool[l].astype(BF16)
    w_xkv_b = jnp.concatenate([w_xk[l], w_xv[l]], axis=1).astype(BF16)
    wts = {
        "w_out_a": w_out[l, :A_WIDTH].astype(BF16), "w_out_b": w_out[l, A_WIDTH:].astype(BF16),
        "g_x": g_x[l].reshape(1, d), "w_xq": w_xq[l].astype(BF16), "w_xo": w_xo[l].astype(BF16),
        "g_ff": g_ff[l].reshape(1, d), "w_router": w_router[l], "b_router": b_router[l].reshape(1, N_EXPERTS),
    }
    band_bias = _band_bias(rel_bias)
    widths = (A_WIDTH, A_WIDTH, A_WIDTH, bw)
    scales = (A_HEAD_DIM ** -0.5, 1.0, 1.0, 1.0)

    xp = x_prompt.reshape(tp, d)
    w_kv_t = jnp.transpose(w_in[l][:, A_WIDTH:3 * A_WIDTH].reshape(d, 2, A_WIDTH), (1, 2, 0)).astype(BF16)
    q, k, v, u, k_t, v_t = _in_proj(xp, g_mix[l], w_in_b, w_kv_t, widths, scales, batch, seq, tm=512)
    o_a = _dil_attn(q, k, v, band_bias, batch, seq)
    o_b = _pool_prompt(u, w_pool_b, pool_scale[l], batch, seq, tm=512)
    mk, mv, mk_b, mv_b = _mem_kv(mem_prompt.reshape(batch * n_mem, d), g_mem[l], w_xkv_b, X_HEADS, tm=512)
    x2p, hp, lgp = _mix_xattn(xp, o_a, o_b, mk_b, mv_b, wts, groups=batch, rows_per_group=seq, tm=512,
                              rows_per_sub=None, keys_per_sub=None, logits_t=True)

    xs = x_sample.reshape(ts, d)
    qs, ks, vs, us = _norm_proj(xs, g_mix[l], w_in_b, widths, scales, tm=ts)
    ck, cv = cache_k_win[l], cache_v_win[l]
    new3 = lambda a: a.reshape(bd, n_new, A_WIDTH)
    to_t = lambda c: jnp.transpose(c, (0, 2, 3, 1)).reshape(bd, A_WIDTH, n_hist)
    from_t = lambda c: jnp.transpose(c.reshape(bd, A_HEADS, A_HEAD_DIM, n_hist), (0, 3, 1, 2))[None]
    bias_c, bias_n = _decode_bias(rel_bias, n_hist, n_new)
    o_as, ck_t, cv_t = _decode_attn(new3(qs), new3(ks), new3(vs), to_t(ck), to_t(cv), bias_c, bias_n,
                                    heads_per_step=A_HEADS)
    o_as = o_as.reshape(ts, A_WIDTH)
    us3 = us.reshape(bd, n_new, bw)
    hist = jnp.concatenate([jnp.zeros((bd, POOL_HIST - n_pool, bw), F32), state_pool[l]], axis=1)
    cur = jnp.concatenate([us3, jnp.zeros((bd, SUBLANES - n_new, bw), F32)], axis=1)
    o_bs = _pool_dec(hist, cur, n_pool, w_pool_b, pool_scale[l])[:, :n_new].reshape(ts, bw)
    sub = 8
    xhd = d // X_HEADS
    x2s, hs, lgs = _mix_xattn(xs, o_as, o_bs, cache_mem_k[l].reshape(bd * n_mem * X_HEADS, xhd),
                              cache_mem_v[l].reshape(bd * n_mem * X_HEADS, xhd), wts, groups=bd // sub,
                              rows_per_group=sub * n_new, tm=sub * n_new, rows_per_sub=n_new, keys_per_sub=n_mem,
                              logits_t=False, heads_in_rows=True)

    assert ts <= TILE_TOKENS and w_gate_up.shape[1] == N_EXPERTS
    pad_s = lambda a: jnp.pad(a, ((0, TILE_TOKENS - ts), (0, 0)))
    gates_t, ls_t, tabs, n_rows = _route(lgp, lgs)
    gates, ls = gates_t.T, ls_t.T
    x_sorted = _dispatch(hp, pad_s(hs), ls_t, tabs, n_rows)
    y_sorted = _expert_ffn(x_sorted, tabs["blk_start"], tabs["nblk"], w_gate_up[l], b_gate_up[l], w_down[l],
                           b_down[l])
    y_prompt, y_sample = _combine(y_sorted, x2p, pad_s(x2s), ls, gates, tabs, g_final)
    y_prompt = y_prompt.reshape(batch, seq, d)
    y_sample = y_sample[:ts].reshape(bd, n_new, d)

    from_tp = lambda c: jnp.transpose(c.reshape(batch, A_HEADS, A_HEAD_DIM, seq), (0, 3, 1, 2))[None]
    k_win_prompt, v_win_prompt = from_tp(k_t), from_tp(v_t)
    pool_prompt = u.reshape(batch, seq, bw)[:, seq - n_pool:][None]
    mem_k_prompt = mk.reshape(1, batch, n_mem, X_HEADS, xhd)
    mem_v_prompt = mv.reshape(1, batch, n_mem, X_HEADS, xhd)
    k_win_sample, v_win_sample = from_t(ck_t), from_t(cv_t)
    pool_sample = jnp.concatenate([state_pool[l][:, n_new:], us3], axis=1)[None]
    return (y_prompt, y_sample, k_win_prompt, v_win_prompt, pool_prompt, mem_k_prompt, mem_v_prompt,
            k_win_sample, v_win_sample, pool_sample)
```

```python
import functools
import math

import numpy as np
import jax
import jax.numpy as jnp
from jax import lax
from jax.experimental import pallas as pl
from jax.experimental.pallas import tpu as pltpu

F32 = jnp.float32
BF16 = jnp.bfloat16

LANES = 128
SUBLANES = 8
VMEM_LIMIT_BYTES = 56 * 1024 * 1024

A_HEADS = 8
A_HEAD_DIM = 64
A_WIDTH = A_HEADS * A_HEAD_DIM
DILATED = ((128, 1), (512, 4), (2048, 16))
QB = 128
POOL_WINDOWS = (2, 4, 8, 16)
POOL_HIST = 16
X_HEADS = 4
N_EXPERTS = 32
TOP_K = 4
SWIGLU_LIMIT = 7.0
SWIGLU_ALPHA = 1.702
N_BUCKETS = 32
RMS_EPS = 1e-6
NEG = -1e30

MOE_BM = 256
CACHE_RING = 3


def _cparams(sem):
    return pltpu.CompilerParams(dimension_semantics=sem, vmem_limit_bytes=VMEM_LIMIT_BYTES)


def _rms(x, g):
    return x * lax.rsqrt(jnp.mean(x * x, axis=-1, keepdims=True) + RMS_EPS) * g


def _norm_proj_kernel(x_ref, g_ref, w_ref, *o_refs, widths, scales):
    hb = _rms(x_ref[...], g_ref[...]).astype(BF16)
    off = 0
    for o_ref, width, scale in zip(o_refs, widths, scales):
        p = jnp.dot(hb, w_ref[:, off:off + width], preferred_element_type=F32)
        o_ref[...] = p if scale == 1.0 else p * scale
        off += width


def _norm_proj(x, g, w_bf16, widths, scales, tm):
    t, d = x.shape
    n = w_bf16.shape[1]
    assert sum(widths) == n and t % tm == 0
    return pl.pallas_call(
        functools.partial(_norm_proj_kernel, widths=widths, scales=scales),
        grid=(t // tm,),
        in_specs=[
            pl.BlockSpec((tm, d), lambda i: (i, 0)),
            pl.BlockSpec((1, d), lambda i: (0, 0)),
            pl.BlockSpec((d, n), lambda i: (0, 0)),
        ],
        out_specs=[pl.BlockSpec((tm, wd), lambda i: (i, 0)) for wd in widths],
        out_shape=[jax.ShapeDtypeStruct((t, wd), F32) for wd in widths],
        compiler_params=_cparams(("arbitrary",)),
        name="norm_proj",
    )(x, g.reshape(1, d), w_bf16)


def _mem_kv_kernel(x_ref, g_ref, w_ref, k_ref, v_ref, kb_ref, vb_ref, *, heads):
    hb = _rms(x_ref[...], g_ref[...]).astype(BF16)
    d = x_ref.shape[1]
    hd = d // heads
    for j, (o_ref, ob_ref) in enumerate(((k_ref, kb_ref), (v_ref, vb_ref))):
        p = jnp.dot(hb, w_ref[:, j * d:(j + 1) * d], preferred_element_type=F32)
        ob_ref[...] = p.astype(BF16)
        for h in range(heads):
            o_ref[:, h, :] = p[:, h * hd:(h + 1) * hd]


def _mem_kv(x, g, w_bf16, heads, tm):
    t, d = x.shape
    assert w_bf16.shape == (d, 2 * d) and t % tm == 0
    hd = d // heads
    o3 = pl.BlockSpec((tm, heads, hd), lambda i: (i, 0, 0))
    o2 = pl.BlockSpec((tm, d), lambda i: (i, 0))
    return pl.pallas_call(
        functools.partial(_mem_kv_kernel, heads=heads),
        grid=(t // tm,),
        in_specs=[
            pl.BlockSpec((tm, d), lambda i: (i, 0)),
            pl.BlockSpec((1, d), lambda i: (0, 0)),
            pl.BlockSpec((d, 2 * d), lambda i: (0, 0)),
        ],
        out_specs=[o3, o3, o2, o2],
        out_shape=[jax.ShapeDtypeStruct((t, heads, hd), F32)] * 2 + [jax.ShapeDtypeStruct((t, d), BF16)] * 2,
        compiler_params=_cparams(("arbitrary",)),
        name="mem_kv",
    )(x, g.reshape(1, d), w_bf16)


def _in_proj_kernel(x_ref, g_ref, w_ref, wt_ref, *o_refs, widths, scales, n_t):
    hb = _rms(x_ref[...], g_ref[...]).astype(BF16)
    off = 0
    for o_ref, width, scale in zip(o_refs, widths, scales):
        p = jnp.dot(hb, w_ref[:, off:off + width], preferred_element_type=F32)
        o_ref[...] = p if scale == 1.0 else p * scale
        off += width
    for j in range(n_t):
        o_refs[len(widths) + j][0] = lax.dot_general(wt_ref[j], hb, (((1,), (1,)), ((), ())),
                                                    preferred_element_type=F32)


def _in_proj(x, g, w_bf16, wt_bf16, widths, scales, batch, seq, tm):
    t, d = x.shape
    n = w_bf16.shape[1]
    n_t, wt_width, _ = wt_bf16.shape
    nt = seq // tm
    assert sum(widths) == n and seq % tm == 0 and t == batch * seq
    return pl.pallas_call(
        functools.partial(_in_proj_kernel, widths=widths, scales=scales, n_t=n_t),
        grid=(t // tm,),
        in_specs=[
            pl.BlockSpec((tm, d), lambda i: (i, 0)),
            pl.BlockSpec((1, d), lambda i: (0, 0)),
            pl.BlockSpec((d, n), lambda i: (0, 0)),
            pl.BlockSpec(wt_bf16.shape, lambda i: (0, 0, 0)),
        ],
        out_specs=[pl.BlockSpec((tm, wd), lambda i: (i, 0)) for wd in widths]
        + [pl.BlockSpec((1, wt_width, tm), lambda i: (i // nt, 0, i % nt))] * n_t,
        out_shape=[jax.ShapeDtypeStruct((t, wd), F32) for wd in widths]
        + [jax.ShapeDtypeStruct((batch, wt_width, seq), F32)] * n_t,
        compiler_params=_cparams(("arbitrary",)),
        name="in_proj",
    )(x, g.reshape(1, d), w_bf16, wt_bf16)


def _t5_bucket_np(n, max_dist):
    max_exact = N_BUCKETS // 2
    nf = np.maximum(n, 1).astype(np.float32)
    large = max_exact + (
        np.log(nf / np.float32(max_exact)) / np.float32(math.log(max_dist / max_exact))
        * np.float32(N_BUCKETS - max_exact)
    ).astype(np.int32)
    return np.where(n < max_exact, n, np.minimum(large, N_BUCKETS - 1))


def _band_bias(rel_bias):
    max_dist = max(w for w, _ in DILATED)
    qi = np.arange(QB)[:, None]
    ki = np.arange(2 * QB)[None, :]
    j = qi + QB - ki
    tabs = []
    for window, dil in DILATED:
        steps = window // dil
        in_band = (j >= 0) & (j <= steps)
        bucket = _t5_bucket_np(np.clip(j, 0, steps) * dil, max_dist)
        onehot = (bucket[..., None] == np.arange(N_BUCKETS)).astype(np.float32)
        b = jnp.einsum("qkb,bh->hqk", onehot, rel_bias.astype(F32), precision=lax.Precision.HIGHEST)
        tabs.append(jnp.where(in_band[None], b, NEG))
    return jnp.stack(tabs)


def _decode_bias(rel_bias, n_hist, n_new):
    max_dist = max(w for w, _ in DILATED)
    t = np.arange(n_new)[:, None]
    tabs_c, tabs_n = [], []
    for window, dil in DILATED:
        out = []
        for dist in (n_hist + t - np.arange(n_hist)[None, :], t - np.arange(n_new)[None, :]):
            ok = (dist >= 0) & (dist % dil == 0) & (dist <= window)
            onehot = (_t5_bucket_np(np.clip(dist, 0, window), max_dist)[..., None] == np.arange(N_BUCKETS))
            b = jnp.einsum("tpb,bh->htp", onehot.astype(np.float32), rel_bias.astype(F32),
                           precision=lax.Precision.HIGHEST)
            out.append(jnp.where(ok[None], b, NEG))
        tabs_c.append(out[0])
        tabs_n.append(out[1])
    return jnp.stack(tabs_c), jnp.stack(tabs_n)


def _decode_attn_kernel(q_ref, kn_ref, vn_ref, knt_ref, vnt_ref, ck_hbm, cv_hbm, bc_ref, bn_ref,
                        o_ref, ok_ref, ov_ref, kbuf, vbuf, sem, *, n_new, heads, n_batches):
    n_hist = kbuf.shape[2]
    hd = A_HEAD_DIM
    b = pl.program_id(0)
    slot = b % CACHE_RING

    def fetch(bb, wait=False):
        for j, (src, dst) in enumerate(((ck_hbm, kbuf), (cv_hbm, vbuf))):
            cp = pltpu.make_async_copy(src.at[bb], dst.at[bb % CACHE_RING], sem.at[j, bb % CACHE_RING])
            if wait:
                cp.wait()
            else:
                cp.start()

    @pl.when(b == 0)
    def _():
        for bb in range(min(CACHE_RING - 1, n_batches)):
            fetch(bb)

    @pl.when(b + CACHE_RING - 1 < n_batches)
    def _():
        fetch(b + CACHE_RING - 1)

    fetch(b, wait=True)
    outs = []
    for h in range(heads):
        rows = slice(h * hd, (h + 1) * hd)
        q = q_ref[0, :, rows].astype(BF16)
        kt = kbuf[slot, rows, :].astype(BF16)
        vt = vbuf[slot, rows, :].astype(BF16)
        lc = jnp.dot(q, kt, preferred_element_type=F32)
        ln = lax.dot_general(q, kn_ref[0, :, rows].astype(BF16), (((1,), (1,)), ((), ())),
                             preferred_element_type=F32)
        vn = vn_ref[0, :, rows].astype(BF16)
        o_br, lse_br = [], []
        for br in range(len(DILATED)):
            bl = lc + bc_ref[br, h]
            bln = ln + bn_ref[br, h]
            m = jnp.maximum(jnp.max(bl, axis=-1, keepdims=True), jnp.max(bln, axis=-1, keepdims=True))
            p = jnp.exp(bl - m)
            pn = jnp.exp(bln - m)
            s = jnp.sum(p, axis=-1, keepdims=True) + jnp.sum(pn, axis=-1, keepdims=True)
            o = lax.dot_general(p.astype(BF16), vt, (((1,), (1,)), ((), ())), preferred_element_type=F32)
            o = o + jnp.dot(pn.astype(BF16), vn, preferred_element_type=F32)
            o_br.append(o / s)
            lse_br.append(m + jnp.log(s))
        m = jnp.maximum(jnp.maximum(lse_br[0], lse_br[1]), lse_br[2])
        es = [jnp.exp(l - m) for l in lse_br]
        outs.append((es[0] * o_br[0] + es[1] * o_br[1] + es[2] * o_br[2]) / (es[0] + es[1] + es[2]))
    o_ref[0] = jnp.concatenate(outs, axis=-1)

    lane = lax.broadcasted_iota(jnp.int32, (kbuf.shape[1], LANES), 1)
    for c_buf, nt_ref, dst in ((kbuf, knt_ref, ok_ref), (vbuf, vnt_ref, ov_ref)):
        shifted = pltpu.roll(c_buf[slot], n_hist - n_new, axis=1)
        tail = shifted[:, n_hist - LANES:]
        for j in range(n_new):
            tail = jnp.where(lane == LANES - n_new + j, nt_ref[0, :, j:j + 1], tail)
        dst[0, :, :n_hist - LANES] = shifted[:, :n_hist - LANES]
        dst[0, :, n_hist - LANES:] = tail


def _decode_attn(q, k_new, v_new, ck_t, cv_t, bias_c, bias_n, heads_per_step):
    bd, n_new, aw = q.shape
    n_hist = ck_t.shape[2]
    gw = heads_per_step * A_HEAD_DIM
    ng = aw // gw
    row = pl.BlockSpec((1, n_new, gw), lambda b, g: (b, 0, g))
    col = pl.BlockSpec((1, gw, n_new), lambda b, g: (b, g, 0))
    cache = pl.BlockSpec((1, gw, n_hist), lambda b, g: (b, g, 0))
    nbr = len(DILATED)
    assert ng == 1
    any_spec = pl.BlockSpec(memory_space=pl.ANY)
    return pl.pallas_call(
        functools.partial(_decode_attn_kernel, n_new=n_new, heads=heads_per_step, n_batches=bd),
        grid=(bd, ng),
        in_specs=[row, row, row, col, col, any_spec, any_spec,
                  pl.BlockSpec((nbr, heads_per_step, n_new, n_hist), lambda b, g: (0, g, 0, 0)),
                  pl.BlockSpec((nbr, heads_per_step, n_new, n_new), lambda b, g: (0, g, 0, 0))],
        out_specs=[row, cache, cache],
        out_shape=[jax.ShapeDtypeStruct(q.shape, F32), jax.ShapeDtypeStruct(ck_t.shape, F32),
                   jax.ShapeDtypeStruct(cv_t.shape, F32)],
        scratch_shapes=[pltpu.VMEM((CACHE_RING, gw, n_hist), F32), pltpu.VMEM((CACHE_RING, gw, n_hist), F32),
                        pltpu.SemaphoreType.DMA((2, CACHE_RING))],
        compiler_params=_cparams(("arbitrary", "arbitrary")),
        name="decode_attn",
    )(q, k_new, v_new, jnp.swapaxes(k_new, 1, 2), jnp.swapaxes(v_new, 1, 2), ck_t, cv_t, bias_c, bias_n)


def _dil_attn_kernel(q_ref, k_ref, v_ref, bias_ref, o_ref, obr_ref, lbr_ref, *, seq):
    lane = lax.broadcasted_iota(jnp.int32, (QB, LANES), 1)
    head0 = lane < A_HEAD_DIM

    def rows(ref, start, n, dil):
        if dil == 1:
            return ref[pl.ds(start, n), :]
        return ref[pl.ds(start, n, stride=dil), :]

    def block(br, dil, qstart, kstart, nk):
        qs = rows(q_ref, qstart, QB, dil)
        ks = rows(k_ref, kstart, nk, dil).astype(BF16)
        vs = rows(v_ref, kstart, nk, dil).astype(BF16)
        outs, lses = [], []
        for hh in range(2):
            keep = head0 if hh == 0 else jnp.logical_not(head0)
            qm = jnp.where(keep, qs, 0.0).astype(BF16)
            logits = lax.dot_general(qm, ks, (((1,), (1,)), ((), ())), preferred_element_type=F32)
            logits = logits + bias_ref[br, hh, :, 2 * QB - nk:]
            m = jnp.max(logits, axis=-1, keepdims=True)
            p = jnp.exp(logits - m)
            s = jnp.sum(p, axis=-1, keepdims=True)
            o = jnp.dot(p.astype(BF16), vs, preferred_element_type=F32)
            outs.append(o / s)
            lses.append(jnp.broadcast_to(m + jnp.log(s), (QB, LANES)))
        o = jnp.where(head0, outs[0], outs[1])
        lse = jnp.where(head0, lses[0], lses[1])
        if dil == 1:
            obr_ref[br, pl.ds(qstart, QB), :] = o
            lbr_ref[br, pl.ds(qstart, QB), :] = lse
        else:
            obr_ref[br, pl.ds(qstart, QB, stride=dil), :] = o
            lbr_ref[br, pl.ds(qstart, QB, stride=dil), :] = lse

    for br, (window, dil) in enumerate(DILATED):
        assert window // dil == QB
        nblk = seq // (dil * QB)
        for r in range(dil):
            block(br, dil, r, r, QB)
            for i in range(1, nblk):
                qstart = r + dil * QB * i
                block(br, dil, qstart, qstart - dil * QB, 2 * QB)

    l0, l1, l2 = lbr_ref[0], lbr_ref[1], lbr_ref[2]
    m = jnp.maximum(jnp.maximum(l0, l1), l2)
    e0, e1, e2 = jnp.exp(l0 - m), jnp.exp(l1 - m), jnp.exp(l2 - m)
    acc = e0 * obr_ref[0] + e1 * obr_ref[1] + e2 * obr_ref[2]
    o_ref[...] = acc / (e0 + e1 + e2)


def _dil_attn(q, k, v, band_bias, batch, seq):
    t, aw = q.shape
    npair = aw // LANES
    bias = band_bias.reshape(len(DILATED), npair, 2, QB, 2 * QB)
    spec = pl.BlockSpec((seq, LANES), lambda b, hp: (b, hp))
    return pl.pallas_call(
        functools.partial(_dil_attn_kernel, seq=seq),
        grid=(batch, npair),
        in_specs=[
            spec, spec, spec,
            pl.BlockSpec((len(DILATED), None, 2, QB, 2 * QB), lambda b, hp: (0, hp, 0, 0, 0)),
        ],
        out_specs=spec,
        out_shape=jax.ShapeDtypeStruct((t, aw), F32),
        scratch_shapes=[
            pltpu.VMEM((len(DILATED), seq, LANES), F32),
            pltpu.VMEM((len(DILATED), seq, LANES), F32),
        ],
        compiler_params=_cparams(("arbitrary", "arbitrary")),
        name="dil_attn",
    )(q, k, v, bias)


def _pool_kernel(hist_ref, cur_ref, w_ref, scale_ref, o_ref, *, pos0_of_tile, tm):
    nb = cur_ref.shape[0]
    gdim = w_ref.shape[1]
    i = pl.program_id(1)
    pos0 = pos0_of_tile(i)
    have_hist = jnp.where(pos0 > 0, 1.0, 0.0).astype(F32)
    t = lax.broadcasted_iota(jnp.int32, (1, tm, 1), 1)
    for g, w in enumerate(POOL_WINDOWS):
        sl = slice(g * gdim, (g + 1) * gdim)
        cur = cur_ref[:, :, sl]
        ext = jnp.concatenate([hist_ref[:, :, sl] * have_hist, cur], axis=1)
        acc, span = ext, 1
        while span < w:
            n = acc.shape[1]
            acc = acc[:, span:n] + acc[:, 0:n - span]
            span *= 2
        wsum = acc[:, POOL_HIST + 1 - w:POOL_HIST + 1 - w + tm]
        cnt = jnp.minimum(pos0 + t + 1, w).astype(F32)
        d = (wsum / cnt - cur).astype(BF16).reshape(nb * tm, gdim)
        y = jnp.dot(d, w_ref[g], preferred_element_type=F32) * scale_ref[:, sl]
        o_ref[:, :, sl] = y.reshape(nb, tm, gdim)


def _pool_prompt(u, w_pool_bf16, pool_scale, batch, seq, tm):
    t, bw = u.shape
    nt = seq // tm
    hb = tm // POOL_HIST
    u3 = u.reshape(1, t, bw)
    out = pl.pallas_call(
        functools.partial(_pool_kernel, pos0_of_tile=lambda i: i * tm, tm=tm),
        grid=(batch, nt),
        in_specs=[
            pl.BlockSpec((1, POOL_HIST, bw), lambda b, i: (0, jnp.maximum((b * nt + i) * hb - 1, 0), 0)),
            pl.BlockSpec((1, tm, bw), lambda b, i: (0, b * nt + i, 0)),
            pl.BlockSpec(w_pool_bf16.shape, lambda b, i: (0, 0, 0)),
            pl.BlockSpec((1, bw), lambda b, i: (0, 0)),
        ],
        out_specs=pl.BlockSpec((1, tm, bw), lambda b, i: (0, b * nt + i, 0)),
        out_shape=jax.ShapeDtypeStruct((1, t, bw), F32),
        compiler_params=_cparams(("arbitrary", "arbitrary")),
        name="pool_prompt",
    )(u3, u3, w_pool_bf16, pool_scale.reshape(1, bw))
    return out.reshape(t, bw)


def _pool_dec(hist, cur, n_prev, w_pool_bf16, pool_scale):
    bd, tm, bw = cur.shape
    return pl.pallas_call(
        functools.partial(_pool_kernel, pos0_of_tile=lambda i: n_prev, tm=tm),
        grid=(1, 1),
        in_specs=[
            pl.BlockSpec(hist.shape, lambda b, i: (0, 0, 0)),
            pl.BlockSpec(cur.shape, lambda b, i: (0, 0, 0)),
            pl.BlockSpec(w_pool_bf16.shape, lambda b, i: (0, 0, 0)),
            pl.BlockSpec((1, bw), lambda b, i: (0, 0)),
        ],
        out_specs=pl.BlockSpec(cur.shape, lambda b, i: (0, 0, 0)),
        out_shape=jax.ShapeDtypeStruct(cur.shape, F32),
        compiler_params=_cparams(("arbitrary", "arbitrary")),
        name="pool_dec",
    )(hist, cur, w_pool_bf16, pool_scale.reshape(1, bw))


def _mix_xattn_kernel(x_ref, oa_ref, ob_ref, mk_ref, mv_ref, woa_ref, wob_ref, gx_ref, wq_ref, wo_ref,
                      gff_ref, wr_ref, br_ref, x2_ref, h_ref, lg_ref, *, rows_per_sub, keys_per_sub, logits_t):
    tm = x_ref.shape[0]
    nkv = mk_ref.shape[0]
    xh = wq_ref.shape[1] // X_HEADS
    x = x_ref[...]
    x = x + jnp.dot(oa_ref[...].astype(BF16), woa_ref[...], preferred_element_type=F32)
    x = x + jnp.dot(ob_ref[...].astype(BF16), wob_ref[...], preferred_element_type=F32)
    hq = _rms(x, gx_ref[...]).astype(BF16)
    if rows_per_sub is not None:
        rsub = (pl.program_id(1) * tm + lax.broadcasted_iota(jnp.int32, (tm, nkv), 0)) // rows_per_sub
        ksub = lax.broadcasted_iota(jnp.int32, (tm, nkv), 1) // keys_per_sub
        same = rsub == ksub
    heads = []
    for h in range(X_HEADS):
        sl = slice(h * xh, (h + 1) * xh)
        q = jnp.dot(hq, wq_ref[:, sl], preferred_element_type=F32) * (xh ** -0.5)
        if len(mk_ref.shape) == 3:
            kh, vh = mk_ref[:, h, :].astype(BF16), mv_ref[:, h, :].astype(BF16)
        else:
            kh, vh = mk_ref[:, sl], mv_ref[:, sl]
        logits = lax.dot_general(q.astype(BF16), kh, (((1,), (1,)), ((), ())), preferred_element_type=F32)
        if rows_per_sub is not None:
            logits = jnp.where(same, logits, NEG)
        m = jnp.max(logits, axis=-1, keepdims=True)
        p = jnp.exp(logits - m)
        s = jnp.sum(p, axis=-1, keepdims=True)
        o = jnp.dot(p.astype(BF16), vh, preferred_element_type=F32) / s
        heads.append(o.astype(BF16))
    x = x + jnp.dot(jnp.concatenate(heads, axis=-1), wo_ref[...], preferred_element_type=F32)
    x2_ref[...] = x
    hf = _rms(x, gff_ref[...])
    h_ref[...] = hf
    if logits_t:
        lg_ref[...] = lax.dot_general(wr_ref[...].astype(BF16), hf.astype(BF16), (((1,), (1,)), ((), ())),
                                      preferred_element_type=F32) + br_ref[...]
    else:
        lg_ref[...] = jnp.dot(hf.astype(BF16), wr_ref[...].astype(BF16), preferred_element_type=F32) + br_ref[...]


def _mix_xattn(x, oa, ob, mk, mv, w, groups, rows_per_group, tm, rows_per_sub, keys_per_sub, logits_t):
    t, d = x.shape
    nt = rows_per_group // tm
    nkv = mk.shape[0] // groups
    ne = w["w_router"].shape[1]
    row = lambda width: pl.BlockSpec((tm, width), lambda g, i: (g * nt + i, 0))
    const = lambda a: pl.BlockSpec(a.shape, lambda g, i: (0,) * a.ndim)
    kv = pl.BlockSpec((nkv,) + mk.shape[1:], lambda g, i: (g,) + (0,) * (mk.ndim - 1))
    w_r, b_r = (w["w_router"].T, w["b_router"].T) if logits_t else (w["w_router"], w["b_router"])
    consts = [w["w_out_a"], w["w_out_b"], w["g_x"], w["w_xq"], w["w_xo"], w["g_ff"], w_r, b_r]
    lg_spec = pl.BlockSpec((ne, tm), lambda g, i: (0, g * nt + i)) if logits_t else row(ne)
    return pl.pallas_call(
        functools.partial(_mix_xattn_kernel, rows_per_sub=rows_per_sub, keys_per_sub=keys_per_sub,
                          logits_t=logits_t),
        grid=(groups, nt),
        in_specs=[row(d), row(oa.shape[1]), row(ob.shape[1]), kv, kv] + [const(a) for a in consts],
        out_specs=[row(d), row(d), lg_spec],
        out_shape=[jax.ShapeDtypeStruct((t, d), F32), jax.ShapeDtypeStruct((t, d), F32),
                   jax.ShapeDtypeStruct((ne, t) if logits_t else (t, ne), F32)],
        compiler_params=_cparams(("arbitrary", "arbitrary")),
        name="mix_xattn",
    )(x, oa, ob, mk, mv, *consts)


TILE_TOKENS = 256
TILE_ROWS = TOP_K * TILE_TOKENS + N_EXPERTS * (SUBLANES - 1)
assert TILE_ROWS % SUBLANES == 0
FFN_PAIR = 2


def _bits(lo, hi):
    return tuple(lo << j for j in range((hi // lo).bit_length()))


STRIP_LEN_BITS = _bits(SUBLANES, TILE_TOKENS)
PAD_LEN_BITS = _bits(SUBLANES, MOE_BM - SUBLANES)
TILE_LEN_BITS = _bits(SUBLANES, TILE_ROWS)


def _strip_dmas(n, bits, make_copy, wait=False):
    for bit in bits:
        @pl.when((n & bit) != 0)
        def _(bit=bit):
            c = make_copy(n & (bit - 1), bit)
            if wait:
                c.wait()
            else:
                c.start()


def _tile_strips(i, slot, d_ref, n_ref, o_ref, local, hbm, sem, to_hbm):
    for e in range(N_EXPERTS):
        l0 = pl.multiple_of(o_ref[i * N_EXPERTS + e], SUBLANES)
        g0 = pl.multiple_of(d_ref[i * N_EXPERTS + e], SUBLANES)

        def make_copy(off, size, l0=l0, g0=g0):
            lo = local.at[slot, pl.ds(pl.multiple_of(l0 + off, SUBLANES), size)]
            gl = hbm.at[pl.ds(pl.multiple_of(g0 + off, SUBLANES), size)]
            src, dst = (lo, gl) if to_hbm else (gl, lo)
            return pltpu.make_async_copy(src, dst, sem.at[slot])
        _strip_dmas(n_ref[i * N_EXPERTS + e], STRIP_LEN_BITS, make_copy)


def _tile_wait(rows, slot, local, hbm, sem):
    _strip_dmas(rows, TILE_LEN_BITS, lambda off, size: pltpu.make_async_copy(
        hbm.at[pl.ds(0, size)], local.at[slot, pl.ds(0, size)], sem.at[slot]), wait=True)


def _dispatch_kernel(d_ref, n_ref, o_ref, r_ref, ps_ref, pn_ref, lst_ref, hp_ref, hs_ref, xs_hbm, buf, zbuf, sem,
                     psem, *, n_tiles, n_prompt_tiles):
    i = pl.program_id(0)
    slot = i % 2

    def pad_strips(wait):
        for e in range(N_EXPERTS):
            start = ps_ref[e]
            _strip_dmas(pn_ref[e], PAD_LEN_BITS, lambda off, size: pltpu.make_async_copy(
                zbuf.at[pl.ds(0, size)], xs_hbm.at[pl.ds(pl.multiple_of(start + off, SUBLANES), size)],
                psem.at[0]), wait)
        chunk = zbuf.shape[0]
        used_rows = ps_ref[N_EXPERTS - 1] + pn_ref[N_EXPERTS - 1]

        def body(c, carry):
            cp = pltpu.make_async_copy(zbuf, xs_hbm.at[pl.ds(pl.multiple_of(c * chunk, chunk), chunk)], psem.at[0])
            if wait:
                cp.wait()
            else:
                cp.start()
            return carry
        lax.fori_loop(used_rows // chunk, xs_hbm.shape[0] // chunk, body, 0)

    @pl.when(i >= 2)
    def _():
        _tile_wait(r_ref[i - 2], slot, buf, xs_hbm, sem)

    @pl.when(i == 0)
    def _():
        zbuf[...] = jnp.zeros_like(zbuf)
        pad_strips(False)

    srow = lax.broadcasted_iota(jnp.int32, (TILE_ROWS, TILE_TOKENS), 0)
    place = srow == lst_ref[0:1, :]
    for k in range(1, TOP_K):
        place = jnp.logical_or(place, srow == lst_ref[k:k + 1, :])
    x = jnp.where(i < n_prompt_tiles, hp_ref[...], hs_ref[...]).astype(BF16)
    buf[slot] = jnp.dot(jnp.where(place, 1.0, 0.0).astype(BF16), x, preferred_element_type=F32)
    _tile_strips(i, slot, d_ref, n_ref, o_ref, buf, xs_hbm, sem, to_hbm=True)

    @pl.when(i == n_tiles - 1)
    def _():
        _tile_wait(r_ref[i], slot, buf, xs_hbm, sem)
        if n_tiles > 1:
            _tile_wait(r_ref[jnp.maximum(i - 1, 0)], 1 - slot, buf, xs_hbm, sem)
        pad_strips(True)


def _dispatch(hp, hs, ls_t, tabs, n_rows):
    tp, d = hp.shape
    npt = tp // TILE_TOKENS
    assert tp % TILE_TOKENS == 0 and hs.shape[0] == TILE_TOKENS
    nt = npt + 1
    grid_spec = pltpu.PrefetchScalarGridSpec(
        num_scalar_prefetch=6,
        grid=(nt,),
        in_specs=[
            pl.BlockSpec((TOP_K, TILE_TOKENS), lambda i, *_: (0, i)),
            pl.BlockSpec((TILE_TOKENS, d), lambda i, *_: (jnp.minimum(i, npt - 1), 0)),
            pl.BlockSpec((TILE_TOKENS, d), lambda i, *_: (0, 0)),
        ],
        out_specs=pl.BlockSpec(memory_space=pl.ANY),
        scratch_shapes=[
            pltpu.VMEM((2, TILE_ROWS, d), F32),
            pltpu.VMEM((PAD_LEN_BITS[-1], d), F32),
            pltpu.SemaphoreType.DMA((2,)),
            pltpu.SemaphoreType.DMA((1,)),
        ],
    )
    return pl.pallas_call(
        functools.partial(_dispatch_kernel, n_tiles=nt, n_prompt_tiles=npt),
        grid_spec=grid_spec,
        out_shape=jax.ShapeDtypeStruct((n_rows, d), F32),
        compiler_params=_cparams(("arbitrary",)),
        name="moe_dispatch",
    )(tabs["d"], tabs["n"], tabs["o"], tabs["rows"], tabs["pad_start"], tabs["pad_len"], ls_t, hp, hs)


def _expert_ffn_kernel(b0_ref, nb_ref, x_hbm, wgu_hbm, bgu_ref, wd_hbm, bd_ref, y_hbm, xbuf, ybuf, xtail, ytail,
                       xsem, ysem, tsem, wgu_st, wd_st, wsem, wgu_bf, wd_bf, *, n_blocks):
    e = pl.program_id(0)
    b0 = b0_ref[e]
    n = nb_ref[e]
    n2 = n // FFN_PAIR
    odd = n % FFN_PAIR
    dff = wd_st.shape[0]
    big = FFN_PAIR * MOE_BM

    def w_copies(ex):
        return (pltpu.make_async_copy(wgu_hbm.at[ex], wgu_st, wsem.at[0]),
                pltpu.make_async_copy(wd_hbm.at[ex], wd_st, wsem.at[1]))

    def rows(blk, size):
        return pl.ds(pl.multiple_of((b0 + blk) * MOE_BM, MOE_BM), size)

    def x_copy(kk, slot):
        return pltpu.make_async_copy(x_hbm.at[rows(FFN_PAIR * kk, big)], xbuf.at[slot], xsem.at[slot])

    def y_copy(kk, slot):
        return pltpu.make_async_copy(ybuf.at[slot], y_hbm.at[rows(FFN_PAIR * kk, big)], ysem.at[slot])

    def xt_copy():
        return pltpu.make_async_copy(x_hbm.at[rows(n - 1, MOE_BM)], xtail, tsem.at[0])

    def yt_copy():
        return pltpu.make_async_copy(ytail, y_hbm.at[rows(n - 1, MOE_BM)], tsem.at[1])

    def ffn(x):
        hu = jnp.dot(x.astype(BF16), wgu_bf[...], preferred_element_type=F32) + bgu_ref[0]
        g = jnp.minimum(hu[:, :dff], SWIGLU_LIMIT)
        u = jnp.clip(hu[:, dff:], -SWIGLU_LIMIT, SWIGLU_LIMIT)
        a = g * jax.nn.sigmoid(SWIGLU_ALPHA * g) * (u + 1.0)
        return jnp.dot(a.astype(BF16), wd_bf[...], preferred_element_type=F32) + bd_ref[0]

    @pl.when(e == 0)
    def _():
        for c in w_copies(0):
            c.start()

    @pl.when(jnp.logical_and(n > 0, odd == 1))
    def _():
        xt_copy().start()

    @pl.when(n2 > 0)
    def _():
        x_copy(0, 0).start()

    @pl.when(n2 > 1)
    def _():
        x_copy(1, 1).start()

    for c in w_copies(e):
        c.wait()

    @pl.when(n > 0)
    def _():
        wgu_bf[...] = wgu_st[...].astype(BF16)
        wd_bf[...] = wd_st[...].astype(BF16)

    @pl.when(e + 1 < N_EXPERTS)
    def _():
        for c in w_copies(e + 1):
            c.start()

    def drain(nblocks):
        m2 = nblocks // FFN_PAIR

        def y_wait(slot):
            pltpu.make_async_copy(ybuf.at[slot], y_hbm.at[pl.ds(0, big)], ysem.at[slot]).wait()

        @pl.when(m2 >= 2)
        def _():
            y_wait(m2 % 2)

        @pl.when(m2 >= 1)
        def _():
            y_wait((m2 - 1) % 2)

        @pl.when(nblocks % FFN_PAIR == 1)
        def _():
            pltpu.make_async_copy(ytail, y_hbm.at[pl.ds(0, MOE_BM)], tsem.at[1]).wait()

    @pl.when(e >= 1)
    def _():
        drain(nb_ref[jnp.maximum(e - 1, 0)])

    @pl.when(n > 0)
    def _():
        def body(kk, carry):
            slot = kk % 2

            @pl.when(jnp.logical_and(kk >= 1, kk + 1 < n2))
            def _():
                x_copy(kk + 1, 1 - slot).start()

            x_copy(kk, slot).wait()

            @pl.when(kk >= 2)
            def _():
                y_copy(kk - 2, slot).wait()

            ybuf[slot] = ffn(xbuf[slot])
            y_copy(kk, slot).start()
            return carry
        lax.fori_loop(0, n2, body, 0)

        @pl.when(odd == 1)
        def _():
            xt_copy().wait()
            ytail[...] = ffn(xtail[...])
            yt_copy().start()

    @pl.when(e == N_EXPERTS - 1)
    def _():
        drain(n)
        ytail[...] = jnp.zeros_like(ytail)

        def fill(wait):
            def body(k, carry):
                cp = pltpu.make_async_copy(ytail, y_hbm.at[pl.ds(pl.multiple_of(k * MOE_BM, MOE_BM), MOE_BM)],
                                           tsem.at[1])
                if wait:
                    cp.wait()
                else:
                    cp.start()
                return carry
            lax.fori_loop(b0 + n, n_blocks, body, 0)
        fill(False)
        fill(True)


def _expert_ffn(x_sorted, blk_start, nblk, w_gate_up, b_gate_up, w_down, b_down):
    n_rows, d = x_sorted.shape
    ne, _, dff2 = w_gate_up.shape
    dff = dff2 // 2
    any_spec = pl.BlockSpec(memory_space=pl.ANY)
    big = FFN_PAIR * MOE_BM
    grid_spec = pltpu.PrefetchScalarGridSpec(
        num_scalar_prefetch=2,
        grid=(ne,),
        in_specs=[
            any_spec,
            any_spec,
            pl.BlockSpec((1, 1, dff2), lambda e, *_: (e, 0, 0)),
            any_spec,
            pl.BlockSpec((1, 1, d), lambda e, *_: (e, 0, 0)),
        ],
        out_specs=any_spec,
        scratch_shapes=[
            pltpu.VMEM((2, big, d), F32), pltpu.VMEM((2, big, d), F32),
            pltpu.VMEM((MOE_BM, d), F32), pltpu.VMEM((MOE_BM, d), F32),
            pltpu.SemaphoreType.DMA((2,)), pltpu.SemaphoreType.DMA((2,)), pltpu.SemaphoreType.DMA((2,)),
            pltpu.VMEM((d, dff2), F32), pltpu.VMEM((dff, d), F32), pltpu.SemaphoreType.DMA((2,)),
            pltpu.VMEM((d, dff2), BF16), pltpu.VMEM((dff, d), BF16),
        ],
    )
    return pl.pallas_call(
        functools.partial(_expert_ffn_kernel, n_blocks=n_rows // MOE_BM),
        grid_spec=grid_spec,
        out_shape=jax.ShapeDtypeStruct((n_rows, d), F32),
        compiler_params=_cparams(("arbitrary",)),
        name="expert_ffn",
    )(blk_start, nblk, x_sorted, w_gate_up, b_gate_up.reshape(ne, 1, dff2), w_down, b_down.reshape(ne, 1, d))


def _combine_kernel(d_ref, n_ref, o_ref, r_ref, ls_ref, gate_ref, xp_ref, xs_ref, g_ref, y_hbm, yp_ref, ys_ref,
                    buf, sem, *, n_tiles, n_prompt_tiles):
    j = pl.program_id(0)

    @pl.when(j == 0)
    def _():
        buf[...] = jnp.zeros_like(buf)

    @pl.when(j < n_tiles)
    def _():
        _tile_strips(j, j % 2, d_ref, n_ref, o_ref, buf, y_hbm, sem, to_hbm=False)

    @pl.when(j >= 1)
    def _():
        slot = (j - 1) % 2
        _tile_wait(r_ref[j - 1], slot, buf, y_hbm, sem)
        r_hi = buf[slot].astype(BF16)
        lane = lax.broadcasted_iota(jnp.int32, (TILE_TOKENS, TILE_ROWS), 1)
        pick = jnp.zeros((TILE_TOKENS, TILE_ROWS), F32)
        for k in range(TOP_K):
            pick = pick + jnp.where(lane == ls_ref[:, k:k + 1], gate_ref[:, k:k + 1], 0.0)
        p_hi = pick.astype(BF16)
        p_lo = (pick - p_hi.astype(F32)).astype(BF16)
        moe = jnp.dot(p_hi, r_hi, preferred_element_type=F32) + jnp.dot(p_lo, r_hi, preferred_element_type=F32)
        is_prompt = j - 1 < n_prompt_tiles
        y = _rms(jnp.where(is_prompt, xp_ref[...], xs_ref[...]) + moe, g_ref[...])

        @pl.when(is_prompt)
        def _():
            yp_ref[...] = y

        @pl.when(jnp.logical_not(is_prompt))
        def _():
            ys_ref[...] = y


def _combine(y_sorted, x2p, x2s, ls, gates, tabs, g_final):
    tp, d = x2p.shape
    npt = tp // TILE_TOKENS
    assert x2s.shape[0] == TILE_TOKENS
    nt = npt + 1
    tile = lambda j: jnp.clip(j - 1, 0, nt - 1)
    ptile = lambda j: jnp.clip(j - 1, 0, npt - 1)
    grid_spec = pltpu.PrefetchScalarGridSpec(
        num_scalar_prefetch=4,
        grid=(nt + 1,),
        in_specs=[
            pl.BlockSpec((TILE_TOKENS, TOP_K), lambda j, *_: (tile(j), 0)),
            pl.BlockSpec((TILE_TOKENS, TOP_K), lambda j, *_: (tile(j), 0)),
            pl.BlockSpec((TILE_TOKENS, d), lambda j, *_: (ptile(j), 0)),
            pl.BlockSpec((TILE_TOKENS, d), lambda j, *_: (0, 0)),
            pl.BlockSpec((1, d), lambda j, *_: (0, 0)),
            pl.BlockSpec(memory_space=pl.ANY),
        ],
        out_specs=[
            pl.BlockSpec((TILE_TOKENS, d), lambda j, *_: (ptile(j), 0)),
            pl.BlockSpec((TILE_TOKENS, d), lambda j, *_: (0, 0)),
        ],
        scratch_shapes=[pltpu.VMEM((2, TILE_ROWS, d), F32), pltpu.SemaphoreType.DMA((2,))],
    )
    return pl.pallas_call(
        functools.partial(_combine_kernel, n_tiles=nt, n_prompt_tiles=npt),
        grid_spec=grid_spec,
        out_shape=[jax.ShapeDtypeStruct((tp, d), F32), jax.ShapeDtypeStruct((TILE_TOKENS, d), F32)],
        compiler_params=_cparams(("arbitrary",)),
        name="moe_combine",
    )(tabs["d"], tabs["n"], tabs["o"], tabs["rows"], ls, gates, x2p, x2s, g_final.reshape(1, d), y_sorted)


def _route_kernel(lp_ref, ls_in_ref, gate_ref, row_ref, cnt_ref, *, n_prompt_tiles, n_real_last):
    i = pl.program_id(0)
    is_prompt = i < n_prompt_tiles
    logits = jnp.where(is_prompt, lp_ref[...], ls_in_ref[...])
    tok = lax.broadcasted_iota(jnp.int32, (1, TILE_TOKENS), 1)
    real = jnp.logical_or(is_prompt, tok < n_real_last)
    eid = lax.broadcasted_iota(jnp.int32, logits.shape, 0)
    work = logits
    vals, sels = [], []
    for _ in range(TOP_K):
        v = jnp.max(work, axis=0, keepdims=True)
        idx = jnp.min(jnp.where(work == v, eid, N_EXPERTS), axis=0, keepdims=True)
        vals.append(v)
        sels.append(jnp.logical_and(eid == idx, real))
        work = jnp.where(eid == idx, -jnp.inf, work)
    es = [jnp.exp(v - vals[0]) for v in vals]
    denom = es[0] + es[1] + es[2] + es[3]
    chosen = jnp.zeros(logits.shape, F32)
    for sel in sels:
        chosen = chosen + jnp.where(sel, 1.0, 0.0)
    r = lax.broadcasted_iota(jnp.int32, (TILE_TOKENS, TILE_TOKENS), 0)
    c = lax.broadcasted_iota(jnp.int32, (TILE_TOKENS, TILE_TOKENS), 1)
    rank = jnp.dot(chosen.astype(BF16), jnp.where(r < c, 1.0, 0.0).astype(BF16), preferred_element_type=F32)
    count = jnp.sum(chosen, axis=1, keepdims=True)
    strip = jnp.floor((count + (SUBLANES - 1)) * (1.0 / SUBLANES)) * SUBLANES
    er = lax.broadcasted_iota(jnp.int32, (N_EXPERTS, N_EXPERTS), 0)
    ec = lax.broadcasted_iota(jnp.int32, (N_EXPERTS, N_EXPERTS), 1)
    start = jnp.dot(jnp.where(ec < er, 1.0, 0.0).astype(BF16),
                    jnp.broadcast_to(strip, (N_EXPERTS, LANES)).astype(BF16), preferred_element_type=F32)[:, 0:1]
    place = start + rank
    for k in range(TOP_K):
        row_k = jnp.sum(jnp.where(sels[k], place, 0.0), axis=0, keepdims=True)
        row_ref[k:k + 1, :] = jnp.where(real, row_k, -1.0).astype(jnp.int32)
        gate_ref[k:k + 1, :] = jnp.where(real, es[k] / denom, 0.0)
    cnt_ref[0] = strip.astype(jnp.int32)


def _route(lgp_t, lgs):
    ne, tp = lgp_t.shape
    ts = lgs.shape[0]
    assert tp % TILE_TOKENS == 0 and ts <= TILE_TOKENS and ne == N_EXPERTS
    npt = tp // TILE_TOKENS
    nt = npt + 1
    lgs_t = jnp.pad(lgs.T, ((0, 0), (0, TILE_TOKENS - ts)))
    tile4 = pl.BlockSpec((TOP_K, TILE_TOKENS), lambda i: (0, i))
    gates_t, ls_t, n_tile = pl.pallas_call(
        functools.partial(_route_kernel, n_prompt_tiles=npt, n_real_last=ts),
        grid=(nt,),
        in_specs=[pl.BlockSpec((ne, TILE_TOKENS), lambda i: (0, jnp.minimum(i, npt - 1))),
                  pl.BlockSpec((ne, TILE_TOKENS), lambda i: (0, 0))],
        out_specs=[tile4, tile4, pl.BlockSpec((1, ne, 1), lambda i: (i, 0, 0))],
        out_shape=[jax.ShapeDtypeStruct((TOP_K, nt * TILE_TOKENS), F32),
                   jax.ShapeDtypeStruct((TOP_K, nt * TILE_TOKENS), jnp.int32),
                   jax.ShapeDtypeStruct((nt, ne, 1), jnp.int32)],
        compiler_params=_cparams(("arbitrary",)),
        name="moe_route",
    )(lgp_t, lgs_t)
    n_tile = n_tile.reshape(nt, ne)
    up = lambda a, m: (a + m - 1) // m * m
    o_tile = jnp.cumsum(n_tile, axis=1) - n_tile
    rows_e = jnp.sum(n_tile, axis=0)
    nblk = up(rows_e, MOE_BM) // MOE_BM
    blk_end = jnp.cumsum(nblk)
    pstart = (blk_end - nblk) * MOE_BM
    d_tile = pstart[None, :] + jnp.cumsum(n_tile, axis=0) - n_tile
    t = tp + ts
    nb = (t * TOP_K + N_EXPERTS * nt * (SUBLANES - 1) + N_EXPERTS * (MOE_BM - 1) + MOE_BM - 1) // MOE_BM
    i32 = lambda a: a.astype(jnp.int32).reshape(-1)
    tabs = {"d": i32(d_tile), "n": i32(n_tile), "o": i32(o_tile), "rows": i32(jnp.sum(n_tile, axis=1)),
            "pad_start": i32(pstart + rows_e), "pad_len": i32(nblk * MOE_BM - rows_e),
            "blk_start": i32(blk_end - nblk), "nblk": i32(nblk)}
    return gates_t, ls_t, tabs, nb * MOE_BM


def kernel(x_prompt, x_sample, cache_k_win, cache_v_win, state_pool, cache_mem_k, cache_mem_v, mem_prompt,
           rel_bias, g_mix, w_in, w_pool, pool_scale, w_out, g_mem, g_x, w_xq, w_xk, w_xv, w_xo, g_ff,
           w_router, b_router, w_gate_up, b_gate_up, w_down, b_down, g_final):
    depth = g_mix.shape[0]
    assert depth == 1
    l = 0
    batch, seq, d = x_prompt.shape
    bd, n_new, _ = x_sample.shape
    n_mem = mem_prompt.shape[1]
    bw = d - A_WIDTH
    n_hist = cache_k_win.shape[2]
    n_pool = state_pool.shape[2]
    tp, ts = batch * seq, bd * n_new

    w_in_b = w_in[l].astype(BF16)
    w_pool_b = w_pool[l].astype(BF16)
    w_xkv_b = jnp.concatenate([w_xk[l], w_xv[l]], axis=1).astype(BF16)
    wts = {
        "w_out_a": w_out[l, :A_WIDTH].astype(BF16), "w_out_b": w_out[l, A_WIDTH:].astype(BF16),
        "g_x": g_x[l].reshape(1, d), "w_xq": w_xq[l].astype(BF16), "w_xo": w_xo[l].astype(BF16),
        "g_ff": g_ff[l].reshape(1, d), "w_router": w_router[l], "b_router": b_router[l].reshape(1, N_EXPERTS),
    }
    band_bias = _band_bias(rel_bias)
    widths = (A_WIDTH, A_WIDTH, A_WIDTH, bw)
    scales = (A_HEAD_DIM ** -0.5, 1.0, 1.0, 1.0)

    xp = x_prompt.reshape(tp, d)
    w_kv_t = jnp.transpose(w_in[l][:, A_WIDTH:3 * A_WIDTH].reshape(d, 2, A_WIDTH), (1, 2, 0)).astype(BF16)
    q, k, v, u, k_t, v_t = _in_proj(xp, g_mix[l], w_in_b, w_kv_t, widths, scales, batch, seq, tm=512)
    o_a = _dil_attn(q, k, v, band_bias, batch, seq)
    o_b = _pool_prompt(u, w_pool_b, pool_scale[l], batch, seq, tm=512)
    mk, mv, mk_b, mv_b = _mem_kv(mem_prompt.reshape(batch * n_mem, d), g_mem[l], w_xkv_b, X_HEADS, tm=512)
    x2p, hp, lgp = _mix_xattn(xp, o_a, o_b, mk_b, mv_b, wts, groups=batch, rows_per_group=seq, tm=512,
                              rows_per_sub=None, keys_per_sub=None, logits_t=True)

    xs = x_sample.reshape(ts, d)
    qs, ks, vs, us = _norm_proj(xs, g_mix[l], w_in_b, widths, scales, tm=ts)
    ck, cv = cache_k_win[l], cache_v_win[l]
    new3 = lambda a: a.reshape(bd, n_new, A_WIDTH)
    to_t = lambda c: jnp.transpose(c, (0, 2, 3, 1)).reshape(bd, A_WIDTH, n_hist)
    from_t = lambda c: jnp.transpose(c.reshape(bd, A_HEADS, A_HEAD_DIM, n_hist), (0, 3, 1, 2))[None]
    bias_c, bias_n = _decode_bias(rel_bias, n_hist, n_new)
    o_as, ck_t, cv_t = _decode_attn(new3(qs), new3(ks), new3(vs), to_t(ck), to_t(cv), bias_c, bias_n,
                                    heads_per_step=A_HEADS)
    o_as = o_as.reshape(ts, A_WIDTH)
    us3 = us.reshape(bd, n_new, bw)
    hist = jnp.concatenate([jnp.zeros((bd, POOL_HIST - n_pool, bw), F32), state_pool[l]], axis=1)
    cur = jnp.concatenate([us3, jnp.zeros((bd, SUBLANES - n_new, bw), F32)], axis=1)
    o_bs = _pool_dec(hist, cur, n_pool, w_pool_b, pool_scale[l])[:, :n_new].reshape(ts, bw)
    sub = 8
    xhd = d // X_HEADS
    x2s, hs, lgs = _mix_xattn(xs, o_as, o_bs, cache_mem_k[l].reshape(bd * n_mem, X_HEADS, xhd),
                              cache_mem_v[l].reshape(bd * n_mem, X_HEADS, xhd), wts, groups=bd // sub,
                              rows_per_group=sub * n_new, tm=sub * n_new, rows_per_sub=n_new, keys_per_sub=n_mem,
                              logits_t=False)

    assert ts <= TILE_TOKENS and w_gate_up.shape[1] == N_EXPERTS
    pad_s = lambda a: jnp.pad(a, ((0, TILE_TOKENS - ts), (0, 0)))
    gates_t, ls_t, tabs, n_rows = _route(lgp, lgs)
    gates, ls = gates_t.T, ls_t.T
    x_sorted = _dispatch(hp, pad_s(hs), ls_t, tabs, n_rows)
    y_sorted = _expert_ffn(x_sorted, tabs["blk_start"], tabs["nblk"], w_gate_up[l], b_gate_up[l], w_down[l],
                           b_down[l])
    y_prompt, y_sample = _combine(y_sorted, x2p, pad_s(x2s), ls, gates, tabs, g_final)
    y_prompt = y_prompt.reshape(batch, seq, d)
    y_sample = y_sample[:ts].reshape(bd, n_new, d)

    from_tp = lambda c: jnp.transpose(c.reshape(batch, A_HEADS, A_HEAD_DIM, seq), (0, 3, 1, 2))[None]
    k_win_prompt, v_win_prompt = from_tp(k_t), from_tp(v_t)
    pool_prompt = u.reshape(batch, seq, bw)[:, seq - n_pool:][None]
    mem_k_prompt = mk.reshape(1, batch, n_mem, X_HEADS, xhd)
    mem_v_prompt = mv.reshape(1, batch, n_mem, X_HEADS, xhd)
    k_win_sample, v_win_sample = from_t(ck_t), from_t(cv_t)
    pool_sample = jnp.concatenate([state_pool[l][:, n_new:], us3], axis=1)[None]
    return (y_prompt, y_sample, k_win_prompt, v_win_prompt, pool_prompt, mem_k_prompt, mem_v_prompt,
            k_win_sample, v_win_sample, pool_sample)
```
